```python
import jax, jax.numpy as jnp
from jax import lax
import numpy as np

D_MODEL = 2048
BATCH = 8
SEQ = 8192
DEPTH = 2

CTX_LEN = 256
GRID_W = 64
D_MIX = D_MODEL
D_CONV = D_MIX // 2
D_RET = D_MIX - D_CONV
RET_HEADS = 8
RET_HEAD_DIM = D_RET // RET_HEADS
CONV_WIDTH = 3
CHUNK = 128
D_IN = 4 * D_CONV + 4 * D_RET
ROPE_BASE = 10000.0
RET_DECAY_OFFSET = 5.0
EPS = 1e-6

kernel_name = "hybrid_conv_retention_prefix_dit_block"


def _rmsnorm(x, w):
    xf = x.astype(jnp.float32)
    y = xf * lax.rsqrt(jnp.mean(xf * xf, axis=-1, keepdims=True) + EPS)
    return (y * w.astype(jnp.float32)).astype(x.dtype)


def _split_in(u):
    idx = [D_CONV, 2 * D_CONV, 3 * D_CONV, 4 * D_CONV,
           4 * D_CONV + D_RET, 4 * D_CONV + 2 * D_RET, 4 * D_CONV + 3 * D_RET]
    return jnp.split(u, idx, axis=-1)


def _short_conv(u, w):
    up = jnp.pad(u, ((0, 0), (1, 1), (0, 0)))
    return up[:, :-2] * w[0] + up[:, 1:-1] * w[1] + up[:, 2:] * w[2]


def _conv_branch(h, b, c, z, conv_w, norm_w):
    y = b * _short_conv(c * h, conv_w)
    return jax.nn.silu(z) * _rmsnorm(y, norm_w)


def _heads(t):
    b, l, _ = t.shape
    return t.reshape(b, l, RET_HEADS, RET_HEAD_DIM).transpose(0, 2, 1, 3)


def _rope_1d(x, pos):
    f = x.shape[-1] // 2
    inv = ROPE_BASE ** (-jnp.arange(f, dtype=jnp.float32) / f)
    ang = pos.astype(jnp.float32)[:, None] * inv[None, :]
    cos, sin = jnp.cos(ang), jnp.sin(ang)
    x1, x2 = x[..., :f], x[..., f:]
    return jnp.concatenate([x1 * cos - x2 * sin, x1 * sin + x2 * cos], axis=-1).astype(x.dtype)


def _axial_rope(x, row_pos, col_pos):
    half = x.shape[-1] // 2
    return jnp.concatenate([_rope_1d(x[..., :half], row_pos),
                            _rope_1d(x[..., half:], col_pos)], axis=-1)


def _chunk_retention(q, k, v, lg, s0):
    b, h, l, dk = q.shape
    n = l // CHUNK
    qc = q.reshape(b, h, n, CHUNK, dk)
    kc = k.reshape(b, h, n, CHUNK, dk)
    vc = v.reshape(b, h, n, CHUNK, v.shape[-1])
    pos = jnp.arange(CHUNK, dtype=jnp.float32)
    diff = pos[:, None] - pos[None, :]
    dmask = jnp.where(diff >= 0, jnp.exp(lg[:, None, None] * jnp.maximum(diff, 0.0)[None]), 0.0)
    scores = jnp.einsum('bhnid,bhnjd->bhnij', qc, kc) * dmask[None, :, None]
    intra = jnp.einsum('bhnij,bhnje->bhnie', scores, vc)
    k_decay = jnp.exp(lg[:, None] * (CHUNK - 1 - pos)[None])
    q_decay = jnp.exp(lg[:, None] * (pos + 1.0)[None])
    chunk_decay = jnp.exp(lg * CHUNK)
    chunk_kv = jnp.einsum('bhnjd,hj,bhnje->nbhde', kc, k_decay, vc)

    def step(s, kv):
        return chunk_decay[None, :, None, None] * s + kv, s

    _, s_prev = lax.scan(step, s0, chunk_kv)
    inter = jnp.einsum('bhnid,hi,nbhde->bhnie', qc, q_decay, s_prev)
    return (intra + inter).reshape(b, h, l, -1)


def _bidir_retention(q, k, v, lg_f, lg_b, s0_f, s0_b):
    o_f = _chunk_retention(q, k, v, lg_f, s0_f)
    flip = lambda t: jnp.flip(t, axis=2)
    o_b = _chunk_retention(flip(q), flip(k), flip(v), lg_b, s0_b)
    return o_f + flip(o_b)


def _context_states(k, v, lg_f, lg_b):
    lc = k.shape[2]
    t = jnp.arange(lc, dtype=jnp.float32)
    w_f = jnp.exp(lg_f[:, None] * (lc - 1.0 - t)[None])
    w_b = jnp.exp(lg_b[:, None] * t[None])
    s_f = jnp.einsum('bhtd,ht,bhte->bhde', k, w_f, v)
    s_b = jnp.einsum('bhtd,ht,bhte->bhde', k, w_b, v)
    return s_f, s_b


def _ret_out(o, z, gn_w):
    of = o.astype(jnp.float32)
    mu = jnp.mean(of, axis=-1, keepdims=True)
    var = jnp.mean(jnp.square(of - mu), axis=-1, keepdims=True)
    on = (of - mu) * lax.rsqrt(var + EPS)
    b, h, l, d = on.shape
    on = on.transpose(0, 2, 1, 3).reshape(b, l, h * d) * gn_w.astype(jnp.float32)
    return jax.nn.silu(z) * on.astype(z.dtype)


def _layer(x, ctx, c, c_ctx, norm_w, w_mod, b_mod, w_in, conv_w, conv_norm_w, ret_norm_w,
           decay_f, decay_b, w_out, row_pos, col_pos, update_ctx):
    d = D_MODEL
    lg_f = -jnp.exp(decay_f.astype(jnp.float32))
    lg_b = -jnp.exp(decay_b.astype(jnp.float32))
    k_scale = RET_HEAD_DIM ** -0.5

    shift, scale, gate = jnp.split(jax.nn.silu(c) @ w_mod + b_mod, 3, axis=-1)
    hx = _rmsnorm(x, norm_w) * (1 + scale[:, None]) + shift[:, None]
    a_h, a_b, a_c, a_z, q, k, v, r_z = _split_in(hx @ w_in)

    n_mod = 3 if update_ctx else 2
    mod_c = jax.nn.silu(c_ctx) @ w_mod[:, :n_mod * d] + b_mod[:n_mod * d]
    hc = _rmsnorm(ctx, norm_w) * (1 + mod_c[d:2 * d]) + mod_c[:d]
    if update_ctx:
        ca_h, ca_b, ca_c, ca_z, cq, ck, cv, cr_z = _split_in(hc @ w_in)
    else:
        kv0 = 4 * D_CONV + D_RET
        ck, cv = jnp.split(hc @ w_in[:, kv0:kv0 + 2 * D_RET], 2, axis=-1)
    ck_h = _heads(ck) * k_scale
    cv_h = _heads(cv)
    s_f, s_b = _context_states(ck_h, cv_h, lg_f, lg_b)

    q_h = _axial_rope(_heads(q), row_pos, col_pos)
    k_h = _axial_rope(_heads(k), row_pos, col_pos) * k_scale
    o = _bidir_retention(q_h, k_h, _heads(v), lg_f, lg_b, s_f, s_b)
    y_ret = _ret_out(o, r_z, ret_norm_w)
    y_conv = _conv_branch(a_h, a_b, a_c, a_z, conv_w, conv_norm_w)
    x = x + gate[:, None] * (jnp.concatenate([y_conv, y_ret], axis=-1) @ w_out)

    if update_ctx:
        zeros = jnp.zeros_like(s_f)
        oc = _bidir_retention(_heads(cq), ck_h, cv_h, lg_f, lg_b, zeros, zeros)
        yc_ret = _ret_out(oc, cr_z, ret_norm_w)
        yc_conv = _conv_branch(ca_h, ca_b, ca_c, ca_z, conv_w, conv_norm_w)
        ctx = ctx + mod_c[2 * d:] * (jnp.concatenate([yc_conv, yc_ret], axis=-1) @ w_out)
    return x, ctx


def _fwd_setup_inputs(seed: int = 0) -> dict:
    key = jax.random.key(seed)
    ks = jax.random.split(key, 16)
    f32 = jnp.float32
    nrm = lambda k, s: jax.random.normal(k, s, f32)
    base = jnp.log(-jnp.log1p(-(2.0 ** -(RET_DECAY_OFFSET + jnp.arange(RET_HEADS, dtype=f32)))))
    return {
        "x": nrm(ks[0], (BATCH, SEQ, D_MODEL)),
        "c": nrm(ks[1], (BATCH, D_MODEL)),
        "ctx": nrm(ks[2], (BATCH, CTX_LEN, D_MODEL)),
        "c_ctx": nrm(ks[3], (D_MODEL,)),
        "norm_w": 1.0 + 0.05 * nrm(ks[4], (DEPTH, D_MODEL)),
        "w_mod": nrm(ks[5], (DEPTH, D_MODEL, 3 * D_MODEL)) * (0.5 * D_MODEL ** -0.5),
        "b_mod": 0.02 * nrm(ks[6], (DEPTH, 3 * D_MODEL)),
        "w_in": nrm(ks[7], (DEPTH, D_MODEL, D_IN)) * D_MODEL ** -0.5,
        "conv_w": nrm(ks[8], (DEPTH, CONV_WIDTH, D_CONV)) * CONV_WIDTH ** -0.5,
        "conv_norm_w": 1.0 + 0.05 * nrm(ks[9], (DEPTH, D_CONV)),
        "ret_norm_w": 1.0 + 0.05 * nrm(ks[10], (DEPTH, D_RET)),
        "ret_decay_f": base[None] + 0.05 * nrm(ks[11], (DEPTH, RET_HEADS)),
        "ret_decay_b": base[None] + 0.05 * nrm(ks[12], (DEPTH, RET_HEADS)),
        "w_out": nrm(ks[13], (DEPTH, D_MIX, D_MODEL)) * D_MIX ** -0.5,
        "final_norm_w": 1.0 + 0.05 * nrm(ks[14], (D_MODEL,)),
    }


def _fwd_reference(x, c, ctx, c_ctx, norm_w, w_mod, b_mod, w_in, conv_w, conv_norm_w, ret_norm_w,
              ret_decay_f, ret_decay_b, w_out, final_norm_w):
    seq = x.shape[1]
    rows = seq // GRID_W
    row_pos = jnp.repeat(jnp.arange(rows), GRID_W)
    col_pos = jnp.tile(jnp.arange(GRID_W), rows)
    for layer in range(DEPTH):
        x, ctx = _layer(x, ctx, c, c_ctx, norm_w[layer], w_mod[layer], b_mod[layer], w_in[layer],
                        conv_w[layer], conv_norm_w[layer], ret_norm_w[layer],
                        ret_decay_f[layer], ret_decay_b[layer], w_out[layer],
                        row_pos, col_pos, layer < DEPTH - 1)
    return _rmsnorm(x, final_norm_w)


import jax as _jax
import jax.numpy as _jnp

TWIN_FORMAT = 'train_step'
FWD_PARAMS = ['x', 'c', 'ctx', 'c_ctx', 'norm_w', 'w_mod', 'b_mod', 'w_in', 'conv_w', 'conv_norm_w', 'ret_norm_w', 'ret_decay_f', 'ret_decay_b', 'w_out', 'final_norm_w']
TWIN_WEIGHTS = ['c_ctx', 'norm_w', 'w_mod', 'b_mod', 'w_in', 'conv_w', 'conv_norm_w', 'ret_norm_w', 'ret_decay_f', 'ret_decay_b', 'w_out', 'final_norm_w']
TWIN_DIFF_INPUT = 'x'
TWIN_INPUTS = ['x', 'c', 'ctx', 'c_ctx', 'norm_w', 'w_mod', 'b_mod', 'w_in', 'conv_w', 'conv_norm_w', 'ret_norm_w', 'ret_decay_f', 'ret_decay_b', 'w_out', 'final_norm_w', 'loss_target', 'm_c_ctx', 'm_norm_w', 'm_w_mod', 'm_b_mod', 'm_w_in', 'm_conv_w', 'm_conv_norm_w', 'm_ret_norm_w', 'm_ret_decay_f', 'm_ret_decay_b', 'm_w_out', 'm_final_norm_w', 'v_c_ctx', 'v_norm_w', 'v_w_mod', 'v_b_mod', 'v_w_in', 'v_conv_w', 'v_conv_norm_w', 'v_ret_norm_w', 'v_ret_decay_f', 'v_ret_decay_b', 'v_w_out', 'v_final_norm_w']
TWIN_OUTPUTS = ['loss', 'grad_x', 'grad_c_ctx', 'grad_norm_w', 'grad_w_mod', 'grad_b_mod', 'grad_w_in', 'grad_conv_w', 'grad_conv_norm_w', 'grad_ret_norm_w', 'grad_ret_decay_f', 'grad_ret_decay_b', 'grad_w_out', 'grad_final_norm_w', 'delta_c_ctx', 'delta_norm_w', 'delta_w_mod', 'delta_b_mod', 'delta_w_in', 'delta_conv_w', 'delta_conv_norm_w', 'delta_ret_norm_w', 'delta_ret_decay_f', 'delta_ret_decay_b', 'delta_w_out', 'delta_final_norm_w', 'new_m_c_ctx', 'new_m_norm_w', 'new_m_w_mod', 'new_m_b_mod', 'new_m_w_in', 'new_m_conv_w', 'new_m_conv_norm_w', 'new_m_ret_norm_w', 'new_m_ret_decay_f', 'new_m_ret_decay_b', 'new_m_w_out', 'new_m_final_norm_w', 'new_v_c_ctx', 'new_v_norm_w', 'new_v_w_mod', 'new_v_b_mod', 'new_v_w_in', 'new_v_conv_w', 'new_v_conv_norm_w', 'new_v_ret_norm_w', 'new_v_ret_decay_f', 'new_v_ret_decay_b', 'new_v_w_out', 'new_v_final_norm_w']
TWIN_LEAF_KINDS = {'loss': 'loss', 'grad_x': 'grad_x', 'grad_c_ctx': 'grad_w', 'grad_norm_w': 'grad_w', 'grad_w_mod': 'grad_w', 'grad_b_mod': 'grad_w', 'grad_w_in': 'grad_w', 'grad_conv_w': 'grad_w', 'grad_conv_norm_w': 'grad_w', 'grad_ret_norm_w': 'grad_w', 'grad_ret_decay_f': 'grad_w', 'grad_ret_decay_b': 'grad_w', 'grad_w_out': 'grad_w', 'grad_final_norm_w': 'grad_w', 'delta_c_ctx': 'delta_w', 'delta_norm_w': 'delta_w', 'delta_w_mod': 'delta_w', 'delta_b_mod': 'delta_w', 'delta_w_in': 'delta_w', 'delta_conv_w': 'delta_w', 'delta_conv_norm_w': 'delta_w', 'delta_ret_norm_w': 'delta_w', 'delta_ret_decay_f': 'delta_w', 'delta_ret_decay_b': 'delta_w', 'delta_w_out': 'delta_w', 'delta_final_norm_w': 'delta_w', 'new_m_c_ctx': 'new_m', 'new_m_norm_w': 'new_m', 'new_m_w_mod': 'new_m', 'new_m_b_mod': 'new_m', 'new_m_w_in': 'new_m', 'new_m_conv_w': 'new_m', 'new_m_conv_norm_w': 'new_m', 'new_m_ret_norm_w': 'new_m', 'new_m_ret_decay_f': 'new_m', 'new_m_ret_decay_b': 'new_m', 'new_m_w_out': 'new_m', 'new_m_final_norm_w': 'new_m', 'new_v_c_ctx': 'new_v', 'new_v_norm_w': 'new_v', 'new_v_w_mod': 'new_v', 'new_v_b_mod': 'new_v', 'new_v_w_in': 'new_v', 'new_v_conv_w': 'new_v', 'new_v_conv_norm_w': 'new_v', 'new_v_ret_norm_w': 'new_v', 'new_v_ret_decay_f': 'new_v', 'new_v_ret_decay_b': 'new_v', 'new_v_w_out': 'new_v', 'new_v_final_norm_w': 'new_v'}


def _forward(args):
    return _fwd_reference(*[args[k] for k in FWD_PARAMS])


def _output_shape():
    def fwd():
        inp = _fwd_setup_inputs(0)
        return _fwd_reference(*[inp[k] for k in FWD_PARAMS])
    out = _jax.eval_shape(fwd)
    return out.shape, out.dtype

N_MICROBATCH = 1
ADAM_LR = 0.001
ADAM_B1 = 0.9
ADAM_B2 = 0.999
ADAM_EPS = 1e-08
ADAM_WD = 0.01
ADAM_STEP = 10
PER_EXAMPLE_BATCH_AXIS = {'x': 0, 'c': 0, 'ctx': 0, 'loss_target': 0}
SHARED_INPUTS = []
_WEIGHT_DTYPES = {'c_ctx': _jnp.float32, 'norm_w': _jnp.float32, 'w_mod': _jnp.float32, 'b_mod': _jnp.float32, 'w_in': _jnp.float32, 'conv_w': _jnp.float32, 'conv_norm_w': _jnp.float32, 'ret_norm_w': _jnp.float32, 'ret_decay_f': _jnp.float32, 'ret_decay_b': _jnp.float32, 'w_out': _jnp.float32, 'final_norm_w': _jnp.float32}
MOMENT_SCALE = {'c_ctx': 1.897469e-02, 'norm_w': 4.822847e-02, 'w_mod': 5.198206e-02, 'b_mod': 9.891285e-02, 'w_in': 2.667359e-02, 'conv_w': 2.521049e-02, 'conv_norm_w': 2.498867e-02, 'ret_norm_w': 2.555137e-02, 'ret_decay_f': 6.979882e-02, 'ret_decay_b': 5.755479e-02, 'w_out': 2.494031e-02, 'final_norm_w': 3.195141e+01}


def _to_microbatches(a, axis):
    t = _jnp.moveaxis(a, axis, 0)
    t = t.reshape((N_MICROBATCH, t.shape[0] // N_MICROBATCH) + t.shape[1:])
    return _jnp.moveaxis(t, 1, axis + 1)


def setup_inputs(seed: int = 0) -> dict:
    inp = _fwd_setup_inputs(seed)
    key = _jax.random.fold_in(_jax.random.key(seed), 7919)
    shape, _ = _output_shape()
    out = dict(inp)
    out["loss_target"] = _jax.random.normal(_jax.random.fold_in(key, 0), shape, _jnp.float32)
    for i, name in enumerate(TWIN_WEIGHTS):
        w = inp[name].astype(_jnp.float32)
        if MOMENT_SCALE is None:
            s = _jnp.sqrt(_jnp.mean(_jnp.square(w)) + 1e-30)
        else:
            s = MOMENT_SCALE[name]
        km, kv = _jax.random.split(_jax.random.fold_in(key, i + 1))
        out[name] = w
        out["m_" + name] = s * _jax.random.normal(km, w.shape, _jnp.float32)
        out["v_" + name] = (s * s) * _jax.random.uniform(kv, w.shape, _jnp.float32, 0.5, 1.5)
    if N_MICROBATCH > 1:
        for name, axis in PER_EXAMPLE_BATCH_AXIS.items():
            out[name] = _to_microbatches(out[name], axis)
    return {'x': out['x'], 'c': out['c'], 'ctx': out['ctx'], 'c_ctx': out['c_ctx'], 'norm_w': out['norm_w'], 'w_mod': out['w_mod'], 'b_mod': out['b_mod'], 'w_in': out['w_in'], 'conv_w': out['conv_w'], 'conv_norm_w': out['conv_norm_w'], 'ret_norm_w': out['ret_norm_w'], 'ret_decay_f': out['ret_decay_f'], 'ret_decay_b': out['ret_decay_b'], 'w_out': out['w_out'], 'final_norm_w': out['final_norm_w'], 'loss_target': out['loss_target'], 'm_c_ctx': out['m_c_ctx'], 'm_norm_w': out['m_norm_w'], 'm_w_mod': out['m_w_mod'], 'm_b_mod': out['m_b_mod'], 'm_w_in': out['m_w_in'], 'm_conv_w': out['m_conv_w'], 'm_conv_norm_w': out['m_conv_norm_w'], 'm_ret_norm_w': out['m_ret_norm_w'], 'm_ret_decay_f': out['m_ret_decay_f'], 'm_ret_decay_b': out['m_ret_decay_b'], 'm_w_out': out['m_w_out'], 'm_final_norm_w': out['m_final_norm_w'], 'v_c_ctx': out['v_c_ctx'], 'v_norm_w': out['v_norm_w'], 'v_w_mod': out['v_w_mod'], 'v_b_mod': out['v_b_mod'], 'v_w_in': out['v_w_in'], 'v_conv_w': out['v_conv_w'], 'v_conv_norm_w': out['v_conv_norm_w'], 'v_ret_norm_w': out['v_ret_norm_w'], 'v_ret_decay_f': out['v_ret_decay_f'], 'v_ret_decay_b': out['v_ret_decay_b'], 'v_w_out': out['v_w_out'], 'v_final_norm_w': out['v_final_norm_w']}


def _loss(weights, diff, rest, loss_target):
    with _jax.named_scope("forward"):
        args = {**rest, TWIN_DIFF_INPUT: diff, **{k: w.astype(_WEIGHT_DTYPES[k]) for k, w in weights.items()}}
        y = _forward(args)
    with _jax.named_scope("loss_head"):
        err = _jnp.square(y.astype(_jnp.float32) - loss_target)
        return 0.5 * _jnp.sum(_jnp.mean(err, axis=-1)) if err.ndim else 0.5 * err


def _adamw(w, g, m, v):
    m = ADAM_B1 * m + (1.0 - ADAM_B1) * g
    v = ADAM_B2 * v + (1.0 - ADAM_B2) * _jnp.square(g)
    m_hat = m / (1.0 - ADAM_B1 ** ADAM_STEP)
    v_hat = v / (1.0 - ADAM_B2 ** ADAM_STEP)
    delta = -ADAM_LR * (m_hat / (_jnp.sqrt(v_hat) + ADAM_EPS) + ADAM_WD * w)
    return delta, m, v


def reference(x, c, ctx, c_ctx, norm_w, w_mod, b_mod, w_in, conv_w, conv_norm_w, ret_norm_w, ret_decay_f, ret_decay_b, w_out, final_norm_w, loss_target, m_c_ctx, m_norm_w, m_w_mod, m_b_mod, m_w_in, m_conv_w, m_conv_norm_w, m_ret_norm_w, m_ret_decay_f, m_ret_decay_b, m_w_out, m_final_norm_w, v_c_ctx, v_norm_w, v_w_mod, v_b_mod, v_w_in, v_conv_w, v_conv_norm_w, v_ret_norm_w, v_ret_decay_f, v_ret_decay_b, v_w_out, v_final_norm_w):
    given = dict(x=x, c=c, ctx=ctx, c_ctx=c_ctx, norm_w=norm_w, w_mod=w_mod, b_mod=b_mod, w_in=w_in, conv_w=conv_w, conv_norm_w=conv_norm_w, ret_norm_w=ret_norm_w, ret_decay_f=ret_decay_f, ret_decay_b=ret_decay_b, w_out=w_out, final_norm_w=final_norm_w, loss_target=loss_target, m_c_ctx=m_c_ctx, m_norm_w=m_norm_w, m_w_mod=m_w_mod, m_b_mod=m_b_mod, m_w_in=m_w_in, m_conv_w=m_conv_w, m_conv_norm_w=m_conv_norm_w, m_ret_norm_w=m_ret_norm_w, m_ret_decay_f=m_ret_decay_f, m_ret_decay_b=m_ret_decay_b, m_w_out=m_w_out, m_final_norm_w=m_final_norm_w, v_c_ctx=v_c_ctx, v_norm_w=v_norm_w, v_w_mod=v_w_mod, v_b_mod=v_b_mod, v_w_in=v_w_in, v_conv_w=v_conv_w, v_conv_norm_w=v_conv_norm_w, v_ret_norm_w=v_ret_norm_w, v_ret_decay_f=v_ret_decay_f, v_ret_decay_b=v_ret_decay_b, v_w_out=v_w_out, v_final_norm_w=v_final_norm_w)
    weights = {n: given[n] for n in TWIN_WEIGHTS}
    shared = {n: given[n] for n in SHARED_INPUTS}
    per_example = {n: given[n] for n in ['x', 'c', 'ctx']}
    grad_fn = _jax.value_and_grad(_loss, argnums=(0, 1))

    def one_microbatch(ex, loss_target):
        ex = dict(ex)
        diff = ex.pop(TWIN_DIFF_INPUT)
        return grad_fn(weights, diff, {**shared, **ex}, loss_target)

    if N_MICROBATCH == 1:
        loss, (grad_w, grad_x) = one_microbatch(per_example, given["loss_target"])
    else:
        def body(carry, xs):
            loss_sum, grad_sum = carry
            l_k, (gw_k, gx_k) = one_microbatch(xs[0], xs[1])
            with _jax.named_scope("update"):
                return (loss_sum + l_k, _jax.tree.map(_jnp.add, grad_sum, gw_k)), gx_k

        init = (_jnp.zeros((), _jnp.float32), _jax.tree.map(_jnp.zeros_like, weights))
        (loss, grad_w), grad_x = _jax.lax.scan(body, init, (per_example, given["loss_target"]))
    with _jax.named_scope("update"):
        delta_w, new_m, new_v = {}, {}, {}
        for n in TWIN_WEIGHTS:
            delta_w[n], new_m[n], new_v[n] = _adamw(weights[n], grad_w[n], given["m_" + n], given["v_" + n])
    return (loss, grad_x, *[grad_w[n] for n in TWIN_WEIGHTS], *[delta_w[n] for n in TWIN_WEIGHTS],
            *[new_m[n] for n in TWIN_WEIGHTS], *[new_v[n] for n in TWIN_WEIGHTS])
```

```python
import functools

import jax
import jax.numpy as jnp
from jax import lax
from jax.experimental import pallas as pl
from jax.experimental.pallas import tpu as pltpu

F32 = jnp.float32
BF16 = jnp.bfloat16
EPS = 1e-6
CHUNK = 128
HEAD = 128
GRID_W = 64
ROPE_BASE = 10000.0
ROW_TILE = 256
VMEM_LIMIT_BYTES = 56 * 1024 * 1024
N_DEV = 8
MESH = pl.DeviceIdType.MESH

ADAM_LR = 0.001
ADAM_B1 = 0.9
ADAM_B2 = 0.999
ADAM_EPS = 1e-08
ADAM_WD = 0.01
ADAM_STEP = 10

ANY = pl.BlockSpec(memory_space=pl.ANY)
VMEM_SPEC = pl.BlockSpec(memory_space=pltpu.VMEM)
SMEM_SPEC = pl.BlockSpec(memory_space=pltpu.SMEM)


def _params(*sem):
    return pltpu.CompilerParams(dimension_semantics=sem, vmem_limit_bytes=VMEM_LIMIT_BYTES)


def _pick(n, target, mult):
    best = None
    for d in range(mult, min(n, target) + 1, mult):
        if n % d == 0:
            best = d
    assert best is not None, (n, target, mult)
    return best


def _silu(z):
    return z * jax.nn.sigmoid(z)


def _dsilu(z):
    s = jax.nn.sigmoid(z)
    return s * (1.0 + z * (1.0 - s))


def _nn(a, b):
    return lax.dot_general(a, b, (((1,), (0,)), ((), ())), preferred_element_type=F32)


def _nt(a, b):
    return lax.dot_general(a, b, (((1,), (1,)), ((), ())), preferred_element_type=F32)


def _tn(a, b):
    return lax.dot_general(a, b, (((0,), (0,)), ((), ())), preferred_element_type=F32)


def _bf(a):
    return a.astype(BF16)


def _mesh_pos():
    return lax.axis_index("x"), lax.axis_index("y"), lax.axis_index("c")


def _allgather_small(blk, name):
    m_per, n = blk.shape

    def body(x_ref, out_ref, send_sems, recv_sems, local_sem):
        x, y, c = _mesh_pos()
        me, sibling = (x, y, c), (x, y, 1 - c)
        chips = [(1 - x, y), (x, 1 - y), (1 - x, 1 - y)]

        def rows(px, py, pc):
            return out_ref.at[pl.ds((4 * px + 2 * py + pc) * m_per, m_per), :]

        def copy(k, block, to, src=None):
            return pltpu.make_async_remote_copy(
                src_ref=rows(*block) if src is None else src, dst_ref=rows(*block),
                send_sem=send_sems.at[k], recv_sem=recv_sems.at[k], device_id=to, device_id_type=MESH)

        mine = pltpu.make_async_copy(x_ref, rows(*me), local_sem)
        mine.start()
        first = [copy(0, me, sibling, src=x_ref)]
        first += [copy(1 + j, me, (*chip, c), src=x_ref) for j, chip in enumerate(chips)]
        for cp in first:
            cp.start()
        passed = [copy(4 + j, (*chip, c), sibling) for j, chip in enumerate(chips)]
        for j, chip in enumerate(chips):
            copy(1 + j, (*chip, c), me).wait_recv()
            passed[j].start()
        copy(0, sibling, me).wait_recv()
        for j, chip in enumerate(chips):
            copy(4 + j, (*chip, 1 - c), me).wait_recv()
        for cp in first + passed:
            cp.wait_send()
        mine.wait()

    return pl.pallas_call(
        body, name=name,
        out_shape=jax.ShapeDtypeStruct((N_DEV * m_per, n), blk.dtype),
        in_specs=[VMEM_SPEC], out_specs=VMEM_SPEC,
        scratch_shapes=[pltpu.SemaphoreType.DMA((7,)), pltpu.SemaphoreType.DMA((7,)), pltpu.SemaphoreType.DMA],
    )(blk)


def _gather_weights(win_b, wout_b):
    nl, d, n4 = win_b.shape
    r4 = wout_b.shape[1]
    n_cp = nl * 2 * 3

    def body(win_ref, wout_ref, win_g, wout_g, send_sems, recv_sems, loc_sems):
        x, y, c = _mesh_pos()
        peers = [(1 - x, y), (x, 1 - y), (1 - x, 1 - y)]

        def slot(a, l, s):
            if a == 0:
                return win_g.at[l, :, pl.ds(pl.multiple_of(s * n4, 128), n4)]
            return wout_g.at[l, pl.ds(pl.multiple_of(s * r4, 8), r4), :]

        srcs = (win_ref, wout_ref)
        s_me = 2 * x + y
        locs, sends = [], []
        for l in range(nl):
            for a in range(2):
                cp = pltpu.make_async_copy(srcs[a].at[l], slot(a, l, s_me), loc_sems.at[l * 2 + a])
                cp.start()
                locs.append(cp)
        for l in range(nl):
            for a in range(2):
                for j, (px, py) in enumerate(peers):
                    k = (l * 2 + a) * 3 + j
                    cp = pltpu.make_async_remote_copy(
                        src_ref=srcs[a].at[l], dst_ref=slot(a, l, s_me), send_sem=send_sems.at[k],
                        recv_sem=recv_sems.at[k], device_id=(px, py, c), device_id_type=MESH)
                    cp.start()
                    sends.append(cp)
        for l in range(nl):
            for a in range(2):
                for j, (px, py) in enumerate(peers):
                    k = (l * 2 + a) * 3 + j
                    pltpu.make_async_remote_copy(
                        src_ref=srcs[a].at[l], dst_ref=slot(a, l, 2 * px + py), send_sem=send_sems.at[k],
                        recv_sem=recv_sems.at[k], device_id=(px, py, c), device_id_type=MESH).wait_recv()
        for cp in sends:
            cp.wait_send()
        for cp in locs:
            cp.wait()

    return pl.pallas_call(
        body, name="gather_weights",
        out_shape=(jax.ShapeDtypeStruct((nl, d, 4 * n4), BF16), jax.ShapeDtypeStruct((nl, 4 * r4, d), BF16)),
        in_specs=[ANY, ANY], out_specs=(ANY, ANY),
        scratch_shapes=[pltpu.SemaphoreType.DMA((n_cp,)), pltpu.SemaphoreType.DMA((n_cp,)),
                        pltpu.SemaphoreType.DMA((nl * 2,))],
    )(win_b, wout_b)


def _swap_halves(gin_v, gout_v):
    nl, _, dh, w = gin_v.shape
    _, ns, _, rh, d = gout_v.shape

    def body(gin_ref, gout_ref, land_in, land_out, send_sems, recv_sems):
        x, y, c = _mesh_pos()
        sib = (x, y, 1 - c)
        cps = [
            pltpu.make_async_remote_copy(src_ref=gin_ref.at[:, 1 - c], dst_ref=land_in, send_sem=send_sems.at[0],
                                         recv_sem=recv_sems.at[0], device_id=sib, device_id_type=MESH),
            pltpu.make_async_remote_copy(src_ref=gout_ref.at[:, :, 1 - c], dst_ref=land_out, send_sem=send_sems.at[1],
                                         recv_sem=recv_sems.at[1], device_id=sib, device_id_type=MESH),
        ]
        for cp in cps:
            cp.start()
        for cp in cps:
            cp.wait()

    return pl.pallas_call(
        body, name="grad_swap_halves",
        out_shape=(jax.ShapeDtypeStruct((nl, dh, w), F32), jax.ShapeDtypeStruct((nl, ns, rh, d), F32)),
        in_specs=[ANY, ANY], out_specs=(ANY, ANY),
        scratch_shapes=[pltpu.SemaphoreType.DMA((2,)), pltpu.SemaphoreType.DMA((2,))],
    )(gin_v, gout_v)


def _exchange_quarters(p_in, p_out, n4):
    nl, dh, _ = p_in.shape
    _, _, rh, d = p_out.shape
    n_cp = nl * 2 * 3

    def body(pin_ref, pout_ref, land_in, land_out, send_sems, recv_sems):
        x, y, c = _mesh_pos()
        peers = [(1 - x, y), (x, 1 - y), (1 - x, 1 - y)]
        sends = []
        for l in range(nl):
            for j, (px, py) in enumerate(peers):
                s_p = 2 * px + py
                k = (l * 2) * 3 + j
                cp = pltpu.make_async_remote_copy(
                    src_ref=pin_ref.at[l, :, pl.ds(pl.multiple_of(s_p * n4, 128), n4)], dst_ref=land_in.at[l, j],
                    send_sem=send_sems.at[k], recv_sem=recv_sems.at[k], device_id=(px, py, c), device_id_type=MESH)
                cp.start()
                sends.append(cp)
                k = (l * 2 + 1) * 3 + j
                cp = pltpu.make_async_remote_copy(
                    src_ref=pout_ref.at[l, s_p], dst_ref=land_out.at[l, j],
                    send_sem=send_sems.at[k], recv_sem=recv_sems.at[k], device_id=(px, py, c), device_id_type=MESH)
                cp.start()
                sends.append(cp)
        for cp in sends:
            cp.wait_recv()
        for cp in sends:
            cp.wait_send()

    return pl.pallas_call(
        body, name="grad_exchange_quarters",
        out_shape=(jax.ShapeDtypeStruct((nl, 3, dh, n4), F32), jax.ShapeDtypeStruct((nl, 3, rh, d), F32)),
        in_specs=[ANY, ANY], out_specs=(ANY, ANY),
        scratch_shapes=[pltpu.SemaphoreType.DMA((n_cp,)), pltpu.SemaphoreType.DMA((n_cp,))],
    )(p_in, p_out)


def _join_halves(q_in, q_out):
    nl, dh, n4 = q_in.shape
    _, rh, d = q_out.shape

    def body(qin_ref, qout_ref, full_in, full_out, send_sems, recv_sems, loc_sems):
        x, y, c = _mesh_pos()
        sib = (x, y, 1 - c)
        locs = [pltpu.make_async_copy(qin_ref, full_in.at[:, c], loc_sems.at[0]),
                pltpu.make_async_copy(qout_ref, full_out.at[:, c], loc_sems.at[1])]
        sends = [
            pltpu.make_async_remote_copy(src_ref=qin_ref, dst_ref=full_in.at[:, c], send_sem=send_sems.at[0],
                                         recv_sem=recv_sems.at[0], device_id=sib, device_id_type=MESH),
            pltpu.make_async_remote_copy(src_ref=qout_ref, dst_ref=full_out.at[:, c], send_sem=send_sems.at[1],
                                         recv_sem=recv_sems.at[1], device_id=sib, device_id_type=MESH),
        ]
        for cp in locs + sends:
            cp.start()
        pltpu.make_async_remote_copy(src_ref=qin_ref, dst_ref=full_in.at[:, 1 - c], send_sem=send_sems.at[0],
                                     recv_sem=recv_sems.at[0], device_id=sib, device_id_type=MESH).wait_recv()
        pltpu.make_async_remote_copy(src_ref=qout_ref, dst_ref=full_out.at[:, 1 - c], send_sem=send_sems.at[1],
                                     recv_sem=recv_sems.at[1], device_id=sib, device_id_type=MESH).wait_recv()
        for cp in sends:
            cp.wait_send()
        for cp in locs:
            cp.wait()

    return pl.pallas_call(
        body, name="grad_join_halves",
        out_shape=(jax.ShapeDtypeStruct((nl, 2, dh, n4), F32), jax.ShapeDtypeStruct((nl, 2, rh, d), F32)),
        in_specs=[ANY, ANY], out_specs=(ANY, ANY),
        scratch_shapes=[pltpu.SemaphoreType.DMA((2,)), pltpu.SemaphoreType.DMA((2,)), pltpu.SemaphoreType.DMA((2,))],
    )(q_in, q_out)


def _sum_half_in(pos, gin_v, land):
    nl, _, dh, w = gin_v.shape
    tr, tc = _pick(dh, 256, 8), _pick(w, 2048, 128)

    def body(pos_ref, a_ref, b_ref, o_ref):
        o_ref[...] = a_ref[...] + b_ref[...]

    return pl.pallas_call(
        body, name="grad_sum_half_in",
        out_shape=jax.ShapeDtypeStruct((nl, dh, w), F32),
        grid_spec=pltpu.PrefetchScalarGridSpec(
            num_scalar_prefetch=1, grid=(nl, dh // tr, w // tc),
            in_specs=[pl.BlockSpec((None, None, tr, tc), lambda l, i, j, p: (l, p[0], i, j)),
                      pl.BlockSpec((None, tr, tc), lambda l, i, j, p: (l, i, j))],
            out_specs=pl.BlockSpec((None, tr, tc), lambda l, i, j, p: (l, i, j))),
        compiler_params=_params("parallel", "parallel", "parallel"),
    )(pos, gin_v, land)


def _sum_half_out(pos, gout_v, land):
    nl, ns, _, rh, d = gout_v.shape

    def body(pos_ref, a_ref, b_ref, o_ref):
        o_ref[...] = a_ref[...] + b_ref[...]

    return pl.pallas_call(
        body, name="grad_sum_half_out",
        out_shape=jax.ShapeDtypeStruct((nl, ns, rh, d), F32),
        grid_spec=pltpu.PrefetchScalarGridSpec(
            num_scalar_prefetch=1, grid=(nl, ns),
            in_specs=[pl.BlockSpec((None, None, None, rh, d), lambda l, s, p: (l, s, p[0], 0, 0)),
                      pl.BlockSpec((None, None, rh, d), lambda l, s, p: (l, s, 0, 0))],
            out_specs=pl.BlockSpec((None, None, rh, d), lambda l, s, p: (l, s, 0, 0))),
        compiler_params=_params("parallel", "parallel"),
    )(pos, gout_v, land)


def _sum_quarters_in(pos, p_in, land3, n4):
    nl, dh, _ = p_in.shape
    tr = _pick(dh, 128, 8)

    def body(pos_ref, a_ref, b_ref, o_ref):
        o_ref[...] = ((a_ref[...] + b_ref[0]) + b_ref[1]) + b_ref[2]

    return pl.pallas_call(
        body, name="grad_sum_quarters_in",
        out_shape=jax.ShapeDtypeStruct((nl, dh, n4), F32),
        grid_spec=pltpu.PrefetchScalarGridSpec(
            num_scalar_prefetch=1, grid=(nl, dh // tr),
            in_specs=[pl.BlockSpec((None, tr, n4), lambda l, i, p: (l, i, p[1])),
                      pl.BlockSpec((None, 3, tr, n4), lambda l, i, p: (l, 0, i, 0))],
            out_specs=pl.BlockSpec((None, tr, n4), lambda l, i, p: (l, i, 0))),
        compiler_params=_params("parallel", "parallel"),
    )(pos, p_in, land3)


def _sum_quarters_out(pos, p_out, land3):
    nl, _, rh, d = p_out.shape

    def body(pos_ref, a_ref, b_ref, o_ref):
        o_ref[...] = ((a_ref[...] + b_ref[0]) + b_ref[1]) + b_ref[2]

    return pl.pallas_call(
        body, name="grad_sum_quarters_out",
        out_shape=jax.ShapeDtypeStruct((nl, rh, d), F32),
        grid_spec=pltpu.PrefetchScalarGridSpec(
            num_scalar_prefetch=1, grid=(nl,),
            in_specs=[pl.BlockSpec((None, None, rh, d), lambda l, p: (l, p[1], 0, 0)),
                      pl.BlockSpec((None, 3, rh, d), lambda l, p: (l, 0, 0, 0))],
            out_specs=pl.BlockSpec((None, rh, d), lambda l, p: (l, 0, 0))),
        compiler_params=_params("parallel"),
    )(pos, p_out, land3)


def _matmul(a, b, kind, name, tm_t=1056, tn_t=1024, tk_t=2048):
    if kind == "nn":
        (m, k), n = a.shape, b.shape[1]
    elif kind == "nt":
        (m, k), n = a.shape, b.shape[0]
    else:
        (k, m), n = a.shape, b.shape[1]
    tm = _pick(m, tm_t, 128 if kind == "tn" else 16)
    tn = _pick(n, tn_t, 128)
    tk = _pick(k, tk_t, 16 if kind == "tn" else 128)
    nk = k // tk
    dot = {"nn": _nn, "nt": _nt, "tn": _tn}[kind]
    a_spec = {"nn": pl.BlockSpec((tm, tk), lambda i, j, q: (i, q)),
              "nt": pl.BlockSpec((tm, tk), lambda i, j, q: (i, q)),
              "tn": pl.BlockSpec((tk, tm), lambda i, j, q: (q, i))}[kind]
    b_spec = {"nn": pl.BlockSpec((tk, tn), lambda i, j, q: (q, j)),
              "nt": pl.BlockSpec((tn, tk), lambda i, j, q: (j, q)),
              "tn": pl.BlockSpec((tk, tn), lambda i, j, q: (q, j))}[kind]

    def body(a_ref, b_ref, o_ref, acc_ref):
        q = pl.program_id(2)
        p = dot(a_ref[...], b_ref[...])
        if nk == 1:
            o_ref[...] = p
        else:
            @pl.when(q == 0)
            def _():
                acc_ref[...] = p

            @pl.when(q > 0)
            def _():
                acc_ref[...] += p

            @pl.when(q == nk - 1)
            def _():
                o_ref[...] = acc_ref[...]

    return pl.pallas_call(
        body, name=name,
        out_shape=jax.ShapeDtypeStruct((m, n), F32),
        grid=(m // tm, n // tn, nk),
        in_specs=[a_spec, b_spec],
        out_specs=pl.BlockSpec((tm, tn), lambda i, j, q: (i, j)),
        scratch_shapes=[pltpu.VMEM((tm, tn) if nk > 1 else (8, 128), F32)],
        compiler_params=_params("parallel", "parallel", "arbitrary"),
    )(a, b)


def _out_proj(y, w_out, xs, rows, n_ctx, name):
    t, k = y.shape
    d = w_out.shape[1]
    tm, tn = _pick(t, 1056, 16), _pick(d, 1024, 128)

    def body(y_ref, w_ref, x_ref, r_ref, xo_ref, p_ref):
        i = pl.program_id(0)
        p = _nn(y_ref[...], w_ref[...])
        ridx = i * tm + lax.broadcasted_iota(jnp.int32, (tm, 1), 0)
        gate = jnp.where(ridx < n_ctx, r_ref[5:6, :], r_ref[2:3, :])
        p_ref[...] = p
        xo_ref[...] = x_ref[...] + gate * p

    return pl.pallas_call(
        body, name=name,
        out_shape=(jax.ShapeDtypeStruct((t, d), F32), jax.ShapeDtypeStruct((t, d), F32)),
        grid=(t // tm, d // tn),
        in_specs=[pl.BlockSpec((tm, k), lambda i, j: (i, 0)), pl.BlockSpec((k, tn), lambda i, j: (0, j)),
                  pl.BlockSpec((tm, tn), lambda i, j: (i, j)), pl.BlockSpec((8, tn), lambda i, j: (0, j))],
        out_specs=(pl.BlockSpec((tm, tn), lambda i, j: (i, j)), pl.BlockSpec((tm, tn), lambda i, j: (i, j))),
        compiler_params=_params("parallel", "parallel"),
    )(y, w_out, xs, rows)


def _modulate(xs, nw, rows, nct, name):
    t, d = xs.shape
    tr = ROW_TILE

    def body(x_ref, nw_ref, r_ref, o_ref):
        is_ctx = pl.program_id(0) < nct
        x = x_ref[...]
        xn = (x * lax.rsqrt(jnp.mean(x * x, axis=-1, keepdims=True) + EPS)) * nw_ref[...]
        shift = jnp.where(is_ctx, r_ref[3:4, :], r_ref[0:1, :])
        scale = jnp.where(is_ctx, r_ref[4:5, :], r_ref[1:2, :])
        o_ref[...] = (xn * (1.0 + scale) + shift).astype(BF16)

    return pl.pallas_call(
        body, name=name, out_shape=jax.ShapeDtypeStruct((t, d), BF16), grid=(t // tr,),
        in_specs=[pl.BlockSpec((tr, d), lambda i: (i, 0)), pl.BlockSpec((1, d), lambda i: (0, 0)),
                  pl.BlockSpec((8, d), lambda i: (0, 0))],
        out_specs=pl.BlockSpec((tr, d), lambda i: (i, 0)),
        compiler_params=_params("parallel"),
    )(xs, nw, rows)


def _modulate_bwd(dhx, xs, g, nw, rows, nct, name):
    t, d = xs.shape
    tr = ROW_TILE

    def body(dh_ref, x_ref, g_ref, nw_ref, r_ref, go_ref, acc_ref):
        i = pl.program_id(0)
        is_ctx = i < nct

        @pl.when(i == 0)
        def _():
            acc_ref[...] = jnp.zeros_like(acc_ref)

        x, dh = x_ref[...], dh_ref[...]
        r = lax.rsqrt(jnp.mean(x * x, axis=-1, keepdims=True) + EPS)
        xh = x * r
        xn = xh * nw_ref[...]
        scale = jnp.where(is_ctx, r_ref[4:5, :], r_ref[1:2, :])
        d_shift = jnp.sum(dh, axis=0, keepdims=True)
        d_scale = jnp.sum(dh * xn, axis=0, keepdims=True)
        dxn = dh * (1.0 + scale)
        acc_ref[6:7, :] += jnp.sum(dxn * xh, axis=0, keepdims=True)
        dxh = dxn * nw_ref[...]
        dx = r * (dxh - xh * jnp.mean(dxh * xh, axis=-1, keepdims=True))
        go_ref[...] = g_ref[...] + dx
        zero = jnp.zeros_like(d_shift)
        acc_ref[0:1, :] += jnp.where(is_ctx, zero, d_shift)
        acc_ref[1:2, :] += jnp.where(is_ctx, zero, d_scale)
        acc_ref[3:4, :] += jnp.where(is_ctx, d_shift, zero)
        acc_ref[4:5, :] += jnp.where(is_ctx, d_scale, zero)

    return pl.pallas_call(
        body, name=name,
        out_shape=(jax.ShapeDtypeStruct((t, d), F32), jax.ShapeDtypeStruct((8, d), F32)), grid=(t // tr,),
        in_specs=[pl.BlockSpec((tr, d), lambda i: (i, 0)), pl.BlockSpec((tr, d), lambda i: (i, 0)),
                  pl.BlockSpec((tr, d), lambda i: (i, 0)), pl.BlockSpec((1, d), lambda i: (0, 0)),
                  pl.BlockSpec((8, d), lambda i: (0, 0))],
        out_specs=(pl.BlockSpec((tr, d), lambda i: (i, 0)), pl.BlockSpec((8, d), lambda i: (0, 0))),
        compiler_params=_params("arbitrary"),
    )(dhx, xs, g, nw, rows)


def _gate_bwd(g, proj, rows, nct, name):
    t, d = g.shape
    tr = ROW_TILE

    def body(g_ref, p_ref, r_ref, gp_ref, acc_ref):
        i = pl.program_id(0)
        is_ctx = i < nct

        @pl.when(i == 0)
        def _():
            acc_ref[...] = jnp.zeros_like(acc_ref)

        gg = g_ref[...]
        gate = jnp.where(is_ctx, r_ref[5:6, :], r_ref[2:3, :])
        gp_ref[...] = (gg * gate).astype(BF16)
        s = jnp.sum(gg * p_ref[...], axis=0, keepdims=True)
        zero = jnp.zeros_like(s)
        acc_ref[2:3, :] += jnp.where(is_ctx, zero, s)
        acc_ref[5:6, :] += jnp.where(is_ctx, s, zero)

    return pl.pallas_call(
        body, name=name,
        out_shape=(jax.ShapeDtypeStruct((t, d), BF16), jax.ShapeDtypeStruct((8, d), F32)), grid=(t // tr,),
        in_specs=[pl.BlockSpec((tr, d), lambda i: (i, 0)), pl.BlockSpec((tr, d), lambda i: (i, 0)),
                  pl.BlockSpec((8, d), lambda i: (0, 0))],
        out_specs=(pl.BlockSpec((tr, d), lambda i: (i, 0)), pl.BlockSpec((8, d), lambda i: (0, 0))),
        compiler_params=_params("arbitrary"),
    )(g, proj, rows)


def _final_loss(xs, tgt, fw, nct, name):
    t, d = xs.shape
    tr = ROW_TILE

    def body(x_ref, t_ref, w_ref, g_ref, acc_ref):
        i = pl.program_id(0)

        @pl.when(i == 0)
        def _():
            acc_ref[...] = jnp.zeros_like(acc_ref)

        @pl.when(i < nct)
        def _():
            g_ref[...] = jnp.zeros_like(g_ref)

        @pl.when(i >= nct)
        def _():
            x = x_ref[...]
            r = lax.rsqrt(jnp.mean(x * x, axis=-1, keepdims=True) + EPS)
            xh = x * r
            e = xh * w_ref[...] - t_ref[...]
            acc_ref[1:2, :] += jnp.zeros((1, d), F32) + jnp.sum(e * e) * (0.5 / d)
            dy = e * (1.0 / d)
            acc_ref[0:1, :] += jnp.sum(dy * xh, axis=0, keepdims=True)
            dxh = dy * w_ref[...]
            g_ref[...] = r * (dxh - xh * jnp.mean(dxh * xh, axis=-1, keepdims=True))

    return pl.pallas_call(
        body, name=name,
        out_shape=(jax.ShapeDtypeStruct((t, d), F32), jax.ShapeDtypeStruct((8, d), F32)), grid=(t // tr,),
        in_specs=[pl.BlockSpec((tr, d), lambda i: (i, 0)),
                  pl.BlockSpec((tr, d), lambda i: (jnp.maximum(i - nct, 0), 0)),
                  pl.BlockSpec((1, d), lambda i: (0, 0))],
        out_specs=(pl.BlockSpec((tr, d), lambda i: (i, 0)), pl.BlockSpec((8, d), lambda i: (0, 0))),
        compiler_params=_params("arbitrary"),
    )(xs, tgt, fw)


def _seq_edges(i, nct, nt):
    has_prev = jnp.logical_and(i != 0, i != nct)
    has_next = jnp.logical_and(i != nct - 1, i != nt - 1)
    return has_prev, has_next


def _shift_rows(a, before, after):
    tr = a.shape[0]
    ridx = lax.broadcasted_iota(jnp.int32, (tr, 1), 0)
    down = jnp.where(ridx == 0, before, pltpu.roll(a, 1, 0))
    up = jnp.where(ridx == tr - 1, after, pltpu.roll(a, tr - 1, 0))
    return down, up


def _halo_specs(tr, dc, col, nrow8):
    per = tr // 8
    prev = pl.BlockSpec((8, dc), lambda i: (jnp.maximum(i * per - 1, 0), col))
    nxt = pl.BlockSpec((8, dc), lambda i: (jnp.minimum((i + 1) * per, nrow8 - 1), col))
    return prev, nxt


def _mix_fwd(u, o_f, o_b, cw, cnw, rnw, nct, name):
    t, w = u.shape
    dc = w // 8
    nh = dc // HEAD
    tr = ROW_TILE
    nt = t // tr
    col = lambda cidx: pl.BlockSpec((tr, dc), lambda i: (i, cidx))
    acp, acn = _halo_specs(tr, dc, 2, t // 8)
    ahp, ahn = _halo_specs(tr, dc, 0, t // 8)

    def body(ah, ab, ac, az, rz, acp_r, acn_r, ahp_r, ahn_r, of_r, ob_r, cw_r, cnw_r, rnw_r, y_ref, cv_ref):
        i = pl.program_id(0)
        has_prev, has_next = _seq_edges(i, nct, nt)
        ch = ac[...] * ah[...]
        before = jnp.where(has_prev, acp_r[7:8, :] * ahp_r[7:8, :], 0.0)
        after = jnp.where(has_next, acn_r[0:1, :] * ahn_r[0:1, :], 0.0)
        down, up = _shift_rows(ch, before, after)
        cv = down * cw_r[0:1, :] + ch * cw_r[1:2, :] + up * cw_r[2:3, :]
        cv_ref[...] = cv
        y1 = ab[...] * cv
        yn = (y1 * lax.rsqrt(jnp.mean(y1 * y1, axis=-1, keepdims=True) + EPS)) * cnw_r[...]
        y_ref[:, 0:dc] = (_silu(az[...]) * yn).astype(BF16)
        for h in range(nh):
            sl = slice(h * HEAD, (h + 1) * HEAD)
            o = of_r[:, sl] + ob_r[:, sl]
            dev = o - jnp.mean(o, axis=-1, keepdims=True)
            on = dev * lax.rsqrt(jnp.mean(dev * dev, axis=-1, keepdims=True) + EPS)
            y_ref[:, dc + h * HEAD:dc + (h + 1) * HEAD] = (_silu(rz[:, sl]) * (on * rnw_r[:, sl])).astype(BF16)

    vec = pl.BlockSpec((1, dc), lambda i: (0, 0))
    return pl.pallas_call(
        body, name=name,
        out_shape=(jax.ShapeDtypeStruct((t, 2 * dc), BF16), jax.ShapeDtypeStruct((t, dc), F32)), grid=(nt,),
        in_specs=[col(0), col(1), col(2), col(3), col(7), acp, acn, ahp, ahn,
                  pl.BlockSpec((tr, dc), lambda i: (i, 0)), pl.BlockSpec((tr, dc), lambda i: (i, 0)),
                  pl.BlockSpec((8, dc), lambda i: (0, 0)), vec, vec],
        out_specs=(pl.BlockSpec((tr, 2 * dc), lambda i: (i, 0)), pl.BlockSpec((tr, dc), lambda i: (i, 0))),
        compiler_params=_params("parallel"),
    )(u, u, u, u, u, u, u, u, u, o_f, o_b, cw, cnw, rnw)


def _mix_bwd(dy, u, cv, o_f, o_b, cnw, rnw, name):
    t, w = u.shape
    dc = w // 8
    nh = dc // HEAD
    tr = ROW_TILE
    col = lambda cidx: pl.BlockSpec((tr, dc), lambda i: (i, cidx))

    def body(dyc, dyr, ab, az, rz, cv_r, of_r, ob_r, cnw_r, rnw_r, dab, daz, drz, dcv, do, acc_ref):
        @pl.when(pl.program_id(0) == 0)
        def _():
            acc_ref[...] = jnp.zeros_like(acc_ref)

        z, b, c_out = az[...], ab[...], cv_r[...]
        y1 = b * c_out
        r = lax.rsqrt(jnp.mean(y1 * y1, axis=-1, keepdims=True) + EPS)
        yh = y1 * r
        yn = yh * cnw_r[...]
        dyc_v = dyc[...]
        daz[...] = (dyc_v * yn * _dsilu(z)).astype(BF16)
        dyn = dyc_v * _silu(z)
        acc_ref[0:1, :] += jnp.sum(dyn * yh, axis=0, keepdims=True)
        dyh = dyn * cnw_r[...]
        dy1 = r * (dyh - yh * jnp.mean(dyh * yh, axis=-1, keepdims=True))
        dab[...] = (dy1 * c_out).astype(BF16)
        dcv[...] = dy1 * b
        for h in range(nh):
            sl = slice(h * HEAD, (h + 1) * HEAD)
            o = of_r[:, sl] + ob_r[:, sl]
            dev = o - jnp.mean(o, axis=-1, keepdims=True)
            rs = lax.rsqrt(jnp.mean(dev * dev, axis=-1, keepdims=True) + EPS)
            on = dev * rs
            zz, dyr_v, gw = rz[:, sl], dyr[:, sl], rnw_r[:, sl]
            drz[:, sl] = (dyr_v * (on * gw) * _dsilu(zz)).astype(BF16)
            dong = dyr_v * _silu(zz)
            acc_ref[1:2, sl] += jnp.sum(dong * on, axis=0, keepdims=True)
            don = dong * gw
            do[:, sl] = rs * (don - jnp.mean(don, axis=-1, keepdims=True)
                              - on * jnp.mean(don * on, axis=-1, keepdims=True))

    vec = pl.BlockSpec((1, dc), lambda i: (0, 0))
    row = pl.BlockSpec((tr, dc), lambda i: (i, 0))
    return pl.pallas_call(
        body, name=name,
        out_shape=(jax.ShapeDtypeStruct((t, dc), BF16), jax.ShapeDtypeStruct((t, dc), BF16),
                   jax.ShapeDtypeStruct((t, dc), BF16), jax.ShapeDtypeStruct((t, dc), F32),
                   jax.ShapeDtypeStruct((t, dc), F32), jax.ShapeDtypeStruct((8, dc), F32)),
        grid=(t // tr,),
        in_specs=[col(0), col(1), col(1), col(3), col(7), row, row, row, vec, vec],
        out_specs=(row, row, row, row, row, pl.BlockSpec((8, dc), lambda i: (0, 0))),
        compiler_params=_params("arbitrary"),
    )(dy, dy, u, u, u, cv, o_f, o_b, cnw, rnw)


def _rope_masks():
    lane = lax.broadcasted_iota(jnp.int32, (1, HEAD), 1)
    return (lane % (HEAD // 2)) >= (HEAD // 4)


def _rot(a, upper):
    return jnp.where(upper, pltpu.roll(a, HEAD // 4, 1), pltpu.roll(a, HEAD - HEAD // 4, 1))


def _assemble_du(dcv, u, cw, dab, daz, drz, dq_f, dq_b, dk_f, dk_b, dv_f, dv_b, cos, sin, nct, name):
    t, w = u.shape
    dc = w // 8
    nh = dc // HEAD
    tr = ROW_TILE
    nt = t // tr
    k_scale = HEAD ** -0.5
    col = lambda cidx: pl.BlockSpec((tr, dc), lambda i: (i, cidx))
    row = pl.BlockSpec((tr, dc), lambda i: (i, 0))
    dcp, dcn = _halo_specs(tr, dc, 0, t // 8)

    def body(dcv_r, dcp_r, dcn_r, ah, ac, cw_r, dab_r, daz_r, drz_r, dqf, dqb, dkf, dkb, dvf, dvb, cos_r, sin_r,
             du, acc_ref):
        i = pl.program_id(0)

        @pl.when(i == 0)
        def _():
            acc_ref[...] = jnp.zeros_like(acc_ref)

        has_prev, has_next = _seq_edges(i, nct, nt)
        d_c = dcv_r[...]
        before = jnp.where(has_prev, dcp_r[7:8, :], 0.0)
        after = jnp.where(has_next, dcn_r[0:1, :], 0.0)
        d_prev, d_next = _shift_rows(d_c, before, after)
        a_h, a_c = ah[...], ac[...]
        ch = a_c * a_h
        dch = d_next * cw_r[0:1, :] + d_c * cw_r[1:2, :] + d_prev * cw_r[2:3, :]
        acc_ref[0:1, :] += jnp.sum(ch * d_next, axis=0, keepdims=True)
        acc_ref[1:2, :] += jnp.sum(ch * d_c, axis=0, keepdims=True)
        acc_ref[2:3, :] += jnp.sum(ch * d_prev, axis=0, keepdims=True)
        du[:, 0:dc] = (dch * a_c).astype(BF16)
        du[:, dc:2 * dc] = dab_r[...]
        du[:, 2 * dc:3 * dc] = (dch * a_h).astype(BF16)
        du[:, 3 * dc:4 * dc] = daz_r[...]
        du[:, 7 * dc:8 * dc] = drz_r[...]
        du[:, 6 * dc:7 * dc] = (dvf[...] + dvb[...]).astype(BF16)
        upper = _rope_masks()
        cs, sn = cos_r[...], sin_r[...]
        for h in range(nh):
            sl = slice(h * HEAD, (h + 1) * HEAD)
            dq = dqf[:, sl] + dqb[:, sl]
            dk = (dkf[:, sl] + dkb[:, sl]) * k_scale
            du[:, 4 * dc + h * HEAD:4 * dc + (h + 1) * HEAD] = (dq * cs + _rot(dq * sn, upper)).astype(BF16)
            du[:, 5 * dc + h * HEAD:5 * dc + (h + 1) * HEAD] = (dk * cs + _rot(dk * sn, upper)).astype(BF16)

    tab = pl.BlockSpec((tr, HEAD), lambda i: (i, 0))
    return pl.pallas_call(
        body, name=name,
        out_shape=(jax.ShapeDtypeStruct((t, w), BF16), jax.ShapeDtypeStruct((8, dc), F32)), grid=(nt,),
        in_specs=[row, dcp, dcn, col(0), col(2), pl.BlockSpec((8, dc), lambda i: (0, 0)),
                  row, row, row, row, row, row, row, row, row, tab, tab],
        out_specs=(pl.BlockSpec((tr, w), lambda i: (i, 0)), pl.BlockSpec((8, dc), lambda i: (0, 0))),
        compiler_params=_params("arbitrary"),
    )(dcv, dcv, dcv, u, u, cw, dab, daz, drz, dq_f, dq_b, dk_f, dk_b, dv_f, dv_b, cos, sin)


def _bwd_dir_chunk(s, ncc, ns):
    return jnp.where(s < ncc, ncc - 1 - s, ns + ncc - 1 - s)


def _decay_tables(dec_ref, h):
    c = CHUNK
    ii = lax.broadcasted_iota(jnp.int32, (c, c), 0)
    jj = lax.broadcasted_iota(jnp.int32, (c, c), 1)
    diff = (ii - jj).astype(F32)
    pos = lax.broadcasted_iota(jnp.int32, (c, 1), 0).astype(F32)
    out = []
    for d in range(2):
        lg = -jnp.exp(jnp.zeros((1, HEAD), F32) + dec_ref[d, h])
        lg1 = lg[:, 0:1]
        sd = diff if d == 0 else -diff
        wm = jnp.maximum(sd, 0.0)
        dm = jnp.where(sd >= 0, jnp.exp(lg * wm), 0.0)
        if d == 0:
            qw, kw = pos + 1.0, (c - 1.0) - pos
        else:
            qw, kw = c - pos, pos
        out.append(dict(lg=lg, dm=dm, wm=wm, qw=qw, kw=kw, qd=jnp.exp(lg1 * qw), kd=jnp.exp(lg1 * kw),
                        gc=jnp.exp(lg * float(c))))
    return out


def _ret_fwd(u, cos, sin, dec, ncc, name):
    t, w = u.shape
    dc = w // 8
    nh = dc // HEAD
    c = CHUNK
    ns = t // c
    k_scale = HEAD ** -0.5
    fmap = lambda g: (lambda s: (s, g))
    bmap = lambda g: (lambda s: (_bwd_dir_chunk(s, ncc, ns), g))
    blk = lambda m: pl.BlockSpec((c, dc), m)
    tab = lambda m: pl.BlockSpec((c, HEAD), m)
    st_f = pl.BlockSpec((nh, None, HEAD, HEAD), lambda s: (0, s, 0, 0))
    st_b = pl.BlockSpec((nh, None, HEAD, HEAD), lambda s: (0, _bwd_dir_chunk(s, ncc, ns), 0, 0))

    def body(qf, kf, vf, qb, kb, vb, cf, sf, cb, sb, dec_ref, of_ref, ob_ref, stf_ref, stb_ref, s_f, s_b):
        @pl.when(pl.program_id(0) == 0)
        def _():
            s_f[...] = jnp.zeros_like(s_f)
            s_b[...] = jnp.zeros_like(s_b)

        upper = _rope_masks()
        for h in range(nh):
            sl = slice(h * HEAD, (h + 1) * HEAD)
            tabs = _decay_tables(dec_ref, h)
            for d, (q_r, k_r, v_r, cs_r, sn_r, o_ref, st_ref, st) in enumerate((
                    (qf, kf, vf, cf, sf, of_ref, stf_ref, s_f), (qb, kb, vb, cb, sb, ob_ref, stb_ref, s_b))):
                tb = tabs[d]
                cs, sn = cs_r[...], sn_r[...]
                q = q_r[:, sl]
                q = q * cs + _rot(q, upper) * sn
                k = k_r[:, sl]
                k = (k * cs + _rot(k, upper) * sn) * k_scale
                vv = _bf(v_r[:, sl])
                state = st[h]
                st_ref[h] = state
                p = _nt(_bf(q), _bf(k)) * tb["dm"]
                o_ref[:, sl] = _nn(_bf(p), vv) + _nn(_bf(q * tb["qd"]), _bf(state))
                st[h] = tb["gc"] * state + _tn(_bf(k * tb["kd"]), vv)

    return pl.pallas_call(
        body, name=name,
        out_shape=(jax.ShapeDtypeStruct((t, dc), F32), jax.ShapeDtypeStruct((t, dc), F32),
                   jax.ShapeDtypeStruct((nh, ns, HEAD, HEAD), F32), jax.ShapeDtypeStruct((nh, ns, HEAD, HEAD), F32)),
        grid=(ns,),
        in_specs=[blk(fmap(4)), blk(fmap(5)), blk(fmap(6)), blk(bmap(4)), blk(bmap(5)), blk(bmap(6)),
                  tab(fmap(0)), tab(fmap(0)), tab(bmap(0)), tab(bmap(0)), SMEM_SPEC],
        out_specs=(blk(fmap(0)), blk(bmap(0)), st_f, st_b),
        scratch_shapes=[pltpu.VMEM((nh, HEAD, HEAD), F32), pltpu.VMEM((nh, HEAD, HEAD), F32)],
        compiler_params=_params("arbitrary"),
    )(u, u, u, u, u, u, cos, sin, cos, sin, dec)


def _ret_bwd(u, do, st_f, st_b, cos, sin, dec, ncc, name):
    t, w = u.shape
    dc = w // 8
    nh = dc // HEAD
    c = CHUNK
    ns = t // c
    k_scale = HEAD ** -0.5
    fchunk = lambda s: ns - 1 - s
    bchunk = lambda s: _bwd_dir_chunk(ns - 1 - s, ncc, ns)
    fmap = lambda g: (lambda s: (fchunk(s), g))
    bmap = lambda g: (lambda s: (bchunk(s), g))
    blk = lambda m: pl.BlockSpec((c, dc), m)
    tab = lambda m: pl.BlockSpec((c, HEAD), m)
    stf_spec = pl.BlockSpec((nh, None, HEAD, HEAD), lambda s: (0, fchunk(s), 0, 0))
    stb_spec = pl.BlockSpec((nh, None, HEAD, HEAD), lambda s: (0, bchunk(s), 0, 0))

    def body(qf, kf, vf, dof, qb, kb, vb, dob, stf_ref, stb_ref, cf, sf, cb, sb, dec_ref,
             dqf, dkf, dvf, dqb, dkb, dvb, dlg_ref, ds_f, ds_b):
        step = pl.program_id(0)

        @pl.when(step == 0)
        def _():
            ds_f[...] = jnp.zeros_like(ds_f)
            ds_b[...] = jnp.zeros_like(ds_b)
            dlg_ref[...] = jnp.zeros_like(dlg_ref)

        upper = _rope_masks()
        for h in range(nh):
            sl = slice(h * HEAD, (h + 1) * HEAD)
            tabs = _decay_tables(dec_ref, h)
            for d, (q_r, k_r, v_r, do_r, st_ref, cs_r, sn_r, dq_ref, dk_ref, dv_ref, ds) in enumerate((
                    (qf, kf, vf, dof, stf_ref, cf, sf, dqf, dkf, dvf, ds_f),
                    (qb, kb, vb, dob, stb_ref, cb, sb, dqb, dkb, dvb, ds_b))):
                tb = tabs[d]
                cs, sn = cs_r[...], sn_r[...]
                q = q_r[:, sl]
                q = q * cs + _rot(q, upper) * sn
                k = k_r[:, sl]
                k = (k * cs + _rot(k, upper) * sn) * k_scale
                qb16, kb16, vb16, dob16 = _bf(q), _bf(k), _bf(v_r[:, sl]), _bf(do_r[:, sl])
                state, dstate = st_ref[h], ds[h]
                sb16, dsb16 = _bf(state), _bf(dstate)
                a = _nt(qb16, kb16) * tb["dm"]
                da = _nt(dob16, vb16)
                dp = _bf(da * tb["dm"])
                dq_i = _nt(dob16, sb16) * tb["qd"]
                dk_i = _nt(vb16, dsb16) * tb["kd"]
                dq_ref[:, sl] = _nn(dp, kb16) + dq_i
                dk_ref[:, sl] = _tn(dp, qb16) + dk_i
                dv_ref[:, sl] = _tn(_bf(a), dob16) + _nn(_bf(k * tb["kd"]), dsb16)
                ds[h] = tb["gc"] * dstate + _tn(_bf(q * tb["qd"]), dob16)
                dlg = (jnp.sum(da * a * tb["wm"])
                       + jnp.sum(tb["qw"] * jnp.sum(dq_i * q, axis=-1, keepdims=True))
                       + jnp.sum(tb["kw"] * jnp.sum(dk_i * k, axis=-1, keepdims=True)))
                dlg_row = jnp.sum(tb["gc"] * state * dstate, axis=0, keepdims=True)
                dlg_ref[d * nh + h:d * nh + h + 1, :] += (
                    jnp.zeros((1, HEAD), F32) + dlg + float(c) * jnp.sum(dlg_row))

        @pl.when(step == ns - 1)
        def _():
            for h in range(nh):
                for d in range(2):
                    lg = -jnp.exp(jnp.zeros((1, HEAD), F32) + dec_ref[d, h])
                    r = d * nh + h
                    dlg_ref[r:r + 1, :] = dlg_ref[r:r + 1, :] * lg

    dshape = jax.ShapeDtypeStruct((t, dc), F32)
    return pl.pallas_call(
        body, name=name,
        out_shape=(dshape, dshape, dshape, dshape, dshape, dshape, jax.ShapeDtypeStruct((2 * nh, HEAD), F32)),
        grid=(ns,),
        in_specs=[blk(fmap(4)), blk(fmap(5)), blk(fmap(6)), blk(fmap(0)),
                  blk(bmap(4)), blk(bmap(5)), blk(bmap(6)), blk(bmap(0)), stf_spec, stb_spec,
                  tab(fmap(0)), tab(fmap(0)), tab(bmap(0)), tab(bmap(0)), SMEM_SPEC],
        out_specs=(blk(fmap(0)), blk(fmap(0)), blk(fmap(0)), blk(bmap(0)), blk(bmap(0)), blk(bmap(0)),
                   pl.BlockSpec((2 * nh, HEAD), lambda s: (0, 0))),
        scratch_shapes=[pltpu.VMEM((nh, HEAD, HEAD), F32), pltpu.VMEM((nh, HEAD, HEAD), F32)],
        compiler_params=_params("arbitrary"),
    )(u, u, u, do, u, u, u, do, st_f, st_b, cos, sin, cos, sin, dec)


def _mod_matvec(c16, w_mod, b_mod):
    nl, d, nm = w_mod.shape
    tn = _pick(nm, 512, 128)

    def body(c_ref, w_ref, b_ref, sc_ref, o_ref):
        sc = _silu(c_ref[...])
        sc_ref[...] = sc
        o_ref[...] = _nn(_bf(sc), _bf(w_ref[...])) + b_ref[...]

    return pl.pallas_call(
        body, name="mod_matvec",
        out_shape=(jax.ShapeDtypeStruct((16, d), F32), jax.ShapeDtypeStruct((nl, 16, nm), F32)),
        grid=(nl, nm // tn),
        in_specs=[pl.BlockSpec((16, d), lambda l, j: (0, 0)), pl.BlockSpec((None, d, tn), lambda l, j: (l, 0, j)),
                  pl.BlockSpec((None, 1, tn), lambda l, j: (l, 0, j))],
        out_specs=(pl.BlockSpec((16, d), lambda l, j: (0, 0)), pl.BlockSpec((None, 16, tn), lambda l, j: (l, 0, j))),
        compiler_params=_params("arbitrary", "arbitrary"),
    )(c16, w_mod, b_mod)


def _wmod_grad(sc16, dm, w_mod):
    nl, d, nm = w_mod.shape
    tn = _pick(nm, 512, 128)

    def body(sc_ref, dm_ref, w_ref, gw_ref, pc_ref):
        @pl.when(jnp.logical_and(pl.program_id(0) == 0, pl.program_id(1) == 0))
        def _():
            pc_ref[...] = jnp.zeros_like(pc_ref)

        dmb = _bf(dm_ref[...])
        gw_ref[...] = _tn(_bf(sc_ref[...]), dmb)
        pc_ref[...] += _nt(dmb, _bf(w_ref[...]))

    return pl.pallas_call(
        body, name="wmod_grad",
        out_shape=(jax.ShapeDtypeStruct((nl, d, nm), F32), jax.ShapeDtypeStruct((16, d), F32)),
        grid=(nl, nm // tn),
        in_specs=[pl.BlockSpec((16, d), lambda l, j: (0, 0)), pl.BlockSpec((None, 16, tn), lambda l, j: (l, 0, j)),
                  pl.BlockSpec((None, d, tn), lambda l, j: (l, 0, j))],
        out_specs=(pl.BlockSpec((None, d, tn), lambda l, j: (l, 0, j)), pl.BlockSpec((16, d), lambda l, j: (0, 0))),
        compiler_params=_params("arbitrary", "arbitrary"),
    )(sc16, dm, w_mod)


def _rowsum(a, name):
    r, n = a.shape
    tn = _pick(n, 4096, 128)

    def body(a_ref, o_ref):
        acc = a_ref[0:1, :]
        for i in range(1, r):
            acc = acc + a_ref[i:i + 1, :]
        o_ref[...] = jnp.zeros((8, tn), F32) + acc

    return pl.pallas_call(
        body, name=name, out_shape=jax.ShapeDtypeStruct((8, n), F32), grid=(n // tn,),
        in_specs=[pl.BlockSpec((r, tn), lambda j: (0, j))], out_specs=pl.BlockSpec((8, tn), lambda j: (0, j)),
        compiler_params=_params("parallel"),
    )(a)


def _cctx_grad(parts, c_ctx8):
    _, _, d = parts.shape

    def body(p_ref, c_ref, o_ref):
        acc = ((p_ref[0] + p_ref[1]) + p_ref[2]) + p_ref[3]
        o_ref[...] = acc * _dsilu(c_ref[...])

    return pl.pallas_call(
        body, name="cctx_grad", out_shape=jax.ShapeDtypeStruct((8, d), F32),
        in_specs=[VMEM_SPEC, VMEM_SPEC], out_specs=VMEM_SPEC,
    )(parts, c_ctx8)


def _adamw(g, w, m, v, name):
    r, n = w.shape
    tr = _pick(r, max(8, min(256, (512 * 1024) // n // 8 * 8)), 8)
    bc1 = 1.0 - ADAM_B1 ** ADAM_STEP
    bc2 = 1.0 - ADAM_B2 ** ADAM_STEP

    def body(g_ref, w_ref, m_ref, v_ref, d_ref, mo_ref, vo_ref):
        gg = g_ref[...]
        mn = ADAM_B1 * m_ref[...] + (1.0 - ADAM_B1) * gg
        vn = ADAM_B2 * v_ref[...] + (1.0 - ADAM_B2) * (gg * gg)
        mo_ref[...] = mn
        vo_ref[...] = vn
        d_ref[...] = -ADAM_LR * ((mn / bc1) / (jnp.sqrt(vn / bc2) + ADAM_EPS) + ADAM_WD * w_ref[...])

    spec = pl.BlockSpec((tr, n), lambda i: (i, 0))
    shp = jax.ShapeDtypeStruct((r, n), F32)
    return pl.pallas_call(
        body, name=name, out_shape=(shp, shp, shp), grid=(r // tr,),
        in_specs=[spec, spec, spec, spec], out_specs=(spec, spec, spec),
        compiler_params=_params("parallel"),
    )(g, w, m, v)


def _rope_tables(seq, n_ctx):
    f = HEAD // 4
    inv = ROPE_BASE ** (-jnp.arange(f, dtype=F32) / f)
    pos = jnp.arange(seq)
    ang_r = (pos // GRID_W).astype(F32)[:, None] * inv[None, :]
    ang_c = (pos % GRID_W).astype(F32)[:, None] * inv[None, :]
    cos = jnp.concatenate([jnp.cos(ang_r)] * 2 + [jnp.cos(ang_c)] * 2, axis=-1)
    sin = jnp.concatenate([-jnp.sin(ang_r), jnp.sin(ang_r), -jnp.sin(ang_c), jnp.sin(ang_c)], axis=-1)
    cos = jnp.concatenate([jnp.ones((n_ctx, HEAD), F32), cos], axis=0)
    sin = jnp.concatenate([jnp.zeros((n_ctx, HEAD), F32), sin], axis=0)
    return cos, sin


def _pad_rows(a, rows):
    return jnp.concatenate([a, jnp.zeros((rows - a.shape[0],) + a.shape[1:], a.dtype)], axis=0)


def _pad_flat(parts, mult):
    flat = jnp.concatenate([p.reshape(-1) for p in parts])
    pad = (-flat.shape[0]) % mult
    return jnp.concatenate([flat, jnp.zeros((pad,), flat.dtype)])


def kernel(x, c, ctx, c_ctx, norm_w, w_mod, b_mod, w_in, conv_w, conv_norm_w, ret_norm_w, ret_decay_f, ret_decay_b, w_out, final_norm_w, loss_target, m_c_ctx, m_norm_w, m_w_mod, m_b_mod, m_w_in, m_conv_w, m_conv_norm_w, m_ret_norm_w, m_ret_decay_f, m_ret_decay_b, m_w_out, m_final_norm_w, v_c_ctx, v_norm_w, v_w_mod, v_b_mod, v_w_in, v_conv_w, v_conv_norm_w, v_ret_norm_w, v_ret_decay_f, v_ret_decay_b, v_w_out, v_final_norm_w):
    xi, yi, ci = _mesh_pos()
    b_idx = 4 * xi + 2 * yi + ci
    s_idx = 2 * xi + yi
    pos = jnp.stack([ci, s_idx]).astype(jnp.int32)

    x2, ctx2, tgt = x[0], ctx[0], loss_target[0]
    seq, d = x2.shape
    n_ctx = ctx2.shape[0]
    t = seq + n_ctx
    dc = d // 2
    nh = dc // HEAD
    nl, _, nm = w_mod.shape
    n4 = w_in.shape[2]
    r4 = w_out.shape[1]
    cw4 = conv_w.shape[2]
    nct = n_ctx // ROW_TILE
    ncc = n_ctx // CHUNK
    assert n_ctx % ROW_TILE == 0 and seq % ROW_TILE == 0 and dc % HEAD == 0 and nl * 3 <= 8

    win_g, wout_g = _gather_weights(w_in.astype(BF16), w_out.astype(BF16))
    small = jnp.concatenate([jnp.broadcast_to(c, (8, d)), _pad_rows(conv_w.reshape(nl * 3, cw4), 8)], axis=1)
    small_g = _allgather_small(small, "gather_c_convw").reshape(N_DEV, 8, d + cw4)
    c_all = small_g[:, 0, :d]
    cw_full = small_g[0::2, :, d:].transpose(1, 0, 2).reshape(8, 4 * cw4)
    cw_l = [_pad_rows(cw_full[l * 3:(l + 1) * 3], 8) for l in range(nl)]

    c16 = _pad_rows(jnp.concatenate([c_all, c_ctx[None]], axis=0), 16)
    b_shard = lax.dynamic_slice(b_mod, (0, s_idx * nm), (nl, nm)).reshape(nl, 1, nm)
    sc16, modp = _mod_matvec(c16, w_mod, b_shard)
    modg = _allgather_small(modp.reshape(nl * 16, nm), "gather_mod").reshape(N_DEV, nl, 16, nm)
    mod_full = modg[0::2].transpose(1, 2, 0, 3).reshape(nl, 16, 4 * nm)
    mod_me = lax.dynamic_index_in_dim(mod_full, b_idx, axis=1, keepdims=False)
    mod_cx = mod_full[:, 8]
    zrow = jnp.zeros((d,), F32)
    rows = []
    for l in range(nl):
        gate_c = mod_cx[l, 2 * d:] if l < nl - 1 else zrow
        rows.append(jnp.stack([mod_me[l, :d], mod_me[l, d:2 * d], mod_me[l, 2 * d:],
                               mod_cx[l, :d], mod_cx[l, d:2 * d], gate_c, zrow, zrow]))

    cos, sin = _rope_tables(seq, n_ctx)
    xs = jnp.concatenate([ctx2, x2], axis=0)
    saved = []
    for l in range(nl):
        dec = jnp.stack([ret_decay_f[l], ret_decay_b[l]])
        hx = _modulate(xs, norm_w[l][None], rows[l], nct, f"modulate_{l}")
        u = _matmul(hx, win_g[l], "nn", f"in_proj_{l}")
        o_f, o_b, st_f, st_b = _ret_fwd(u, cos, sin, dec, ncc, f"ret_fwd_{l}")
        y, cv = _mix_fwd(u, o_f, o_b, cw_l[l], conv_norm_w[l][None], ret_norm_w[l][None], nct, f"mix_fwd_{l}")
        xs_new, proj = _out_proj(y, wout_g[l], xs, rows[l], n_ctx, f"out_proj_{l}")
        saved.append((xs, hx, u, o_f, o_b, st_f, st_b, y, cv, proj, dec))
        xs = xs_new

    g, facc = _final_loss(xs, tgt, final_norm_w[None], nct, "final_loss")
    gin_l, gout_l = [None] * nl, [None] * nl
    dmx, dmc, gnw, gcnw, grnw, gcw, gdf, gdb = ([None] * nl for _ in range(8))
    for l in reversed(range(nl)):
        xs_l, hx, u, o_f, o_b, st_f, st_b, y, cv, proj, dec = saved[l]
        gp, gacc = _gate_bwd(g, proj, rows[l], nct, f"gate_bwd_{l}")
        dy = _matmul(gp, wout_g[l], "nt", f"out_proj_dx_{l}")
        gout_l[l] = _matmul(y, gp, "tn", f"out_proj_dw_{l}", tm_t=1024, tn_t=1024, tk_t=1056)
        dab, daz, drz, dcv, do, macc = _mix_bwd(dy, u, cv, o_f, o_b, conv_norm_w[l][None], ret_norm_w[l][None],
                                                f"mix_bwd_{l}")
        dqf, dkf, dvf, dqb, dkb, dvb, dlg = _ret_bwd(u, do, st_f, st_b, cos, sin, dec, ncc, f"ret_bwd_{l}")
        du, cacc = _assemble_du(dcv, u, cw_l[l], dab, daz, drz, dqf, dqb, dkf, dkb, dvf, dvb, cos, sin, nct,
                                f"assemble_du_{l}")
        dhx = _matmul(du, win_g[l], "nt", f"in_proj_dx_{l}")
        gin_l[l] = _matmul(hx, du, "tn", f"in_proj_dw_{l}", tm_t=1024, tn_t=1024, tk_t=1056)
        g, nacc = _modulate_bwd(dhx, xs_l, g, norm_w[l][None], rows[l], nct, f"modulate_bwd_{l}")
        dmx[l] = jnp.concatenate([nacc[0], nacc[1], gacc[2]])
        dmc[l] = jnp.concatenate([nacc[3], nacc[4], gacc[5]])
        gnw[l], gcnw[l], grnw[l], gcw[l] = nacc[6], macc[0], macc[1], cacc[0:3]
        gdf[l], gdb[l] = dlg[0:nh, 0], dlg[nh:2 * nh, 0]
    grad_x = g[n_ctx:][None]

    gin = jnp.stack(gin_l).reshape(nl, 2, d // 2, 4 * n4)
    gout = jnp.stack(gout_l).reshape(nl, 4, 2, r4 // 2, d)
    land_in, land_out = _swap_halves(gin, gout)
    p_in = _sum_half_in(pos, gin, land_in)
    p_out = _sum_half_out(pos, gout, land_out)
    land3_in, land3_out = _exchange_quarters(p_in, p_out, n4)
    q_in = _sum_quarters_in(pos, p_in, land3_in, n4)
    q_out = _sum_quarters_out(pos, p_out, land3_out)
    full_in, full_out = _join_halves(q_in, q_out)
    g_w_in = full_in.reshape(nl * d, n4)
    g_w_out = full_out.reshape(nl * r4, d)
    d_w_in, nm_w_in, nv_w_in = _adamw(g_w_in, w_in.reshape(nl * d, n4), m_w_in.reshape(nl * d, n4),
                                      v_w_in.reshape(nl * d, n4), "adamw_w_in")
    d_w_out, nm_w_out, nv_w_out = _adamw(g_w_out, w_out.reshape(nl * r4, d), m_w_out.reshape(nl * r4, d),
                                         v_w_out.reshape(nl * r4, d), "adamw_w_out")

    seg = [jnp.stack(dmx), jnp.stack(dmc), jnp.stack(gnw), jnp.stack(gcnw), jnp.stack(grnw), facc[0],
           jnp.stack(gcw), jnp.stack(gdf), jnp.stack(gdb), facc[1, 0:1]]
    sizes = [int(s.size) for s in seg]
    offs = [sum(sizes[:i]) for i in range(len(sizes))]
    packed = _pad_flat(seg, 8 * 128)
    pw = packed.shape[0] // 8
    allv = _allgather_small(packed.reshape(8, pw), "gather_small_grads").reshape(N_DEV, 8 * pw)
    n_mod = nl * 3 * d
    dmx_all, dmc_all = allv[:, offs[0]:offs[0] + n_mod], allv[:, offs[1]:offs[1] + n_mod]
    rest_all = allv[:, offs[2]:]
    zeros_mod = jnp.zeros_like(dmc_all)
    stack = jnp.concatenate([
        jnp.concatenate([dmx_all, dmc_all], axis=0),
        jnp.concatenate([dmc_all, zeros_mod], axis=0),
        jnp.concatenate([rest_all, jnp.zeros_like(rest_all)], axis=0)], axis=1)
    sums = _rowsum(stack, "sum_small_grads")[0]
    g_b_mod = sums[:n_mod].reshape(nl, 3 * d)
    dmc_tot = sums[n_mod:2 * n_mod].reshape(nl, 3 * d)
    rest = sums[2 * n_mod:]

    def take(i):
        o = offs[i] - offs[2]
        return rest[o:o + sizes[i]]

    g_norm_w = take(2).reshape(nl, d)
    g_conv_norm_w = take(3).reshape(nl, dc)
    g_ret_norm_w = take(4).reshape(nl, dc)
    g_final_norm_w = take(5)
    g_conv_w = lax.dynamic_slice(take(6).reshape(nl, 3, dc), (0, 0, s_idx * cw4), (nl, 3, cw4))
    g_decay_f = take(7).reshape(nl, nh)
    g_decay_b = take(8).reshape(nl, nh)
    loss = take(9)[0]

    dmx_mine = lax.dynamic_slice(dmx_all.reshape(N_DEV, nl, 3 * d), (0, 0, s_idx * nm), (N_DEV, nl, nm))
    dmc_mine = lax.dynamic_slice(dmc_tot, (0, s_idx * nm), (nl, nm))
    dm = jnp.concatenate([dmx_mine.transpose(1, 0, 2), dmc_mine[:, None, :], jnp.zeros((nl, 7, nm), F32)], axis=1)
    g_w_mod, pc = _wmod_grad(sc16, dm, w_mod)
    pc_g = _allgather_small(pc[8:16], "gather_cctx_partials").reshape(N_DEV, 8, d)
    g_c_ctx = _cctx_grad(pc_g[0::2], jnp.broadcast_to(c_ctx[None], (8, d)))[0]
    d_w_mod, nm_w_mod, nv_w_mod = _adamw(g_w_mod.reshape(nl * d, nm), w_mod.reshape(nl * d, nm),
                                         m_w_mod.reshape(nl * d, nm), v_w_mod.reshape(nl * d, nm), "adamw_w_mod")

    small_g = [g_c_ctx, g_norm_w, g_b_mod, g_conv_w, g_conv_norm_w, g_ret_norm_w, g_decay_f, g_decay_b, g_final_norm_w]
    small_w = [c_ctx, norm_w, b_mod, conv_w, conv_norm_w, ret_norm_w, ret_decay_f, ret_decay_b, final_norm_w]
    small_m = [m_c_ctx, m_norm_w, m_b_mod, m_conv_w, m_conv_norm_w, m_ret_norm_w, m_ret_decay_f, m_ret_decay_b,
               m_final_norm_w]
    small_v = [v_c_ctx, v_norm_w, v_b_mod, v_conv_w, v_conv_norm_w, v_ret_norm_w, v_ret_decay_f, v_ret_decay_b,
               v_final_norm_w]
    pk = lambda parts: _pad_flat(parts, 8 * 128).reshape(-1, 128)
    sd, sm, sv = _adamw(pk(small_g), pk(small_w), pk(small_m), pk(small_v), "adamw_small")

    def unpack(flat2d):
        flat = flat2d.reshape(-1)
        out, o = [], 0
        for wgt in small_w:
            out.append(flat[o:o + wgt.size].reshape(wgt.shape))
            o += wgt.size
        return out

    sd, sm, sv = unpack(sd), unpack(sm), unpack(sv)

    def order(small, mod, w_in_, w_out_):
        return [small[0], small[1], mod, small[2], w_in_, small[3], small[4], small[5], small[6], small[7], w_out_,
                small[8]]

    grads = order(small_g, g_w_mod, g_w_in.reshape(nl, d, n4), g_w_out.reshape(nl, r4, d))
    deltas = order(sd, d_w_mod.reshape(nl, d, nm), d_w_in.reshape(nl, d, n4), d_w_out.reshape(nl, r4, d))
    new_m = order(sm, nm_w_mod.reshape(nl, d, nm), nm_w_in.reshape(nl, d, n4), nm_w_out.reshape(nl, r4, d))
    new_v = order(sv, nv_w_mod.reshape(nl, d, nm), nv_w_in.reshape(nl, d, n4), nv_w_out.reshape(nl, r4, d))
    return (loss, grad_x, *grads, *deltas, *new_m, *new_v)
```

```python
import functools

import jax
import jax.numpy as jnp
from jax import lax
from jax.experimental import pallas as pl
from jax.experimental.pallas import tpu as pltpu

F32 = jnp.float32
BF16 = jnp.bfloat16
EPS = 1e-6
CHUNK = 128
HEAD = 128
GRID_W = 64
ROPE_BASE = 10000.0
ROW_TILE = 256
VMEM_LIMIT_BYTES = 56 * 1024 * 1024
N_DEV = 8
MESH = pl.DeviceIdType.MESH

ADAM_LR = 0.001
ADAM_B1 = 0.9
ADAM_B2 = 0.999
ADAM_EPS = 1e-08
ADAM_WD = 0.01
ADAM_STEP = 10

ANY = pl.BlockSpec(memory_space=pl.ANY)
VMEM_SPEC = pl.BlockSpec(memory_space=pltpu.VMEM)
SMEM_SPEC = pl.BlockSpec(memory_space=pltpu.SMEM)


def _params(*sem):
    return pltpu.CompilerParams(dimension_semantics=sem, vmem_limit_bytes=VMEM_LIMIT_BYTES)


def _pick(n, target, mult):
    best = None
    for d in range(mult, min(n, target) + 1, mult):
        if n % d == 0:
            best = d
    assert best is not None, (n, target, mult)
    return best


def _silu(z):
    return z * jax.nn.sigmoid(z)


def _dsilu(z):
    s = jax.nn.sigmoid(z)
    return s * (1.0 + z * (1.0 - s))


def _nn(a, b):
    return lax.dot_general(a, b, (((1,), (0,)), ((), ())), preferred_element_type=F32)


def _nt(a, b):
    return lax.dot_general(a, b, (((1,), (1,)), ((), ())), preferred_element_type=F32)


def _tn(a, b):
    return lax.dot_general(a, b, (((0,), (0,)), ((), ())), preferred_element_type=F32)


def _bf(a):
    return a.astype(BF16)


def _mesh_pos():
    return lax.axis_index("x"), lax.axis_index("y"), lax.axis_index("c")


def _allgather_small(blk, name):
    m_per, n = blk.shape

    def body(x_ref, out_ref, send_sems, recv_sems, local_sem):
        x, y, c = _mesh_pos()
        me, sibling = (x, y, c), (x, y, 1 - c)
        chips = [(1 - x, y), (x, 1 - y), (1 - x, 1 - y)]

        def rows(px, py, pc):
            return out_ref.at[pl.ds((4 * px + 2 * py + pc) * m_per, m_per), :]

        def copy(k, block, to, src=None):
            return pltpu.make_async_remote_copy(
                src_ref=rows(*block) if src is None else src, dst_ref=rows(*block),
                send_sem=send_sems.at[k], recv_sem=recv_sems.at[k], device_id=to, device_id_type=MESH)

        mine = pltpu.make_async_copy(x_ref, rows(*me), local_sem)
        mine.start()
        first = [copy(0, me, sibling, src=x_ref)]
        first += [copy(1 + j, me, (*chip, c), src=x_ref) for j, chip in enumerate(chips)]
        for cp in first:
            cp.start()
        passed = [copy(4 + j, (*chip, c), sibling) for j, chip in enumerate(chips)]
        for j, chip in enumerate(chips):
            copy(1 + j, (*chip, c), me).wait_recv()
            passed[j].start()
        copy(0, sibling, me).wait_recv()
        for j, chip in enumerate(chips):
            copy(4 + j, (*chip, 1 - c), me).wait_recv()
        for cp in first + passed:
            cp.wait_send()
        mine.wait()

    return pl.pallas_call(
        body, name=name,
        out_shape=jax.ShapeDtypeStruct((N_DEV * m_per, n), blk.dtype),
        in_specs=[VMEM_SPEC], out_specs=VMEM_SPEC,
        scratch_shapes=[pltpu.SemaphoreType.DMA((7,)), pltpu.SemaphoreType.DMA((7,)), pltpu.SemaphoreType.DMA],
    )(blk)


class _Comm:
    def __init__(self, ins, outs, n_remote, n_local, start, finish, aliases=None):
        self.ins, self.outs, self.n_remote, self.n_local = list(ins), list(outs), n_remote, n_local
        self.start, self.finish, self.aliases = start, finish, dict(aliases or {})

    def scratch(self):
        return [pltpu.SemaphoreType.DMA((self.n_remote,)), pltpu.SemaphoreType.DMA((self.n_remote,)),
                pltpu.SemaphoreType.DMA((max(self.n_local, 1),))]


def _run_comm(comm, name):
    n_in, n_out = len(comm.ins), len(comm.outs)

    def body(*refs):
        args = (refs[:n_in], refs[n_in:n_in + n_out]) + tuple(refs[n_in + n_out:])
        comm.start(*args)
        comm.finish(*args)

    return pl.pallas_call(
        body, name=name, out_shape=tuple(comm.outs), in_specs=[ANY] * n_in, out_specs=tuple([ANY] * n_out),
        scratch_shapes=comm.scratch(), input_output_aliases=comm.aliases,
    )(*comm.ins)


def _remote(src, dst, send, recv, k, to):
    return pltpu.make_async_remote_copy(src_ref=src, dst_ref=dst, send_sem=send.at[k], recv_sem=recv.at[k],
                                        device_id=to, device_id_type=MESH)


def _gather_comm(win_s, wout_s):
    d, n4 = win_s.shape
    r4 = wout_s.shape[0]
    hi, ho = d // 2, r4 // 2

    def build(ins, outs, send, recv, loc):
        x, y, c = _mesh_pos()
        sib = (x, y, 1 - c)
        peers = [(1 - x, y), (x, 1 - y), (1 - x, 1 - y)]
        s_me = 2 * x + y

        def slot(a, s, h):
            if a == 0:
                return outs[0].at[pl.ds(pl.multiple_of(h * hi, 16), hi), pl.ds(pl.multiple_of(s * n4, 128), n4)]
            return outs[1].at[pl.ds(pl.multiple_of(s * r4 + h * ho, 16), ho), :]

        def half(a, h):
            return ins[a].at[pl.ds(pl.multiple_of(h * (hi, ho)[a], 16), (hi, ho)[a]), :]

        def local():
            return [pltpu.make_async_copy(ins[0], outs[0].at[:, pl.ds(pl.multiple_of(s_me * n4, 128), n4)], loc.at[0]),
                    pltpu.make_async_copy(ins[1], outs[1].at[pl.ds(pl.multiple_of(s_me * r4, 16), r4), :], loc.at[1])]

        def per_copy(make):
            return [make(j, a, 2 * px + py, (px, py, c)) for j, (px, py) in enumerate(peers) for a in range(2)]

        ici_send = lambda: per_copy(lambda j, a, s_p, to: _remote(half(a, c), slot(a, s_me, c), send, recv, j * 2 + a, to))
        ici_recv = lambda: per_copy(lambda j, a, s_p, to: _remote(half(a, c), slot(a, s_p, c), send, recv, j * 2 + a, to))
        d2d_send = lambda: per_copy(
            lambda j, a, s_p, to: _remote(slot(a, s_p, c), slot(a, s_p, c), send, recv, 6 + j * 2 + a, sib))
        d2d_recv = lambda: per_copy(
            lambda j, a, s_p, to: _remote(slot(a, s_p, c), slot(a, s_p, 1 - c), send, recv, 6 + j * 2 + a, sib))
        return local, ici_send, ici_recv, d2d_send, d2d_recv

    def start(*refs):
        local, ici_send, _, _, _ = build(*refs)
        for cp in local() + ici_send():
            cp.start()

    def finish(*refs):
        local, ici_send, ici_recv, d2d_send, d2d_recv = build(*refs)
        forwards = d2d_send()
        for arrived, forward in zip(ici_recv(), forwards):
            arrived.wait_recv()
            forward.start()
        for cp in d2d_recv():
            cp.wait_recv()
        for cp in ici_send() + forwards:
            cp.wait_send()
        for cp in local():
            cp.wait()

    outs = [jax.ShapeDtypeStruct((d, 4 * n4), BF16), jax.ShapeDtypeStruct((4 * r4, d), BF16)]
    return _Comm([win_s, wout_s], outs, 12, 2, start, finish)


def _swap_comm(gin_v, gout_v):
    _, dh, w = gin_v.shape
    ns, _, rh, d = gout_v.shape

    def build(ins, outs, send, recv, loc):
        x, y, c = _mesh_pos()
        sib = (x, y, 1 - c)
        return [_remote(ins[0].at[1 - c], outs[0], send, recv, 0, sib),
                _remote(ins[1].at[:, 1 - c], outs[1], send, recv, 1, sib)]

    def start(*refs):
        for cp in build(*refs):
            cp.start()

    def finish(*refs):
        for cp in build(*refs):
            cp.wait()

    outs = [jax.ShapeDtypeStruct((dh, w), F32), jax.ShapeDtypeStruct((ns, rh, d), F32)]
    return _Comm([gin_v, gout_v], outs, 2, 0, start, finish)


def _exchange_comm(p_in, p_out, n4):
    dh, _ = p_in.shape
    _, rh, d = p_out.shape

    def build(ins, outs, send, recv, loc):
        x, y, c = _mesh_pos()
        cps = []
        for j, (px, py) in enumerate([(1 - x, y), (x, 1 - y), (1 - x, 1 - y)]):
            s_p = 2 * px + py
            cps.append(_remote(ins[0].at[:, pl.ds(pl.multiple_of(s_p * n4, 128), n4)], outs[0].at[j], send, recv,
                               2 * j, (px, py, c)))
            cps.append(_remote(ins[1].at[s_p], outs[1].at[j], send, recv, 2 * j + 1, (px, py, c)))
        return cps

    def start(*refs):
        for cp in build(*refs):
            cp.start()

    def finish(*refs):
        cps = build(*refs)
        for cp in cps:
            cp.wait_recv()
        for cp in cps:
            cp.wait_send()

    outs = [jax.ShapeDtypeStruct((3, dh, n4), p_in.dtype), jax.ShapeDtypeStruct((3, rh, d), p_out.dtype)]
    return _Comm([p_in, p_out], outs, 6, 0, start, finish)


def _join_comm(q_in, q_out):
    def build(ins, outs, send, recv, loc):
        x, y, c = _mesh_pos()
        sib = (x, y, 1 - c)
        sends = lambda: [_remote(outs[a].at[c], outs[a].at[c], send, recv, a, sib) for a in range(2)]
        recvs = lambda: [_remote(outs[a].at[c], outs[a].at[1 - c], send, recv, a, sib) for a in range(2)]
        return sends, recvs

    def start(*refs):
        for cp in build(*refs)[0]():
            cp.start()

    def finish(*refs):
        sends, recvs = build(*refs)
        for cp in recvs():
            cp.wait_recv()
        for cp in sends():
            cp.wait_send()

    outs = [jax.ShapeDtypeStruct(q_in.shape, F32), jax.ShapeDtypeStruct(q_out.shape, F32)]
    return _Comm([q_in, q_out], outs, 2, 0, start, finish, aliases={0: 0, 1: 1})


def _sum_half_in(pos, gin_v, land, name):
    _, dh, w = gin_v.shape
    tr, tc = _pick(dh, 256, 16), _pick(w, 2048, 128)

    def body(pos_ref, a_ref, b_ref, o_ref):
        o_ref[...] = (a_ref[...] + b_ref[...]).astype(BF16)

    return pl.pallas_call(
        body, name=name,
        out_shape=jax.ShapeDtypeStruct((dh, w), BF16),
        grid_spec=pltpu.PrefetchScalarGridSpec(
            num_scalar_prefetch=1, grid=(dh // tr, w // tc),
            in_specs=[pl.BlockSpec((None, tr, tc), lambda i, j, p: (p[0], i, j)),
                      pl.BlockSpec((tr, tc), lambda i, j, p: (i, j))],
            out_specs=pl.BlockSpec((tr, tc), lambda i, j, p: (i, j))),
        compiler_params=_params("parallel", "parallel"),
    )(pos, gin_v, land)


def _sum_half_out(pos, gout_v, land, name):
    ns, _, rh, d = gout_v.shape

    def body(pos_ref, a_ref, b_ref, o_ref):
        o_ref[...] = (a_ref[...] + b_ref[...]).astype(BF16)

    return pl.pallas_call(
        body, name=name,
        out_shape=jax.ShapeDtypeStruct((ns, rh, d), BF16),
        grid_spec=pltpu.PrefetchScalarGridSpec(
            num_scalar_prefetch=1, grid=(ns,),
            in_specs=[pl.BlockSpec((None, None, rh, d), lambda s, p: (s, p[0], 0, 0)),
                      pl.BlockSpec((None, rh, d), lambda s, p: (s, 0, 0))],
            out_specs=pl.BlockSpec((None, rh, d), lambda s, p: (s, 0, 0))),
        compiler_params=_params("parallel"),
    )(pos, gout_v, land)


def _sum_quarters_in(pos, p_in, land3, n4, name):
    dh, _ = p_in.shape
    tr = _pick(dh, 128, 16)

    def body(pos_ref, a_ref, b_ref, o_ref):
        f = lambda v: v.astype(F32)
        o_ref[...] = ((f(a_ref[...]) + f(b_ref[0])) + f(b_ref[1])) + f(b_ref[2])

    return pl.pallas_call(
        body, name=name,
        out_shape=jax.ShapeDtypeStruct((2, dh, n4), F32),
        grid_spec=pltpu.PrefetchScalarGridSpec(
            num_scalar_prefetch=1, grid=(dh // tr,),
            in_specs=[pl.BlockSpec((tr, n4), lambda i, p: (i, p[1])),
                      pl.BlockSpec((3, tr, n4), lambda i, p: (0, i, 0))],
            out_specs=pl.BlockSpec((None, tr, n4), lambda i, p: (p[0], i, 0))),
        compiler_params=_params("parallel"),
    )(pos, p_in, land3)


def _sum_quarters_out(pos, p_out, land3, name):
    _, rh, d = p_out.shape

    def body(pos_ref, a_ref, b_ref, o_ref):
        f = lambda v: v.astype(F32)
        o_ref[...] = ((f(a_ref[...]) + f(b_ref[0])) + f(b_ref[1])) + f(b_ref[2])

    return pl.pallas_call(
        body, name=name,
        out_shape=jax.ShapeDtypeStruct((2, rh, d), F32),
        grid_spec=pltpu.PrefetchScalarGridSpec(
            num_scalar_prefetch=1, grid=(1,),
            in_specs=[pl.BlockSpec((None, rh, d), lambda i, p: (p[1], 0, 0)),
                      pl.BlockSpec((3, rh, d), lambda i, p: (0, 0, 0))],
            out_specs=pl.BlockSpec((None, rh, d), lambda i, p: (p[0], 0, 0))),
        compiler_params=_params("arbitrary"),
    )(pos, p_out, land3)


def _matmul(a, b, kind, name, tm_t=1056, tn_t=1024, tk_t=2048, comms=()):
    if kind == "nn":
        (m, k), n = a.shape, b.shape[1]
    elif kind == "nt":
        (m, k), n = a.shape, b.shape[0]
    else:
        (k, m), n = a.shape, b.shape[1]
    tm = _pick(m, tm_t, 128 if kind == "tn" else 16)
    tn = _pick(n, tn_t, 128)
    tk = _pick(k, tk_t, 16 if kind == "tn" else 128)
    nk = k // tk
    grid = (m // tm, n // tn, nk)
    dot = {"nn": _nn, "nt": _nt, "tn": _tn}[kind]
    a_spec = {"nn": pl.BlockSpec((tm, tk), lambda i, j, q: (i, q)),
              "nt": pl.BlockSpec((tm, tk), lambda i, j, q: (i, q)),
              "tn": pl.BlockSpec((tk, tm), lambda i, j, q: (q, i))}[kind]
    b_spec = {"nn": pl.BlockSpec((tk, tn), lambda i, j, q: (q, j)),
              "nt": pl.BlockSpec((tn, tk), lambda i, j, q: (j, q)),
              "tn": pl.BlockSpec((tk, tn), lambda i, j, q: (q, j))}[kind]
    comm_ins = [x for cm in comms for x in cm.ins]
    comm_outs = [x for cm in comms for x in cm.outs]
    aliases, scratch = {}, []
    i_off, o_off = 2, 1
    for cm in comms:
        for ci, co in cm.aliases.items():
            aliases[i_off + ci] = o_off + co
        i_off, o_off = i_off + len(cm.ins), o_off + len(cm.outs)
        scratch += cm.scratch()

    def body(a_ref, b_ref, *rest):
        cin = rest[:len(comm_ins)]
        o_ref = rest[len(comm_ins)]
        cout = rest[len(comm_ins) + 1:len(comm_ins) + 1 + len(comm_outs)]
        acc_ref = rest[len(comm_ins) + 1 + len(comm_outs)]
        sems = rest[len(comm_ins) + 2 + len(comm_outs):]
        ids = [pl.program_id(ax) for ax in range(3)]
        first = functools.reduce(jnp.logical_and, [ids[ax] == 0 for ax in range(3)])
        last = functools.reduce(jnp.logical_and, [ids[ax] == grid[ax] - 1 for ax in range(3)])

        def comm_args(idx):
            i0 = sum(len(cm.ins) for cm in comms[:idx])
            o0 = sum(len(cm.outs) for cm in comms[:idx])
            cm = comms[idx]
            return (cin[i0:i0 + len(cm.ins)], cout[o0:o0 + len(cm.outs)]) + tuple(sems[3 * idx:3 * idx + 3])

        if comms:
            @pl.when(first)
            def _():
                for idx, cm in enumerate(comms):
                    cm.start(*comm_args(idx))

        q = ids[2]
        p = dot(a_ref[...], b_ref[...])
        if nk == 1:
            o_ref[...] = p
        else:
            @pl.when(q == 0)
            def _():
                acc_ref[...] = p

            @pl.when(q > 0)
            def _():
                acc_ref[...] += p

            @pl.when(q == nk - 1)
            def _():
                o_ref[...] = acc_ref[...]

        if comms:
            @pl.when(last)
            def _():
                for idx, cm in enumerate(comms):
                    cm.finish(*comm_args(idx))

    sem = ("arbitrary",) * 3 if comms else ("parallel", "parallel", "arbitrary")
    res = pl.pallas_call(
        body, name=name,
        out_shape=(jax.ShapeDtypeStruct((m, n), F32), *comm_outs),
        grid=grid,
        in_specs=[a_spec, b_spec] + [ANY] * len(comm_ins),
        out_specs=(pl.BlockSpec((tm, tn), lambda i, j, q: (i, j)), *([ANY] * len(comm_outs))),
        scratch_shapes=[pltpu.VMEM((tm, tn) if nk > 1 else (8, 128), F32)] + scratch,
        input_output_aliases=aliases,
        compiler_params=_params(*sem),
    )(a, b, *comm_ins)
    return res if comms else res[0]


def _out_proj(y, w_out, xs, rows, n_ctx, name):
    t, k = y.shape
    d = w_out.shape[1]
    tm, tn = _pick(t, 1056, 16), _pick(d, 1024, 128)

    def body(y_ref, w_ref, x_ref, r_ref, xo_ref, p_ref):
        i = pl.program_id(0)
        p = _nn(y_ref[...], w_ref[...])
        ridx = i * tm + lax.broadcasted_iota(jnp.int32, (tm, 1), 0)
        gate = jnp.where(ridx < n_ctx, r_ref[5:6, :], r_ref[2:3, :])
        p_ref[...] = p
        xo_ref[...] = x_ref[...] + gate * p

    return pl.pallas_call(
        body, name=name,
        out_shape=(jax.ShapeDtypeStruct((t, d), F32), jax.ShapeDtypeStruct((t, d), F32)),
        grid=(t // tm, d // tn),
        in_specs=[pl.BlockSpec((tm, k), lambda i, j: (i, 0)), pl.BlockSpec((k, tn), lambda i, j: (0, j)),
                  pl.BlockSpec((tm, tn), lambda i, j: (i, j)), pl.BlockSpec((8, tn), lambda i, j: (0, j))],
        out_specs=(pl.BlockSpec((tm, tn), lambda i, j: (i, j)), pl.BlockSpec((tm, tn), lambda i, j: (i, j))),
        compiler_params=_params("parallel", "parallel"),
    )(y, w_out, xs, rows)


def _modulate(xs, nw, rows, nct, name):
    t, d = xs.shape
    tr = ROW_TILE

    def body(x_ref, nw_ref, r_ref, o_ref, ot_ref):
        is_ctx = pl.program_id(0) < nct
        x = x_ref[...]
        xn = (x * lax.rsqrt(jnp.mean(x * x, axis=-1, keepdims=True) + EPS)) * nw_ref[...]
        shift = jnp.where(is_ctx, r_ref[3:4, :], r_ref[0:1, :])
        scale = jnp.where(is_ctx, r_ref[4:5, :], r_ref[1:2, :])
        hx = xn * (1.0 + scale) + shift
        o_ref[...] = hx.astype(BF16)
        ot_ref[...] = hx.T.astype(BF16)

    return pl.pallas_call(
        body, name=name,
        out_shape=(jax.ShapeDtypeStruct((t, d), BF16), jax.ShapeDtypeStruct((d, t), BF16)), grid=(t // tr,),
        in_specs=[pl.BlockSpec((tr, d), lambda i: (i, 0)), pl.BlockSpec((1, d), lambda i: (0, 0)),
                  pl.BlockSpec((8, d), lambda i: (0, 0))],
        out_specs=(pl.BlockSpec((tr, d), lambda i: (i, 0)), pl.BlockSpec((d, tr), lambda i: (0, i))),
        compiler_params=_params("parallel"),
    )(xs, nw, rows)


def _modulate_bwd(dhx, xs, g, nw, rows, nct, name, latent_only=False):
    t, d = xs.shape
    tr = ROW_TILE
    t_out = t - nct * tr if latent_only else t
    out_row = (lambda i: (jnp.maximum(i - nct, 0), 0)) if latent_only else (lambda i: (i, 0))

    def body(dh_ref, x_ref, g_ref, nw_ref, r_ref, go_ref, acc_ref):
        i = pl.program_id(0)
        is_ctx = i < nct

        @pl.when(i == 0)
        def _():
            acc_ref[...] = jnp.zeros_like(acc_ref)

        x, dh = x_ref[...], dh_ref[...]
        r = lax.rsqrt(jnp.mean(x * x, axis=-1, keepdims=True) + EPS)
        xh = x * r
        xn = xh * nw_ref[...]
        scale = jnp.where(is_ctx, r_ref[4:5, :], r_ref[1:2, :])
        d_shift = jnp.sum(dh, axis=0, keepdims=True)
        d_scale = jnp.sum(dh * xn, axis=0, keepdims=True)
        dxn = dh * (1.0 + scale)
        acc_ref[6:7, :] += jnp.sum(dxn * xh, axis=0, keepdims=True)
        dxh = dxn * nw_ref[...]
        dx = r * (dxh - xh * jnp.mean(dxh * xh, axis=-1, keepdims=True))
        go_ref[...] = g_ref[...] + dx
        zero = jnp.zeros_like(d_shift)
        acc_ref[0:1, :] += jnp.where(is_ctx, zero, d_shift)
        acc_ref[1:2, :] += jnp.where(is_ctx, zero, d_scale)
        acc_ref[3:4, :] += jnp.where(is_ctx, d_shift, zero)
        acc_ref[4:5, :] += jnp.where(is_ctx, d_scale, zero)

    return pl.pallas_call(
        body, name=name,
        out_shape=(jax.ShapeDtypeStruct((t_out, d), F32), jax.ShapeDtypeStruct((8, d), F32)), grid=(t // tr,),
        in_specs=[pl.BlockSpec((tr, d), lambda i: (i, 0)), pl.BlockSpec((tr, d), lambda i: (i, 0)),
                  pl.BlockSpec((tr, d), lambda i: (i, 0)), pl.BlockSpec((1, d), lambda i: (0, 0)),
                  pl.BlockSpec((8, d), lambda i: (0, 0))],
        out_specs=(pl.BlockSpec((tr, d), out_row), pl.BlockSpec((8, d), lambda i: (0, 0))),
        compiler_params=_params("arbitrary"),
    )(dhx, xs, g, nw, rows)


def _gate_bwd(g, proj, rows, nct, name):
    t, d = g.shape
    tr = ROW_TILE

    def body(g_ref, p_ref, r_ref, gp_ref, acc_ref):
        i = pl.program_id(0)
        is_ctx = i < nct

        @pl.when(i == 0)
        def _():
            acc_ref[...] = jnp.zeros_like(acc_ref)

        gg = g_ref[...]
        gate = jnp.where(is_ctx, r_ref[5:6, :], r_ref[2:3, :])
        gp_ref[...] = (gg * gate).astype(BF16)
        s = jnp.sum(gg * p_ref[...], axis=0, keepdims=True)
        zero = jnp.zeros_like(s)
        acc_ref[2:3, :] += jnp.where(is_ctx, zero, s)
        acc_ref[5:6, :] += jnp.where(is_ctx, s, zero)

    return pl.pallas_call(
        body, name=name,
        out_shape=(jax.ShapeDtypeStruct((t, d), BF16), jax.ShapeDtypeStruct((8, d), F32)), grid=(t // tr,),
        in_specs=[pl.BlockSpec((tr, d), lambda i: (i, 0)), pl.BlockSpec((tr, d), lambda i: (i, 0)),
                  pl.BlockSpec((8, d), lambda i: (0, 0))],
        out_specs=(pl.BlockSpec((tr, d), lambda i: (i, 0)), pl.BlockSpec((8, d), lambda i: (0, 0))),
        compiler_params=_params("arbitrary"),
    )(g, proj, rows)


def _final_loss(xs, tgt, fw, nct, name):
    t, d = xs.shape
    tr = ROW_TILE

    def body(x_ref, t_ref, w_ref, g_ref, acc_ref):
        i = pl.program_id(0)

        @pl.when(i == 0)
        def _():
            acc_ref[...] = jnp.zeros_like(acc_ref)

        @pl.when(i < nct)
        def _():
            g_ref[...] = jnp.zeros_like(g_ref)

        @pl.when(i >= nct)
        def _():
            x = x_ref[...]
            r = lax.rsqrt(jnp.mean(x * x, axis=-1, keepdims=True) + EPS)
            xh = x * r
            e = xh * w_ref[...] - t_ref[...]
            acc_ref[1:2, :] += jnp.zeros((1, d), F32) + jnp.sum(e * e) * (0.5 / d)
            dy = e * (1.0 / d)
            acc_ref[0:1, :] += jnp.sum(dy * xh, axis=0, keepdims=True)
            dxh = dy * w_ref[...]
            g_ref[...] = r * (dxh - xh * jnp.mean(dxh * xh, axis=-1, keepdims=True))

    return pl.pallas_call(
        body, name=name,
        out_shape=(jax.ShapeDtypeStruct((t, d), F32), jax.ShapeDtypeStruct((8, d), F32)), grid=(t // tr,),
        in_specs=[pl.BlockSpec((tr, d), lambda i: (i, 0)),
                  pl.BlockSpec((tr, d), lambda i: (jnp.maximum(i - nct, 0), 0)),
                  pl.BlockSpec((1, d), lambda i: (0, 0))],
        out_specs=(pl.BlockSpec((tr, d), lambda i: (i, 0)), pl.BlockSpec((8, d), lambda i: (0, 0))),
        compiler_params=_params("arbitrary"),
    )(xs, tgt, fw)


def _seq_edges(i, nct, nt):
    has_prev = jnp.logical_and(i != 0, i != nct)
    has_next = jnp.logical_and(i != nct - 1, i != nt - 1)
    return has_prev, has_next


def _shift_rows(a, before, after):
    tr = a.shape[0]
    ridx = lax.broadcasted_iota(jnp.int32, (tr, 1), 0)
    down = jnp.where(ridx == 0, before, pltpu.roll(a, 1, 0))
    up = jnp.where(ridx == tr - 1, after, pltpu.roll(a, tr - 1, 0))
    return down, up


def _halo_specs(tr, dc, col, nrow8):
    per = tr // 8
    prev = pl.BlockSpec((8, dc), lambda i: (jnp.maximum(i * per - 1, 0), col))
    nxt = pl.BlockSpec((8, dc), lambda i: (jnp.minimum((i + 1) * per, nrow8 - 1), col))
    return prev, nxt


def _mix_fwd(u, o_f, o_b, cw, cnw, rnw, nct, name):
    t, w = u.shape
    dc = w // 8
    nh = dc // HEAD
    tr = ROW_TILE
    nt = t // tr
    col = lambda cidx: pl.BlockSpec((tr, dc), lambda i: (i, cidx))
    acp, acn = _halo_specs(tr, dc, 2, t // 8)
    ahp, ahn = _halo_specs(tr, dc, 0, t // 8)

    def body(ah, ab, ac, az, rz, acp_r, acn_r, ahp_r, ahn_r, of_r, ob_r, cw_r, cnw_r, rnw_r, y_ref, yt_ref, cv_ref):
        i = pl.program_id(0)
        has_prev, has_next = _seq_edges(i, nct, nt)
        ch = ac[...] * ah[...]
        before = jnp.where(has_prev, acp_r[7:8, :] * ahp_r[7:8, :], 0.0)
        after = jnp.where(has_next, acn_r[0:1, :] * ahn_r[0:1, :], 0.0)
        down, up = _shift_rows(ch, before, after)
        cv = down * cw_r[0:1, :] + ch * cw_r[1:2, :] + up * cw_r[2:3, :]
        cv_ref[...] = cv
        y1 = ab[...] * cv
        yn = (y1 * lax.rsqrt(jnp.mean(y1 * y1, axis=-1, keepdims=True) + EPS)) * cnw_r[...]
        yc = _silu(az[...]) * yn
        y_ref[:, 0:dc] = yc.astype(BF16)
        yt_ref[0:dc, :] = yc.T.astype(BF16)
        for h in range(nh):
            sl = slice(h * HEAD, (h + 1) * HEAD)
            o = of_r[:, sl] + ob_r[:, sl]
            dev = o - jnp.mean(o, axis=-1, keepdims=True)
            on = dev * lax.rsqrt(jnp.mean(dev * dev, axis=-1, keepdims=True) + EPS)
            yr = _silu(rz[:, sl]) * (on * rnw_r[:, sl])
            y_ref[:, dc + h * HEAD:dc + (h + 1) * HEAD] = yr.astype(BF16)
            yt_ref[dc + h * HEAD:dc + (h + 1) * HEAD, :] = yr.T.astype(BF16)

    vec = pl.BlockSpec((1, dc), lambda i: (0, 0))
    return pl.pallas_call(
        body, name=name,
        out_shape=(jax.ShapeDtypeStruct((t, 2 * dc), BF16), jax.ShapeDtypeStruct((2 * dc, t), BF16),
                   jax.ShapeDtypeStruct((t, dc), F32)), grid=(nt,),
        in_specs=[col(0), col(1), col(2), col(3), col(7), acp, acn, ahp, ahn,
                  pl.BlockSpec((tr, dc), lambda i: (i, 0)), pl.BlockSpec((tr, dc), lambda i: (i, 0)),
                  pl.BlockSpec((8, dc), lambda i: (0, 0)), vec, vec],
        out_specs=(pl.BlockSpec((tr, 2 * dc), lambda i: (i, 0)), pl.BlockSpec((2 * dc, tr), lambda i: (0, i)),
                   pl.BlockSpec((tr, dc), lambda i: (i, 0))),
        compiler_params=_params("parallel"),
    )(u, u, u, u, u, u, u, u, u, o_f, o_b, cw, cnw, rnw)


def _mix_bwd(dy, u, cv, o_f, o_b, cnw, rnw, name):
    t, w = u.shape
    dc = w // 8
    nh = dc // HEAD
    tr = ROW_TILE
    col = lambda cidx: pl.BlockSpec((tr, dc), lambda i: (i, cidx))

    def body(dyc, dyr, ab, az, rz, cv_r, of_r, ob_r, cnw_r, rnw_r, dab, daz, drz, dcv, do, acc_ref):
        @pl.when(pl.program_id(0) == 0)
        def _():
            acc_ref[...] = jnp.zeros_like(acc_ref)

        z, b, c_out = az[...], ab[...], cv_r[...]
        y1 = b * c_out
        r = lax.rsqrt(jnp.mean(y1 * y1, axis=-1, keepdims=True) + EPS)
        yh = y1 * r
        yn = yh * cnw_r[...]
        dyc_v = dyc[...]
        daz[...] = (dyc_v * yn * _dsilu(z)).astype(BF16)
        dyn = dyc_v * _silu(z)
        acc_ref[0:1, :] += jnp.sum(dyn * yh, axis=0, keepdims=True)
        dyh = dyn * cnw_r[...]
        dy1 = r * (dyh - yh * jnp.mean(dyh * yh, axis=-1, keepdims=True))
        dab[...] = (dy1 * c_out).astype(BF16)
        dcv[...] = dy1 * b
        for h in range(nh):
            sl = slice(h * HEAD, (h + 1) * HEAD)
            o = of_r[:, sl] + ob_r[:, sl]
            dev = o - jnp.mean(o, axis=-1, keepdims=True)
            rs = lax.rsqrt(jnp.mean(dev * dev, axis=-1, keepdims=True) + EPS)
            on = dev * rs
            zz, dyr_v, gw = rz[:, sl], dyr[:, sl], rnw_r[:, sl]
            drz[:, sl] = (dyr_v * (on * gw) * _dsilu(zz)).astype(BF16)
            dong = dyr_v * _silu(zz)
            acc_ref[1:2, sl] += jnp.sum(dong * on, axis=0, keepdims=True)
            don = dong * gw
            do[:, sl] = rs * (don - jnp.mean(don, axis=-1, keepdims=True)
                              - on * jnp.mean(don * on, axis=-1, keepdims=True))

    vec = pl.BlockSpec((1, dc), lambda i: (0, 0))
    row = pl.BlockSpec((tr, dc), lambda i: (i, 0))
    return pl.pallas_call(
        body, name=name,
        out_shape=(jax.ShapeDtypeStruct((t, dc), BF16), jax.ShapeDtypeStruct((t, dc), BF16),
                   jax.ShapeDtypeStruct((t, dc), BF16), jax.ShapeDtypeStruct((t, dc), F32),
                   jax.ShapeDtypeStruct((t, dc), F32), jax.ShapeDtypeStruct((8, dc), F32)),
        grid=(t // tr,),
        in_specs=[col(0), col(1), col(1), col(3), col(7), row, row, row, vec, vec],
        out_specs=(row, row, row, row, row, pl.BlockSpec((8, dc), lambda i: (0, 0))),
        compiler_params=_params("arbitrary"),
    )(dy, dy, u, u, u, cv, o_f, o_b, cnw, rnw)


def _rope_masks():
    lane = lax.broadcasted_iota(jnp.int32, (1, HEAD), 1)
    return (lane % (HEAD // 2)) >= (HEAD // 4)


def _rot(a, upper):
    return jnp.where(upper, pltpu.roll(a, HEAD // 4, 1), pltpu.roll(a, HEAD - HEAD // 4, 1))


def _assemble_du(dcv, u, cw, dab, daz, drz, dq_f, dq_b, dk_f, dk_b, dv_f, dv_b, cos, sin, nct, name):
    t, w = u.shape
    dc = w // 8
    nh = dc // HEAD
    tr = ROW_TILE
    nt = t // tr
    k_scale = HEAD ** -0.5
    col = lambda cidx: pl.BlockSpec((tr, dc), lambda i: (i, cidx))
    row = pl.BlockSpec((tr, dc), lambda i: (i, 0))
    dcp, dcn = _halo_specs(tr, dc, 0, t // 8)

    def body(dcv_r, dcp_r, dcn_r, ah, ac, cw_r, dab_r, daz_r, drz_r, dqf, dqb, dkf, dkb, dvf, dvb, cos_r, sin_r,
             du, acc_ref):
        i = pl.program_id(0)

        @pl.when(i == 0)
        def _():
            acc_ref[...] = jnp.zeros_like(acc_ref)

        has_prev, has_next = _seq_edges(i, nct, nt)
        d_c = dcv_r[...]
        before = jnp.where(has_prev, dcp_r[7:8, :], 0.0)
        after = jnp.where(has_next, dcn_r[0:1, :], 0.0)
        d_prev, d_next = _shift_rows(d_c, before, after)
        a_h, a_c = ah[...], ac[...]
        ch = a_c * a_h
        dch = d_next * cw_r[0:1, :] + d_c * cw_r[1:2, :] + d_prev * cw_r[2:3, :]
        acc_ref[0:1, :] += jnp.sum(ch * d_next, axis=0, keepdims=True)
        acc_ref[1:2, :] += jnp.sum(ch * d_c, axis=0, keepdims=True)
        acc_ref[2:3, :] += jnp.sum(ch * d_prev, axis=0, keepdims=True)
        du[:, 0:dc] = (dch * a_c).astype(BF16)
        du[:, dc:2 * dc] = dab_r[...]
        du[:, 2 * dc:3 * dc] = (dch * a_h).astype(BF16)
        du[:, 3 * dc:4 * dc] = daz_r[...]
        du[:, 7 * dc:8 * dc] = drz_r[...]
        du[:, 6 * dc:7 * dc] = (dvf[...] + dvb[...]).astype(BF16)
        upper = _rope_masks()
        cs, sn = cos_r[...], sin_r[...]
        for h in range(nh):
            sl = slice(h * HEAD, (h + 1) * HEAD)
            dq = dqf[:, sl] + dqb[:, sl]
            dk = (dkf[:, sl] + dkb[:, sl]) * k_scale
            du[:, 4 * dc + h * HEAD:4 * dc + (h + 1) * HEAD] = (dq * cs + _rot(dq * sn, upper)).astype(BF16)
            du[:, 5 * dc + h * HEAD:5 * dc + (h + 1) * HEAD] = (dk * cs + _rot(dk * sn, upper)).astype(BF16)

    tab = pl.BlockSpec((tr, HEAD), lambda i: (i, 0))
    return pl.pallas_call(
        body, name=name,
        out_shape=(jax.ShapeDtypeStruct((t, w), BF16), jax.ShapeDtypeStruct((8, dc), F32)), grid=(nt,),
        in_specs=[row, dcp, dcn, col(0), col(2), pl.BlockSpec((8, dc), lambda i: (0, 0)),
                  row, row, row, row, row, row, row, row, row, tab, tab],
        out_specs=(pl.BlockSpec((tr, w), lambda i: (i, 0)), pl.BlockSpec((8, dc), lambda i: (0, 0))),
        compiler_params=_params("arbitrary"),
    )(dcv, dcv, dcv, u, u, cw, dab, daz, drz, dq_f, dq_b, dk_f, dk_b, dv_f, dv_b, cos, sin)


def _bwd_dir_chunk(s, ncc, ns):
    return jnp.where(s < ncc, ncc - 1 - s, ns + ncc - 1 - s)


def _decay_tables(dec_ref, h):
    c = CHUNK
    ii = lax.broadcasted_iota(jnp.int32, (c, c), 0)
    jj = lax.broadcasted_iota(jnp.int32, (c, c), 1)
    diff = (ii - jj).astype(F32)
    pos = lax.broadcasted_iota(jnp.int32, (c, 1), 0).astype(F32)
    out = []
    for d in range(2):
        lg = -jnp.exp(jnp.zeros((1, HEAD), F32) + dec_ref[d, h])
        lg1 = lg[:, 0:1]
        sd = diff if d == 0 else -diff
        wm = jnp.maximum(sd, 0.0)
        dm = jnp.where(sd >= 0, jnp.exp(lg * wm), 0.0)
        if d == 0:
            qw, kw = pos + 1.0, (c - 1.0) - pos
        else:
            qw, kw = c - pos, pos
        out.append(dict(lg=lg, dm=dm, wm=wm, qw=qw, kw=kw, qd=jnp.exp(lg1 * qw), kd=jnp.exp(lg1 * kw),
                        gc=jnp.exp(lg * float(c))))
    return out


def _ret_fwd(u, cos, sin, dec, ncc, name):
    t, w = u.shape
    dc = w // 8
    nh = dc // HEAD
    c = CHUNK
    ns = t // c
    k_scale = HEAD ** -0.5
    fmap = lambda g: (lambda s: (s, g))
    bmap = lambda g: (lambda s: (_bwd_dir_chunk(s, ncc, ns), g))
    blk = lambda m: pl.BlockSpec((c, dc), m)
    tab = lambda m: pl.BlockSpec((c, HEAD), m)
    st_f = pl.BlockSpec((nh, None, HEAD, HEAD), lambda s: (0, s, 0, 0))
    st_b = pl.BlockSpec((nh, None, HEAD, HEAD), lambda s: (0, _bwd_dir_chunk(s, ncc, ns), 0, 0))

    def body(qf, kf, vf, qb, kb, vb, cf, sf, cb, sb, dec_ref, of_ref, ob_ref, stf_ref, stb_ref, s_f, s_b):
        @pl.when(pl.program_id(0) == 0)
        def _():
            s_f[...] = jnp.zeros_like(s_f)
            s_b[...] = jnp.zeros_like(s_b)

        upper = _rope_masks()
        for h in range(nh):
            sl = slice(h * HEAD, (h + 1) * HEAD)
            tabs = _decay_tables(dec_ref, h)
            for d, (q_r, k_r, v_r, cs_r, sn_r, o_ref, st_ref, st) in enumerate((
                    (qf, kf, vf, cf, sf, of_ref, stf_ref, s_f), (qb, kb, vb, cb, sb, ob_ref, stb_ref, s_b))):
                tb = tabs[d]
                cs, sn = cs_r[...], sn_r[...]
                q = q_r[:, sl]
                q = q * cs + _rot(q, upper) * sn
                k = k_r[:, sl]
                k = (k * cs + _rot(k, upper) * sn) * k_scale
                vv = _bf(v_r[:, sl])
                state = st[h]
                st_ref[h] = state
                p = _nt(_bf(q), _bf(k)) * tb["dm"]
                o_ref[:, sl] = _nn(_bf(p), vv) + _nn(_bf(q * tb["qd"]), _bf(state))
                st[h] = tb["gc"] * state + _tn(_bf(k * tb["kd"]), vv)

    return pl.pallas_call(
        body, name=name,
        out_shape=(jax.ShapeDtypeStruct((t, dc), F32), jax.ShapeDtypeStruct((t, dc), F32),
                   jax.ShapeDtypeStruct((nh, ns, HEAD, HEAD), F32), jax.ShapeDtypeStruct((nh, ns, HEAD, HEAD), F32)),
        grid=(ns,),
        in_specs=[blk(fmap(4)), blk(fmap(5)), blk(fmap(6)), blk(bmap(4)), blk(bmap(5)), blk(bmap(6)),
                  tab(fmap(0)), tab(fmap(0)), tab(bmap(0)), tab(bmap(0)), SMEM_SPEC],
        out_specs=(blk(fmap(0)), blk(bmap(0)), st_f, st_b),
        scratch_shapes=[pltpu.VMEM((nh, HEAD, HEAD), F32), pltpu.VMEM((nh, HEAD, HEAD), F32)],
        compiler_params=_params("arbitrary"),
    )(u, u, u, u, u, u, cos, sin, cos, sin, dec)


def _ret_bwd(u, do, st_f, st_b, cos, sin, dec, ncc, name):
    t, w = u.shape
    dc = w // 8
    nh = dc // HEAD
    c = CHUNK
    ns = t // c
    k_scale = HEAD ** -0.5
    fchunk = lambda s: ns - 1 - s
    bchunk = lambda s: _bwd_dir_chunk(ns - 1 - s, ncc, ns)
    fmap = lambda g: (lambda s: (fchunk(s), g))
    bmap = lambda g: (lambda s: (bchunk(s), g))
    blk = lambda m: pl.BlockSpec((c, dc), m)
    tab = lambda m: pl.BlockSpec((c, HEAD), m)
    stf_spec = pl.BlockSpec((nh, None, HEAD, HEAD), lambda s: (0, fchunk(s), 0, 0))
    stb_spec = pl.BlockSpec((nh, None, HEAD, HEAD), lambda s: (0, bchunk(s), 0, 0))

    def body(qf, kf, vf, dof, qb, kb, vb, dob, stf_ref, stb_ref, cf, sf, cb, sb, dec_ref,
             dqf, dkf, dvf, dqb, dkb, dvb, dlg_ref, ds_f, ds_b):
        step = pl.program_id(0)

        @pl.when(step == 0)
        def _():
            ds_f[...] = jnp.zeros_like(ds_f)
            ds_b[...] = jnp.zeros_like(ds_b)
            dlg_ref[...] = jnp.zeros_like(dlg_ref)

        upper = _rope_masks()
        for h in range(nh):
            sl = slice(h * HEAD, (h + 1) * HEAD)
            tabs = _decay_tables(dec_ref, h)
            for d, (q_r, k_r, v_r, do_r, st_ref, cs_r, sn_r, dq_ref, dk_ref, dv_ref, ds) in enumerate((
                    (qf, kf, vf, dof, stf_ref, cf, sf, dqf, dkf, dvf, ds_f),
                    (qb, kb, vb, dob, stb_ref, cb, sb, dqb, dkb, dvb, ds_b))):
                tb = tabs[d]
                cs, sn = cs_r[...], sn_r[...]
                q = q_r[:, sl]
                q = q * cs + _rot(q, upper) * sn
                k = k_r[:, sl]
                k = (k * cs + _rot(k, upper) * sn) * k_scale
                qb16, kb16, vb16, dob16 = _bf(q), _bf(k), _bf(v_r[:, sl]), _bf(do_r[:, sl])
                state, dstate = st_ref[h], ds[h]
                sb16, dsb16 = _bf(state), _bf(dstate)
                a = _nt(qb16, kb16) * tb["dm"]
                da = _nt(dob16, vb16)
                dp = _bf(da * tb["dm"])
                dq_i = _nt(dob16, sb16) * tb["qd"]
                dk_i = _nt(vb16, dsb16) * tb["kd"]
                dq_ref[:, sl] = _nn(dp, kb16) + dq_i
                dk_ref[:, sl] = _tn(dp, qb16) + dk_i
                dv_ref[:, sl] = _tn(_bf(a), dob16) + _nn(_bf(k * tb["kd"]), dsb16)
                ds[h] = tb["gc"] * dstate + _tn(_bf(q * tb["qd"]), dob16)
                dlg = (jnp.sum(da * a * tb["wm"])
                       + jnp.sum(tb["qw"] * jnp.sum(dq_i * q, axis=-1, keepdims=True))
                       + jnp.sum(tb["kw"] * jnp.sum(dk_i * k, axis=-1, keepdims=True)))
                dlg_row = jnp.sum(tb["gc"] * state * dstate, axis=0, keepdims=True)
                dlg_ref[d * nh + h:d * nh + h + 1, :] += (
                    jnp.zeros((1, HEAD), F32) + dlg + float(c) * jnp.sum(dlg_row))

        @pl.when(step == ns - 1)
        def _():
            for h in range(nh):
                for d in range(2):
                    lg = -jnp.exp(jnp.zeros((1, HEAD), F32) + dec_ref[d, h])
                    r = d * nh + h
                    dlg_ref[r:r + 1, :] = dlg_ref[r:r + 1, :] * lg

    dshape = jax.ShapeDtypeStruct((t, dc), F32)
    return pl.pallas_call(
        body, name=name,
        out_shape=(dshape, dshape, dshape, dshape, dshape, dshape, jax.ShapeDtypeStruct((2 * nh, HEAD), F32)),
        grid=(ns,),
        in_specs=[blk(fmap(4)), blk(fmap(5)), blk(fmap(6)), blk(fmap(0)),
                  blk(bmap(4)), blk(bmap(5)), blk(bmap(6)), blk(bmap(0)), stf_spec, stb_spec,
                  tab(fmap(0)), tab(fmap(0)), tab(bmap(0)), tab(bmap(0)), SMEM_SPEC],
        out_specs=(blk(fmap(0)), blk(fmap(0)), blk(fmap(0)), blk(bmap(0)), blk(bmap(0)), blk(bmap(0)),
                   pl.BlockSpec((2 * nh, HEAD), lambda s: (0, 0))),
        scratch_shapes=[pltpu.VMEM((nh, HEAD, HEAD), F32), pltpu.VMEM((nh, HEAD, HEAD), F32)],
        compiler_params=_params("arbitrary"),
    )(u, u, u, do, u, u, u, do, st_f, st_b, cos, sin, cos, sin, dec)


def _mod_matvec(c16, w_mod, b_mod):
    nl, d, nm = w_mod.shape
    tn = _pick(nm, 512, 128)

    def body(c_ref, w_ref, b_ref, sc_ref, o_ref):
        sc = _silu(c_ref[...])
        sc_ref[...] = sc
        o_ref[...] = _nn(_bf(sc), _bf(w_ref[...])) + b_ref[...]

    return pl.pallas_call(
        body, name="mod_matvec",
        out_shape=(jax.ShapeDtypeStruct((16, d), F32), jax.ShapeDtypeStruct((nl, 16, nm), F32)),
        grid=(nl, nm // tn),
        in_specs=[pl.BlockSpec((16, d), lambda l, j: (0, 0)), pl.BlockSpec((None, d, tn), lambda l, j: (l, 0, j)),
                  pl.BlockSpec((None, 1, tn), lambda l, j: (l, 0, j))],
        out_specs=(pl.BlockSpec((16, d), lambda l, j: (0, 0)), pl.BlockSpec((None, 16, tn), lambda l, j: (l, 0, j))),
        compiler_params=_params("arbitrary", "arbitrary"),
    )(c16, w_mod, b_mod)


def _wmod_grad(sc16, dm, w_mod):
    nl, d, nm = w_mod.shape
    tn = _pick(nm, 512, 128)

    def body(sc_ref, dm_ref, w_ref, gw_ref, pc_ref):
        @pl.when(jnp.logical_and(pl.program_id(0) == 0, pl.program_id(1) == 0))
        def _():
            pc_ref[...] = jnp.zeros_like(pc_ref)

        dmb = _bf(dm_ref[...])
        gw_ref[...] = _tn(_bf(sc_ref[...]), dmb)
        pc_ref[...] += _nt(dmb, _bf(w_ref[...]))

    return pl.pallas_call(
        body, name="wmod_grad",
        out_shape=(jax.ShapeDtypeStruct((nl, d, nm), F32), jax.ShapeDtypeStruct((16, d), F32)),
        grid=(nl, nm // tn),
        in_specs=[pl.BlockSpec((16, d), lambda l, j: (0, 0)), pl.BlockSpec((None, 16, tn), lambda l, j: (l, 0, j)),
                  pl.BlockSpec((None, d, tn), lambda l, j: (l, 0, j))],
        out_specs=(pl.BlockSpec((None, d, tn), lambda l, j: (l, 0, j)), pl.BlockSpec((16, d), lambda l, j: (0, 0))),
        compiler_params=_params("arbitrary", "arbitrary"),
    )(sc16, dm, w_mod)


def _rowsum(a, name):
    r, n = a.shape
    tn = _pick(n, 4096, 128)

    def body(a_ref, o_ref):
        acc = a_ref[0:1, :]
        for i in range(1, r):
            acc = acc + a_ref[i:i + 1, :]
        o_ref[...] = jnp.zeros((8, tn), F32) + acc

    return pl.pallas_call(
        body, name=name, out_shape=jax.ShapeDtypeStruct((8, n), F32), grid=(n // tn,),
        in_specs=[pl.BlockSpec((r, tn), lambda j: (0, j))], out_specs=pl.BlockSpec((8, tn), lambda j: (0, j)),
        compiler_params=_params("parallel"),
    )(a)


def _cctx_grad(parts, c_ctx8):
    _, _, d = parts.shape

    def body(p_ref, c_ref, o_ref):
        acc = ((p_ref[0] + p_ref[1]) + p_ref[2]) + p_ref[3]
        o_ref[...] = acc * _dsilu(c_ref[...])

    return pl.pallas_call(
        body, name="cctx_grad", out_shape=jax.ShapeDtypeStruct((8, d), F32),
        in_specs=[VMEM_SPEC, VMEM_SPEC], out_specs=VMEM_SPEC,
    )(parts, c_ctx8)


def _adamw(g, w, m, v, name):
    r, n = w.shape
    tr = _pick(r, max(8, min(256, (512 * 1024) // n // 8 * 8)), 8)
    bc1 = 1.0 - ADAM_B1 ** ADAM_STEP
    bc2 = 1.0 - ADAM_B2 ** ADAM_STEP

    def body(g_ref, w_ref, m_ref, v_ref, d_ref, mo_ref, vo_ref):
        gg = g_ref[...]
        mn = ADAM_B1 * m_ref[...] + (1.0 - ADAM_B1) * gg
        vn = ADAM_B2 * v_ref[...] + (1.0 - ADAM_B2) * (gg * gg)
        mo_ref[...] = mn
        vo_ref[...] = vn
        d_ref[...] = -ADAM_LR * ((mn / bc1) / (jnp.sqrt(vn / bc2) + ADAM_EPS) + ADAM_WD * w_ref[...])

    spec = pl.BlockSpec((tr, n), lambda i: (i, 0))
    shp = jax.ShapeDtypeStruct((r, n), F32)
    return pl.pallas_call(
        body, name=name, out_shape=(shp, shp, shp), grid=(r // tr,),
        in_specs=[spec, spec, spec, spec], out_specs=(spec, spec, spec),
        compiler_params=_params("parallel"),
    )(g, w, m, v)


def _adamw_layer(g, w, m, v, layer, prev, name):
    r, n = g.shape
    tr = _pick(r, max(8, min(256, (512 * 1024) // n // 8 * 8)), 8)
    nb = r // tr
    bc1 = 1.0 - ADAM_B1 ** ADAM_STEP
    bc2 = 1.0 - ADAM_B2 ** ADAM_STEP
    n_prev = 0 if prev is None else 4

    def body(g_ref, w_ref, m_ref, v_ref, *rest):
        go_ref, d_ref, mo_ref, vo_ref = rest[n_prev:]
        gg = g_ref[...]
        mn = ADAM_B1 * m_ref[...] + (1.0 - ADAM_B1) * gg
        vn = ADAM_B2 * v_ref[...] + (1.0 - ADAM_B2) * (gg * gg)
        go_ref[...] = gg
        mo_ref[...] = mn
        vo_ref[...] = vn
        d_ref[...] = -ADAM_LR * ((mn / bc1) / (jnp.sqrt(vn / bc2) + ADAM_EPS) + ADAM_WD * w_ref[...])

    spec = pl.BlockSpec((tr, n), lambda i: (layer * nb + i, 0))
    shp = jax.ShapeDtypeStruct(w.shape, F32)
    return pl.pallas_call(
        body, name=name, out_shape=(shp, shp, shp, shp), grid=(nb,),
        in_specs=[pl.BlockSpec((tr, n), lambda i: (i, 0)), spec, spec, spec] + [ANY] * n_prev,
        out_specs=(spec, spec, spec, spec),
        input_output_aliases={4 + k: k for k in range(n_prev)},
        compiler_params=_params("parallel"),
    )(g, w, m, v, *(prev or ()))


def _rope_tables(seq, n_ctx):
    f = HEAD // 4
    inv = ROPE_BASE ** (-jnp.arange(f, dtype=F32) / f)
    pos = jnp.arange(seq)
    ang_r = (pos // GRID_W).astype(F32)[:, None] * inv[None, :]
    ang_c = (pos % GRID_W).astype(F32)[:, None] * inv[None, :]
    cos = jnp.concatenate([jnp.cos(ang_r)] * 2 + [jnp.cos(ang_c)] * 2, axis=-1)
    sin = jnp.concatenate([-jnp.sin(ang_r), jnp.sin(ang_r), -jnp.sin(ang_c), jnp.sin(ang_c)], axis=-1)
    cos = jnp.concatenate([jnp.ones((n_ctx, HEAD), F32), cos], axis=0)
    sin = jnp.concatenate([jnp.zeros((n_ctx, HEAD), F32), sin], axis=0)
    return cos, sin


def _pad_rows(a, rows):
    return jnp.concatenate([a, jnp.zeros((rows - a.shape[0],) + a.shape[1:], a.dtype)], axis=0)


def _pad_flat(parts, mult):
    flat = jnp.concatenate([p.reshape(-1) for p in parts])
    pad = (-flat.shape[0]) % mult
    return jnp.concatenate([flat, jnp.zeros((pad,), flat.dtype)])


def kernel(x, c, ctx, c_ctx, norm_w, w_mod, b_mod, w_in, conv_w, conv_norm_w, ret_norm_w, ret_decay_f, ret_decay_b, w_out, final_norm_w, loss_target, m_c_ctx, m_norm_w, m_w_mod, m_b_mod, m_w_in, m_conv_w, m_conv_norm_w, m_ret_norm_w, m_ret_decay_f, m_ret_decay_b, m_w_out, m_final_norm_w, v_c_ctx, v_norm_w, v_w_mod, v_b_mod, v_w_in, v_conv_w, v_conv_norm_w, v_ret_norm_w, v_ret_decay_f, v_ret_decay_b, v_w_out, v_final_norm_w):
    xi, yi, ci = _mesh_pos()
    b_idx = 4 * xi + 2 * yi + ci
    s_idx = 2 * xi + yi
    pos = jnp.stack([ci, s_idx]).astype(jnp.int32)

    x2, ctx2, tgt = x[0], ctx[0], loss_target[0]
    seq, d = x2.shape
    n_ctx = ctx2.shape[0]
    t = seq + n_ctx
    dc = d // 2
    nh = dc // HEAD
    nl, _, nm = w_mod.shape
    n4 = w_in.shape[2]
    r4 = w_out.shape[1]
    cw4 = conv_w.shape[2]
    nct = n_ctx // ROW_TILE
    ncc = n_ctx // CHUNK
    assert n_ctx % ROW_TILE == 0 and seq % ROW_TILE == 0 and dc % HEAD == 0 and nl * 3 <= 8

    win_b, wout_b = w_in.astype(BF16), w_out.astype(BF16)
    wg = [None] * nl
    wg[0] = _run_comm(_gather_comm(win_b[0], wout_b[0]), "gather_weights_0")

    small = jnp.concatenate([jnp.broadcast_to(c, (8, d)), _pad_rows(conv_w.reshape(nl * 3, cw4), 8)], axis=1)
    small_g = _allgather_small(small, "gather_c_convw").reshape(N_DEV, 8, d + cw4)
    c_all = small_g[:, 0, :d]
    cw_full = small_g[0::2, :, d:].transpose(1, 0, 2).reshape(8, 4 * cw4)
    cw_l = [_pad_rows(cw_full[l * 3:(l + 1) * 3], 8) for l in range(nl)]

    c16 = _pad_rows(jnp.concatenate([c_all, c_ctx[None]], axis=0), 16)
    b_shard = lax.dynamic_slice(b_mod, (0, s_idx * nm), (nl, nm)).reshape(nl, 1, nm)
    sc16, modp = _mod_matvec(c16, w_mod, b_shard)
    modg = _allgather_small(modp.reshape(nl * 16, nm), "gather_mod").reshape(N_DEV, nl, 16, nm)
    mod_full = modg[0::2].transpose(1, 2, 0, 3).reshape(nl, 16, 4 * nm)
    mod_me = lax.dynamic_index_in_dim(mod_full, b_idx, axis=1, keepdims=False)
    mod_cx = mod_full[:, 8]
    zrow = jnp.zeros((d,), F32)
    rows = []
    for l in range(nl):
        gate_c = mod_cx[l, 2 * d:] if l < nl - 1 else zrow
        rows.append(jnp.stack([mod_me[l, :d], mod_me[l, d:2 * d], mod_me[l, 2 * d:],
                               mod_cx[l, :d], mod_cx[l, d:2 * d], gate_c, zrow, zrow]))

    cos, sin = _rope_tables(seq, n_ctx)
    xs = jnp.concatenate([ctx2, x2], axis=0)
    saved = []
    for l in range(nl):
        dec = jnp.stack([ret_decay_f[l], ret_decay_b[l]])
        hx, hx_t = _modulate(xs, norm_w[l][None], rows[l], nct, f"modulate_{l}")
        if l + 1 < nl:
            u, *wg[l + 1] = _matmul(hx, wg[l][0], "nn", f"in_proj_{l}",
                                    comms=[_gather_comm(win_b[l + 1], wout_b[l + 1])])
        else:
            u = _matmul(hx, wg[l][0], "nn", f"in_proj_{l}")
        o_f, o_b, st_f, st_b = _ret_fwd(u, cos, sin, dec, ncc, f"ret_fwd_{l}")
        y, y_t, cv = _mix_fwd(u, o_f, o_b, cw_l[l], conv_norm_w[l][None], ret_norm_w[l][None], nct, f"mix_fwd_{l}")
        xs_new, proj = _out_proj(y, wg[l][1], xs, rows[l], n_ctx, f"out_proj_{l}")
        saved.append((xs, hx_t, u, o_f, o_b, st_f, st_b, y_t, cv, proj, dec))
        xs = xs_new

    g, facc = _final_loss(xs, tgt, final_norm_w[None], nct, "final_loss")
    full = [None] * nl
    pend = None
    dmx, dmc, gnw, gcnw, grnw, gcw, gdf, gdb = ([None] * nl for _ in range(8))
    for l in reversed(range(nl)):
        xs_l, hx_t, u, o_f, o_b, st_f, st_b, y_t, cv, proj, dec = saved[l]
        gp, gacc = _gate_bwd(g, proj, rows[l], nct, f"gate_bwd_{l}")
        gout = _matmul(y_t, gp, "nn", f"out_proj_dw_{l}", tm_t=1024, tn_t=1024, tk_t=1408)
        dy = _matmul(gp, wg[l][1], "nt", f"out_proj_dx_{l}")
        dab, daz, drz, dcv, do, macc = _mix_bwd(dy, u, cv, o_f, o_b, conv_norm_w[l][None], ret_norm_w[l][None],
                                                f"mix_bwd_{l}")
        dqf, dkf, dvf, dqb, dkb, dvb, dlg = _ret_bwd(u, do, st_f, st_b, cos, sin, dec, ncc, f"ret_bwd_{l}")
        du, cacc = _assemble_du(dcv, u, cw_l[l], dab, daz, drz, dqf, dqb, dkf, dkb, dvf, dvb, cos, sin, nct,
                                f"assemble_du_{l}")
        join = []
        if pend is None:
            gin = _matmul(hx_t, du, "nn", f"in_proj_dw_{l}", tm_t=1024, tn_t=1024, tk_t=1408)
        else:
            gin, l3_in, l3_out = _matmul(hx_t, du, "nn", f"in_proj_dw_{l}", tm_t=1024, tn_t=1024, tk_t=1408,
                                         comms=[_exchange_comm(*pend, n4)])
            join = [_join_comm(_sum_quarters_in(pos, pend[0], l3_in, n4, f"grad_sum_quarters_in_{l + 1}"),
                               _sum_quarters_out(pos, pend[1], l3_out, f"grad_sum_quarters_out_{l + 1}"))]
        gin_v, gout_v = gin.reshape(2, d // 2, 4 * n4), gout.reshape(4, 2, r4 // 2, d)
        dhx, *landed = _matmul(du, wg[l][0], "nt", f"in_proj_dx_{l}", comms=join + [_swap_comm(gin_v, gout_v)])
        if join:
            full[l + 1], landed = landed[:2], landed[2:]
        g, nacc = _modulate_bwd(dhx, xs_l, g, norm_w[l][None], rows[l], nct, f"modulate_bwd_{l}", latent_only=l == 0)
        pend = (_sum_half_in(pos, gin_v, landed[0], f"grad_sum_half_in_{l}"),
                _sum_half_out(pos, gout_v, landed[1], f"grad_sum_half_out_{l}"))
        dmx[l] = jnp.concatenate([nacc[0], nacc[1], gacc[2]])
        dmc[l] = jnp.concatenate([nacc[3], nacc[4], gacc[5]])
        gnw[l], gcnw[l], grnw[l], gcw[l] = nacc[6], macc[0], macc[1], cacc[0:3]
        gdf[l], gdb[l] = dlg[0:nh, 0], dlg[nh:2 * nh, 0]
    grad_x = g[None]
    l3_in, l3_out = _run_comm(_exchange_comm(*pend, n4), "grad_exchange_0")
    full[0] = _run_comm(_join_comm(_sum_quarters_in(pos, pend[0], l3_in, n4, "grad_sum_quarters_in_0"),
                                   _sum_quarters_out(pos, pend[1], l3_out, "grad_sum_quarters_out_0")), "grad_join_0")

    res_in, res_out = None, None
    for l in reversed(range(nl)):
        res_in = _adamw_layer(full[l][0].reshape(d, n4), w_in.reshape(nl * d, n4), m_w_in.reshape(nl * d, n4),
                              v_w_in.reshape(nl * d, n4), l, res_in, f"adamw_w_in_{l}")
        res_out = _adamw_layer(full[l][1].reshape(r4, d), w_out.reshape(nl * r4, d), m_w_out.reshape(nl * r4, d),
                               v_w_out.reshape(nl * r4, d), l, res_out, f"adamw_w_out_{l}")
    g_w_in, d_w_in, nm_w_in, nv_w_in = res_in
    g_w_out, d_w_out, nm_w_out, nv_w_out = res_out

    seg = [jnp.stack(dmx), jnp.stack(dmc), jnp.stack(gnw), jnp.stack(gcnw), jnp.stack(grnw), facc[0],
           jnp.stack(gcw), jnp.stack(gdf), jnp.stack(gdb), facc[1, 0:1]]
    sizes = [int(s.size) for s in seg]
    offs = [sum(sizes[:i]) for i in range(len(sizes))]
    packed = _pad_flat(seg, 8 * 128)
    pw = packed.shape[0] // 8
    allv = _allgather_small(packed.reshape(8, pw), "gather_small_grads").reshape(N_DEV, 8 * pw)
    n_mod = nl * 3 * d
    dmx_all, dmc_all = allv[:, offs[0]:offs[0] + n_mod], allv[:, offs[1]:offs[1] + n_mod]
    rest_all = allv[:, offs[2]:]
    zeros_mod = jnp.zeros_like(dmc_all)
    stack = jnp.concatenate([
        jnp.concatenate([dmx_all, dmc_all], axis=0),
        jnp.concatenate([dmc_all, zeros_mod], axis=0),
        jnp.concatenate([rest_all, jnp.zeros_like(rest_all)], axis=0)], axis=1)
    sums = _rowsum(stack, "sum_small_grads")[0]
    g_b_mod = sums[:n_mod].reshape(nl, 3 * d)
    dmc_tot = sums[n_mod:2 * n_mod].reshape(nl, 3 * d)
    rest = sums[2 * n_mod:]

    def take(i):
        o = offs[i] - offs[2]
        return rest[o:o + sizes[i]]

    g_norm_w = take(2).reshape(nl, d)
    g_conv_norm_w = take(3).reshape(nl, dc)
    g_ret_norm_w = take(4).reshape(nl, dc)
    g_final_norm_w = take(5)
    g_conv_w = lax.dynamic_slice(take(6).reshape(nl, 3, dc), (0, 0, s_idx * cw4), (nl, 3, cw4))
    g_decay_f = take(7).reshape(nl, nh)
    g_decay_b = take(8).reshape(nl, nh)
    loss = take(9)[0]

    dmx_mine = lax.dynamic_slice(dmx_all.reshape(N_DEV, nl, 3 * d), (0, 0, s_idx * nm), (N_DEV, nl, nm))
    dmc_mine = lax.dynamic_slice(dmc_tot, (0, s_idx * nm), (nl, nm))
    dm = jnp.concatenate([dmx_mine.transpose(1, 0, 2), dmc_mine[:, None, :], jnp.zeros((nl, 7, nm), F32)], axis=1)
    g_w_mod, pc = _wmod_grad(sc16, dm, w_mod)
    pc_g = _allgather_small(pc[8:16], "gather_cctx_partials").reshape(N_DEV, 8, d)
    g_c_ctx = _cctx_grad(pc_g[0::2], jnp.broadcast_to(c_ctx[None], (8, d)))[0]
    d_w_mod, nm_w_mod, nv_w_mod = _adamw(g_w_mod.reshape(nl * d, nm), w_mod.reshape(nl * d, nm),
                                         m_w_mod.reshape(nl * d, nm), v_w_mod.reshape(nl * d, nm), "adamw_w_mod")

    small_g = [g_c_ctx, g_norm_w, g_b_mod, g_conv_w, g_conv_norm_w, g_ret_norm_w, g_decay_f, g_decay_b, g_final_norm_w]
    small_w = [c_ctx, norm_w, b_mod, conv_w, conv_norm_w, ret_norm_w, ret_decay_f, ret_decay_b, final_norm_w]
    small_m = [m_c_ctx, m_norm_w, m_b_mod, m_conv_w, m_conv_norm_w, m_ret_norm_w, m_ret_decay_f, m_ret_decay_b,
               m_final_norm_w]
    small_v = [v_c_ctx, v_norm_w, v_b_mod, v_conv_w, v_conv_norm_w, v_ret_norm_w, v_ret_decay_f, v_ret_decay_b,
               v_final_norm_w]
    pk = lambda parts: _pad_flat(parts, 8 * 128).reshape(-1, 128)
    sd, sm, sv = _adamw(pk(small_g), pk(small_w), pk(small_m), pk(small_v), "adamw_small")

    def unpack(flat2d):
        flat = flat2d.reshape(-1)
        out, o = [], 0
        for wgt in small_w:
            out.append(flat[o:o + wgt.size].reshape(wgt.shape))
            o += wgt.size
        return out

    sd, sm, sv = unpack(sd), unpack(sm), unpack(sv)

    def order(small, mod, w_in_, w_out_):
        return [small[0], small[1], mod, small[2], w_in_, small[3], small[4], small[5], small[6], small[7], w_out_,
                small[8]]

    grads = order(small_g, g_w_mod, g_w_in.reshape(nl, d, n4), g_w_out.reshape(nl, r4, d))
    deltas = order(sd, d_w_mod.reshape(nl, d, nm), d_w_in.reshape(nl, d, n4), d_w_out.reshape(nl, r4, d))
    new_m = order(sm, nm_w_mod.reshape(nl, d, nm), nm_w_in.reshape(nl, d, n4), nm_w_out.reshape(nl, r4, d))
    new_v = order(sv, nv_w_mod.reshape(nl, d, nm), nv_w_in.reshape(nl, d, n4), nv_w_out.reshape(nl, r4, d))
    return (loss, grad_x, *grads, *deltas, *new_m, *new_v)
```

```python
import functools

import jax
import jax.numpy as jnp
from jax import lax
from jax.experimental import pallas as pl
from jax.experimental.pallas import tpu as pltpu

F32 = jnp.float32
BF16 = jnp.bfloat16
EPS = 1e-6
CHUNK = 128
HEAD = 128
GRID_W = 64
ROPE_BASE = 10000.0
ROW_TILE = 256
RET_INTERLEAVE = 4
VMEM_LIMIT_BYTES = 56 * 1024 * 1024
N_DEV = 8
MESH = pl.DeviceIdType.MESH

ADAM_LR = 0.001
ADAM_B1 = 0.9
ADAM_B2 = 0.999
ADAM_EPS = 1e-08
ADAM_WD = 0.01
ADAM_STEP = 10

ANY = pl.BlockSpec(memory_space=pl.ANY)
VMEM_SPEC = pl.BlockSpec(memory_space=pltpu.VMEM)
SMEM_SPEC = pl.BlockSpec(memory_space=pltpu.SMEM)


def _params(*sem):
    return pltpu.CompilerParams(dimension_semantics=sem, vmem_limit_bytes=VMEM_LIMIT_BYTES)


def _pick(n, target, mult):
    best = None
    for d in range(mult, min(n, target) + 1, mult):
        if n % d == 0:
            best = d
    assert best is not None, (n, target, mult)
    return best


def _silu(z):
    return z * jax.nn.sigmoid(z)


def _dsilu(z):
    s = jax.nn.sigmoid(z)
    return s * (1.0 + z * (1.0 - s))


def _nn(a, b):
    return lax.dot_general(a, b, (((1,), (0,)), ((), ())), preferred_element_type=F32)


def _nt(a, b):
    return lax.dot_general(a, b, (((1,), (1,)), ((), ())), preferred_element_type=F32)


def _tn(a, b):
    return lax.dot_general(a, b, (((0,), (0,)), ((), ())), preferred_element_type=F32)


def _bf(a):
    return a.astype(BF16)


def _mesh_pos():
    return lax.axis_index("x"), lax.axis_index("y"), lax.axis_index("c")


def _allgather_small(blk, name):
    m_per, n = blk.shape

    def body(x_ref, out_ref, send_sems, recv_sems, local_sem):
        x, y, c = _mesh_pos()
        me, sibling = (x, y, c), (x, y, 1 - c)
        chips = [(1 - x, y), (x, 1 - y), (1 - x, 1 - y)]

        def rows(px, py, pc):
            return out_ref.at[pl.ds((4 * px + 2 * py + pc) * m_per, m_per), :]

        def copy(k, block, to, src=None):
            return pltpu.make_async_remote_copy(
                src_ref=rows(*block) if src is None else src, dst_ref=rows(*block),
                send_sem=send_sems.at[k], recv_sem=recv_sems.at[k], device_id=to, device_id_type=MESH)

        mine = pltpu.make_async_copy(x_ref, rows(*me), local_sem)
        mine.start()
        first = [copy(0, me, sibling, src=x_ref)]
        first += [copy(1 + j, me, (*chip, c), src=x_ref) for j, chip in enumerate(chips)]
        for cp in first:
            cp.start()
        passed = [copy(4 + j, (*chip, c), sibling) for j, chip in enumerate(chips)]
        for j, chip in enumerate(chips):
            copy(1 + j, (*chip, c), me).wait_recv()
            passed[j].start()
        copy(0, sibling, me).wait_recv()
        for j, chip in enumerate(chips):
            copy(4 + j, (*chip, 1 - c), me).wait_recv()
        for cp in first + passed:
            cp.wait_send()
        mine.wait()

    return pl.pallas_call(
        body, name=name,
        out_shape=jax.ShapeDtypeStruct((N_DEV * m_per, n), blk.dtype),
        in_specs=[VMEM_SPEC], out_specs=VMEM_SPEC,
        scratch_shapes=[pltpu.SemaphoreType.DMA((7,)), pltpu.SemaphoreType.DMA((7,)), pltpu.SemaphoreType.DMA],
    )(blk)


class _Comm:
    def __init__(self, ins, outs, n_remote, n_local, start, finish, aliases=None):
        self.ins, self.outs, self.n_remote, self.n_local = list(ins), list(outs), n_remote, n_local
        self.start, self.finish, self.aliases = start, finish, dict(aliases or {})

    def scratch(self):
        return [pltpu.SemaphoreType.DMA((self.n_remote,)), pltpu.SemaphoreType.DMA((self.n_remote,)),
                pltpu.SemaphoreType.DMA((max(self.n_local, 1),))]


def _run_comm(comm, name):
    n_in, n_out = len(comm.ins), len(comm.outs)

    def body(*refs):
        args = (refs[:n_in], refs[n_in:n_in + n_out]) + tuple(refs[n_in + n_out:])
        comm.start(*args)
        comm.finish(*args)

    return pl.pallas_call(
        body, name=name, out_shape=tuple(comm.outs), in_specs=[ANY] * n_in, out_specs=tuple([ANY] * n_out),
        scratch_shapes=comm.scratch(), input_output_aliases=comm.aliases,
    )(*comm.ins)


def _remote(src, dst, send, recv, k, to):
    return pltpu.make_async_remote_copy(src_ref=src, dst_ref=dst, send_sem=send.at[k], recv_sem=recv.at[k],
                                        device_id=to, device_id_type=MESH)


def _gather_comm(shard, by_columns):
    rows, cols = shard.shape
    hr = rows // 2

    def build(ins, outs, send, recv, loc):
        x, y, c = _mesh_pos()
        sib = (x, y, 1 - c)
        peers = [(1 - x, y), (x, 1 - y), (1 - x, 1 - y)]
        s_me = 2 * x + y

        def slot(s, h):
            if by_columns:
                return outs[0].at[pl.ds(pl.multiple_of(h * hr, 16), hr), pl.ds(pl.multiple_of(s * cols, 128), cols)]
            return outs[0].at[pl.ds(pl.multiple_of(s * rows + h * hr, 16), hr), :]

        def half(h):
            return ins[0].at[pl.ds(pl.multiple_of(h * hr, 16), hr), :]

        def local():
            if by_columns:
                mine = outs[0].at[:, pl.ds(pl.multiple_of(s_me * cols, 128), cols)]
            else:
                mine = outs[0].at[pl.ds(pl.multiple_of(s_me * rows, 16), rows), :]
            return [pltpu.make_async_copy(ins[0], mine, loc.at[0])]

        def per_copy(make):
            return [make(j, 2 * px + py, (px, py, c)) for j, (px, py) in enumerate(peers)]

        ici_send = lambda: per_copy(lambda j, s_p, to: _remote(half(c), slot(s_me, c), send, recv, j, to))
        ici_recv = lambda: per_copy(lambda j, s_p, to: _remote(half(c), slot(s_p, c), send, recv, j, to))
        d2d_send = lambda: per_copy(lambda j, s_p, to: _remote(slot(s_p, c), slot(s_p, c), send, recv, 3 + j, sib))
        d2d_recv = lambda: per_copy(lambda j, s_p, to: _remote(slot(s_p, c), slot(s_p, 1 - c), send, recv, 3 + j, sib))
        return local, ici_send, ici_recv, d2d_send, d2d_recv

    def start(*refs):
        local, ici_send, _, _, _ = build(*refs)
        for cp in local() + ici_send():
            cp.start()

    def finish(*refs):
        local, ici_send, ici_recv, d2d_send, d2d_recv = build(*refs)
        forwards = d2d_send()
        for arrived, forward in zip(ici_recv(), forwards):
            arrived.wait_recv()
            forward.start()
        for cp in d2d_recv():
            cp.wait_recv()
        for cp in ici_send() + forwards:
            cp.wait_send()
        for cp in local():
            cp.wait()

    full = (rows, 4 * cols) if by_columns else (4 * rows, cols)
    return _Comm([shard], [jax.ShapeDtypeStruct(full, BF16)], 6, 1, start, finish)


def _swap_comm(gin_other, gout_v):
    dh, w = gin_other.shape
    ns, _, rh, d = gout_v.shape

    def build(ins, outs, send, recv, loc):
        x, y, c = _mesh_pos()
        sib = (x, y, 1 - c)
        return [_remote(ins[0], outs[0], send, recv, 0, sib),
                _remote(ins[1].at[:, 1 - c], outs[1], send, recv, 1, sib)]

    def start(*refs):
        for cp in build(*refs):
            cp.start()

    def finish(*refs):
        for cp in build(*refs):
            cp.wait()

    outs = [jax.ShapeDtypeStruct((dh, w), F32), jax.ShapeDtypeStruct((ns, rh, d), F32)]
    return _Comm([gin_other, gout_v], outs, 2, 0, start, finish)


def _exchange_comm(p_in, p_out, n4):
    dh, _ = p_in.shape
    _, rh, d = p_out.shape

    def build(ins, outs, send, recv, loc):
        x, y, c = _mesh_pos()
        cps = []
        for j, (px, py) in enumerate([(1 - x, y), (x, 1 - y), (1 - x, 1 - y)]):
            s_p = 2 * px + py
            cps.append(_remote(ins[0].at[:, pl.ds(pl.multiple_of(s_p * n4, 128), n4)], outs[0].at[j], send, recv,
                               2 * j, (px, py, c)))
            cps.append(_remote(ins[1].at[s_p], outs[1].at[j], send, recv, 2 * j + 1, (px, py, c)))
        return cps

    def start(*refs):
        for cp in build(*refs):
            cp.start()

    def finish(*refs):
        cps = build(*refs)
        for cp in cps:
            cp.wait_recv()
        for cp in cps:
            cp.wait_send()

    outs = [jax.ShapeDtypeStruct((3, dh, n4), p_in.dtype), jax.ShapeDtypeStruct((3, rh, d), p_out.dtype)]
    return _Comm([p_in, p_out], outs, 6, 0, start, finish)


def _join_comm(q_in, q_out):
    def build(ins, outs, send, recv, loc):
        x, y, c = _mesh_pos()
        sib = (x, y, 1 - c)
        sends = lambda: [_remote(outs[a].at[c], outs[a].at[c], send, recv, a, sib) for a in range(2)]
        recvs = lambda: [_remote(outs[a].at[c], outs[a].at[1 - c], send, recv, a, sib) for a in range(2)]
        return sends, recvs

    def start(*refs):
        for cp in build(*refs)[0]():
            cp.start()

    def finish(*refs):
        sends, recvs = build(*refs)
        for cp in recvs():
            cp.wait_recv()
        for cp in sends():
            cp.wait_send()

    outs = [jax.ShapeDtypeStruct(q_in.shape, F32), jax.ShapeDtypeStruct(q_out.shape, F32)]
    return _Comm([q_in, q_out], outs, 2, 0, start, finish, aliases={0: 0, 1: 1})


def _sum_half_in(mine, land, name):
    dh, w = mine.shape
    tr, tc = _pick(dh, 256, 16), _pick(w, 2048, 128)

    def body(a_ref, b_ref, o_ref):
        o_ref[...] = (a_ref[...] + b_ref[...]).astype(BF16)

    spec = pl.BlockSpec((tr, tc), lambda i, j: (i, j))
    return pl.pallas_call(
        body, name=name, out_shape=jax.ShapeDtypeStruct((dh, w), BF16), grid=(dh // tr, w // tc),
        in_specs=[spec, spec], out_specs=spec, compiler_params=_params("parallel", "parallel"),
    )(mine, land)


def _sum_half_out(pos, gout_v, land, name):
    ns, _, rh, d = gout_v.shape

    def body(pos_ref, a_ref, b_ref, o_ref):
        o_ref[...] = (a_ref[...] + b_ref[...]).astype(BF16)

    return pl.pallas_call(
        body, name=name,
        out_shape=jax.ShapeDtypeStruct((ns, rh, d), BF16),
        grid_spec=pltpu.PrefetchScalarGridSpec(
            num_scalar_prefetch=1, grid=(ns,),
            in_specs=[pl.BlockSpec((None, None, rh, d), lambda s, p: (s, p[0], 0, 0)),
                      pl.BlockSpec((None, rh, d), lambda s, p: (s, 0, 0))],
            out_specs=pl.BlockSpec((None, rh, d), lambda s, p: (s, 0, 0))),
        compiler_params=_params("parallel"),
    )(pos, gout_v, land)


def _sum_quarters_in(pos, p_in, land3, n4, name):
    dh, _ = p_in.shape
    tr = _pick(dh, 128, 16)

    def body(pos_ref, a_ref, b_ref, o_ref):
        f = lambda v: v.astype(F32)
        o_ref[...] = ((f(a_ref[...]) + f(b_ref[0])) + f(b_ref[1])) + f(b_ref[2])

    return pl.pallas_call(
        body, name=name,
        out_shape=jax.ShapeDtypeStruct((2, dh, n4), F32),
        grid_spec=pltpu.PrefetchScalarGridSpec(
            num_scalar_prefetch=1, grid=(dh // tr,),
            in_specs=[pl.BlockSpec((tr, n4), lambda i, p: (i, p[1])),
                      pl.BlockSpec((3, tr, n4), lambda i, p: (0, i, 0))],
            out_specs=pl.BlockSpec((None, tr, n4), lambda i, p: (p[0], i, 0))),
        compiler_params=_params("parallel"),
    )(pos, p_in, land3)


def _sum_quarters_out(pos, p_out, land3, name):
    _, rh, d = p_out.shape

    def body(pos_ref, a_ref, b_ref, o_ref):
        f = lambda v: v.astype(F32)
        o_ref[...] = ((f(a_ref[...]) + f(b_ref[0])) + f(b_ref[1])) + f(b_ref[2])

    return pl.pallas_call(
        body, name=name,
        out_shape=jax.ShapeDtypeStruct((2, rh, d), F32),
        grid_spec=pltpu.PrefetchScalarGridSpec(
            num_scalar_prefetch=1, grid=(1,),
            in_specs=[pl.BlockSpec((None, rh, d), lambda i, p: (p[1], 0, 0)),
                      pl.BlockSpec((3, rh, d), lambda i, p: (0, 0, 0))],
            out_specs=pl.BlockSpec((None, rh, d), lambda i, p: (p[0], 0, 0))),
        compiler_params=_params("arbitrary"),
    )(pos, p_out, land3)


def _matmul(a, b, kind, name, tm_t=1056, tn_t=1024, tk_t=2048, comms=(), row_half=None):
    if kind == "nn":
        (m, k), n = a.shape, b.shape[1]
    elif kind == "nt":
        (m, k), n = a.shape, b.shape[0]
    else:
        (k, m), n = a.shape, b.shape[1]
    if row_half is None:
        row_half = jnp.zeros((1,), jnp.int32)
    else:
        assert kind == "nn"
        m = m // 2
    tm = _pick(m, tm_t, 128 if kind == "tn" else 16)
    tn = _pick(n, tn_t, 128)
    tk = _pick(k, tk_t, 16 if kind == "tn" else 128)
    nk = k // tk
    nbm = m // tm
    grid = (nbm, n // tn, nk)
    dot = {"nn": _nn, "nt": _nt, "tn": _tn}[kind]
    a_spec = {"nn": pl.BlockSpec((tm, tk), lambda i, j, q, p: (i + p[0] * nbm, q)),
              "nt": pl.BlockSpec((tm, tk), lambda i, j, q, p: (i, q)),
              "tn": pl.BlockSpec((tk, tm), lambda i, j, q, p: (q, i))}[kind]
    b_spec = {"nn": pl.BlockSpec((tk, tn), lambda i, j, q, p: (q, j)),
              "nt": pl.BlockSpec((tn, tk), lambda i, j, q, p: (j, q)),
              "tn": pl.BlockSpec((tk, tn), lambda i, j, q, p: (q, j))}[kind]
    comm_ins = [x for cm in comms for x in cm.ins]
    comm_outs = [x for cm in comms for x in cm.outs]
    aliases, scratch = {}, []
    i_off, o_off = 3, 1
    for cm in comms:
        for ci, co in cm.aliases.items():
            aliases[i_off + ci] = o_off + co
        i_off, o_off = i_off + len(cm.ins), o_off + len(cm.outs)
        scratch += cm.scratch()

    def body(half_ref, a_ref, b_ref, *rest):
        cin = rest[:len(comm_ins)]
        o_ref = rest[len(comm_ins)]
        cout = rest[len(comm_ins) + 1:len(comm_ins) + 1 + len(comm_outs)]
        acc_ref = rest[len(comm_ins) + 1 + len(comm_outs)]
        sems = rest[len(comm_ins) + 2 + len(comm_outs):]
        ids = [pl.program_id(ax) for ax in range(3)]
        first = functools.reduce(jnp.logical_and, [ids[ax] == 0 for ax in range(3)])
        last = functools.reduce(jnp.logical_and, [ids[ax] == grid[ax] - 1 for ax in range(3)])

        def comm_args(idx):
            i0 = sum(len(cm.ins) for cm in comms[:idx])
            o0 = sum(len(cm.outs) for cm in comms[:idx])
            cm = comms[idx]
            return (cin[i0:i0 + len(cm.ins)], cout[o0:o0 + len(cm.outs)]) + tuple(sems[3 * idx:3 * idx + 3])

        if comms:
            @pl.when(first)
            def _():
                for idx, cm in enumerate(comms):
                    cm.start(*comm_args(idx))

        q = ids[2]
        p = dot(a_ref[...], b_ref[...])
        if nk == 1:
            o_ref[...] = p
        else:
            @pl.when(q == 0)
            def _():
                acc_ref[...] = p

            @pl.when(q > 0)
            def _():
                acc_ref[...] += p

            @pl.when(q == nk - 1)
            def _():
                o_ref[...] = acc_ref[...]

        if comms:
            @pl.when(last)
            def _():
                for idx, cm in enumerate(comms):
                    cm.finish(*comm_args(idx))

    sem = ("arbitrary",) * 3 if comms else ("parallel", "parallel", "arbitrary")
    res = pl.pallas_call(
        body, name=name,
        out_shape=(jax.ShapeDtypeStruct((m, n), F32), *comm_outs),
        grid_spec=pltpu.PrefetchScalarGridSpec(
            num_scalar_prefetch=1, grid=grid,
            in_specs=[a_spec, b_spec] + [ANY] * len(comm_ins),
            out_specs=(pl.BlockSpec((tm, tn), lambda i, j, q, p: (i, j)), *([ANY] * len(comm_outs))),
            scratch_shapes=[pltpu.VMEM((tm, tn) if nk > 1 else (8, 128), F32)] + scratch),
        input_output_aliases=aliases,
        compiler_params=_params(*sem),
    )(row_half, a, b, *comm_ins)
    return res if comms else res[0]


def _out_proj(y, w_out, xs, rows, n_ctx, name):
    t, k = y.shape
    d = w_out.shape[1]
    tm, tn = _pick(t, 1056, 16), _pick(d, 1024, 128)

    def body(y_ref, w_ref, x_ref, r_ref, xo_ref, p_ref):
        i = pl.program_id(0)
        p = _nn(y_ref[...], w_ref[...])
        ridx = i * tm + lax.broadcasted_iota(jnp.int32, (tm, 1), 0)
        gate = jnp.where(ridx < n_ctx, r_ref[5:6, :], r_ref[2:3, :])
        p_ref[...] = p
        xo_ref[...] = x_ref[...] + gate * p

    return pl.pallas_call(
        body, name=name,
        out_shape=(jax.ShapeDtypeStruct((t, d), F32), jax.ShapeDtypeStruct((t, d), F32)),
        grid=(t // tm, d // tn),
        in_specs=[pl.BlockSpec((tm, k), lambda i, j: (i, 0)), pl.BlockSpec((k, tn), lambda i, j: (0, j)),
                  pl.BlockSpec((tm, tn), lambda i, j: (i, j)), pl.BlockSpec((8, tn), lambda i, j: (0, j))],
        out_specs=(pl.BlockSpec((tm, tn), lambda i, j: (i, j)), pl.BlockSpec((tm, tn), lambda i, j: (i, j))),
        compiler_params=_params("parallel", "parallel"),
    )(y, w_out, xs, rows)


def _modulate(xs, nw, rows, nct, name):
    t, d = xs.shape
    tr = ROW_TILE

    def body(x_ref, nw_ref, r_ref, o_ref, ot_ref):
        is_ctx = pl.program_id(0) < nct
        x = x_ref[...]
        xn = (x * lax.rsqrt(jnp.mean(x * x, axis=-1, keepdims=True) + EPS)) * nw_ref[...]
        shift = jnp.where(is_ctx, r_ref[3:4, :], r_ref[0:1, :])
        scale = jnp.where(is_ctx, r_ref[4:5, :], r_ref[1:2, :])
        hx = xn * (1.0 + scale) + shift
        o_ref[...] = hx.astype(BF16)
        ot_ref[...] = hx.T.astype(BF16)

    return pl.pallas_call(
        body, name=name,
        out_shape=(jax.ShapeDtypeStruct((t, d), BF16), jax.ShapeDtypeStruct((d, t), BF16)), grid=(t // tr,),
        in_specs=[pl.BlockSpec((tr, d), lambda i: (i, 0)), pl.BlockSpec((1, d), lambda i: (0, 0)),
                  pl.BlockSpec((8, d), lambda i: (0, 0))],
        out_specs=(pl.BlockSpec((tr, d), lambda i: (i, 0)), pl.BlockSpec((d, tr), lambda i: (0, i))),
        compiler_params=_params("parallel"),
    )(xs, nw, rows)


def _modulate_bwd(dhx, xs, g, nw, rows, nct, name, latent_only=False):
    t, d = xs.shape
    tr = ROW_TILE
    t_out = t - nct * tr if latent_only else t
    out_row = (lambda i: (jnp.maximum(i - nct, 0), 0)) if latent_only else (lambda i: (i, 0))

    def body(dh_ref, x_ref, g_ref, nw_ref, r_ref, go_ref, acc_ref):
        i = pl.program_id(0)
        is_ctx = i < nct

        @pl.when(i == 0)
        def _():
            acc_ref[...] = jnp.zeros_like(acc_ref)

        x, dh = x_ref[...], dh_ref[...]
        r = lax.rsqrt(jnp.mean(x * x, axis=-1, keepdims=True) + EPS)
        xh = x * r
        xn = xh * nw_ref[...]
        scale = jnp.where(is_ctx, r_ref[4:5, :], r_ref[1:2, :])
        d_shift = jnp.sum(dh, axis=0, keepdims=True)
        d_scale = jnp.sum(dh * xn, axis=0, keepdims=True)
        dxn = dh * (1.0 + scale)
        acc_ref[6:7, :] += jnp.sum(dxn * xh, axis=0, keepdims=True)
        dxh = dxn * nw_ref[...]
        dx = r * (dxh - xh * jnp.mean(dxh * xh, axis=-1, keepdims=True))
        go_ref[...] = g_ref[...] + dx
        zero = jnp.zeros_like(d_shift)
        acc_ref[0:1, :] += jnp.where(is_ctx, zero, d_shift)
        acc_ref[1:2, :] += jnp.where(is_ctx, zero, d_scale)
        acc_ref[3:4, :] += jnp.where(is_ctx, d_shift, zero)
        acc_ref[4:5, :] += jnp.where(is_ctx, d_scale, zero)

    return pl.pallas_call(
        body, name=name,
        out_shape=(jax.ShapeDtypeStruct((t_out, d), F32), jax.ShapeDtypeStruct((8, d), F32)), grid=(t // tr,),
        in_specs=[pl.BlockSpec((tr, d), lambda i: (i, 0)), pl.BlockSpec((tr, d), lambda i: (i, 0)),
                  pl.BlockSpec((tr, d), lambda i: (i, 0)), pl.BlockSpec((1, d), lambda i: (0, 0)),
                  pl.BlockSpec((8, d), lambda i: (0, 0))],
        out_specs=(pl.BlockSpec((tr, d), out_row), pl.BlockSpec((8, d), lambda i: (0, 0))),
        compiler_params=_params("arbitrary"),
    )(dhx, xs, g, nw, rows)


def _gate_bwd(g, proj, rows, nct, name):
    t, d = g.shape
    tr = ROW_TILE

    def body(g_ref, p_ref, r_ref, gp_ref, acc_ref):
        i = pl.program_id(0)
        is_ctx = i < nct

        @pl.when(i == 0)
        def _():
            acc_ref[...] = jnp.zeros_like(acc_ref)

        gg = g_ref[...]
        gate = jnp.where(is_ctx, r_ref[5:6, :], r_ref[2:3, :])
        gp_ref[...] = (gg * gate).astype(BF16)
        s = jnp.sum(gg * p_ref[...], axis=0, keepdims=True)
        zero = jnp.zeros_like(s)
        acc_ref[2:3, :] += jnp.where(is_ctx, zero, s)
        acc_ref[5:6, :] += jnp.where(is_ctx, s, zero)

    return pl.pallas_call(
        body, name=name,
        out_shape=(jax.ShapeDtypeStruct((t, d), BF16), jax.ShapeDtypeStruct((8, d), F32)), grid=(t // tr,),
        in_specs=[pl.BlockSpec((tr, d), lambda i: (i, 0)), pl.BlockSpec((tr, d), lambda i: (i, 0)),
                  pl.BlockSpec((8, d), lambda i: (0, 0))],
        out_specs=(pl.BlockSpec((tr, d), lambda i: (i, 0)), pl.BlockSpec((8, d), lambda i: (0, 0))),
        compiler_params=_params("arbitrary"),
    )(g, proj, rows)


def _final_loss(xs, tgt, fw, nct, name):
    t, d = xs.shape
    tr = ROW_TILE

    def body(x_ref, t_ref, w_ref, g_ref, acc_ref):
        i = pl.program_id(0)

        @pl.when(i == 0)
        def _():
            acc_ref[...] = jnp.zeros_like(acc_ref)

        @pl.when(i < nct)
        def _():
            g_ref[...] = jnp.zeros_like(g_ref)

        @pl.when(i >= nct)
        def _():
            x = x_ref[...]
            r = lax.rsqrt(jnp.mean(x * x, axis=-1, keepdims=True) + EPS)
            xh = x * r
            e = xh * w_ref[...] - t_ref[...]
            acc_ref[1:2, :] += jnp.zeros((1, d), F32) + jnp.sum(e * e) * (0.5 / d)
            dy = e * (1.0 / d)
            acc_ref[0:1, :] += jnp.sum(dy * xh, axis=0, keepdims=True)
            dxh = dy * w_ref[...]
            g_ref[...] = r * (dxh - xh * jnp.mean(dxh * xh, axis=-1, keepdims=True))

    return pl.pallas_call(
        body, name=name,
        out_shape=(jax.ShapeDtypeStruct((t, d), F32), jax.ShapeDtypeStruct((8, d), F32)), grid=(t // tr,),
        in_specs=[pl.BlockSpec((tr, d), lambda i: (i, 0)),
                  pl.BlockSpec((tr, d), lambda i: (jnp.maximum(i - nct, 0), 0)),
                  pl.BlockSpec((1, d), lambda i: (0, 0))],
        out_specs=(pl.BlockSpec((tr, d), lambda i: (i, 0)), pl.BlockSpec((8, d), lambda i: (0, 0))),
        compiler_params=_params("arbitrary"),
    )(xs, tgt, fw)


def _seq_edges(i, nct, nt):
    has_prev = jnp.logical_and(i != 0, i != nct)
    has_next = jnp.logical_and(i != nct - 1, i != nt - 1)
    return has_prev, has_next


def _shift_rows(a, before, after):
    tr = a.shape[0]
    ridx = lax.broadcasted_iota(jnp.int32, (tr, 1), 0)
    down = jnp.where(ridx == 0, before, pltpu.roll(a, 1, 0))
    up = jnp.where(ridx == tr - 1, after, pltpu.roll(a, tr - 1, 0))
    return down, up


def _halo_specs(tr, dc, col, nrow8):
    per = tr // 8
    prev = pl.BlockSpec((8, dc), lambda i: (jnp.maximum(i * per - 1, 0), col))
    nxt = pl.BlockSpec((8, dc), lambda i: (jnp.minimum((i + 1) * per, nrow8 - 1), col))
    return prev, nxt


def _mix_fwd(u, o_f, o_b, cw, cnw, rnw, nct, name):
    t, w = u.shape
    dc = w // 8
    nh = dc // HEAD
    tr = ROW_TILE
    nt = t // tr
    col = lambda cidx: pl.BlockSpec((tr, dc), lambda i: (i, cidx))
    acp, acn = _halo_specs(tr, dc, 2, t // 8)
    ahp, ahn = _halo_specs(tr, dc, 0, t // 8)

    def body(ah, ab, ac, az, rz, acp_r, acn_r, ahp_r, ahn_r, of_r, ob_r, cw_r, cnw_r, rnw_r, y_ref, yt_ref, cv_ref):
        i = pl.program_id(0)
        has_prev, has_next = _seq_edges(i, nct, nt)
        ch = ac[...] * ah[...]
        before = jnp.where(has_prev, acp_r[7:8, :] * ahp_r[7:8, :], 0.0)
        after = jnp.where(has_next, acn_r[0:1, :] * ahn_r[0:1, :], 0.0)
        down, up = _shift_rows(ch, before, after)
        cv = down * cw_r[0:1, :] + ch * cw_r[1:2, :] + up * cw_r[2:3, :]
        cv_ref[...] = cv
        y1 = ab[...] * cv
        yn = (y1 * lax.rsqrt(jnp.mean(y1 * y1, axis=-1, keepdims=True) + EPS)) * cnw_r[...]
        yc = _silu(az[...]) * yn
        y_ref[:, 0:dc] = yc.astype(BF16)
        yt_ref[0:dc, :] = yc.T.astype(BF16)
        for h in range(nh):
            sl = slice(h * HEAD, (h + 1) * HEAD)
            o = of_r[:, sl] + ob_r[:, sl]
            dev = o - jnp.mean(o, axis=-1, keepdims=True)
            on = dev * lax.rsqrt(jnp.mean(dev * dev, axis=-1, keepdims=True) + EPS)
            yr = _silu(rz[:, sl]) * (on * rnw_r[:, sl])
            y_ref[:, dc + h * HEAD:dc + (h + 1) * HEAD] = yr.astype(BF16)
            yt_ref[dc + h * HEAD:dc + (h + 1) * HEAD, :] = yr.T.astype(BF16)

    vec = pl.BlockSpec((1, dc), lambda i: (0, 0))
    return pl.pallas_call(
        body, name=name,
        out_shape=(jax.ShapeDtypeStruct((t, 2 * dc), BF16), jax.ShapeDtypeStruct((2 * dc, t), BF16),
                   jax.ShapeDtypeStruct((t, dc), F32)), grid=(nt,),
        in_specs=[col(0), col(1), col(2), col(3), col(7), acp, acn, ahp, ahn,
                  pl.BlockSpec((tr, dc), lambda i: (i, 0)), pl.BlockSpec((tr, dc), lambda i: (i, 0)),
                  pl.BlockSpec((8, dc), lambda i: (0, 0)), vec, vec],
        out_specs=(pl.BlockSpec((tr, 2 * dc), lambda i: (i, 0)), pl.BlockSpec((2 * dc, tr), lambda i: (0, i)),
                   pl.BlockSpec((tr, dc), lambda i: (i, 0))),
        compiler_params=_params("parallel"),
    )(u, u, u, u, u, u, u, u, u, o_f, o_b, cw, cnw, rnw)


def _mix_bwd(dy, u, cv, o_f, o_b, cnw, rnw, name):
    t, w = u.shape
    dc = w // 8
    nh = dc // HEAD
    tr = ROW_TILE
    col = lambda cidx: pl.BlockSpec((tr, dc), lambda i: (i, cidx))

    def body(dyc, dyr, ab, az, rz, cv_r, of_r, ob_r, cnw_r, rnw_r, dab, daz, drz, dcv, do, acc_ref):
        @pl.when(pl.program_id(0) == 0)
        def _():
            acc_ref[...] = jnp.zeros_like(acc_ref)

        z, b, c_out = az[...], ab[...], cv_r[...]
        y1 = b * c_out
        r = lax.rsqrt(jnp.mean(y1 * y1, axis=-1, keepdims=True) + EPS)
        yh = y1 * r
        yn = yh * cnw_r[...]
        dyc_v = dyc[...]
        daz[...] = (dyc_v * yn * _dsilu(z)).astype(BF16)
        dyn = dyc_v * _silu(z)
        acc_ref[0:1, :] += jnp.sum(dyn * yh, axis=0, keepdims=True)
        dyh = dyn * cnw_r[...]
        dy1 = r * (dyh - yh * jnp.mean(dyh * yh, axis=-1, keepdims=True))
        dab[...] = (dy1 * c_out).astype(BF16)
        dcv[...] = dy1 * b
        for h in range(nh):
            sl = slice(h * HEAD, (h + 1) * HEAD)
            o = of_r[:, sl] + ob_r[:, sl]
            dev = o - jnp.mean(o, axis=-1, keepdims=True)
            rs = lax.rsqrt(jnp.mean(dev * dev, axis=-1, keepdims=True) + EPS)
            on = dev * rs
            zz, dyr_v, gw = rz[:, sl], dyr[:, sl], rnw_r[:, sl]
            drz[:, sl] = (dyr_v * (on * gw) * _dsilu(zz)).astype(BF16)
            dong = dyr_v * _silu(zz)
            acc_ref[1:2, sl] += jnp.sum(dong * on, axis=0, keepdims=True)
            don = dong * gw
            do[:, sl] = rs * (don - jnp.mean(don, axis=-1, keepdims=True)
                              - on * jnp.mean(don * on, axis=-1, keepdims=True))

    vec = pl.BlockSpec((1, dc), lambda i: (0, 0))
    row = pl.BlockSpec((tr, dc), lambda i: (i, 0))
    return pl.pallas_call(
        body, name=name,
        out_shape=(jax.ShapeDtypeStruct((t, dc), BF16), jax.ShapeDtypeStruct((t, dc), BF16),
                   jax.ShapeDtypeStruct((t, dc), BF16), jax.ShapeDtypeStruct((t, dc), F32),
                   jax.ShapeDtypeStruct((t, dc), F32), jax.ShapeDtypeStruct((8, dc), F32)),
        grid=(t // tr,),
        in_specs=[col(0), col(1), col(1), col(3), col(7), row, row, row, vec, vec],
        out_specs=(row, row, row, row, row, pl.BlockSpec((8, dc), lambda i: (0, 0))),
        compiler_params=_params("arbitrary"),
    )(dy, dy, u, u, u, cv, o_f, o_b, cnw, rnw)


def _rope_masks():
    lane = lax.broadcasted_iota(jnp.int32, (1, HEAD), 1)
    return (lane % (HEAD // 2)) >= (HEAD // 4)


def _rot(a, upper):
    return jnp.where(upper, pltpu.roll(a, HEAD // 4, 1), pltpu.roll(a, HEAD - HEAD // 4, 1))


def _assemble_du(dcv, u, cw, dab, daz, drz, dq_f, dq_b, dk_f, dk_b, dv_f, dv_b, cos, sin, nct, name):
    t, w = u.shape
    dc = w // 8
    nh = dc // HEAD
    tr = ROW_TILE
    nt = t // tr
    k_scale = HEAD ** -0.5
    col = lambda cidx: pl.BlockSpec((tr, dc), lambda i: (i, cidx))
    row = pl.BlockSpec((tr, dc), lambda i: (i, 0))
    dcp, dcn = _halo_specs(tr, dc, 0, t // 8)

    def body(dcv_r, dcp_r, dcn_r, ah, ac, cw_r, dab_r, daz_r, drz_r, dqf, dqb, dkf, dkb, dvf, dvb, cos_r, sin_r,
             du, acc_ref):
        i = pl.program_id(0)

        @pl.when(i == 0)
        def _():
            acc_ref[...] = jnp.zeros_like(acc_ref)

        has_prev, has_next = _seq_edges(i, nct, nt)
        d_c = dcv_r[...]
        before = jnp.where(has_prev, dcp_r[7:8, :], 0.0)
        after = jnp.where(has_next, dcn_r[0:1, :], 0.0)
        d_prev, d_next = _shift_rows(d_c, before, after)
        a_h, a_c = ah[...], ac[...]
        ch = a_c * a_h
        dch = d_next * cw_r[0:1, :] + d_c * cw_r[1:2, :] + d_prev * cw_r[2:3, :]
        acc_ref[0:1, :] += jnp.sum(ch * d_next, axis=0, keepdims=True)
        acc_ref[1:2, :] += jnp.sum(ch * d_c, axis=0, keepdims=True)
        acc_ref[2:3, :] += jnp.sum(ch * d_prev, axis=0, keepdims=True)
        du[:, 0:dc] = (dch * a_c).astype(BF16)
        du[:, dc:2 * dc] = dab_r[...]
        du[:, 2 * dc:3 * dc] = (dch * a_h).astype(BF16)
        du[:, 3 * dc:4 * dc] = daz_r[...]
        du[:, 7 * dc:8 * dc] = drz_r[...]
        du[:, 6 * dc:7 * dc] = (dvf[...] + dvb[...]).astype(BF16)
        upper = _rope_masks()
        cs, sn = cos_r[...], sin_r[...]
        for h in range(nh):
            sl = slice(h * HEAD, (h + 1) * HEAD)
            dq = dqf[:, sl] + dqb[:, sl]
            dk = (dkf[:, sl] + dkb[:, sl]) * k_scale
            du[:, 4 * dc + h * HEAD:4 * dc + (h + 1) * HEAD] = (dq * cs + _rot(dq * sn, upper)).astype(BF16)
            du[:, 5 * dc + h * HEAD:5 * dc + (h + 1) * HEAD] = (dk * cs + _rot(dk * sn, upper)).astype(BF16)

    tab = pl.BlockSpec((tr, HEAD), lambda i: (i, 0))
    return pl.pallas_call(
        body, name=name,
        out_shape=(jax.ShapeDtypeStruct((t, w), BF16), jax.ShapeDtypeStruct((8, dc), F32)), grid=(nt,),
        in_specs=[row, dcp, dcn, col(0), col(2), pl.BlockSpec((8, dc), lambda i: (0, 0)),
                  row, row, row, row, row, row, row, row, row, tab, tab],
        out_specs=(pl.BlockSpec((tr, w), lambda i: (i, 0)), pl.BlockSpec((8, dc), lambda i: (0, 0))),
        compiler_params=_params("arbitrary"),
    )(dcv, dcv, dcv, u, u, cw, dab, daz, drz, dq_f, dq_b, dk_f, dk_b, dv_f, dv_b, cos, sin)


def _bwd_dir_chunk(s, ncc, ns):
    return jnp.where(s < ncc, ncc - 1 - s, ns + ncc - 1 - s)


def _position_weights(d):
    c = CHUNK
    ii = lax.broadcasted_iota(jnp.int32, (c, c), 0)
    jj = lax.broadcasted_iota(jnp.int32, (c, c), 1)
    sd = ((ii - jj) if d == 0 else (jj - ii)).astype(F32)
    pos = lax.broadcasted_iota(jnp.int32, (c, HEAD), 0).astype(F32)
    qw, kw = (pos + 1.0, (c - 1.0) - pos) if d == 0 else (c - pos, pos)
    return sd, qw, kw


def _fill_decay_tables(dec_ref, nh, dm_t, qd_t, kd_t, gc_t):
    for d in range(2):
        sd, qw, kw = _position_weights(d)
        for h in range(nh):
            r = d * nh + h
            lg = -jnp.exp(jnp.zeros((1, HEAD), F32) + dec_ref[d, h])
            dm_t[r] = jnp.where(sd >= 0, jnp.exp(lg * jnp.maximum(sd, 0.0)), 0.0)
            qd_t[r] = jnp.exp(lg * qw)
            kd_t[r] = jnp.exp(lg * kw)
            gc_t[r] = jnp.zeros((8, HEAD), F32) + jnp.exp(lg * float(CHUNK))


def _decay_table_scratch(nh):
    return [pltpu.VMEM((2 * nh, CHUNK, CHUNK), F32), pltpu.VMEM((2 * nh, CHUNK, HEAD), F32),
            pltpu.VMEM((2 * nh, CHUNK, HEAD), F32), pltpu.VMEM((2 * nh, 8, HEAD), F32)]


def _ret_fwd(u, cos, sin, dec, ncc, name):
    t, w = u.shape
    dc = w // 8
    nh = dc // HEAD
    c = CHUNK
    ns = t // c
    k_scale = HEAD ** -0.5
    fmap = lambda g: (lambda s: (s, g))
    bmap = lambda g: (lambda s: (_bwd_dir_chunk(s, ncc, ns), g))
    blk = lambda m: pl.BlockSpec((c, dc), m)
    tab = lambda m: pl.BlockSpec((c, HEAD), m)
    st_f = pl.BlockSpec((nh, None, HEAD, HEAD), lambda s: (0, s, 0, 0))
    st_b = pl.BlockSpec((nh, None, HEAD, HEAD), lambda s: (0, _bwd_dir_chunk(s, ncc, ns), 0, 0))

    def body(qf, kf, vf, qb, kb, vb, cf, sf, cb, sb, dec_ref, of_ref, ob_ref, stf_ref, stb_ref, qr_ref, kr_ref,
             s_f, s_b, dm_t, qd_t, kd_t, gc_t):
        @pl.when(pl.program_id(0) == 0)
        def _():
            s_f[...] = jnp.zeros_like(s_f)
            s_b[...] = jnp.zeros_like(s_b)
            _fill_decay_tables(dec_ref, nh, dm_t, qd_t, kd_t, gc_t)

        upper = _rope_masks()
        dirs = ((qf, kf, vf, cf, sf, of_ref, stf_ref, s_f), (qb, kb, vb, cb, sb, ob_ref, stb_ref, s_b))
        group = RET_INTERLEAVE if nh % RET_INTERLEAVE == 0 else 1
        blocks = [[(h, d) for h in range(h0, h0 + group) for d in range(2)] for h0 in range(0, nh, group)]
        for blk_list in blocks:
            first = []
            for h, d in blk_list:
                q_r, k_r, v_r, cs_r, sn_r, _, st_ref, st = dirs[d]
                sl = slice(h * HEAD, (h + 1) * HEAD)
                r = d * nh + h
                cs, sn = cs_r[...], sn_r[...]
                q = q_r[:, sl]
                q = q * cs + _rot(q, upper) * sn
                k = k_r[:, sl]
                k = (k * cs + _rot(k, upper) * sn) * k_scale
                if d == 0:
                    qr_ref[:, sl] = q
                    kr_ref[:, sl] = k
                vv = _bf(v_r[:, sl])
                state = st[h]
                st_ref[h] = state
                p = _nt(_bf(q), _bf(k)) * dm_t[r]
                kv = _tn(_bf(k * kd_t[r]), vv)
                first.append((q, vv, state, p, kv))
            for idx, (h, d) in enumerate(blk_list):
                _, _, _, _, _, o_ref, _, st = dirs[d]
                sl = slice(h * HEAD, (h + 1) * HEAD)
                r = d * nh + h
                q, vv, state, p, kv = first[idx]
                o_ref[:, sl] = _nn(jnp.concatenate([_bf(p), _bf(q * qd_t[r])], axis=1),
                                   jnp.concatenate([vv, _bf(state)], axis=0))
                st[h] = gc_t[r, 0:1, :] * state + kv

    return pl.pallas_call(
        body, name=name,
        out_shape=(jax.ShapeDtypeStruct((t, dc), F32), jax.ShapeDtypeStruct((t, dc), F32),
                   jax.ShapeDtypeStruct((nh, ns, HEAD, HEAD), F32), jax.ShapeDtypeStruct((nh, ns, HEAD, HEAD), F32),
                   jax.ShapeDtypeStruct((t, dc), F32), jax.ShapeDtypeStruct((t, dc), F32)),
        grid=(ns,),
        in_specs=[blk(fmap(4)), blk(fmap(5)), blk(fmap(6)), blk(bmap(4)), blk(bmap(5)), blk(bmap(6)),
                  tab(fmap(0)), tab(fmap(0)), tab(bmap(0)), tab(bmap(0)), SMEM_SPEC],
        out_specs=(blk(fmap(0)), blk(bmap(0)), st_f, st_b, blk(fmap(0)), blk(fmap(0))),
        scratch_shapes=[pltpu.VMEM((nh, HEAD, HEAD), F32), pltpu.VMEM((nh, HEAD, HEAD), F32)]
        + _decay_table_scratch(nh),
        compiler_params=_params("arbitrary"),
    )(u, u, u, u, u, u, cos, sin, cos, sin, dec)


def _ret_bwd(u, qr, kr, do, st_f, st_b, dec, ncc, name):
    t, w = u.shape
    dc = w // 8
    nh = dc // HEAD
    c = CHUNK
    ns = t // c
    fchunk = lambda s: ns - 1 - s
    bchunk = lambda s: _bwd_dir_chunk(ns - 1 - s, ncc, ns)
    fmap = lambda g: (lambda s: (fchunk(s), g))
    bmap = lambda g: (lambda s: (bchunk(s), g))
    blk = lambda m: pl.BlockSpec((c, dc), m)
    stf_spec = pl.BlockSpec((nh, None, HEAD, HEAD), lambda s: (0, fchunk(s), 0, 0))
    stb_spec = pl.BlockSpec((nh, None, HEAD, HEAD), lambda s: (0, bchunk(s), 0, 0))

    def body(qf, kf, vf, dof, qb, kb, vb, dob, stf_ref, stb_ref, dec_ref,
             dqf, dkf, dvf, dqb, dkb, dvb, dlg_ref, ds_f, ds_b, dm_t, qd_t, kd_t, gc_t, lg_acc):
        step = pl.program_id(0)

        @pl.when(step == 0)
        def _():
            ds_f[...] = jnp.zeros_like(ds_f)
            ds_b[...] = jnp.zeros_like(ds_b)
            lg_acc[...] = jnp.zeros_like(lg_acc)
            _fill_decay_tables(dec_ref, nh, dm_t, qd_t, kd_t, gc_t)

        dirs = ((qf, kf, vf, dof, stf_ref, dqf, dkf, dvf, ds_f), (qb, kb, vb, dob, stb_ref, dqb, dkb, dvb, ds_b))
        weights = [_position_weights(d) for d in range(2)]
        group = RET_INTERLEAVE if nh % RET_INTERLEAVE == 0 else 1
        blocks = [[(h, d) for h in range(h0, h0 + group) for d in range(2)] for h0 in range(0, nh, group)]
        for blk_list in blocks:
            first = []
            for h, d in blk_list:
                q_r, k_r, v_r, do_r, st_ref, _, _, _, ds = dirs[d]
                sl = slice(h * HEAD, (h + 1) * HEAD)
                r = d * nh + h
                q, k = q_r[:, sl], k_r[:, sl]
                qb16, kb16, vb16, dob16 = _bf(q), _bf(k), _bf(v_r[:, sl]), _bf(do_r[:, sl])
                state, dstate = st_ref[h], ds[h]
                dsb16 = _bf(dstate)
                a = _nt(qb16, kb16) * dm_t[r]
                both = _nt(dob16, jnp.concatenate([vb16, _bf(state)], axis=0))
                dk_i = _nt(vb16, dsb16) * kd_t[r]
                first.append((q, k, qb16, kb16, dob16, state, dstate, dsb16, a, both, dk_i))
            for idx, (h, d) in enumerate(blk_list):
                _, _, _, _, _, dq_ref, dk_ref, dv_ref, ds = dirs[d]
                sl = slice(h * HEAD, (h + 1) * HEAD)
                r = d * nh + h
                q, k, qb16, kb16, dob16, state, dstate, dsb16, a, both, dk_i = first[idx]
                sd, qw, kw = weights[d]
                da = both[:, 0:c]
                dp = _bf(da * dm_t[r])
                dq_i = both[:, c:2 * c] * qd_t[r]
                dq_ref[:, sl] = _nn(dp, kb16) + dq_i
                dk_ref[:, sl] = _tn(dp, qb16) + dk_i
                dv_ref[:, sl] = _nn(jnp.concatenate([_bf(a.T), _bf(k * kd_t[r])], axis=1),
                                    jnp.concatenate([dob16, dsb16], axis=0))
                gcr = gc_t[r, 0:1, :]
                ds[h] = gcr * dstate + _tn(_bf(q * qd_t[r]), dob16)
                lg_acc[r] += (da * a * jnp.maximum(sd, 0.0) + (dq_i * q) * qw + (dk_i * k) * kw
                              + (float(c) * gcr) * (state * dstate))

        @pl.when(step == ns - 1)
        def _():
            for d in range(2):
                for h in range(nh):
                    r = d * nh + h
                    lg = -jnp.exp(jnp.zeros((1, HEAD), F32) + dec_ref[d, h])
                    dlg_ref[r:r + 1, :] = (jnp.zeros((1, HEAD), F32) + jnp.sum(lg_acc[r])) * lg

    dshape = jax.ShapeDtypeStruct((t, dc), F32)
    return pl.pallas_call(
        body, name=name,
        out_shape=(dshape, dshape, dshape, dshape, dshape, dshape, jax.ShapeDtypeStruct((2 * nh, HEAD), F32)),
        grid=(ns,),
        in_specs=[blk(fmap(0)), blk(fmap(0)), blk(fmap(6)), blk(fmap(0)),
                  blk(bmap(0)), blk(bmap(0)), blk(bmap(6)), blk(bmap(0)), stf_spec, stb_spec, SMEM_SPEC],
        out_specs=(blk(fmap(0)), blk(fmap(0)), blk(fmap(0)), blk(bmap(0)), blk(bmap(0)), blk(bmap(0)),
                   pl.BlockSpec((2 * nh, HEAD), lambda s: (0, 0))),
        scratch_shapes=[pltpu.VMEM((nh, HEAD, HEAD), F32), pltpu.VMEM((nh, HEAD, HEAD), F32)]
        + _decay_table_scratch(nh) + [pltpu.VMEM((2 * nh, CHUNK, HEAD), F32)],
        compiler_params=_params("arbitrary"),
    )(qr, kr, u, do, qr, kr, u, do, st_f, st_b, dec)


def _mod_matvec(c16, w_mod, b_mod):
    nl, d, nm = w_mod.shape
    tn = _pick(nm, 512, 128)

    def body(c_ref, w_ref, b_ref, sc_ref, o_ref):
        sc = _silu(c_ref[...])
        sc_ref[...] = sc
        o_ref[...] = _nn(_bf(sc), _bf(w_ref[...])) + b_ref[...]

    return pl.pallas_call(
        body, name="mod_matvec",
        out_shape=(jax.ShapeDtypeStruct((16, d), F32), jax.ShapeDtypeStruct((nl, 16, nm), F32)),
        grid=(nl, nm // tn),
        in_specs=[pl.BlockSpec((16, d), lambda l, j: (0, 0)), pl.BlockSpec((None, d, tn), lambda l, j: (l, 0, j)),
                  pl.BlockSpec((None, 1, tn), lambda l, j: (l, 0, j))],
        out_specs=(pl.BlockSpec((16, d), lambda l, j: (0, 0)), pl.BlockSpec((None, 16, tn), lambda l, j: (l, 0, j))),
        compiler_params=_params("arbitrary", "arbitrary"),
    )(c16, w_mod, b_mod)


def _wmod_grad(sc16, dm, w_mod):
    nl, d, nm = w_mod.shape
    tn = _pick(nm, 512, 128)

    def body(sc_ref, dm_ref, w_ref, gw_ref, pc_ref):
        @pl.when(jnp.logical_and(pl.program_id(0) == 0, pl.program_id(1) == 0))
        def _():
            pc_ref[...] = jnp.zeros_like(pc_ref)

        dmb = _bf(dm_ref[...])
        gw_ref[...] = _tn(_bf(sc_ref[...]), dmb)
        pc_ref[...] += _nt(dmb, _bf(w_ref[...]))

    return pl.pallas_call(
        body, name="wmod_grad",
        out_shape=(jax.ShapeDtypeStruct((nl, d, nm), F32), jax.ShapeDtypeStruct((16, d), F32)),
        grid=(nl, nm // tn),
        in_specs=[pl.BlockSpec((16, d), lambda l, j: (0, 0)), pl.BlockSpec((None, 16, tn), lambda l, j: (l, 0, j)),
                  pl.BlockSpec((None, d, tn), lambda l, j: (l, 0, j))],
        out_specs=(pl.BlockSpec((None, d, tn), lambda l, j: (l, 0, j)), pl.BlockSpec((16, d), lambda l, j: (0, 0))),
        compiler_params=_params("arbitrary", "arbitrary"),
    )(sc16, dm, w_mod)


def _rowsum(a, name):
    r, n = a.shape
    tn = _pick(n, 4096, 128)

    def body(a_ref, o_ref):
        acc = a_ref[0:1, :]
        for i in range(1, r):
            acc = acc + a_ref[i:i + 1, :]
        o_ref[...] = jnp.zeros((8, tn), F32) + acc

    return pl.pallas_call(
        body, name=name, out_shape=jax.ShapeDtypeStruct((8, n), F32), grid=(n // tn,),
        in_specs=[pl.BlockSpec((r, tn), lambda j: (0, j))], out_specs=pl.BlockSpec((8, tn), lambda j: (0, j)),
        compiler_params=_params("parallel"),
    )(a)


def _cctx_grad(parts, c_ctx8):
    _, _, d = parts.shape

    def body(p_ref, c_ref, o_ref):
        acc = ((p_ref[0] + p_ref[1]) + p_ref[2]) + p_ref[3]
        o_ref[...] = acc * _dsilu(c_ref[...])

    return pl.pallas_call(
        body, name="cctx_grad", out_shape=jax.ShapeDtypeStruct((8, d), F32),
        in_specs=[VMEM_SPEC, VMEM_SPEC], out_specs=VMEM_SPEC,
    )(parts, c_ctx8)


def _adamw(g, w, m, v, name):
    r, n = w.shape
    tr = _pick(r, max(8, min(256, (512 * 1024) // n // 8 * 8)), 8)
    bc1 = 1.0 - ADAM_B1 ** ADAM_STEP
    bc2 = 1.0 - ADAM_B2 ** ADAM_STEP

    def body(g_ref, w_ref, m_ref, v_ref, d_ref, mo_ref, vo_ref):
        gg = g_ref[...]
        mn = ADAM_B1 * m_ref[...] + (1.0 - ADAM_B1) * gg
        vn = ADAM_B2 * v_ref[...] + (1.0 - ADAM_B2) * (gg * gg)
        mo_ref[...] = mn
        vo_ref[...] = vn
        d_ref[...] = -ADAM_LR * ((mn / bc1) / (jnp.sqrt(vn / bc2) + ADAM_EPS) + ADAM_WD * w_ref[...])

    spec = pl.BlockSpec((tr, n), lambda i: (i, 0))
    shp = jax.ShapeDtypeStruct((r, n), F32)
    return pl.pallas_call(
        body, name=name, out_shape=(shp, shp, shp), grid=(r // tr,),
        in_specs=[spec, spec, spec, spec], out_specs=(spec, spec, spec),
        compiler_params=_params("parallel"),
    )(g, w, m, v)


def _adamw_layer(g, w, m, v, layer, prev, name):
    r, n = g.shape
    tr = _pick(r, max(8, min(256, (512 * 1024) // n // 8 * 8)), 8)
    nb = r // tr
    bc1 = 1.0 - ADAM_B1 ** ADAM_STEP
    bc2 = 1.0 - ADAM_B2 ** ADAM_STEP
    n_prev = 0 if prev is None else 4

    def body(g_ref, w_ref, m_ref, v_ref, *rest):
        go_ref, d_ref, mo_ref, vo_ref = rest[n_prev:]
        gg = g_ref[...]
        mn = ADAM_B1 * m_ref[...] + (1.0 - ADAM_B1) * gg
        vn = ADAM_B2 * v_ref[...] + (1.0 - ADAM_B2) * (gg * gg)
        go_ref[...] = gg
        mo_ref[...] = mn
        vo_ref[...] = vn
        d_ref[...] = -ADAM_LR * ((mn / bc1) / (jnp.sqrt(vn / bc2) + ADAM_EPS) + ADAM_WD * w_ref[...])

    spec = pl.BlockSpec((tr, n), lambda i: (layer * nb + i, 0))
    shp = jax.ShapeDtypeStruct(w.shape, F32)
    return pl.pallas_call(
        body, name=name, out_shape=(shp, shp, shp, shp), grid=(nb,),
        in_specs=[pl.BlockSpec((tr, n), lambda i: (i, 0)), spec, spec, spec] + [ANY] * n_prev,
        out_specs=(spec, spec, spec, spec),
        input_output_aliases={4 + k: k for k in range(n_prev)},
        compiler_params=_params("parallel"),
    )(g, w, m, v, *(prev or ()))


def _rope_tables(seq, n_ctx):
    f = HEAD // 4
    inv = ROPE_BASE ** (-jnp.arange(f, dtype=F32) / f)
    pos = jnp.arange(seq)
    ang_r = (pos // GRID_W).astype(F32)[:, None] * inv[None, :]
    ang_c = (pos % GRID_W).astype(F32)[:, None] * inv[None, :]
    cos = jnp.concatenate([jnp.cos(ang_r)] * 2 + [jnp.cos(ang_c)] * 2, axis=-1)
    sin = jnp.concatenate([-jnp.sin(ang_r), jnp.sin(ang_r), -jnp.sin(ang_c), jnp.sin(ang_c)], axis=-1)
    cos = jnp.concatenate([jnp.ones((n_ctx, HEAD), F32), cos], axis=0)
    sin = jnp.concatenate([jnp.zeros((n_ctx, HEAD), F32), sin], axis=0)
    return cos, sin


def _pad_rows(a, rows):
    return jnp.concatenate([a, jnp.zeros((rows - a.shape[0],) + a.shape[1:], a.dtype)], axis=0)


def _pad_flat(parts, mult):
    flat = jnp.concatenate([p.reshape(-1) for p in parts])
    pad = (-flat.shape[0]) % mult
    return jnp.concatenate([flat, jnp.zeros((pad,), flat.dtype)])


def kernel(x, c, ctx, c_ctx, norm_w, w_mod, b_mod, w_in, conv_w, conv_norm_w, ret_norm_w, ret_decay_f, ret_decay_b, w_out, final_norm_w, loss_target, m_c_ctx, m_norm_w, m_w_mod, m_b_mod, m_w_in, m_conv_w, m_conv_norm_w, m_ret_norm_w, m_ret_decay_f, m_ret_decay_b, m_w_out, m_final_norm_w, v_c_ctx, v_norm_w, v_w_mod, v_b_mod, v_w_in, v_conv_w, v_conv_norm_w, v_ret_norm_w, v_ret_decay_f, v_ret_decay_b, v_w_out, v_final_norm_w):
    xi, yi, ci = _mesh_pos()
    b_idx = 4 * xi + 2 * yi + ci
    s_idx = 2 * xi + yi
    pos = jnp.stack([ci, s_idx]).astype(jnp.int32)

    x2, ctx2, tgt = x[0], ctx[0], loss_target[0]
    seq, d = x2.shape
    n_ctx = ctx2.shape[0]
    t = seq + n_ctx
    dc = d // 2
    nh = dc // HEAD
    nl, _, nm = w_mod.shape
    n4 = w_in.shape[2]
    r4 = w_out.shape[1]
    cw4 = conv_w.shape[2]
    nct = n_ctx // ROW_TILE
    ncc = n_ctx // CHUNK
    assert n_ctx % ROW_TILE == 0 and seq % ROW_TILE == 0 and dc % HEAD == 0 and nl * 3 <= 8

    win_b, wout_b = w_in.astype(BF16), w_out.astype(BF16)
    wg_in, wg_out = [None] * nl, [None] * nl
    wg_in[0], = _run_comm(_gather_comm(win_b[0], True), "gather_w_in_0")

    small = jnp.concatenate([jnp.broadcast_to(c, (8, d)), _pad_rows(conv_w.reshape(nl * 3, cw4), 8)], axis=1)
    small_g = _allgather_small(small, "gather_c_convw").reshape(N_DEV, 8, d + cw4)
    c_all = small_g[:, 0, :d]
    cw_full = small_g[0::2, :, d:].transpose(1, 0, 2).reshape(8, 4 * cw4)
    cw_l = [_pad_rows(cw_full[l * 3:(l + 1) * 3], 8) for l in range(nl)]

    c16 = _pad_rows(jnp.concatenate([c_all, c_ctx[None]], axis=0), 16)
    b_shard = lax.dynamic_slice(b_mod, (0, s_idx * nm), (nl, nm)).reshape(nl, 1, nm)
    sc16, modp = _mod_matvec(c16, w_mod, b_shard)
    modg = _allgather_small(modp.reshape(nl * 16, nm), "gather_mod").reshape(N_DEV, nl, 16, nm)
    mod_full = modg[0::2].transpose(1, 2, 0, 3).reshape(nl, 16, 4 * nm)
    mod_me = lax.dynamic_index_in_dim(mod_full, b_idx, axis=1, keepdims=False)
    mod_cx = mod_full[:, 8]
    zrow = jnp.zeros((d,), F32)
    rows = []
    for l in range(nl):
        gate_c = mod_cx[l, 2 * d:] if l < nl - 1 else zrow
        rows.append(jnp.stack([mod_me[l, :d], mod_me[l, d:2 * d], mod_me[l, 2 * d:],
                               mod_cx[l, :d], mod_cx[l, d:2 * d], gate_c, zrow, zrow]))

    cos, sin = _rope_tables(seq, n_ctx)
    xs = jnp.concatenate([ctx2, x2], axis=0)
    saved = []
    for l in range(nl):
        dec = jnp.stack([ret_decay_f[l], ret_decay_b[l]])
        hx, hx_t = _modulate(xs, norm_w[l][None], rows[l], nct, f"modulate_{l}")
        comms = [_gather_comm(wout_b[l], False)]
        if l + 1 < nl:
            comms += [_gather_comm(win_b[l + 1], True)]
        u, wg_out[l], *nxt = _matmul(hx, wg_in[l], "nn", f"in_proj_{l}", comms=comms)
        if nxt:
            wg_in[l + 1] = nxt[0]
        o_f, o_b, st_f, st_b, qr, kr = _ret_fwd(u, cos, sin, dec, ncc, f"ret_fwd_{l}")
        y, y_t, cv = _mix_fwd(u, o_f, o_b, cw_l[l], conv_norm_w[l][None], ret_norm_w[l][None], nct, f"mix_fwd_{l}")
        xs_new, proj = _out_proj(y, wg_out[l], xs, rows[l], n_ctx, f"out_proj_{l}")
        saved.append((xs, hx_t, u, o_f, o_b, st_f, st_b, qr, kr, y_t, cv, proj, dec))
        xs = xs_new

    g, facc = _final_loss(xs, tgt, final_norm_w[None], nct, "final_loss")
    full = [None] * nl
    join = []
    mine, other = ci.reshape(1).astype(jnp.int32), (1 - ci).reshape(1).astype(jnp.int32)
    dmx, dmc, gnw, gcnw, grnw, gcw, gdf, gdb = ([None] * nl for _ in range(8))
    for l in reversed(range(nl)):
        xs_l, hx_t, u, o_f, o_b, st_f, st_b, qr, kr, y_t, cv, proj, dec = saved[l]
        gp, gacc = _gate_bwd(g, proj, rows[l], nct, f"gate_bwd_{l}")
        gout = _matmul(y_t, gp, "nn", f"out_proj_dw_{l}", tm_t=1024, tn_t=1024, tk_t=1408, comms=join)
        if join:
            gout, *full[l + 1] = gout
        dy = _matmul(gp, wg_out[l], "nt", f"out_proj_dx_{l}")
        dab, daz, drz, dcv, do, macc = _mix_bwd(dy, u, cv, o_f, o_b, conv_norm_w[l][None], ret_norm_w[l][None],
                                                f"mix_bwd_{l}")
        dqf, dkf, dvf, dqb, dkb, dvb, dlg = _ret_bwd(u, qr, kr, do, st_f, st_b, dec, ncc, f"ret_bwd_{l}")
        du, cacc = _assemble_du(dcv, u, cw_l[l], dab, daz, drz, dqf, dqb, dkf, dkb, dvf, dvb, cos, sin, nct,
                                f"assemble_du_{l}")
        gout_v = gout.reshape(4, 2, r4 // 2, d)
        gin_other = _matmul(hx_t, du, "nn", f"in_proj_dw_other_{l}", tm_t=1024, tn_t=1024, tk_t=1408, row_half=other)
        gin_mine, land_in, land_out = _matmul(hx_t, du, "nn", f"in_proj_dw_mine_{l}", tm_t=1024, tn_t=1024,
                                              tk_t=1408, row_half=mine, comms=[_swap_comm(gin_other, gout_v)])
        p_in = _sum_half_in(gin_mine, land_in, f"grad_sum_half_in_{l}")
        p_out = _sum_half_out(pos, gout_v, land_out, f"grad_sum_half_out_{l}")
        dhx, l3_in, l3_out = _matmul(du, wg_in[l], "nt", f"in_proj_dx_{l}", comms=[_exchange_comm(p_in, p_out, n4)])
        join = [_join_comm(_sum_quarters_in(pos, p_in, l3_in, n4, f"grad_sum_quarters_in_{l}"),
                           _sum_quarters_out(pos, p_out, l3_out, f"grad_sum_quarters_out_{l}"))]
        g, nacc = _modulate_bwd(dhx, xs_l, g, norm_w[l][None], rows[l], nct, f"modulate_bwd_{l}", latent_only=l == 0)
        dmx[l] = jnp.concatenate([nacc[0], nacc[1], gacc[2]])
        dmc[l] = jnp.concatenate([nacc[3], nacc[4], gacc[5]])
        gnw[l], gcnw[l], grnw[l], gcw[l] = nacc[6], macc[0], macc[1], cacc[0:3]
        gdf[l], gdb[l] = dlg[0:nh, 0], dlg[nh:2 * nh, 0]
    grad_x = g[None]
    full[0] = _run_comm(join[0], "grad_join_0")

    res_in, res_out = None, None
    for l in reversed(range(nl)):
        res_in = _adamw_layer(full[l][0].reshape(d, n4), w_in.reshape(nl * d, n4), m_w_in.reshape(nl * d, n4),
                              v_w_in.reshape(nl * d, n4), l, res_in, f"adamw_w_in_{l}")
        res_out = _adamw_layer(full[l][1].reshape(r4, d), w_out.reshape(nl * r4, d), m_w_out.reshape(nl * r4, d),
                               v_w_out.reshape(nl * r4, d), l, res_out, f"adamw_w_out_{l}")
    g_w_in, d_w_in, nm_w_in, nv_w_in = res_in
    g_w_out, d_w_out, nm_w_out, nv_w_out = res_out

    seg = [jnp.stack(dmx), jnp.stack(dmc), jnp.stack(gnw), jnp.stack(gcnw), jnp.stack(grnw), facc[0],
           jnp.stack(gcw), jnp.stack(gdf), jnp.stack(gdb), facc[1, 0:1]]
    sizes = [int(s.size) for s in seg]
    offs = [sum(sizes[:i]) for i in range(len(sizes))]
    packed = _pad_flat(seg, 8 * 128)
    pw = packed.shape[0] // 8
    allv = _allgather_small(packed.reshape(8, pw), "gather_small_grads").reshape(N_DEV, 8 * pw)
    n_mod = nl * 3 * d
    dmx_all, dmc_all = allv[:, offs[0]:offs[0] + n_mod], allv[:, offs[1]:offs[1] + n_mod]
    rest_all = allv[:, offs[2]:]
    zeros_mod = jnp.zeros_like(dmc_all)
    stack = jnp.concatenate([
        jnp.concatenate([dmx_all, dmc_all], axis=0),
        jnp.concatenate([dmc_all, zeros_mod], axis=0),
        jnp.concatenate([rest_all, jnp.zeros_like(rest_all)], axis=0)], axis=1)
    sums = _rowsum(stack, "sum_small_grads")[0]
    g_b_mod = sums[:n_mod].reshape(nl, 3 * d)
    dmc_tot = sums[n_mod:2 * n_mod].reshape(nl, 3 * d)
    rest = sums[2 * n_mod:]

    def take(i):
        o = offs[i] - offs[2]
        return rest[o:o + sizes[i]]

    g_norm_w = take(2).reshape(nl, d)
    g_conv_norm_w = take(3).reshape(nl, dc)
    g_ret_norm_w = take(4).reshape(nl, dc)
    g_final_norm_w = take(5)
    g_conv_w = lax.dynamic_slice(take(6).reshape(nl, 3, dc), (0, 0, s_idx * cw4), (nl, 3, cw4))
    g_decay_f = take(7).reshape(nl, nh)
    g_decay_b = take(8).reshape(nl, nh)
    loss = take(9)[0]

    dmx_mine = lax.dynamic_slice(dmx_all.reshape(N_DEV, nl, 3 * d), (0, 0, s_idx * nm), (N_DEV, nl, nm))
    dmc_mine = lax.dynamic_slice(dmc_tot, (0, s_idx * nm), (nl, nm))
    dm = jnp.concatenate([dmx_mine.transpose(1, 0, 2), dmc_mine[:, None, :], jnp.zeros((nl, 7, nm), F32)], axis=1)
    g_w_mod, pc = _wmod_grad(sc16, dm, w_mod)
    pc_g = _allgather_small(pc[8:16], "gather_cctx_partials").reshape(N_DEV, 8, d)
    g_c_ctx = _cctx_grad(pc_g[0::2], jnp.broadcast_to(c_ctx[None], (8, d)))[0]
    d_w_mod, nm_w_mod, nv_w_mod = _adamw(g_w_mod.reshape(nl * d, nm), w_mod.reshape(nl * d, nm),
                                         m_w_mod.reshape(nl * d, nm), v_w_mod.reshape(nl * d, nm), "adamw_w_mod")

    small_g = [g_c_ctx, g_norm_w, g_b_mod, g_conv_w, g_conv_norm_w, g_ret_norm_w, g_decay_f, g_decay_b, g_final_norm_w]
    small_w = [c_ctx, norm_w, b_mod, conv_w, conv_norm_w, ret_norm_w, ret_decay_f, ret_decay_b, final_norm_w]
    small_m = [m_c_ctx, m_norm_w, m_b_mod, m_conv_w, m_conv_norm_w, m_ret_norm_w, m_ret_decay_f, m_ret_decay_b,
               m_final_norm_w]
    small_v = [v_c_ctx, v_norm_w, v_b_mod, v_conv_w, v_conv_norm_w, v_ret_norm_w, v_ret_decay_f, v_ret_decay_b,
               v_final_norm_w]
    pk = lambda parts: _pad_flat(parts, 8 * 128).reshape(-1, 128)
    sd, sm, sv = _adamw(pk(small_g), pk(small_w), pk(small_m), pk(small_v), "adamw_small")

    def unpack(flat2d):
        flat = flat2d.reshape(-1)
        out, o = [], 0
        for wgt in small_w:
            out.append(flat[o:o + wgt.size].reshape(wgt.shape))
            o += wgt.size
        return out

    sd, sm, sv = unpack(sd), unpack(sm), unpack(sv)

    def order(small, mod, w_in_, w_out_):
        return [small[0], small[1], mod, small[2], w_in_, small[3], small[4], small[5], small[6], small[7], w_out_,
                small[8]]

    grads = order(small_g, g_w_mod, g_w_in.reshape(nl, d, n4), g_w_out.reshape(nl, r4, d))
    deltas = order(sd, d_w_mod.reshape(nl, d, nm), d_w_in.reshape(nl, d, n4), d_w_out.reshape(nl, r4, d))
    new_m = order(sm, nm_w_mod.reshape(nl, d, nm), nm_w_in.reshape(nl, d, n4), nm_w_out.reshape(nl, r4, d))
    new_v = order(sv, nv_w_mod.reshape(nl, d, nm), nv_w_in.reshape(nl, d, n4), nv_w_out.reshape(nl, r4, d))
    return (loss, grad_x, *grads, *deltas, *new_m, *new_v)
```

```python
import functools

import jax
import jax.numpy as jnp
from jax import lax
from jax.experimental import pallas as pl
from jax.experimental.pallas import tpu as pltpu

F32 = jnp.float32
BF16 = jnp.bfloat16
EPS = 1e-6
CHUNK = 128
HEAD = 128
GRID_W = 64
ROPE_BASE = 10000.0
ROW_TILE = 256
RET_INTERLEAVE = 4
VMEM_LIMIT_BYTES = 56 * 1024 * 1024
N_DEV = 8
MESH = pl.DeviceIdType.MESH

ADAM_LR = 0.001
ADAM_B1 = 0.9
ADAM_B2 = 0.999
ADAM_EPS = 1e-08
ADAM_WD = 0.01
ADAM_STEP = 10

ANY = pl.BlockSpec(memory_space=pl.ANY)
VMEM_SPEC = pl.BlockSpec(memory_space=pltpu.VMEM)
SMEM_SPEC = pl.BlockSpec(memory_space=pltpu.SMEM)


def _params(*sem):
    return pltpu.CompilerParams(dimension_semantics=sem, vmem_limit_bytes=VMEM_LIMIT_BYTES)


def _pick(n, target, mult):
    best = None
    for d in range(mult, min(n, target) + 1, mult):
        if n % d == 0:
            best = d
    assert best is not None, (n, target, mult)
    return best


def _silu(z):
    return z * jax.nn.sigmoid(z)


def _dsilu(z):
    s = jax.nn.sigmoid(z)
    return s * (1.0 + z * (1.0 - s))


def _nn(a, b):
    return lax.dot_general(a, b, (((1,), (0,)), ((), ())), preferred_element_type=F32)


def _nt(a, b):
    return lax.dot_general(a, b, (((1,), (1,)), ((), ())), preferred_element_type=F32)


def _tn(a, b):
    return lax.dot_general(a, b, (((0,), (0,)), ((), ())), preferred_element_type=F32)


def _bf(a):
    return a.astype(BF16)


def _mesh_pos():
    return lax.axis_index("x"), lax.axis_index("y"), lax.axis_index("c")


def _allgather_small(blk, name):
    m_per, n = blk.shape

    def body(x_ref, out_ref, send_sems, recv_sems, local_sem):
        x, y, c = _mesh_pos()
        me, sibling = (x, y, c), (x, y, 1 - c)
        chips = [(1 - x, y), (x, 1 - y), (1 - x, 1 - y)]

        def rows(px, py, pc):
            return out_ref.at[pl.ds((4 * px + 2 * py + pc) * m_per, m_per), :]

        def copy(k, block, to, src=None):
            return pltpu.make_async_remote_copy(
                src_ref=rows(*block) if src is None else src, dst_ref=rows(*block),
                send_sem=send_sems.at[k], recv_sem=recv_sems.at[k], device_id=to, device_id_type=MESH)

        mine = pltpu.make_async_copy(x_ref, rows(*me), local_sem)
        mine.start()
        first = [copy(0, me, sibling, src=x_ref)]
        first += [copy(1 + j, me, (*chip, c), src=x_ref) for j, chip in enumerate(chips)]
        for cp in first:
            cp.start()
        passed = [copy(4 + j, (*chip, c), sibling) for j, chip in enumerate(chips)]
        for j, chip in enumerate(chips):
            copy(1 + j, (*chip, c), me).wait_recv()
            passed[j].start()
        copy(0, sibling, me).wait_recv()
        for j, chip in enumerate(chips):
            copy(4 + j, (*chip, 1 - c), me).wait_recv()
        for cp in first + passed:
            cp.wait_send()
        mine.wait()

    return pl.pallas_call(
        body, name=name,
        out_shape=jax.ShapeDtypeStruct((N_DEV * m_per, n), blk.dtype),
        in_specs=[VMEM_SPEC], out_specs=VMEM_SPEC,
        scratch_shapes=[pltpu.SemaphoreType.DMA((7,)), pltpu.SemaphoreType.DMA((7,)), pltpu.SemaphoreType.DMA],
    )(blk)


class _Comm:
    def __init__(self, ins, outs, n_remote, n_local, start, finish, aliases=None):
        self.ins, self.outs, self.n_remote, self.n_local = list(ins), list(outs), n_remote, n_local
        self.start, self.finish, self.aliases = start, finish, dict(aliases or {})

    def scratch(self):
        return [pltpu.SemaphoreType.DMA((self.n_remote,)), pltpu.SemaphoreType.DMA((self.n_remote,)),
                pltpu.SemaphoreType.DMA((max(self.n_local, 1),))]


def _run_comm(comm, name):
    n_in, n_out = len(comm.ins), len(comm.outs)

    def body(*refs):
        args = (refs[:n_in], refs[n_in:n_in + n_out]) + tuple(refs[n_in + n_out:])
        comm.start(*args)
        comm.finish(*args)

    return pl.pallas_call(
        body, name=name, out_shape=tuple(comm.outs), in_specs=[ANY] * n_in, out_specs=tuple([ANY] * n_out),
        scratch_shapes=comm.scratch(), input_output_aliases=comm.aliases,
    )(*comm.ins)


def _remote(src, dst, send, recv, k, to):
    return pltpu.make_async_remote_copy(src_ref=src, dst_ref=dst, send_sem=send.at[k], recv_sem=recv.at[k],
                                        device_id=to, device_id_type=MESH)


def _gather_comm(shard, by_columns):
    rows, cols = shard.shape
    hr = rows // 2

    def build(ins, outs, send, recv, loc):
        x, y, c = _mesh_pos()
        sib = (x, y, 1 - c)
        peers = [(1 - x, y), (x, 1 - y), (1 - x, 1 - y)]
        s_me = 2 * x + y

        def slot(s, h):
            if by_columns:
                return outs[0].at[pl.ds(pl.multiple_of(h * hr, 16), hr), pl.ds(pl.multiple_of(s * cols, 128), cols)]
            return outs[0].at[pl.ds(pl.multiple_of(s * rows + h * hr, 16), hr), :]

        def half(h):
            return ins[0].at[pl.ds(pl.multiple_of(h * hr, 16), hr), :]

        def local():
            if by_columns:
                mine = outs[0].at[:, pl.ds(pl.multiple_of(s_me * cols, 128), cols)]
            else:
                mine = outs[0].at[pl.ds(pl.multiple_of(s_me * rows, 16), rows), :]
            return [pltpu.make_async_copy(ins[0], mine, loc.at[0])]

        def per_copy(make):
            return [make(j, 2 * px + py, (px, py, c)) for j, (px, py) in enumerate(peers)]

        ici_send = lambda: per_copy(lambda j, s_p, to: _remote(half(c), slot(s_me, c), send, recv, j, to))
        ici_recv = lambda: per_copy(lambda j, s_p, to: _remote(half(c), slot(s_p, c), send, recv, j, to))
        d2d_send = lambda: per_copy(lambda j, s_p, to: _remote(slot(s_p, c), slot(s_p, c), send, recv, 3 + j, sib))
        d2d_recv = lambda: per_copy(lambda j, s_p, to: _remote(slot(s_p, c), slot(s_p, 1 - c), send, recv, 3 + j, sib))
        return local, ici_send, ici_recv, d2d_send, d2d_recv

    def start(*refs):
        local, ici_send, _, _, _ = build(*refs)
        for cp in local() + ici_send():
            cp.start()

    def finish(*refs):
        local, ici_send, ici_recv, d2d_send, d2d_recv = build(*refs)
        forwards = d2d_send()
        for arrived, forward in zip(ici_recv(), forwards):
            arrived.wait_recv()
            forward.start()
        for cp in d2d_recv():
            cp.wait_recv()
        for cp in ici_send() + forwards:
            cp.wait_send()
        for cp in local():
            cp.wait()

    full = (rows, 4 * cols) if by_columns else (4 * rows, cols)
    return _Comm([shard], [jax.ShapeDtypeStruct(full, BF16)], 6, 1, start, finish)


def _swap_comm(gin_other, gout_v):
    dh, w = gin_other.shape
    ns, _, rh, d = gout_v.shape

    def build(ins, outs, send, recv, loc):
        x, y, c = _mesh_pos()
        sib = (x, y, 1 - c)
        return [_remote(ins[0], outs[0], send, recv, 0, sib),
                _remote(ins[1].at[:, 1 - c], outs[1], send, recv, 1, sib)]

    def start(*refs):
        for cp in build(*refs):
            cp.start()

    def finish(*refs):
        for cp in build(*refs):
            cp.wait()

    outs = [jax.ShapeDtypeStruct((dh, w), F32), jax.ShapeDtypeStruct((ns, rh, d), F32)]
    return _Comm([gin_other, gout_v], outs, 2, 0, start, finish)


def _exchange_comm(p_in, p_out, n4):
    dh, _ = p_in.shape
    _, rh, d = p_out.shape

    def build(ins, outs, send, recv, loc):
        x, y, c = _mesh_pos()
        cps = []
        for j, (px, py) in enumerate([(1 - x, y), (x, 1 - y), (1 - x, 1 - y)]):
            s_p = 2 * px + py
            cps.append(_remote(ins[0].at[:, pl.ds(pl.multiple_of(s_p * n4, 128), n4)], outs[0].at[j], send, recv,
                               2 * j, (px, py, c)))
            cps.append(_remote(ins[1].at[s_p], outs[1].at[j], send, recv, 2 * j + 1, (px, py, c)))
        return cps

    def start(*refs):
        for cp in build(*refs):
            cp.start()

    def finish(*refs):
        cps = build(*refs)
        for cp in cps:
            cp.wait_recv()
        for cp in cps:
            cp.wait_send()

    outs = [jax.ShapeDtypeStruct((3, dh, n4), p_in.dtype), jax.ShapeDtypeStruct((3, rh, d), p_out.dtype)]
    return _Comm([p_in, p_out], outs, 6, 0, start, finish)


def _join_comm(q_in, q_out):
    def build(ins, outs, send, recv, loc):
        x, y, c = _mesh_pos()
        sib = (x, y, 1 - c)
        sends = lambda: [_remote(outs[a].at[c], outs[a].at[c], send, recv, a, sib) for a in range(2)]
        recvs = lambda: [_remote(outs[a].at[c], outs[a].at[1 - c], send, recv, a, sib) for a in range(2)]
        return sends, recvs

    def start(*refs):
        for cp in build(*refs)[0]():
            cp.start()

    def finish(*refs):
        sends, recvs = build(*refs)
        for cp in recvs():
            cp.wait_recv()
        for cp in sends():
            cp.wait_send()

    outs = [jax.ShapeDtypeStruct(q_in.shape, F32), jax.ShapeDtypeStruct(q_out.shape, F32)]
    return _Comm([q_in, q_out], outs, 2, 0, start, finish, aliases={0: 0, 1: 1})


def _sum_half_in(mine, land, name):
    dh, w = mine.shape
    tr, tc = _pick(dh, 256, 16), _pick(w, 2048, 128)

    def body(a_ref, b_ref, o_ref):
        o_ref[...] = (a_ref[...] + b_ref[...]).astype(BF16)

    spec = pl.BlockSpec((tr, tc), lambda i, j: (i, j))
    return pl.pallas_call(
        body, name=name, out_shape=jax.ShapeDtypeStruct((dh, w), BF16), grid=(dh // tr, w // tc),
        in_specs=[spec, spec], out_specs=spec, compiler_params=_params("parallel", "parallel"),
    )(mine, land)


def _sum_half_out(pos, gout_v, land, name):
    ns, _, rh, d = gout_v.shape

    def body(pos_ref, a_ref, b_ref, o_ref):
        o_ref[...] = (a_ref[...] + b_ref[...]).astype(BF16)

    return pl.pallas_call(
        body, name=name,
        out_shape=jax.ShapeDtypeStruct((ns, rh, d), BF16),
        grid_spec=pltpu.PrefetchScalarGridSpec(
            num_scalar_prefetch=1, grid=(ns,),
            in_specs=[pl.BlockSpec((None, None, rh, d), lambda s, p: (s, p[0], 0, 0)),
                      pl.BlockSpec((None, rh, d), lambda s, p: (s, 0, 0))],
            out_specs=pl.BlockSpec((None, rh, d), lambda s, p: (s, 0, 0))),
        compiler_params=_params("parallel"),
    )(pos, gout_v, land)


def _sum_quarters_in(pos, p_in, land3, n4, name):
    dh, _ = p_in.shape
    tr = _pick(dh, 128, 16)

    def body(pos_ref, a_ref, b_ref, o_ref):
        f = lambda v: v.astype(F32)
        o_ref[...] = ((f(a_ref[...]) + f(b_ref[0])) + f(b_ref[1])) + f(b_ref[2])

    return pl.pallas_call(
        body, name=name,
        out_shape=jax.ShapeDtypeStruct((2, dh, n4), F32),
        grid_spec=pltpu.PrefetchScalarGridSpec(
            num_scalar_prefetch=1, grid=(dh // tr,),
            in_specs=[pl.BlockSpec((tr, n4), lambda i, p: (i, p[1])),
                      pl.BlockSpec((3, tr, n4), lambda i, p: (0, i, 0))],
            out_specs=pl.BlockSpec((None, tr, n4), lambda i, p: (p[0], i, 0))),
        compiler_params=_params("parallel"),
    )(pos, p_in, land3)


def _sum_quarters_out(pos, p_out, land3, name):
    _, rh, d = p_out.shape

    def body(pos_ref, a_ref, b_ref, o_ref):
        f = lambda v: v.astype(F32)
        o_ref[...] = ((f(a_ref[...]) + f(b_ref[0])) + f(b_ref[1])) + f(b_ref[2])

    return pl.pallas_call(
        body, name=name,
        out_shape=jax.ShapeDtypeStruct((2, rh, d), F32),
        grid_spec=pltpu.PrefetchScalarGridSpec(
            num_scalar_prefetch=1, grid=(1,),
            in_specs=[pl.BlockSpec((None, rh, d), lambda i, p: (p[1], 0, 0)),
                      pl.BlockSpec((3, rh, d), lambda i, p: (0, 0, 0))],
            out_specs=pl.BlockSpec((None, rh, d), lambda i, p: (p[0], 0, 0))),
        compiler_params=_params("arbitrary"),
    )(pos, p_out, land3)


def _matmul(a, b, kind, name, tm_t=1056, tn_t=1024, tk_t=2048, comms=(), row_half=None, out_dtype=F32):
    if kind == "nn":
        (m, k), n = a.shape, b.shape[1]
    elif kind == "nt":
        (m, k), n = a.shape, b.shape[0]
    else:
        (k, m), n = a.shape, b.shape[1]
    if row_half is None:
        row_half = jnp.zeros((1,), jnp.int32)
    else:
        assert kind == "nn"
        m = m // 2
    tm = _pick(m, tm_t, 128 if kind == "tn" else 16)
    tn = _pick(n, tn_t, 128)
    tk = _pick(k, tk_t, 16 if kind == "tn" else 128)
    nk = k // tk
    nbm = m // tm
    grid = (nbm, n // tn, nk)
    dot = {"nn": _nn, "nt": _nt, "tn": _tn}[kind]
    a_spec = {"nn": pl.BlockSpec((tm, tk), lambda i, j, q, p: (i + p[0] * nbm, q)),
              "nt": pl.BlockSpec((tm, tk), lambda i, j, q, p: (i, q)),
              "tn": pl.BlockSpec((tk, tm), lambda i, j, q, p: (q, i))}[kind]
    b_spec = {"nn": pl.BlockSpec((tk, tn), lambda i, j, q, p: (q, j)),
              "nt": pl.BlockSpec((tn, tk), lambda i, j, q, p: (j, q)),
              "tn": pl.BlockSpec((tk, tn), lambda i, j, q, p: (q, j))}[kind]
    comm_ins = [x for cm in comms for x in cm.ins]
    comm_outs = [x for cm in comms for x in cm.outs]
    aliases, scratch = {}, []
    i_off, o_off = 3, 1
    for cm in comms:
        for ci, co in cm.aliases.items():
            aliases[i_off + ci] = o_off + co
        i_off, o_off = i_off + len(cm.ins), o_off + len(cm.outs)
        scratch += cm.scratch()

    def body(half_ref, a_ref, b_ref, *rest):
        cin = rest[:len(comm_ins)]
        o_ref = rest[len(comm_ins)]
        cout = rest[len(comm_ins) + 1:len(comm_ins) + 1 + len(comm_outs)]
        acc_ref = rest[len(comm_ins) + 1 + len(comm_outs)]
        sems = rest[len(comm_ins) + 2 + len(comm_outs):]
        ids = [pl.program_id(ax) for ax in range(3)]
        first = functools.reduce(jnp.logical_and, [ids[ax] == 0 for ax in range(3)])
        last = functools.reduce(jnp.logical_and, [ids[ax] == grid[ax] - 1 for ax in range(3)])

        def comm_args(idx):
            i0 = sum(len(cm.ins) for cm in comms[:idx])
            o0 = sum(len(cm.outs) for cm in comms[:idx])
            cm = comms[idx]
            return (cin[i0:i0 + len(cm.ins)], cout[o0:o0 + len(cm.outs)]) + tuple(sems[3 * idx:3 * idx + 3])

        if comms:
            @pl.when(first)
            def _():
                for idx, cm in enumerate(comms):
                    cm.start(*comm_args(idx))

        q = ids[2]
        p = dot(a_ref[...], b_ref[...])
        if nk == 1:
            o_ref[...] = p.astype(out_dtype)
        else:
            @pl.when(q == 0)
            def _():
                acc_ref[...] = p

            @pl.when(q > 0)
            def _():
                acc_ref[...] += p

            @pl.when(q == nk - 1)
            def _():
                o_ref[...] = acc_ref[...].astype(out_dtype)

        if comms:
            @pl.when(last)
            def _():
                for idx, cm in enumerate(comms):
                    cm.finish(*comm_args(idx))

    sem = ("arbitrary",) * 3 if comms else ("parallel", "parallel", "arbitrary")
    res = pl.pallas_call(
        body, name=name,
        out_shape=(jax.ShapeDtypeStruct((m, n), out_dtype), *comm_outs),
        grid_spec=pltpu.PrefetchScalarGridSpec(
            num_scalar_prefetch=1, grid=grid,
            in_specs=[a_spec, b_spec] + [ANY] * len(comm_ins),
            out_specs=(pl.BlockSpec((tm, tn), lambda i, j, q, p: (i, j)), *([ANY] * len(comm_outs))),
            scratch_shapes=[pltpu.VMEM((tm, tn) if nk > 1 else (8, 128), F32)] + scratch),
        input_output_aliases=aliases,
        compiler_params=_params(*sem),
    )(row_half, a, b, *comm_ins)
    return res if comms else res[0]


def _out_proj(y, w_out, xs, rows, n_ctx, name):
    t, k = y.shape
    d = w_out.shape[1]
    tm, tn = _pick(t, 1056, 16), _pick(d, 1024, 128)

    def body(y_ref, w_ref, x_ref, r_ref, xo_ref, p_ref):
        i = pl.program_id(0)
        p = _nn(y_ref[...], w_ref[...])
        ridx = i * tm + lax.broadcasted_iota(jnp.int32, (tm, 1), 0)
        gate = jnp.where(ridx < n_ctx, r_ref[5:6, :], r_ref[2:3, :])
        p_ref[...] = p
        xo_ref[...] = x_ref[...] + gate * p

    return pl.pallas_call(
        body, name=name,
        out_shape=(jax.ShapeDtypeStruct((t, d), F32), jax.ShapeDtypeStruct((t, d), F32)),
        grid=(t // tm, d // tn),
        in_specs=[pl.BlockSpec((tm, k), lambda i, j: (i, 0)), pl.BlockSpec((k, tn), lambda i, j: (0, j)),
                  pl.BlockSpec((tm, tn), lambda i, j: (i, j)), pl.BlockSpec((8, tn), lambda i, j: (0, j))],
        out_specs=(pl.BlockSpec((tm, tn), lambda i, j: (i, j)), pl.BlockSpec((tm, tn), lambda i, j: (i, j))),
        compiler_params=_params("parallel", "parallel"),
    )(y, w_out, xs, rows)


def _modulate(xs, nw, rows, nct, name):
    t, d = xs.shape
    tr = ROW_TILE

    def body(x_ref, nw_ref, r_ref, o_ref, ot_ref):
        is_ctx = pl.program_id(0) < nct
        x = x_ref[...]
        xn = (x * lax.rsqrt(jnp.mean(x * x, axis=-1, keepdims=True) + EPS)) * nw_ref[...]
        shift = jnp.where(is_ctx, r_ref[3:4, :], r_ref[0:1, :])
        scale = jnp.where(is_ctx, r_ref[4:5, :], r_ref[1:2, :])
        hx = xn * (1.0 + scale) + shift
        o_ref[...] = hx.astype(BF16)
        ot_ref[...] = hx.T.astype(BF16)

    return pl.pallas_call(
        body, name=name,
        out_shape=(jax.ShapeDtypeStruct((t, d), BF16), jax.ShapeDtypeStruct((d, t), BF16)), grid=(t // tr,),
        in_specs=[pl.BlockSpec((tr, d), lambda i: (i, 0)), pl.BlockSpec((1, d), lambda i: (0, 0)),
                  pl.BlockSpec((8, d), lambda i: (0, 0))],
        out_specs=(pl.BlockSpec((tr, d), lambda i: (i, 0)), pl.BlockSpec((d, tr), lambda i: (0, i))),
        compiler_params=_params("parallel"),
    )(xs, nw, rows)


def _modulate_bwd(dhx, xs, g, nw, rows, nct, name, latent_only=False):
    t, d = xs.shape
    tr = ROW_TILE
    t_out = t - nct * tr if latent_only else t
    out_row = (lambda i: (jnp.maximum(i - nct, 0), 0)) if latent_only else (lambda i: (i, 0))

    def body(dh_ref, x_ref, g_ref, nw_ref, r_ref, go_ref, acc_ref):
        i = pl.program_id(0)
        is_ctx = i < nct

        @pl.when(i == 0)
        def _():
            acc_ref[...] = jnp.zeros_like(acc_ref)

        x, dh = x_ref[...], dh_ref[...]
        r = lax.rsqrt(jnp.mean(x * x, axis=-1, keepdims=True) + EPS)
        xh = x * r
        xn = xh * nw_ref[...]
        scale = jnp.where(is_ctx, r_ref[4:5, :], r_ref[1:2, :])
        d_shift = jnp.sum(dh, axis=0, keepdims=True)
        d_scale = jnp.sum(dh * xn, axis=0, keepdims=True)
        dxn = dh * (1.0 + scale)
        acc_ref[6:7, :] += jnp.sum(dxn * xh, axis=0, keepdims=True)
        dxh = dxn * nw_ref[...]
        dx = r * (dxh - xh * jnp.mean(dxh * xh, axis=-1, keepdims=True))
        go_ref[...] = g_ref[...] + dx
        zero = jnp.zeros_like(d_shift)
        acc_ref[0:1, :] += jnp.where(is_ctx, zero, d_shift)
        acc_ref[1:2, :] += jnp.where(is_ctx, zero, d_scale)
        acc_ref[3:4, :] += jnp.where(is_ctx, d_shift, zero)
        acc_ref[4:5, :] += jnp.where(is_ctx, d_scale, zero)

    return pl.pallas_call(
        body, name=name,
        out_shape=(jax.ShapeDtypeStruct((t_out, d), F32), jax.ShapeDtypeStruct((8, d), F32)), grid=(t // tr,),
        in_specs=[pl.BlockSpec((tr, d), lambda i: (i, 0)), pl.BlockSpec((tr, d), lambda i: (i, 0)),
                  pl.BlockSpec((tr, d), lambda i: (i, 0)), pl.BlockSpec((1, d), lambda i: (0, 0)),
                  pl.BlockSpec((8, d), lambda i: (0, 0))],
        out_specs=(pl.BlockSpec((tr, d), out_row), pl.BlockSpec((8, d), lambda i: (0, 0))),
        compiler_params=_params("arbitrary"),
    )(dhx, xs, g, nw, rows)


def _gate_bwd(g, proj, rows, nct, name):
    t, d = g.shape
    tr = ROW_TILE

    def body(g_ref, p_ref, r_ref, gp_ref, acc_ref):
        i = pl.program_id(0)
        is_ctx = i < nct

        @pl.when(i == 0)
        def _():
            acc_ref[...] = jnp.zeros_like(acc_ref)

        gg = g_ref[...]
        gate = jnp.where(is_ctx, r_ref[5:6, :], r_ref[2:3, :])
        gp_ref[...] = (gg * gate).astype(BF16)
        s = jnp.sum(gg * p_ref[...], axis=0, keepdims=True)
        zero = jnp.zeros_like(s)
        acc_ref[2:3, :] += jnp.where(is_ctx, zero, s)
        acc_ref[5:6, :] += jnp.where(is_ctx, s, zero)

    return pl.pallas_call(
        body, name=name,
        out_shape=(jax.ShapeDtypeStruct((t, d), BF16), jax.ShapeDtypeStruct((8, d), F32)), grid=(t // tr,),
        in_specs=[pl.BlockSpec((tr, d), lambda i: (i, 0)), pl.BlockSpec((tr, d), lambda i: (i, 0)),
                  pl.BlockSpec((8, d), lambda i: (0, 0))],
        out_specs=(pl.BlockSpec((tr, d), lambda i: (i, 0)), pl.BlockSpec((8, d), lambda i: (0, 0))),
        compiler_params=_params("arbitrary"),
    )(g, proj, rows)


def _final_loss(xs, tgt, fw, nct, name):
    t, d = xs.shape
    tr = ROW_TILE

    def body(x_ref, t_ref, w_ref, g_ref, acc_ref):
        i = pl.program_id(0)

        @pl.when(i == 0)
        def _():
            acc_ref[...] = jnp.zeros_like(acc_ref)

        @pl.when(i < nct)
        def _():
            g_ref[...] = jnp.zeros_like(g_ref)

        @pl.when(i >= nct)
        def _():
            x = x_ref[...]
            r = lax.rsqrt(jnp.mean(x * x, axis=-1, keepdims=True) + EPS)
            xh = x * r
            e = xh * w_ref[...] - t_ref[...]
            acc_ref[1:2, :] += jnp.zeros((1, d), F32) + jnp.sum(e * e) * (0.5 / d)
            dy = e * (1.0 / d)
            acc_ref[0:1, :] += jnp.sum(dy * xh, axis=0, keepdims=True)
            dxh = dy * w_ref[...]
            g_ref[...] = r * (dxh - xh * jnp.mean(dxh * xh, axis=-1, keepdims=True))

    return pl.pallas_call(
        body, name=name,
        out_shape=(jax.ShapeDtypeStruct((t, d), F32), jax.ShapeDtypeStruct((8, d), F32)), grid=(t // tr,),
        in_specs=[pl.BlockSpec((tr, d), lambda i: (i, 0)),
                  pl.BlockSpec((tr, d), lambda i: (jnp.maximum(i - nct, 0), 0)),
                  pl.BlockSpec((1, d), lambda i: (0, 0))],
        out_specs=(pl.BlockSpec((tr, d), lambda i: (i, 0)), pl.BlockSpec((8, d), lambda i: (0, 0))),
        compiler_params=_params("arbitrary"),
    )(xs, tgt, fw)


def _seq_edges(i, nct, nt):
    has_prev = jnp.logical_and(i != 0, i != nct)
    has_next = jnp.logical_and(i != nct - 1, i != nt - 1)
    return has_prev, has_next


def _shift_rows(a, before, after):
    tr = a.shape[0]
    ridx = lax.broadcasted_iota(jnp.int32, (tr, 1), 0)
    down = jnp.where(ridx == 0, before, pltpu.roll(a, 1, 0))
    up = jnp.where(ridx == tr - 1, after, pltpu.roll(a, tr - 1, 0))
    return down, up


def _halo_specs(tr, dc, col, t, hr):
    per = tr // hr
    prev = pl.BlockSpec((hr, dc), lambda i: (jnp.maximum(i * per - 1, 0), col))
    nxt = pl.BlockSpec((hr, dc), lambda i: (jnp.minimum((i + 1) * per, t // hr - 1), col))
    return prev, nxt


def _mix_fwd(u, o_f, o_b, cw, cnw, rnw, nct, name):
    t, w = u.shape
    dc = w // 8
    nh = dc // HEAD
    tr = ROW_TILE
    nt = t // tr
    col = lambda cidx: pl.BlockSpec((tr, dc), lambda i: (i, cidx))
    hr = 16
    acp, acn = _halo_specs(tr, dc, 2, t, hr)
    ahp, ahn = _halo_specs(tr, dc, 0, t, hr)
    f32 = lambda v: v.astype(F32)

    def body(ah, ab, ac, az, rz, acp_r, acn_r, ahp_r, ahn_r, of_r, ob_r, cw_r, cnw_r, rnw_r, y_ref, yt_ref, cv_ref):
        i = pl.program_id(0)
        has_prev, has_next = _seq_edges(i, nct, nt)
        ch = f32(ac[...]) * f32(ah[...])
        before = jnp.where(has_prev, f32(acp_r[hr - 1:hr, :]) * f32(ahp_r[hr - 1:hr, :]), 0.0)
        after = jnp.where(has_next, f32(acn_r[0:1, :]) * f32(ahn_r[0:1, :]), 0.0)
        down, up = _shift_rows(ch, before, after)
        cv = down * cw_r[0:1, :] + ch * cw_r[1:2, :] + up * cw_r[2:3, :]
        cv_ref[...] = cv
        y1 = f32(ab[...]) * cv
        yn = (y1 * lax.rsqrt(jnp.mean(y1 * y1, axis=-1, keepdims=True) + EPS)) * cnw_r[...]
        yc = _silu(f32(az[...])) * yn
        y_ref[:, 0:dc] = yc.astype(BF16)
        yt_ref[0:dc, :] = yc.T.astype(BF16)
        for h in range(nh):
            sl = slice(h * HEAD, (h + 1) * HEAD)
            o = of_r[:, sl] + ob_r[:, sl]
            dev = o - jnp.mean(o, axis=-1, keepdims=True)
            on = dev * lax.rsqrt(jnp.mean(dev * dev, axis=-1, keepdims=True) + EPS)
            yr = _silu(f32(rz[:, sl])) * (on * rnw_r[:, sl])
            y_ref[:, dc + h * HEAD:dc + (h + 1) * HEAD] = yr.astype(BF16)
            yt_ref[dc + h * HEAD:dc + (h + 1) * HEAD, :] = yr.T.astype(BF16)

    vec = pl.BlockSpec((1, dc), lambda i: (0, 0))
    return pl.pallas_call(
        body, name=name,
        out_shape=(jax.ShapeDtypeStruct((t, 2 * dc), BF16), jax.ShapeDtypeStruct((2 * dc, t), BF16),
                   jax.ShapeDtypeStruct((t, dc), F32)), grid=(nt,),
        in_specs=[col(0), col(1), col(2), col(3), col(7), acp, acn, ahp, ahn,
                  pl.BlockSpec((tr, dc), lambda i: (i, 0)), pl.BlockSpec((tr, dc), lambda i: (i, 0)),
                  pl.BlockSpec((8, dc), lambda i: (0, 0)), vec, vec],
        out_specs=(pl.BlockSpec((tr, 2 * dc), lambda i: (i, 0)), pl.BlockSpec((2 * dc, tr), lambda i: (0, i)),
                   pl.BlockSpec((tr, dc), lambda i: (i, 0))),
        compiler_params=_params("parallel"),
    )(u, u, u, u, u, u, u, u, u, o_f, o_b, cw, cnw, rnw)


def _mix_bwd(dy, u, cv, o_f, o_b, cnw, rnw, name):
    t, w = u.shape
    dc = w // 8
    nh = dc // HEAD
    tr = ROW_TILE
    col = lambda cidx: pl.BlockSpec((tr, dc), lambda i: (i, cidx))

    def body(dyc, dyr, ab, az, rz, cv_r, of_r, ob_r, cnw_r, rnw_r, dab, daz, drz, dcv, do, acc_ref):
        @pl.when(pl.program_id(0) == 0)
        def _():
            acc_ref[...] = jnp.zeros_like(acc_ref)

        z, b, c_out = az[...].astype(F32), ab[...].astype(F32), cv_r[...]
        y1 = b * c_out
        r = lax.rsqrt(jnp.mean(y1 * y1, axis=-1, keepdims=True) + EPS)
        yh = y1 * r
        yn = yh * cnw_r[...]
        dyc_v = dyc[...]
        daz[...] = (dyc_v * yn * _dsilu(z)).astype(BF16)
        dyn = dyc_v * _silu(z)
        acc_ref[0:1, :] += jnp.sum(dyn * yh, axis=0, keepdims=True)
        dyh = dyn * cnw_r[...]
        dy1 = r * (dyh - yh * jnp.mean(dyh * yh, axis=-1, keepdims=True))
        dab[...] = (dy1 * c_out).astype(BF16)
        dcv[...] = dy1 * b
        for h in range(nh):
            sl = slice(h * HEAD, (h + 1) * HEAD)
            o = of_r[:, sl] + ob_r[:, sl]
            dev = o - jnp.mean(o, axis=-1, keepdims=True)
            rs = lax.rsqrt(jnp.mean(dev * dev, axis=-1, keepdims=True) + EPS)
            on = dev * rs
            zz, dyr_v, gw = rz[:, sl].astype(F32), dyr[:, sl], rnw_r[:, sl]
            drz[:, sl] = (dyr_v * (on * gw) * _dsilu(zz)).astype(BF16)
            dong = dyr_v * _silu(zz)
            acc_ref[1:2, sl] += jnp.sum(dong * on, axis=0, keepdims=True)
            don = dong * gw
            do[:, sl] = rs * (don - jnp.mean(don, axis=-1, keepdims=True)
                              - on * jnp.mean(don * on, axis=-1, keepdims=True))

    vec = pl.BlockSpec((1, dc), lambda i: (0, 0))
    row = pl.BlockSpec((tr, dc), lambda i: (i, 0))
    return pl.pallas_call(
        body, name=name,
        out_shape=(jax.ShapeDtypeStruct((t, dc), BF16), jax.ShapeDtypeStruct((t, dc), BF16),
                   jax.ShapeDtypeStruct((t, dc), BF16), jax.ShapeDtypeStruct((t, dc), F32),
                   jax.ShapeDtypeStruct((t, dc), F32), jax.ShapeDtypeStruct((8, dc), F32)),
        grid=(t // tr,),
        in_specs=[col(0), col(1), col(1), col(3), col(7), row, row, row, vec, vec],
        out_specs=(row, row, row, row, row, pl.BlockSpec((8, dc), lambda i: (0, 0))),
        compiler_params=_params("arbitrary"),
    )(dy, dy, u, u, u, cv, o_f, o_b, cnw, rnw)


def _rope_masks():
    lane = lax.broadcasted_iota(jnp.int32, (1, HEAD), 1)
    return (lane % (HEAD // 2)) >= (HEAD // 4)


def _rot(a, upper):
    return jnp.where(upper, pltpu.roll(a, HEAD // 4, 1), pltpu.roll(a, HEAD - HEAD // 4, 1))


def _assemble_du(dcv, u, cw, dab, daz, drz, dq_f, dq_b, dk_f, dk_b, dv_f, dv_b, cos, sin, nct, name):
    t, w = u.shape
    dc = w // 8
    nh = dc // HEAD
    tr = ROW_TILE
    nt = t // tr
    k_scale = HEAD ** -0.5
    col = lambda cidx: pl.BlockSpec((tr, dc), lambda i: (i, cidx))
    row = pl.BlockSpec((tr, dc), lambda i: (i, 0))
    dcp, dcn = _halo_specs(tr, dc, 0, t, 8)
    f32 = lambda v: v.astype(F32)

    def body(dcv_r, dcp_r, dcn_r, ah, ac, cw_r, dab_r, daz_r, drz_r, dqf, dqb, dkf, dkb, dvf, dvb, cos_r, sin_r,
             du, acc_ref):
        i = pl.program_id(0)

        @pl.when(i == 0)
        def _():
            acc_ref[...] = jnp.zeros_like(acc_ref)

        has_prev, has_next = _seq_edges(i, nct, nt)
        d_c = dcv_r[...]
        before = jnp.where(has_prev, dcp_r[7:8, :], 0.0)
        after = jnp.where(has_next, dcn_r[0:1, :], 0.0)
        d_prev, d_next = _shift_rows(d_c, before, after)
        a_h, a_c = f32(ah[...]), f32(ac[...])
        ch = a_c * a_h
        dch = d_next * cw_r[0:1, :] + d_c * cw_r[1:2, :] + d_prev * cw_r[2:3, :]
        acc_ref[0:1, :] += jnp.sum(ch * d_next, axis=0, keepdims=True)
        acc_ref[1:2, :] += jnp.sum(ch * d_c, axis=0, keepdims=True)
        acc_ref[2:3, :] += jnp.sum(ch * d_prev, axis=0, keepdims=True)
        du[:, 0:dc] = (dch * a_c).astype(BF16)
        du[:, dc:2 * dc] = dab_r[...]
        du[:, 2 * dc:3 * dc] = (dch * a_h).astype(BF16)
        du[:, 3 * dc:4 * dc] = daz_r[...]
        du[:, 7 * dc:8 * dc] = drz_r[...]
        du[:, 6 * dc:7 * dc] = (f32(dvf[...]) + f32(dvb[...])).astype(BF16)
        upper = _rope_masks()
        cs, sn = cos_r[...], sin_r[...]
        for h in range(nh):
            sl = slice(h * HEAD, (h + 1) * HEAD)
            dq = f32(dqf[:, sl]) + f32(dqb[:, sl])
            dk = (f32(dkf[:, sl]) + f32(dkb[:, sl])) * k_scale
            du[:, 4 * dc + h * HEAD:4 * dc + (h + 1) * HEAD] = (dq * cs + _rot(dq * sn, upper)).astype(BF16)
            du[:, 5 * dc + h * HEAD:5 * dc + (h + 1) * HEAD] = (dk * cs + _rot(dk * sn, upper)).astype(BF16)

    tab = pl.BlockSpec((tr, HEAD), lambda i: (i, 0))
    return pl.pallas_call(
        body, name=name,
        out_shape=(jax.ShapeDtypeStruct((t, w), BF16), jax.ShapeDtypeStruct((8, dc), F32)), grid=(nt,),
        in_specs=[row, dcp, dcn, col(0), col(2), pl.BlockSpec((8, dc), lambda i: (0, 0)),
                  row, row, row, row, row, row, row, row, row, tab, tab],
        out_specs=(pl.BlockSpec((tr, w), lambda i: (i, 0)), pl.BlockSpec((8, dc), lambda i: (0, 0))),
        compiler_params=_params("arbitrary"),
    )(dcv, dcv, dcv, u, u, cw, dab, daz, drz, dq_f, dq_b, dk_f, dk_b, dv_f, dv_b, cos, sin)


def _bwd_dir_chunk(s, ncc, ns):
    return jnp.where(s < ncc, ncc - 1 - s, ns + ncc - 1 - s)


def _position_weights(d):
    c = CHUNK
    ii = lax.broadcasted_iota(jnp.int32, (c, c), 0)
    jj = lax.broadcasted_iota(jnp.int32, (c, c), 1)
    sd = ((ii - jj) if d == 0 else (jj - ii)).astype(F32)
    pos = lax.broadcasted_iota(jnp.int32, (c, HEAD), 0).astype(F32)
    qw, kw = (pos + 1.0, (c - 1.0) - pos) if d == 0 else (c - pos, pos)
    return sd, qw, kw


def _fill_decay_tables(dec_ref, nh, dm_t, qd_t, kd_t, gc_t):
    for d in range(2):
        sd, qw, kw = _position_weights(d)
        for h in range(nh):
            r = d * nh + h
            lg = -jnp.exp(jnp.zeros((1, HEAD), F32) + dec_ref[d, h])
            dm_t[r] = jnp.where(sd >= 0, jnp.exp(lg * jnp.maximum(sd, 0.0)), 0.0)
            qd_t[r] = jnp.exp(lg * qw)
            kd_t[r] = jnp.exp(lg * kw)
            gc_t[r] = jnp.zeros((8, HEAD), F32) + jnp.exp(lg * float(CHUNK))


def _decay_table_scratch(nh):
    return [pltpu.VMEM((2 * nh, CHUNK, CHUNK), F32), pltpu.VMEM((2 * nh, CHUNK, HEAD), F32),
            pltpu.VMEM((2 * nh, CHUNK, HEAD), F32), pltpu.VMEM((2 * nh, 8, HEAD), F32)]


def _ret_fwd(u, cos, sin, dec, ncc, name):
    t, w = u.shape
    dc = w // 8
    nh = dc // HEAD
    c = CHUNK
    ns = t // c
    k_scale = HEAD ** -0.5
    fmap = lambda g: (lambda s: (s, g))
    bmap = lambda g: (lambda s: (_bwd_dir_chunk(s, ncc, ns), g))
    blk = lambda m: pl.BlockSpec((c, dc), m)
    tab = lambda m: pl.BlockSpec((c, HEAD), m)
    st_f = pl.BlockSpec((nh, None, HEAD, HEAD), lambda s: (0, s, 0, 0))
    st_b = pl.BlockSpec((nh, None, HEAD, HEAD), lambda s: (0, _bwd_dir_chunk(s, ncc, ns), 0, 0))

    def body(qf, kf, vf, qb, kb, vb, cf, sf, cb, sb, dec_ref, of_ref, ob_ref, stf_ref, stb_ref, qr_ref, kr_ref,
             s_f, s_b, dm_t, qd_t, kd_t, gc_t):
        @pl.when(pl.program_id(0) == 0)
        def _():
            s_f[...] = jnp.zeros_like(s_f)
            s_b[...] = jnp.zeros_like(s_b)
            _fill_decay_tables(dec_ref, nh, dm_t, qd_t, kd_t, gc_t)

        upper = _rope_masks()
        dirs = ((qf, kf, vf, cf, sf, of_ref, stf_ref, s_f), (qb, kb, vb, cb, sb, ob_ref, stb_ref, s_b))
        group = RET_INTERLEAVE if nh % RET_INTERLEAVE == 0 else 1
        blocks = [[(h, d) for h in range(h0, h0 + group) for d in range(2)] for h0 in range(0, nh, group)]
        for blk_list in blocks:
            first = []
            for h, d in blk_list:
                q_r, k_r, v_r, cs_r, sn_r, _, st_ref, st = dirs[d]
                sl = slice(h * HEAD, (h + 1) * HEAD)
                r = d * nh + h
                cs, sn = cs_r[...], sn_r[...]
                q = q_r[:, sl].astype(F32)
                q = q * cs + _rot(q, upper) * sn
                k = k_r[:, sl].astype(F32)
                k = (k * cs + _rot(k, upper) * sn) * k_scale
                if d == 0:
                    qr_ref[:, sl] = q
                    kr_ref[:, sl] = k
                vv = v_r[:, sl]
                state = st[h]
                st_ref[h] = state
                p = _nt(_bf(q), _bf(k)) * dm_t[r]
                kv = _tn(_bf(k * kd_t[r]), vv)
                first.append((q, vv, state, p, kv))
            for idx, (h, d) in enumerate(blk_list):
                _, _, _, _, _, o_ref, _, st = dirs[d]
                sl = slice(h * HEAD, (h + 1) * HEAD)
                r = d * nh + h
                q, vv, state, p, kv = first[idx]
                o_ref[:, sl] = _nn(jnp.concatenate([_bf(p), _bf(q * qd_t[r])], axis=1),
                                   jnp.concatenate([vv, _bf(state)], axis=0))
                st[h] = gc_t[r, 0:1, :] * state + kv

    return pl.pallas_call(
        body, name=name,
        out_shape=(jax.ShapeDtypeStruct((t, dc), F32), jax.ShapeDtypeStruct((t, dc), F32),
                   jax.ShapeDtypeStruct((nh, ns, HEAD, HEAD), F32), jax.ShapeDtypeStruct((nh, ns, HEAD, HEAD), F32),
                   jax.ShapeDtypeStruct((t, dc), F32), jax.ShapeDtypeStruct((t, dc), F32)),
        grid=(ns,),
        in_specs=[blk(fmap(4)), blk(fmap(5)), blk(fmap(6)), blk(bmap(4)), blk(bmap(5)), blk(bmap(6)),
                  tab(fmap(0)), tab(fmap(0)), tab(bmap(0)), tab(bmap(0)), SMEM_SPEC],
        out_specs=(blk(fmap(0)), blk(bmap(0)), st_f, st_b, blk(fmap(0)), blk(fmap(0))),
        scratch_shapes=[pltpu.VMEM((nh, HEAD, HEAD), F32), pltpu.VMEM((nh, HEAD, HEAD), F32)]
        + _decay_table_scratch(nh),
        compiler_params=_params("arbitrary"),
    )(u, u, u, u, u, u, cos, sin, cos, sin, dec)


def _ret_bwd(u, qr, kr, do, st_f, st_b, dec, ncc, name):
    t, w = u.shape
    dc = w // 8
    nh = dc // HEAD
    c = CHUNK
    ns = t // c
    fchunk = lambda s: ns - 1 - s
    bchunk = lambda s: _bwd_dir_chunk(ns - 1 - s, ncc, ns)
    fmap = lambda g: (lambda s: (fchunk(s), g))
    bmap = lambda g: (lambda s: (bchunk(s), g))
    blk = lambda m: pl.BlockSpec((c, dc), m)
    stf_spec = pl.BlockSpec((nh, None, HEAD, HEAD), lambda s: (0, fchunk(s), 0, 0))
    stb_spec = pl.BlockSpec((nh, None, HEAD, HEAD), lambda s: (0, bchunk(s), 0, 0))

    def body(qf, kf, vf, dof, qb, kb, vb, dob, stf_ref, stb_ref, dec_ref,
             dqf, dkf, dvf, dqb, dkb, dvb, dlg_ref, ds_f, ds_b, dm_t, qd_t, kd_t, gc_t, lg_acc):
        step = pl.program_id(0)

        @pl.when(step == 0)
        def _():
            ds_f[...] = jnp.zeros_like(ds_f)
            ds_b[...] = jnp.zeros_like(ds_b)
            lg_acc[...] = jnp.zeros_like(lg_acc)
            _fill_decay_tables(dec_ref, nh, dm_t, qd_t, kd_t, gc_t)

        dirs = ((qf, kf, vf, dof, stf_ref, dqf, dkf, dvf, ds_f), (qb, kb, vb, dob, stb_ref, dqb, dkb, dvb, ds_b))
        weights = [_position_weights(d) for d in range(2)]
        group = RET_INTERLEAVE if nh % RET_INTERLEAVE == 0 else 1
        blocks = [[(h, d) for h in range(h0, h0 + group) for d in range(2)] for h0 in range(0, nh, group)]
        for blk_list in blocks:
            first = []
            for h, d in blk_list:
                q_r, k_r, v_r, do_r, st_ref, _, _, _, ds = dirs[d]
                sl = slice(h * HEAD, (h + 1) * HEAD)
                r = d * nh + h
                q, k = q_r[:, sl], k_r[:, sl]
                qb16, kb16, vb16, dob16 = _bf(q), _bf(k), v_r[:, sl], _bf(do_r[:, sl])
                state, dstate = st_ref[h], ds[h]
                dsb16 = _bf(dstate)
                a = _nt(qb16, kb16) * dm_t[r]
                both = _nt(dob16, jnp.concatenate([vb16, _bf(state)], axis=0))
                dk_i = _nt(vb16, dsb16) * kd_t[r]
                first.append((q, k, qb16, kb16, dob16, state, dstate, dsb16, a, both, dk_i))
            for idx, (h, d) in enumerate(blk_list):
                _, _, _, _, _, dq_ref, dk_ref, dv_ref, ds = dirs[d]
                sl = slice(h * HEAD, (h + 1) * HEAD)
                r = d * nh + h
                q, k, qb16, kb16, dob16, state, dstate, dsb16, a, both, dk_i = first[idx]
                sd, qw, kw = weights[d]
                da = both[:, 0:c]
                dp = _bf(da * dm_t[r])
                dq_i = both[:, c:2 * c] * qd_t[r]
                dq_ref[:, sl] = _bf(_nn(dp, kb16) + dq_i)
                dk_ref[:, sl] = _bf(_tn(dp, qb16) + dk_i)
                dv_ref[:, sl] = _bf(_nn(jnp.concatenate([_bf(a.T), _bf(k * kd_t[r])], axis=1),
                                        jnp.concatenate([dob16, dsb16], axis=0)))
                gcr = gc_t[r, 0:1, :]
                ds[h] = gcr * dstate + _tn(_bf(q * qd_t[r]), dob16)
                lg_acc[r] += (da * a * jnp.maximum(sd, 0.0) + (dq_i * q) * qw + (dk_i * k) * kw
                              + (float(c) * gcr) * (state * dstate))

        @pl.when(step == ns - 1)
        def _():
            for d in range(2):
                for h in range(nh):
                    r = d * nh + h
                    lg = -jnp.exp(jnp.zeros((1, HEAD), F32) + dec_ref[d, h])
                    dlg_ref[r:r + 1, :] = (jnp.zeros((1, HEAD), F32) + jnp.sum(lg_acc[r])) * lg

    dshape = jax.ShapeDtypeStruct((t, dc), BF16)
    return pl.pallas_call(
        body, name=name,
        out_shape=(dshape, dshape, dshape, dshape, dshape, dshape, jax.ShapeDtypeStruct((2 * nh, HEAD), F32)),
        grid=(ns,),
        in_specs=[blk(fmap(0)), blk(fmap(0)), blk(fmap(6)), blk(fmap(0)),
                  blk(bmap(0)), blk(bmap(0)), blk(bmap(6)), blk(bmap(0)), stf_spec, stb_spec, SMEM_SPEC],
        out_specs=(blk(fmap(0)), blk(fmap(0)), blk(fmap(0)), blk(bmap(0)), blk(bmap(0)), blk(bmap(0)),
                   pl.BlockSpec((2 * nh, HEAD), lambda s: (0, 0))),
        scratch_shapes=[pltpu.VMEM((nh, HEAD, HEAD), F32), pltpu.VMEM((nh, HEAD, HEAD), F32)]
        + _decay_table_scratch(nh) + [pltpu.VMEM((2 * nh, CHUNK, HEAD), F32)],
        compiler_params=_params("arbitrary"),
    )(qr, kr, u, do, qr, kr, u, do, st_f, st_b, dec)


def _mod_matvec(c16, w_mod, b_mod):
    nl, d, nm = w_mod.shape
    tn = _pick(nm, 512, 128)

    def body(c_ref, w_ref, b_ref, sc_ref, o_ref):
        sc = _silu(c_ref[...])
        sc_ref[...] = sc
        o_ref[...] = _nn(_bf(sc), _bf(w_ref[...])) + b_ref[...]

    return pl.pallas_call(
        body, name="mod_matvec",
        out_shape=(jax.ShapeDtypeStruct((16, d), F32), jax.ShapeDtypeStruct((nl, 16, nm), F32)),
        grid=(nl, nm // tn),
        in_specs=[pl.BlockSpec((16, d), lambda l, j: (0, 0)), pl.BlockSpec((None, d, tn), lambda l, j: (l, 0, j)),
                  pl.BlockSpec((None, 1, tn), lambda l, j: (l, 0, j))],
        out_specs=(pl.BlockSpec((16, d), lambda l, j: (0, 0)), pl.BlockSpec((None, 16, tn), lambda l, j: (l, 0, j))),
        compiler_params=_params("arbitrary", "arbitrary"),
    )(c16, w_mod, b_mod)


def _wmod_grad(sc16, dm, w_mod):
    nl, d, nm = w_mod.shape
    tn = _pick(nm, 512, 128)

    def body(sc_ref, dm_ref, w_ref, gw_ref, pc_ref):
        @pl.when(jnp.logical_and(pl.program_id(0) == 0, pl.program_id(1) == 0))
        def _():
            pc_ref[...] = jnp.zeros_like(pc_ref)

        dmb = _bf(dm_ref[...])
        gw_ref[...] = _tn(_bf(sc_ref[...]), dmb)
        pc_ref[...] += _nt(dmb, _bf(w_ref[...]))

    return pl.pallas_call(
        body, name="wmod_grad",
        out_shape=(jax.ShapeDtypeStruct((nl, d, nm), F32), jax.ShapeDtypeStruct((16, d), F32)),
        grid=(nl, nm // tn),
        in_specs=[pl.BlockSpec((16, d), lambda l, j: (0, 0)), pl.BlockSpec((None, 16, tn), lambda l, j: (l, 0, j)),
                  pl.BlockSpec((None, d, tn), lambda l, j: (l, 0, j))],
        out_specs=(pl.BlockSpec((None, d, tn), lambda l, j: (l, 0, j)), pl.BlockSpec((16, d), lambda l, j: (0, 0))),
        compiler_params=_params("arbitrary", "arbitrary"),
    )(sc16, dm, w_mod)


def _rowsum(a, name):
    r, n = a.shape
    tn = _pick(n, 4096, 128)

    def body(a_ref, o_ref):
        acc = a_ref[0:1, :]
        for i in range(1, r):
            acc = acc + a_ref[i:i + 1, :]
        o_ref[...] = jnp.zeros((8, tn), F32) + acc

    return pl.pallas_call(
        body, name=name, out_shape=jax.ShapeDtypeStruct((8, n), F32), grid=(n // tn,),
        in_specs=[pl.BlockSpec((r, tn), lambda j: (0, j))], out_specs=pl.BlockSpec((8, tn), lambda j: (0, j)),
        compiler_params=_params("parallel"),
    )(a)


def _cctx_grad(parts, c_ctx8):
    _, _, d = parts.shape

    def body(p_ref, c_ref, o_ref):
        acc = ((p_ref[0] + p_ref[1]) + p_ref[2]) + p_ref[3]
        o_ref[...] = acc * _dsilu(c_ref[...])

    return pl.pallas_call(
        body, name="cctx_grad", out_shape=jax.ShapeDtypeStruct((8, d), F32),
        in_specs=[VMEM_SPEC, VMEM_SPEC], out_specs=VMEM_SPEC,
    )(parts, c_ctx8)


def _adamw(g, w, m, v, name):
    r, n = w.shape
    tr = _pick(r, max(8, min(256, (512 * 1024) // n // 8 * 8)), 8)
    bc1 = 1.0 - ADAM_B1 ** ADAM_STEP
    bc2 = 1.0 - ADAM_B2 ** ADAM_STEP

    def body(g_ref, w_ref, m_ref, v_ref, d_ref, mo_ref, vo_ref):
        gg = g_ref[...]
        mn = ADAM_B1 * m_ref[...] + (1.0 - ADAM_B1) * gg
        vn = ADAM_B2 * v_ref[...] + (1.0 - ADAM_B2) * (gg * gg)
        mo_ref[...] = mn
        vo_ref[...] = vn
        d_ref[...] = -ADAM_LR * ((mn / bc1) / (jnp.sqrt(vn / bc2) + ADAM_EPS) + ADAM_WD * w_ref[...])

    spec = pl.BlockSpec((tr, n), lambda i: (i, 0))
    shp = jax.ShapeDtypeStruct((r, n), F32)
    return pl.pallas_call(
        body, name=name, out_shape=(shp, shp, shp), grid=(r // tr,),
        in_specs=[spec, spec, spec, spec], out_specs=(spec, spec, spec),
        compiler_params=_params("parallel"),
    )(g, w, m, v)


def _adamw_layer(g, w, m, v, layer, prev, name):
    r, n = g.shape
    tr = _pick(r, max(8, min(256, (512 * 1024) // n // 8 * 8)), 8)
    nb = r // tr
    bc1 = 1.0 - ADAM_B1 ** ADAM_STEP
    bc2 = 1.0 - ADAM_B2 ** ADAM_STEP
    n_prev = 0 if prev is None else 4

    def body(g_ref, w_ref, m_ref, v_ref, *rest):
        go_ref, d_ref, mo_ref, vo_ref = rest[n_prev:]
        gg = g_ref[...]
        mn = ADAM_B1 * m_ref[...] + (1.0 - ADAM_B1) * gg
        vn = ADAM_B2 * v_ref[...] + (1.0 - ADAM_B2) * (gg * gg)
        go_ref[...] = gg
        mo_ref[...] = mn
        vo_ref[...] = vn
        d_ref[...] = -ADAM_LR * ((mn / bc1) / (jnp.sqrt(vn / bc2) + ADAM_EPS) + ADAM_WD * w_ref[...])

    spec = pl.BlockSpec((tr, n), lambda i: (layer * nb + i, 0))
    shp = jax.ShapeDtypeStruct(w.shape, F32)
    return pl.pallas_call(
        body, name=name, out_shape=(shp, shp, shp, shp), grid=(nb,),
        in_specs=[pl.BlockSpec((tr, n), lambda i: (i, 0)), spec, spec, spec] + [ANY] * n_prev,
        out_specs=(spec, spec, spec, spec),
        input_output_aliases={4 + k: k for k in range(n_prev)},
        compiler_params=_params("parallel"),
    )(g, w, m, v, *(prev or ()))


def _rope_tables(seq, n_ctx):
    f = HEAD // 4
    inv = ROPE_BASE ** (-jnp.arange(f, dtype=F32) / f)
    pos = jnp.arange(seq)
    ang_r = (pos // GRID_W).astype(F32)[:, None] * inv[None, :]
    ang_c = (pos % GRID_W).astype(F32)[:, None] * inv[None, :]
    cos = jnp.concatenate([jnp.cos(ang_r)] * 2 + [jnp.cos(ang_c)] * 2, axis=-1)
    sin = jnp.concatenate([-jnp.sin(ang_r), jnp.sin(ang_r), -jnp.sin(ang_c), jnp.sin(ang_c)], axis=-1)
    cos = jnp.concatenate([jnp.ones((n_ctx, HEAD), F32), cos], axis=0)
    sin = jnp.concatenate([jnp.zeros((n_ctx, HEAD), F32), sin], axis=0)
    return cos, sin


def _pad_rows(a, rows):
    return jnp.concatenate([a, jnp.zeros((rows - a.shape[0],) + a.shape[1:], a.dtype)], axis=0)


def _pad_flat(parts, mult):
    flat = jnp.concatenate([p.reshape(-1) for p in parts])
    pad = (-flat.shape[0]) % mult
    return jnp.concatenate([flat, jnp.zeros((pad,), flat.dtype)])


def kernel(x, c, ctx, c_ctx, norm_w, w_mod, b_mod, w_in, conv_w, conv_norm_w, ret_norm_w, ret_decay_f, ret_decay_b, w_out, final_norm_w, loss_target, m_c_ctx, m_norm_w, m_w_mod, m_b_mod, m_w_in, m_conv_w, m_conv_norm_w, m_ret_norm_w, m_ret_decay_f, m_ret_decay_b, m_w_out, m_final_norm_w, v_c_ctx, v_norm_w, v_w_mod, v_b_mod, v_w_in, v_conv_w, v_conv_norm_w, v_ret_norm_w, v_ret_decay_f, v_ret_decay_b, v_w_out, v_final_norm_w):
    xi, yi, ci = _mesh_pos()
    b_idx = 4 * xi + 2 * yi + ci
    s_idx = 2 * xi + yi
    pos = jnp.stack([ci, s_idx]).astype(jnp.int32)

    x2, ctx2, tgt = x[0], ctx[0], loss_target[0]
    seq, d = x2.shape
    n_ctx = ctx2.shape[0]
    t = seq + n_ctx
    dc = d // 2
    nh = dc // HEAD
    nl, _, nm = w_mod.shape
    n4 = w_in.shape[2]
    r4 = w_out.shape[1]
    cw4 = conv_w.shape[2]
    nct = n_ctx // ROW_TILE
    ncc = n_ctx // CHUNK
    assert n_ctx % ROW_TILE == 0 and seq % ROW_TILE == 0 and dc % HEAD == 0 and nl * 3 <= 8

    win_b, wout_b = w_in.astype(BF16), w_out.astype(BF16)
    wg_in, wg_out = [None] * nl, [None] * nl
    wg_in[0], = _run_comm(_gather_comm(win_b[0], True), "gather_w_in_0")

    small = jnp.concatenate([jnp.broadcast_to(c, (8, d)), _pad_rows(conv_w.reshape(nl * 3, cw4), 8)], axis=1)
    small_g = _allgather_small(small, "gather_c_convw").reshape(N_DEV, 8, d + cw4)
    c_all = small_g[:, 0, :d]
    cw_full = small_g[0::2, :, d:].transpose(1, 0, 2).reshape(8, 4 * cw4)
    cw_l = [_pad_rows(cw_full[l * 3:(l + 1) * 3], 8) for l in range(nl)]

    c16 = _pad_rows(jnp.concatenate([c_all, c_ctx[None]], axis=0), 16)
    b_shard = lax.dynamic_slice(b_mod, (0, s_idx * nm), (nl, nm)).reshape(nl, 1, nm)
    sc16, modp = _mod_matvec(c16, w_mod, b_shard)
    modg = _allgather_small(modp.reshape(nl * 16, nm), "gather_mod").reshape(N_DEV, nl, 16, nm)
    mod_full = modg[0::2].transpose(1, 2, 0, 3).reshape(nl, 16, 4 * nm)
    mod_me = lax.dynamic_index_in_dim(mod_full, b_idx, axis=1, keepdims=False)
    mod_cx = mod_full[:, 8]
    zrow = jnp.zeros((d,), F32)
    rows = []
    for l in range(nl):
        gate_c = mod_cx[l, 2 * d:] if l < nl - 1 else zrow
        rows.append(jnp.stack([mod_me[l, :d], mod_me[l, d:2 * d], mod_me[l, 2 * d:],
                               mod_cx[l, :d], mod_cx[l, d:2 * d], gate_c, zrow, zrow]))

    cos, sin = _rope_tables(seq, n_ctx)
    xs = jnp.concatenate([ctx2, x2], axis=0)
    saved = []
    for l in range(nl):
        dec = jnp.stack([ret_decay_f[l], ret_decay_b[l]])
        hx, hx_t = _modulate(xs, norm_w[l][None], rows[l], nct, f"modulate_{l}")
        comms = [_gather_comm(wout_b[l], False)]
        if l + 1 < nl:
            comms += [_gather_comm(win_b[l + 1], True)]
        u, wg_out[l], *nxt = _matmul(hx, wg_in[l], "nn", f"in_proj_{l}", comms=comms, out_dtype=BF16)
        if nxt:
            wg_in[l + 1] = nxt[0]
        o_f, o_b, st_f, st_b, qr, kr = _ret_fwd(u, cos, sin, dec, ncc, f"ret_fwd_{l}")
        y, y_t, cv = _mix_fwd(u, o_f, o_b, cw_l[l], conv_norm_w[l][None], ret_norm_w[l][None], nct, f"mix_fwd_{l}")
        xs_new, proj = _out_proj(y, wg_out[l], xs, rows[l], n_ctx, f"out_proj_{l}")
        saved.append((xs, hx_t, u, o_f, o_b, st_f, st_b, qr, kr, y_t, cv, proj, dec))
        xs = xs_new

    g, facc = _final_loss(xs, tgt, final_norm_w[None], nct, "final_loss")
    full = [None] * nl
    join = []
    mine, other = ci.reshape(1).astype(jnp.int32), (1 - ci).reshape(1).astype(jnp.int32)
    dmx, dmc, gnw, gcnw, grnw, gcw, gdf, gdb = ([None] * nl for _ in range(8))
    for l in reversed(range(nl)):
        xs_l, hx_t, u, o_f, o_b, st_f, st_b, qr, kr, y_t, cv, proj, dec = saved[l]
        gp, gacc = _gate_bwd(g, proj, rows[l], nct, f"gate_bwd_{l}")
        gout = _matmul(y_t, gp, "nn", f"out_proj_dw_{l}", tm_t=1024, tn_t=1024, tk_t=2816, comms=join)
        if join:
            gout, *full[l + 1] = gout
        dy = _matmul(gp, wg_out[l], "nt", f"out_proj_dx_{l}")
        dab, daz, drz, dcv, do, macc = _mix_bwd(dy, u, cv, o_f, o_b, conv_norm_w[l][None], ret_norm_w[l][None],
                                                f"mix_bwd_{l}")
        dqf, dkf, dvf, dqb, dkb, dvb, dlg = _ret_bwd(u, qr, kr, do, st_f, st_b, dec, ncc, f"ret_bwd_{l}")
        du, cacc = _assemble_du(dcv, u, cw_l[l], dab, daz, drz, dqf, dqb, dkf, dkb, dvf, dvb, cos, sin, nct,
                                f"assemble_du_{l}")
        gout_v = gout.reshape(4, 2, r4 // 2, d)
        gin_other = _matmul(hx_t, du, "nn", f"in_proj_dw_other_{l}", tm_t=1024, tn_t=1024, tk_t=2816, row_half=other)
        gin_mine, land_in, land_out = _matmul(hx_t, du, "nn", f"in_proj_dw_mine_{l}", tm_t=1024, tn_t=1024,
                                              tk_t=2816, row_half=mine, comms=[_swap_comm(gin_other, gout_v)])
        p_in = _sum_half_in(gin_mine, land_in, f"grad_sum_half_in_{l}")
        p_out = _sum_half_out(pos, gout_v, land_out, f"grad_sum_half_out_{l}")
        dhx, l3_in, l3_out = _matmul(du, wg_in[l], "nt", f"in_proj_dx_{l}", comms=[_exchange_comm(p_in, p_out, n4)])
        join = [_join_comm(_sum_quarters_in(pos, p_in, l3_in, n4, f"grad_sum_quarters_in_{l}"),
                           _sum_quarters_out(pos, p_out, l3_out, f"grad_sum_quarters_out_{l}"))]
        g, nacc = _modulate_bwd(dhx, xs_l, g, norm_w[l][None], rows[l], nct, f"modulate_bwd_{l}", latent_only=l == 0)
        dmx[l] = jnp.concatenate([nacc[0], nacc[1], gacc[2]])
        dmc[l] = jnp.concatenate([nacc[3], nacc[4], gacc[5]])
        gnw[l], gcnw[l], grnw[l], gcw[l] = nacc[6], macc[0], macc[1], cacc[0:3]
        gdf[l], gdb[l] = dlg[0:nh, 0], dlg[nh:2 * nh, 0]
    grad_x = g[None]
    full[0] = _run_comm(join[0], "grad_join_0")

    res_in, res_out = None, None
    for l in reversed(range(nl)):
        res_in = _adamw_layer(full[l][0].reshape(d, n4), w_in.reshape(nl * d, n4), m_w_in.reshape(nl * d, n4),
                              v_w_in.reshape(nl * d, n4), l, res_in, f"adamw_w_in_{l}")
        res_out = _adamw_layer(full[l][1].reshape(r4, d), w_out.reshape(nl * r4, d), m_w_out.reshape(nl * r4, d),
                               v_w_out.reshape(nl * r4, d), l, res_out, f"adamw_w_out_{l}")
    g_w_in, d_w_in, nm_w_in, nv_w_in = res_in
    g_w_out, d_w_out, nm_w_out, nv_w_out = res_out

    seg = [jnp.stack(dmx), jnp.stack(dmc), jnp.stack(gnw), jnp.stack(gcnw), jnp.stack(grnw), facc[0],
           jnp.stack(gcw), jnp.stack(gdf), jnp.stack(gdb), facc[1, 0:1]]
    sizes = [int(s.size) for s in seg]
    offs = [sum(sizes[:i]) for i in range(len(sizes))]
    packed = _pad_flat(seg, 8 * 128)
    pw = packed.shape[0] // 8
    allv = _allgather_small(packed.reshape(8, pw), "gather_small_grads").reshape(N_DEV, 8 * pw)
    n_mod = nl * 3 * d
    dmx_all, dmc_all = allv[:, offs[0]:offs[0] + n_mod], allv[:, offs[1]:offs[1] + n_mod]
    rest_all = allv[:, offs[2]:]
    zeros_mod = jnp.zeros_like(dmc_all)
    stack = jnp.concatenate([
        jnp.concatenate([dmx_all, dmc_all], axis=0),
        jnp.concatenate([dmc_all, zeros_mod], axis=0),
        jnp.concatenate([rest_all, jnp.zeros_like(rest_all)], axis=0)], axis=1)
    sums = _rowsum(stack, "sum_small_grads")[0]
    g_b_mod = sums[:n_mod].reshape(nl, 3 * d)
    dmc_tot = sums[n_mod:2 * n_mod].reshape(nl, 3 * d)
    rest = sums[2 * n_mod:]

    def take(i):
        o = offs[i] - offs[2]
        return rest[o:o + sizes[i]]

    g_norm_w = take(2).reshape(nl, d)
    g_conv_norm_w = take(3).reshape(nl, dc)
    g_ret_norm_w = take(4).reshape(nl, dc)
    g_final_norm_w = take(5)
    g_conv_w = lax.dynamic_slice(take(6).reshape(nl, 3, dc), (0, 0, s_idx * cw4), (nl, 3, cw4))
    g_decay_f = take(7).reshape(nl, nh)
    g_decay_b = take(8).reshape(nl, nh)
    loss = take(9)[0]

    dmx_mine = lax.dynamic_slice(dmx_all.reshape(N_DEV, nl, 3 * d), (0, 0, s_idx * nm), (N_DEV, nl, nm))
    dmc_mine = lax.dynamic_slice(dmc_tot, (0, s_idx * nm), (nl, nm))
    dm = jnp.concatenate([dmx_mine.transpose(1, 0, 2), dmc_mine[:, None, :], jnp.zeros((nl, 7, nm), F32)], axis=1)
    g_w_mod, pc = _wmod_grad(sc16, dm, w_mod)
    pc_g = _allgather_small(pc[8:16], "gather_cctx_partials").reshape(N_DEV, 8, d)
    g_c_ctx = _cctx_grad(pc_g[0::2], jnp.broadcast_to(c_ctx[None], (8, d)))[0]
    d_w_mod, nm_w_mod, nv_w_mod = _adamw(g_w_mod.reshape(nl * d, nm), w_mod.reshape(nl * d, nm),
                                         m_w_mod.reshape(nl * d, nm), v_w_mod.reshape(nl * d, nm), "adamw_w_mod")

    small_g = [g_c_ctx, g_norm_w, g_b_mod, g_conv_w, g_conv_norm_w, g_ret_norm_w, g_decay_f, g_decay_b, g_final_norm_w]
    small_w = [c_ctx, norm_w, b_mod, conv_w, conv_norm_w, ret_norm_w, ret_decay_f, ret_decay_b, final_norm_w]
    small_m = [m_c_ctx, m_norm_w, m_b_mod, m_conv_w, m_conv_norm_w, m_ret_norm_w, m_ret_decay_f, m_ret_decay_b,
               m_final_norm_w]
    small_v = [v_c_ctx, v_norm_w, v_b_mod, v_conv_w, v_conv_norm_w, v_ret_norm_w, v_ret_decay_f, v_ret_decay_b,
               v_final_norm_w]
    pk = lambda parts: _pad_flat(parts, 8 * 128).reshape(-1, 128)
    sd, sm, sv = _adamw(pk(small_g), pk(small_w), pk(small_m), pk(small_v), "adamw_small")

    def unpack(flat2d):
        flat = flat2d.reshape(-1)
        out, o = [], 0
        for wgt in small_w:
            out.append(flat[o:o + wgt.size].reshape(wgt.shape))
            o += wgt.size
        return out

    sd, sm, sv = unpack(sd), unpack(sm), unpack(sv)

    def order(small, mod, w_in_, w_out_):
        return [small[0], small[1], mod, small[2], w_in_, small[3], small[4], small[5], small[6], small[7], w_out_,
                small[8]]

    grads = order(small_g, g_w_mod, g_w_in.reshape(nl, d, n4), g_w_out.reshape(nl, r4, d))
    deltas = order(sd, d_w_mod.reshape(nl, d, nm), d_w_in.reshape(nl, d, n4), d_w_out.reshape(nl, r4, d))
    new_m = order(sm, nm_w_mod.reshape(nl, d, nm), nm_w_in.reshape(nl, d, n4), nm_w_out.reshape(nl, r4, d))
    new_v = order(sv, nv_w_mod.reshape(nl, d, nm), nv_w_in.reshape(nl, d, n4), nv_w_out.reshape(nl, r4, d))
    return (loss, grad_x, *grads, *deltas, *new_m, *new_v)
```

```python
import functools

import jax
import jax.numpy as jnp
from jax import lax
from jax.experimental import pallas as pl
from jax.experimental.pallas import tpu as pltpu

F32 = jnp.float32
BF16 = jnp.bfloat16
EPS = 1e-6
CHUNK = 128
HEAD = 128
GRID_W = 64
ROPE_BASE = 10000.0
ROW_TILE = 256
RET_INTERLEAVE = 4
VMEM_LIMIT_BYTES = 56 * 1024 * 1024
N_DEV = 8
MESH = pl.DeviceIdType.MESH

ADAM_LR = 0.001
ADAM_B1 = 0.9
ADAM_B2 = 0.999
ADAM_EPS = 1e-08
ADAM_WD = 0.01
ADAM_STEP = 10

ANY = pl.BlockSpec(memory_space=pl.ANY)
VMEM_SPEC = pl.BlockSpec(memory_space=pltpu.VMEM)
SMEM_SPEC = pl.BlockSpec(memory_space=pltpu.SMEM)


def _params(*sem):
    return pltpu.CompilerParams(dimension_semantics=sem, vmem_limit_bytes=VMEM_LIMIT_BYTES)


def _pick(n, target, mult):
    best = None
    for d in range(mult, min(n, target) + 1, mult):
        if n % d == 0:
            best = d
    assert best is not None, (n, target, mult)
    return best


def _silu(z):
    return z * jax.nn.sigmoid(z)


def _dsilu(z):
    s = jax.nn.sigmoid(z)
    return s * (1.0 + z * (1.0 - s))


def _nn(a, b):
    return lax.dot_general(a, b, (((1,), (0,)), ((), ())), preferred_element_type=F32)


def _nt(a, b):
    return lax.dot_general(a, b, (((1,), (1,)), ((), ())), preferred_element_type=F32)


def _tn(a, b):
    return lax.dot_general(a, b, (((0,), (0,)), ((), ())), preferred_element_type=F32)


def _bf(a):
    return a.astype(BF16)


def _mesh_pos():
    return lax.axis_index("x"), lax.axis_index("y"), lax.axis_index("c")


def _allgather_small(blk, name):
    m_per, n = blk.shape

    def body(x_ref, out_ref, send_sems, recv_sems, local_sem):
        x, y, c = _mesh_pos()
        me, sibling = (x, y, c), (x, y, 1 - c)
        chips = [(1 - x, y), (x, 1 - y), (1 - x, 1 - y)]

        def rows(px, py, pc):
            return out_ref.at[pl.ds((4 * px + 2 * py + pc) * m_per, m_per), :]

        def copy(k, block, to, src=None):
            return pltpu.make_async_remote_copy(
                src_ref=rows(*block) if src is None else src, dst_ref=rows(*block),
                send_sem=send_sems.at[k], recv_sem=recv_sems.at[k], device_id=to, device_id_type=MESH)

        mine = pltpu.make_async_copy(x_ref, rows(*me), local_sem)
        mine.start()
        first = [copy(0, me, sibling, src=x_ref)]
        first += [copy(1 + j, me, (*chip, c), src=x_ref) for j, chip in enumerate(chips)]
        for cp in first:
            cp.start()
        passed = [copy(4 + j, (*chip, c), sibling) for j, chip in enumerate(chips)]
        for j, chip in enumerate(chips):
            copy(1 + j, (*chip, c), me).wait_recv()
            passed[j].start()
        copy(0, sibling, me).wait_recv()
        for j, chip in enumerate(chips):
            copy(4 + j, (*chip, 1 - c), me).wait_recv()
        for cp in first + passed:
            cp.wait_send()
        mine.wait()

    return pl.pallas_call(
        body, name=name,
        out_shape=jax.ShapeDtypeStruct((N_DEV * m_per, n), blk.dtype),
        in_specs=[VMEM_SPEC], out_specs=VMEM_SPEC,
        scratch_shapes=[pltpu.SemaphoreType.DMA((7,)), pltpu.SemaphoreType.DMA((7,)), pltpu.SemaphoreType.DMA],
    )(blk)


class _Comm:
    def __init__(self, ins, outs, n_remote, n_local, start, finish, aliases=None):
        self.ins, self.outs, self.n_remote, self.n_local = list(ins), list(outs), n_remote, n_local
        self.start, self.finish, self.aliases = start, finish, dict(aliases or {})

    def scratch(self):
        return [pltpu.SemaphoreType.DMA((self.n_remote,)), pltpu.SemaphoreType.DMA((self.n_remote,)),
                pltpu.SemaphoreType.DMA((max(self.n_local, 1),))]


def _run_comm(comm, name):
    n_in, n_out = len(comm.ins), len(comm.outs)

    def body(*refs):
        args = (refs[:n_in], refs[n_in:n_in + n_out]) + tuple(refs[n_in + n_out:])
        comm.start(*args)
        comm.finish(*args)

    return pl.pallas_call(
        body, name=name, out_shape=tuple(comm.outs), in_specs=[ANY] * n_in, out_specs=tuple([ANY] * n_out),
        scratch_shapes=comm.scratch(), input_output_aliases=comm.aliases,
    )(*comm.ins)


def _remote(src, dst, send, recv, k, to):
    return pltpu.make_async_remote_copy(src_ref=src, dst_ref=dst, send_sem=send.at[k], recv_sem=recv.at[k],
                                        device_id=to, device_id_type=MESH)


def _gather_comm(shard, by_columns):
    rows, cols = shard.shape
    hr = rows // 2

    def build(ins, outs, send, recv, loc):
        x, y, c = _mesh_pos()
        sib = (x, y, 1 - c)
        peers = [(1 - x, y), (x, 1 - y), (1 - x, 1 - y)]
        s_me = 2 * x + y

        def slot(s, h):
            if by_columns:
                return outs[0].at[pl.ds(pl.multiple_of(h * hr, 16), hr), pl.ds(pl.multiple_of(s * cols, 128), cols)]
            return outs[0].at[pl.ds(pl.multiple_of(s * rows + h * hr, 16), hr), :]

        def half(h):
            return ins[0].at[pl.ds(pl.multiple_of(h * hr, 16), hr), :]

        def local():
            if by_columns:
                mine = outs[0].at[:, pl.ds(pl.multiple_of(s_me * cols, 128), cols)]
            else:
                mine = outs[0].at[pl.ds(pl.multiple_of(s_me * rows, 16), rows), :]
            return [pltpu.make_async_copy(ins[0], mine, loc.at[0])]

        def per_copy(make):
            return [make(j, 2 * px + py, (px, py, c)) for j, (px, py) in enumerate(peers)]

        ici_send = lambda: per_copy(lambda j, s_p, to: _remote(half(c), slot(s_me, c), send, recv, j, to))
        ici_recv = lambda: per_copy(lambda j, s_p, to: _remote(half(c), slot(s_p, c), send, recv, j, to))
        d2d_send = lambda: per_copy(lambda j, s_p, to: _remote(slot(s_p, c), slot(s_p, c), send, recv, 3 + j, sib))
        d2d_recv = lambda: per_copy(lambda j, s_p, to: _remote(slot(s_p, c), slot(s_p, 1 - c), send, recv, 3 + j, sib))
        return local, ici_send, ici_recv, d2d_send, d2d_recv

    def start(*refs):
        local, ici_send, _, _, _ = build(*refs)
        for cp in local() + ici_send():
            cp.start()

    def finish(*refs):
        local, ici_send, ici_recv, d2d_send, d2d_recv = build(*refs)
        forwards = d2d_send()
        for arrived, forward in zip(ici_recv(), forwards):
            arrived.wait_recv()
            forward.start()
        for cp in d2d_recv():
            cp.wait_recv()
        for cp in ici_send() + forwards:
            cp.wait_send()
        for cp in local():
            cp.wait()

    full = (rows, 4 * cols) if by_columns else (4 * rows, cols)
    return _Comm([shard], [jax.ShapeDtypeStruct(full, BF16)], 6, 1, start, finish)


def _swap_comm(gin_other, gout_v):
    dh, w = gin_other.shape
    ns, _, rh, d = gout_v.shape

    def build(ins, outs, send, recv, loc):
        x, y, c = _mesh_pos()
        sib = (x, y, 1 - c)
        return [_remote(ins[0], outs[0], send, recv, 0, sib),
                _remote(ins[1].at[:, 1 - c], outs[1], send, recv, 1, sib)]

    def start(*refs):
        for cp in build(*refs):
            cp.start()

    def finish(*refs):
        for cp in build(*refs):
            cp.wait()

    outs = [jax.ShapeDtypeStruct((dh, w), F32), jax.ShapeDtypeStruct((ns, rh, d), F32)]
    return _Comm([gin_other, gout_v], outs, 2, 0, start, finish)


def _exchange_comm(p_in, p_out, n4):
    dh, _ = p_in.shape
    _, rh, d = p_out.shape

    def build(ins, outs, send, recv, loc):
        x, y, c = _mesh_pos()
        cps = []
        for j, (px, py) in enumerate([(1 - x, y), (x, 1 - y), (1 - x, 1 - y)]):
            s_p = 2 * px + py
            cps.append(_remote(ins[0].at[:, pl.ds(pl.multiple_of(s_p * n4, 128), n4)], outs[0].at[j], send, recv,
                               2 * j, (px, py, c)))
            cps.append(_remote(ins[1].at[s_p], outs[1].at[j], send, recv, 2 * j + 1, (px, py, c)))
        return cps

    def start(*refs):
        for cp in build(*refs):
            cp.start()

    def finish(*refs):
        cps = build(*refs)
        for cp in cps:
            cp.wait_recv()
        for cp in cps:
            cp.wait_send()

    outs = [jax.ShapeDtypeStruct((3, dh, n4), p_in.dtype), jax.ShapeDtypeStruct((3, rh, d), p_out.dtype)]
    return _Comm([p_in, p_out], outs, 6, 0, start, finish)


def _join_comm(q_in, q_out):
    def build(ins, outs, send, recv, loc):
        x, y, c = _mesh_pos()
        sib = (x, y, 1 - c)
        sends = lambda: [_remote(outs[a].at[c], outs[a].at[c], send, recv, a, sib) for a in range(2)]
        recvs = lambda: [_remote(outs[a].at[c], outs[a].at[1 - c], send, recv, a, sib) for a in range(2)]
        return sends, recvs

    def start(*refs):
        for cp in build(*refs)[0]():
            cp.start()

    def finish(*refs):
        sends, recvs = build(*refs)
        for cp in recvs():
            cp.wait_recv()
        for cp in sends():
            cp.wait_send()

    outs = [jax.ShapeDtypeStruct(q_in.shape, F32), jax.ShapeDtypeStruct(q_out.shape, F32)]
    return _Comm([q_in, q_out], outs, 2, 0, start, finish, aliases={0: 0, 1: 1})


def _sum_half_in(mine, land, name):
    dh, w = mine.shape
    tr, tc = _pick(dh, 256, 16), _pick(w, 2048, 128)

    def body(a_ref, b_ref, o_ref):
        o_ref[...] = (a_ref[...] + b_ref[...]).astype(BF16)

    spec = pl.BlockSpec((tr, tc), lambda i, j: (i, j))
    return pl.pallas_call(
        body, name=name, out_shape=jax.ShapeDtypeStruct((dh, w), BF16), grid=(dh // tr, w // tc),
        in_specs=[spec, spec], out_specs=spec, compiler_params=_params("parallel", "parallel"),
    )(mine, land)


def _sum_half_out(pos, gout_v, land, name):
    ns, _, rh, d = gout_v.shape

    def body(pos_ref, a_ref, b_ref, o_ref):
        o_ref[...] = (a_ref[...] + b_ref[...]).astype(BF16)

    return pl.pallas_call(
        body, name=name,
        out_shape=jax.ShapeDtypeStruct((ns, rh, d), BF16),
        grid_spec=pltpu.PrefetchScalarGridSpec(
            num_scalar_prefetch=1, grid=(ns,),
            in_specs=[pl.BlockSpec((None, None, rh, d), lambda s, p: (s, p[0], 0, 0)),
                      pl.BlockSpec((None, rh, d), lambda s, p: (s, 0, 0))],
            out_specs=pl.BlockSpec((None, rh, d), lambda s, p: (s, 0, 0))),
        compiler_params=_params("parallel"),
    )(pos, gout_v, land)


def _sum_quarters_in(pos, p_in, land3, n4, name):
    dh, _ = p_in.shape
    tr = _pick(dh, 128, 16)

    def body(pos_ref, a_ref, b_ref, o_ref):
        f = lambda v: v.astype(F32)
        o_ref[...] = ((f(a_ref[...]) + f(b_ref[0])) + f(b_ref[1])) + f(b_ref[2])

    return pl.pallas_call(
        body, name=name,
        out_shape=jax.ShapeDtypeStruct((2, dh, n4), F32),
        grid_spec=pltpu.PrefetchScalarGridSpec(
            num_scalar_prefetch=1, grid=(dh // tr,),
            in_specs=[pl.BlockSpec((tr, n4), lambda i, p: (i, p[1])),
                      pl.BlockSpec((3, tr, n4), lambda i, p: (0, i, 0))],
            out_specs=pl.BlockSpec((None, tr, n4), lambda i, p: (p[0], i, 0))),
        compiler_params=_params("parallel"),
    )(pos, p_in, land3)


def _sum_quarters_out(pos, p_out, land3, name):
    _, rh, d = p_out.shape

    def body(pos_ref, a_ref, b_ref, o_ref):
        f = lambda v: v.astype(F32)
        o_ref[...] = ((f(a_ref[...]) + f(b_ref[0])) + f(b_ref[1])) + f(b_ref[2])

    return pl.pallas_call(
        body, name=name,
        out_shape=jax.ShapeDtypeStruct((2, rh, d), F32),
        grid_spec=pltpu.PrefetchScalarGridSpec(
            num_scalar_prefetch=1, grid=(1,),
            in_specs=[pl.BlockSpec((None, rh, d), lambda i, p: (p[1], 0, 0)),
                      pl.BlockSpec((3, rh, d), lambda i, p: (0, 0, 0))],
            out_specs=pl.BlockSpec((None, rh, d), lambda i, p: (p[0], 0, 0))),
        compiler_params=_params("arbitrary"),
    )(pos, p_out, land3)


def _matmul(a, b, kind, name, tm_t=1056, tn_t=1024, tk_t=2048, comms=(), row_half=None, out_dtype=F32):
    if kind == "nn":
        (m, k), n = a.shape, b.shape[1]
    elif kind == "nt":
        (m, k), n = a.shape, b.shape[0]
    else:
        (k, m), n = a.shape, b.shape[1]
    if row_half is None:
        row_half = jnp.zeros((1,), jnp.int32)
    else:
        assert kind == "nn"
        m = m // 2
    tm = _pick(m, tm_t, 128 if kind == "tn" else 16)
    tn = _pick(n, tn_t, 128)
    tk = _pick(k, tk_t, 16 if kind == "tn" else 128)
    nk = k // tk
    nbm = m // tm
    grid = (nbm, n // tn, nk)
    dot = {"nn": _nn, "nt": _nt, "tn": _tn}[kind]
    a_spec = {"nn": pl.BlockSpec((tm, tk), lambda i, j, q, p: (i + p[0] * nbm, q)),
              "nt": pl.BlockSpec((tm, tk), lambda i, j, q, p: (i, q)),
              "tn": pl.BlockSpec((tk, tm), lambda i, j, q, p: (q, i))}[kind]
    b_spec = {"nn": pl.BlockSpec((tk, tn), lambda i, j, q, p: (q, j)),
              "nt": pl.BlockSpec((tn, tk), lambda i, j, q, p: (j, q)),
              "tn": pl.BlockSpec((tk, tn), lambda i, j, q, p: (q, j))}[kind]
    comm_ins = [x for cm in comms for x in cm.ins]
    comm_outs = [x for cm in comms for x in cm.outs]
    aliases, scratch = {}, []
    i_off, o_off = 3, 1
    for cm in comms:
        for ci, co in cm.aliases.items():
            aliases[i_off + ci] = o_off + co
        i_off, o_off = i_off + len(cm.ins), o_off + len(cm.outs)
        scratch += cm.scratch()

    def body(half_ref, a_ref, b_ref, *rest):
        cin = rest[:len(comm_ins)]
        o_ref = rest[len(comm_ins)]
        cout = rest[len(comm_ins) + 1:len(comm_ins) + 1 + len(comm_outs)]
        acc_ref = rest[len(comm_ins) + 1 + len(comm_outs)]
        sems = rest[len(comm_ins) + 2 + len(comm_outs):]
        ids = [pl.program_id(ax) for ax in range(3)]
        first = functools.reduce(jnp.logical_and, [ids[ax] == 0 for ax in range(3)])
        last = functools.reduce(jnp.logical_and, [ids[ax] == grid[ax] - 1 for ax in range(3)])

        def comm_args(idx):
            i0 = sum(len(cm.ins) for cm in comms[:idx])
            o0 = sum(len(cm.outs) for cm in comms[:idx])
            cm = comms[idx]
            return (cin[i0:i0 + len(cm.ins)], cout[o0:o0 + len(cm.outs)]) + tuple(sems[3 * idx:3 * idx + 3])

        if comms:
            @pl.when(first)
            def _():
                for idx, cm in enumerate(comms):
                    cm.start(*comm_args(idx))

        q = ids[2]
        p = dot(a_ref[...], b_ref[...])
        if nk == 1:
            o_ref[...] = p.astype(out_dtype)
        else:
            @pl.when(q == 0)
            def _():
                acc_ref[...] = p

            @pl.when(q > 0)
            def _():
                acc_ref[...] += p

            @pl.when(q == nk - 1)
            def _():
                o_ref[...] = acc_ref[...].astype(out_dtype)

        if comms:
            @pl.when(last)
            def _():
                for idx, cm in enumerate(comms):
                    cm.finish(*comm_args(idx))

    sem = ("arbitrary",) * 3 if comms else ("parallel", "parallel", "arbitrary")
    res = pl.pallas_call(
        body, name=name,
        out_shape=(jax.ShapeDtypeStruct((m, n), out_dtype), *comm_outs),
        grid_spec=pltpu.PrefetchScalarGridSpec(
            num_scalar_prefetch=1, grid=grid,
            in_specs=[a_spec, b_spec] + [ANY] * len(comm_ins),
            out_specs=(pl.BlockSpec((tm, tn), lambda i, j, q, p: (i, j)), *([ANY] * len(comm_outs))),
            scratch_shapes=[pltpu.VMEM((tm, tn) if nk > 1 else (8, 128), F32)] + scratch),
        input_output_aliases=aliases,
        compiler_params=_params(*sem),
    )(row_half, a, b, *comm_ins)
    return res if comms else res[0]


def _out_proj(y, w_out, xs, rows, n_ctx, name):
    t, k = y.shape
    d = w_out.shape[1]
    tm, tn = _pick(t, 1056, 16), _pick(d, 1024, 128)

    def body(y_ref, w_ref, x_ref, r_ref, xo_ref, p_ref):
        i = pl.program_id(0)
        p = _nn(y_ref[...], w_ref[...])
        ridx = i * tm + lax.broadcasted_iota(jnp.int32, (tm, 1), 0)
        gate = jnp.where(ridx < n_ctx, r_ref[5:6, :], r_ref[2:3, :])
        p_ref[...] = p.astype(BF16)
        xo_ref[...] = x_ref[...] + gate * p

    return pl.pallas_call(
        body, name=name,
        out_shape=(jax.ShapeDtypeStruct((t, d), F32), jax.ShapeDtypeStruct((t, d), BF16)),
        grid=(t // tm, d // tn),
        in_specs=[pl.BlockSpec((tm, k), lambda i, j: (i, 0)), pl.BlockSpec((k, tn), lambda i, j: (0, j)),
                  pl.BlockSpec((tm, tn), lambda i, j: (i, j)), pl.BlockSpec((8, tn), lambda i, j: (0, j))],
        out_specs=(pl.BlockSpec((tm, tn), lambda i, j: (i, j)), pl.BlockSpec((tm, tn), lambda i, j: (i, j))),
        compiler_params=_params("parallel", "parallel"),
    )(y, w_out, xs, rows)


def _modulate(xs, nw, rows, nct, name):
    t, d = xs.shape
    tr = ROW_TILE

    def body(x_ref, nw_ref, r_ref, o_ref, ot_ref):
        is_ctx = pl.program_id(0) < nct
        x = x_ref[...]
        xn = (x * lax.rsqrt(jnp.mean(x * x, axis=-1, keepdims=True) + EPS)) * nw_ref[...]
        shift = jnp.where(is_ctx, r_ref[3:4, :], r_ref[0:1, :])
        scale = jnp.where(is_ctx, r_ref[4:5, :], r_ref[1:2, :])
        hx = xn * (1.0 + scale) + shift
        o_ref[...] = hx.astype(BF16)
        ot_ref[...] = hx.T.astype(BF16)

    return pl.pallas_call(
        body, name=name,
        out_shape=(jax.ShapeDtypeStruct((t, d), BF16), jax.ShapeDtypeStruct((d, t), BF16)), grid=(t // tr,),
        in_specs=[pl.BlockSpec((tr, d), lambda i: (i, 0)), pl.BlockSpec((1, d), lambda i: (0, 0)),
                  pl.BlockSpec((8, d), lambda i: (0, 0))],
        out_specs=(pl.BlockSpec((tr, d), lambda i: (i, 0)), pl.BlockSpec((d, tr), lambda i: (0, i))),
        compiler_params=_params("parallel"),
    )(xs, nw, rows)


def _gate_step(g, is_ctx, below, outs):
    p_ref, r_ref = below
    gp_ref, acc_ref = outs
    gate = jnp.where(is_ctx, r_ref[5:6, :], r_ref[2:3, :])
    gp_ref[...] = (g * gate).astype(BF16)
    s = jnp.sum(g * p_ref[...].astype(F32), axis=0, keepdims=True)
    zero = jnp.zeros_like(s)
    acc_ref[2:3, :] += jnp.where(is_ctx, zero, s)
    acc_ref[5:6, :] += jnp.where(is_ctx, s, zero)


def _modulate_bwd(dhx, xs, g, nw, rows, nct, name, below=None):
    t, d = xs.shape
    tr = ROW_TILE
    latent_only = below is None
    t_out = t - nct * tr if latent_only else t
    out_row = (lambda i: (jnp.maximum(i - nct, 0), 0)) if latent_only else (lambda i: (i, 0))
    n_below = 0 if latent_only else 2

    def body(dh_ref, x_ref, g_ref, nw_ref, r_ref, *rest):
        go_ref, acc_ref = rest[n_below:n_below + 2]
        i = pl.program_id(0)
        is_ctx = i < nct

        @pl.when(i == 0)
        def _():
            acc_ref[...] = jnp.zeros_like(acc_ref)
            if n_below:
                rest[n_below + 3][...] = jnp.zeros_like(rest[n_below + 3])

        x, dh = x_ref[...], dh_ref[...].astype(F32)
        r = lax.rsqrt(jnp.mean(x * x, axis=-1, keepdims=True) + EPS)
        xh = x * r
        xn = xh * nw_ref[...]
        scale = jnp.where(is_ctx, r_ref[4:5, :], r_ref[1:2, :])
        d_shift = jnp.sum(dh, axis=0, keepdims=True)
        d_scale = jnp.sum(dh * xn, axis=0, keepdims=True)
        dxn = dh * (1.0 + scale)
        acc_ref[6:7, :] += jnp.sum(dxn * xh, axis=0, keepdims=True)
        dxh = dxn * nw_ref[...]
        dx = r * (dxh - xh * jnp.mean(dxh * xh, axis=-1, keepdims=True))
        g_in = g_ref[...] + dx
        go_ref[...] = g_in
        zero = jnp.zeros_like(d_shift)
        acc_ref[0:1, :] += jnp.where(is_ctx, zero, d_shift)
        acc_ref[1:2, :] += jnp.where(is_ctx, zero, d_scale)
        acc_ref[3:4, :] += jnp.where(is_ctx, d_shift, zero)
        acc_ref[4:5, :] += jnp.where(is_ctx, d_scale, zero)
        if n_below:
            _gate_step(g_in, is_ctx, rest[:2], rest[n_below + 2:])

    row = pl.BlockSpec((tr, d), lambda i: (i, 0))
    vecs = pl.BlockSpec((8, d), lambda i: (0, 0))
    return pl.pallas_call(
        body, name=name,
        out_shape=(jax.ShapeDtypeStruct((t_out, d), F32), jax.ShapeDtypeStruct((8, d), F32))
        + ((jax.ShapeDtypeStruct((t, d), BF16), jax.ShapeDtypeStruct((8, d), F32)) if n_below else ()),
        grid=(t // tr,),
        in_specs=[row, row, row, pl.BlockSpec((1, d), lambda i: (0, 0)), vecs] + ([row, vecs] if n_below else []),
        out_specs=(pl.BlockSpec((tr, d), out_row), vecs) + ((row, vecs) if n_below else ()),
        compiler_params=_params("arbitrary"),
    )(dhx, xs, g, nw, rows, *(below or ()))


def _final_loss(xs, tgt, fw, below, nct, name):
    t, d = xs.shape
    tr = ROW_TILE

    def body(x_ref, t_ref, w_ref, p_ref, r_ref, g_ref, acc_ref, gp_ref, gacc_ref):
        i = pl.program_id(0)
        is_ctx = i < nct

        @pl.when(i == 0)
        def _():
            acc_ref[...] = jnp.zeros_like(acc_ref)
            gacc_ref[...] = jnp.zeros_like(gacc_ref)

        @pl.when(is_ctx)
        def _():
            g_ref[...] = jnp.zeros_like(g_ref)
            gp_ref[...] = jnp.zeros_like(gp_ref)

        @pl.when(jnp.logical_not(is_ctx))
        def _():
            x = x_ref[...]
            r = lax.rsqrt(jnp.mean(x * x, axis=-1, keepdims=True) + EPS)
            xh = x * r
            e = xh * w_ref[...] - t_ref[...]
            acc_ref[1:2, :] += jnp.zeros((1, d), F32) + jnp.sum(e * e) * (0.5 / d)
            dy = e * (1.0 / d)
            acc_ref[0:1, :] += jnp.sum(dy * xh, axis=0, keepdims=True)
            dxh = dy * w_ref[...]
            g = r * (dxh - xh * jnp.mean(dxh * xh, axis=-1, keepdims=True))
            g_ref[...] = g
            _gate_step(g, is_ctx, (p_ref, r_ref), (gp_ref, gacc_ref))

    row = pl.BlockSpec((tr, d), lambda i: (i, 0))
    vecs = pl.BlockSpec((8, d), lambda i: (0, 0))
    return pl.pallas_call(
        body, name=name,
        out_shape=(jax.ShapeDtypeStruct((t, d), F32), jax.ShapeDtypeStruct((8, d), F32),
                   jax.ShapeDtypeStruct((t, d), BF16), jax.ShapeDtypeStruct((8, d), F32)), grid=(t // tr,),
        in_specs=[row, pl.BlockSpec((tr, d), lambda i: (jnp.maximum(i - nct, 0), 0)),
                  pl.BlockSpec((1, d), lambda i: (0, 0)), row, vecs],
        out_specs=(row, vecs, row, vecs),
        compiler_params=_params("arbitrary"),
    )(xs, tgt, fw, *below)


def _seq_edges(i, nct, nt):
    has_prev = jnp.logical_and(i != 0, i != nct)
    has_next = jnp.logical_and(i != nct - 1, i != nt - 1)
    return has_prev, has_next


def _shift_rows(a, before, after):
    tr = a.shape[0]
    ridx = lax.broadcasted_iota(jnp.int32, (tr, 1), 0)
    down = jnp.where(ridx == 0, before, pltpu.roll(a, 1, 0))
    up = jnp.where(ridx == tr - 1, after, pltpu.roll(a, tr - 1, 0))
    return down, up


def _halo_specs(tr, dc, col, t, hr):
    per = tr // hr
    prev = pl.BlockSpec((hr, dc), lambda i: (jnp.maximum(i * per - 1, 0), col))
    nxt = pl.BlockSpec((hr, dc), lambda i: (jnp.minimum((i + 1) * per, t // hr - 1), col))
    return prev, nxt


def _mix_fwd(u, o_f, o_b, cw, cnw, rnw, nct, name):
    t, w = u.shape
    dc = w // 8
    nh = dc // HEAD
    tr = ROW_TILE
    nt = t // tr
    col = lambda cidx: pl.BlockSpec((tr, dc), lambda i: (i, cidx))
    hr = 16
    acp, acn = _halo_specs(tr, dc, 2, t, hr)
    ahp, ahn = _halo_specs(tr, dc, 0, t, hr)
    f32 = lambda v: v.astype(F32)

    def body(ah, ab, ac, az, rz, acp_r, acn_r, ahp_r, ahn_r, of_r, ob_r, cw_r, cnw_r, rnw_r, y_ref, yt_ref, cv_ref):
        i = pl.program_id(0)
        has_prev, has_next = _seq_edges(i, nct, nt)
        ch = f32(ac[...]) * f32(ah[...])
        before = jnp.where(has_prev, f32(acp_r[hr - 1:hr, :]) * f32(ahp_r[hr - 1:hr, :]), 0.0)
        after = jnp.where(has_next, f32(acn_r[0:1, :]) * f32(ahn_r[0:1, :]), 0.0)
        down, up = _shift_rows(ch, before, after)
        cv = down * cw_r[0:1, :] + ch * cw_r[1:2, :] + up * cw_r[2:3, :]
        cv_ref[...] = cv
        y1 = f32(ab[...]) * cv
        yn = (y1 * lax.rsqrt(jnp.mean(y1 * y1, axis=-1, keepdims=True) + EPS)) * cnw_r[...]
        yc = _silu(f32(az[...])) * yn
        y_ref[:, 0:dc] = yc.astype(BF16)
        yt_ref[0:dc, :] = yc.T.astype(BF16)
        for h in range(nh):
            sl = slice(h * HEAD, (h + 1) * HEAD)
            o = of_r[:, sl] + ob_r[:, sl]
            dev = o - jnp.mean(o, axis=-1, keepdims=True)
            on = dev * lax.rsqrt(jnp.mean(dev * dev, axis=-1, keepdims=True) + EPS)
            yr = _silu(f32(rz[:, sl])) * (on * rnw_r[:, sl])
            y_ref[:, dc + h * HEAD:dc + (h + 1) * HEAD] = yr.astype(BF16)
            yt_ref[dc + h * HEAD:dc + (h + 1) * HEAD, :] = yr.T.astype(BF16)

    vec = pl.BlockSpec((1, dc), lambda i: (0, 0))
    return pl.pallas_call(
        body, name=name,
        out_shape=(jax.ShapeDtypeStruct((t, 2 * dc), BF16), jax.ShapeDtypeStruct((2 * dc, t), BF16),
                   jax.ShapeDtypeStruct((t, dc), F32)), grid=(nt,),
        in_specs=[col(0), col(1), col(2), col(3), col(7), acp, acn, ahp, ahn,
                  pl.BlockSpec((tr, dc), lambda i: (i, 0)), pl.BlockSpec((tr, dc), lambda i: (i, 0)),
                  pl.BlockSpec((8, dc), lambda i: (0, 0)), vec, vec],
        out_specs=(pl.BlockSpec((tr, 2 * dc), lambda i: (i, 0)), pl.BlockSpec((2 * dc, tr), lambda i: (0, i)),
                   pl.BlockSpec((tr, dc), lambda i: (i, 0))),
        compiler_params=_params("parallel"),
    )(u, u, u, u, u, u, u, u, u, o_f, o_b, cw, cnw, rnw)


def _mix_bwd(dy, u, cv, o_f, o_b, cnw, rnw, name):
    t, w = u.shape
    dc = w // 8
    nh = dc // HEAD
    tr = ROW_TILE
    col = lambda cidx: pl.BlockSpec((tr, dc), lambda i: (i, cidx))

    def body(dyc, dyr, ab, az, rz, cv_r, of_r, ob_r, cnw_r, rnw_r, dab, daz, drz, dcv, do, acc_ref):
        @pl.when(pl.program_id(0) == 0)
        def _():
            acc_ref[...] = jnp.zeros_like(acc_ref)

        z, b, c_out = az[...].astype(F32), ab[...].astype(F32), cv_r[...]
        y1 = b * c_out
        r = lax.rsqrt(jnp.mean(y1 * y1, axis=-1, keepdims=True) + EPS)
        yh = y1 * r
        yn = yh * cnw_r[...]
        dyc_v = dyc[...]
        daz[...] = (dyc_v * yn * _dsilu(z)).astype(BF16)
        dyn = dyc_v * _silu(z)
        acc_ref[0:1, :] += jnp.sum(dyn * yh, axis=0, keepdims=True)
        dyh = dyn * cnw_r[...]
        dy1 = r * (dyh - yh * jnp.mean(dyh * yh, axis=-1, keepdims=True))
        dab[...] = (dy1 * c_out).astype(BF16)
        dcv[...] = dy1 * b
        for h in range(nh):
            sl = slice(h * HEAD, (h + 1) * HEAD)
            o = of_r[:, sl] + ob_r[:, sl]
            dev = o - jnp.mean(o, axis=-1, keepdims=True)
            rs = lax.rsqrt(jnp.mean(dev * dev, axis=-1, keepdims=True) + EPS)
            on = dev * rs
            zz, dyr_v, gw = rz[:, sl].astype(F32), dyr[:, sl], rnw_r[:, sl]
            drz[:, sl] = (dyr_v * (on * gw) * _dsilu(zz)).astype(BF16)
            dong = dyr_v * _silu(zz)
            acc_ref[1:2, sl] += jnp.sum(dong * on, axis=0, keepdims=True)
            don = dong * gw
            do[:, sl] = rs * (don - jnp.mean(don, axis=-1, keepdims=True)
                              - on * jnp.mean(don * on, axis=-1, keepdims=True))

    vec = pl.BlockSpec((1, dc), lambda i: (0, 0))
    row = pl.BlockSpec((tr, dc), lambda i: (i, 0))
    return pl.pallas_call(
        body, name=name,
        out_shape=(jax.ShapeDtypeStruct((t, dc), BF16), jax.ShapeDtypeStruct((t, dc), BF16),
                   jax.ShapeDtypeStruct((t, dc), BF16), jax.ShapeDtypeStruct((t, dc), F32),
                   jax.ShapeDtypeStruct((t, dc), F32), jax.ShapeDtypeStruct((8, dc), F32)),
        grid=(t // tr,),
        in_specs=[col(0), col(1), col(1), col(3), col(7), row, row, row, vec, vec],
        out_specs=(row, row, row, row, row, pl.BlockSpec((8, dc), lambda i: (0, 0))),
        compiler_params=_params("arbitrary"),
    )(dy, dy, u, u, u, cv, o_f, o_b, cnw, rnw)


def _rope_masks():
    lane = lax.broadcasted_iota(jnp.int32, (1, HEAD), 1)
    return (lane % (HEAD // 2)) >= (HEAD // 4)


def _rot(a, upper):
    return jnp.where(upper, pltpu.roll(a, HEAD // 4, 1), pltpu.roll(a, HEAD - HEAD // 4, 1))


def _assemble_du(dcv, u, cw, dab, daz, drz, dq_f, dq_b, dk_f, dk_b, dv_f, dv_b, cos, sin, nct, name):
    t, w = u.shape
    dc = w // 8
    nh = dc // HEAD
    tr = ROW_TILE
    nt = t // tr
    k_scale = HEAD ** -0.5
    col = lambda cidx: pl.BlockSpec((tr, dc), lambda i: (i, cidx))
    row = pl.BlockSpec((tr, dc), lambda i: (i, 0))
    dcp, dcn = _halo_specs(tr, dc, 0, t, 8)
    f32 = lambda v: v.astype(F32)

    def body(dcv_r, dcp_r, dcn_r, ah, ac, cw_r, dab_r, daz_r, drz_r, dqf, dqb, dkf, dkb, dvf, dvb, cos_r, sin_r,
             du, acc_ref):
        i = pl.program_id(0)

        @pl.when(i == 0)
        def _():
            acc_ref[...] = jnp.zeros_like(acc_ref)

        has_prev, has_next = _seq_edges(i, nct, nt)
        d_c = dcv_r[...]
        before = jnp.where(has_prev, dcp_r[7:8, :], 0.0)
        after = jnp.where(has_next, dcn_r[0:1, :], 0.0)
        d_prev, d_next = _shift_rows(d_c, before, after)
        a_h, a_c = f32(ah[...]), f32(ac[...])
        ch = a_c * a_h
        dch = d_next * cw_r[0:1, :] + d_c * cw_r[1:2, :] + d_prev * cw_r[2:3, :]
        acc_ref[0:1, :] += jnp.sum(ch * d_next, axis=0, keepdims=True)
        acc_ref[1:2, :] += jnp.sum(ch * d_c, axis=0, keepdims=True)
        acc_ref[2:3, :] += jnp.sum(ch * d_prev, axis=0, keepdims=True)
        du[:, 0:dc] = (dch * a_c).astype(BF16)
        du[:, dc:2 * dc] = dab_r[...]
        du[:, 2 * dc:3 * dc] = (dch * a_h).astype(BF16)
        du[:, 3 * dc:4 * dc] = daz_r[...]
        du[:, 7 * dc:8 * dc] = drz_r[...]
        du[:, 6 * dc:7 * dc] = (f32(dvf[...]) + f32(dvb[...])).astype(BF16)
        upper = _rope_masks()
        cs, sn = cos_r[...], sin_r[...]
        for h in range(nh):
            sl = slice(h * HEAD, (h + 1) * HEAD)
            dq = f32(dqf[:, sl]) + f32(dqb[:, sl])
            dk = (f32(dkf[:, sl]) + f32(dkb[:, sl])) * k_scale
            du[:, 4 * dc + h * HEAD:4 * dc + (h + 1) * HEAD] = (dq * cs + _rot(dq * sn, upper)).astype(BF16)
            du[:, 5 * dc + h * HEAD:5 * dc + (h + 1) * HEAD] = (dk * cs + _rot(dk * sn, upper)).astype(BF16)

    tab = pl.BlockSpec((tr, HEAD), lambda i: (i, 0))
    return pl.pallas_call(
        body, name=name,
        out_shape=(jax.ShapeDtypeStruct((t, w), BF16), jax.ShapeDtypeStruct((8, dc), F32)), grid=(nt,),
        in_specs=[row, dcp, dcn, col(0), col(2), pl.BlockSpec((8, dc), lambda i: (0, 0)),
                  row, row, row, row, row, row, row, row, row, tab, tab],
        out_specs=(pl.BlockSpec((tr, w), lambda i: (i, 0)), pl.BlockSpec((8, dc), lambda i: (0, 0))),
        compiler_params=_params("arbitrary"),
    )(dcv, dcv, dcv, u, u, cw, dab, daz, drz, dq_f, dq_b, dk_f, dk_b, dv_f, dv_b, cos, sin)


def _bwd_dir_chunk(s, ncc, ns):
    return jnp.where(s < ncc, ncc - 1 - s, ns + ncc - 1 - s)


def _position_weights(d):
    c = CHUNK
    ii = lax.broadcasted_iota(jnp.int32, (c, c), 0)
    jj = lax.broadcasted_iota(jnp.int32, (c, c), 1)
    sd = ((ii - jj) if d == 0 else (jj - ii)).astype(F32)
    pos = lax.broadcasted_iota(jnp.int32, (c, HEAD), 0).astype(F32)
    qw, kw = (pos + 1.0, (c - 1.0) - pos) if d == 0 else (c - pos, pos)
    return sd, qw, kw


def _fill_decay_tables(dec_ref, nh, dm_t, qd_t, kd_t, gc_t):
    for d in range(2):
        sd, qw, kw = _position_weights(d)
        for h in range(nh):
            r = d * nh + h
            lg = -jnp.exp(jnp.zeros((1, HEAD), F32) + dec_ref[d, h])
            dm_t[r] = jnp.where(sd >= 0, jnp.exp(lg * jnp.maximum(sd, 0.0)), 0.0)
            qd_t[r] = jnp.exp(lg * qw)
            kd_t[r] = jnp.exp(lg * kw)
            gc_t[r] = jnp.zeros((8, HEAD), F32) + jnp.exp(lg * float(CHUNK))


def _decay_table_scratch(nh):
    return [pltpu.VMEM((2 * nh, CHUNK, CHUNK), F32), pltpu.VMEM((2 * nh, CHUNK, HEAD), F32),
            pltpu.VMEM((2 * nh, CHUNK, HEAD), F32), pltpu.VMEM((2 * nh, 8, HEAD), F32)]


def _ret_fwd(u, cos, sin, dec, ncc, name):
    t, w = u.shape
    dc = w // 8
    nh = dc // HEAD
    c = CHUNK
    ns = t // c
    k_scale = HEAD ** -0.5
    fmap = lambda g: (lambda s: (s, g))
    bmap = lambda g: (lambda s: (_bwd_dir_chunk(s, ncc, ns), g))
    blk = lambda m: pl.BlockSpec((c, dc), m)
    tab = lambda m: pl.BlockSpec((c, HEAD), m)
    st_f = pl.BlockSpec((nh, None, HEAD, HEAD), lambda s: (0, s, 0, 0))
    st_b = pl.BlockSpec((nh, None, HEAD, HEAD), lambda s: (0, _bwd_dir_chunk(s, ncc, ns), 0, 0))

    def body(qf, kf, vf, qb, kb, vb, cf, sf, cb, sb, dec_ref, of_ref, ob_ref, stf_ref, stb_ref, qr_ref, kr_ref,
             s_f, s_b, dm_t, qd_t, kd_t, gc_t):
        @pl.when(pl.program_id(0) == 0)
        def _():
            s_f[...] = jnp.zeros_like(s_f)
            s_b[...] = jnp.zeros_like(s_b)
            _fill_decay_tables(dec_ref, nh, dm_t, qd_t, kd_t, gc_t)

        upper = _rope_masks()
        dirs = ((qf, kf, vf, cf, sf, of_ref, stf_ref, s_f), (qb, kb, vb, cb, sb, ob_ref, stb_ref, s_b))
        group = RET_INTERLEAVE if nh % RET_INTERLEAVE == 0 else 1
        blocks = [[(h, d) for h in range(h0, h0 + group) for d in range(2)] for h0 in range(0, nh, group)]
        for blk_list in blocks:
            first = []
            for h, d in blk_list:
                q_r, k_r, v_r, cs_r, sn_r, _, st_ref, st = dirs[d]
                sl = slice(h * HEAD, (h + 1) * HEAD)
                r = d * nh + h
                cs, sn = cs_r[...], sn_r[...]
                q = q_r[:, sl].astype(F32)
                q = q * cs + _rot(q, upper) * sn
                k = k_r[:, sl].astype(F32)
                k = (k * cs + _rot(k, upper) * sn) * k_scale
                if d == 0:
                    qr_ref[:, sl] = q
                    kr_ref[:, sl] = k
                vv = v_r[:, sl]
                state = st[h]
                st_ref[h] = state
                p = _nt(_bf(q), _bf(k)) * dm_t[r]
                kv = _tn(_bf(k * kd_t[r]), vv)
                first.append((q, vv, state, p, kv))
            for idx, (h, d) in enumerate(blk_list):
                _, _, _, _, _, o_ref, _, st = dirs[d]
                sl = slice(h * HEAD, (h + 1) * HEAD)
                r = d * nh + h
                q, vv, state, p, kv = first[idx]
                o_ref[:, sl] = _nn(jnp.concatenate([_bf(p), _bf(q * qd_t[r])], axis=1),
                                   jnp.concatenate([vv, _bf(state)], axis=0))
                st[h] = gc_t[r, 0:1, :] * state + kv

    return pl.pallas_call(
        body, name=name,
        out_shape=(jax.ShapeDtypeStruct((t, dc), F32), jax.ShapeDtypeStruct((t, dc), F32),
                   jax.ShapeDtypeStruct((nh, ns, HEAD, HEAD), F32), jax.ShapeDtypeStruct((nh, ns, HEAD, HEAD), F32),
                   jax.ShapeDtypeStruct((t, dc), F32), jax.ShapeDtypeStruct((t, dc), F32)),
        grid=(ns,),
        in_specs=[blk(fmap(4)), blk(fmap(5)), blk(fmap(6)), blk(bmap(4)), blk(bmap(5)), blk(bmap(6)),
                  tab(fmap(0)), tab(fmap(0)), tab(bmap(0)), tab(bmap(0)), SMEM_SPEC],
        out_specs=(blk(fmap(0)), blk(bmap(0)), st_f, st_b, blk(fmap(0)), blk(fmap(0))),
        scratch_shapes=[pltpu.VMEM((nh, HEAD, HEAD), F32), pltpu.VMEM((nh, HEAD, HEAD), F32)]
        + _decay_table_scratch(nh),
        compiler_params=_params("arbitrary"),
    )(u, u, u, u, u, u, cos, sin, cos, sin, dec)


def _ret_bwd(u, qr, kr, do, st_f, st_b, dec, ncc, name):
    t, w = u.shape
    dc = w // 8
    nh = dc // HEAD
    c = CHUNK
    ns = t // c
    fchunk = lambda s: ns - 1 - s
    bchunk = lambda s: _bwd_dir_chunk(ns - 1 - s, ncc, ns)
    fmap = lambda g: (lambda s: (fchunk(s), g))
    bmap = lambda g: (lambda s: (bchunk(s), g))
    blk = lambda m: pl.BlockSpec((c, dc), m)
    stf_spec = pl.BlockSpec((nh, None, HEAD, HEAD), lambda s: (0, fchunk(s), 0, 0))
    stb_spec = pl.BlockSpec((nh, None, HEAD, HEAD), lambda s: (0, bchunk(s), 0, 0))

    def body(qf, kf, vf, dof, qb, kb, vb, dob, stf_ref, stb_ref, dec_ref,
             dqf, dkf, dvf, dqb, dkb, dvb, dlg_ref, ds_f, ds_b, dm_t, qd_t, kd_t, gc_t, lg_acc):
        step = pl.program_id(0)

        @pl.when(step == 0)
        def _():
            ds_f[...] = jnp.zeros_like(ds_f)
            ds_b[...] = jnp.zeros_like(ds_b)
            lg_acc[...] = jnp.zeros_like(lg_acc)
            _fill_decay_tables(dec_ref, nh, dm_t, qd_t, kd_t, gc_t)

        dirs = ((qf, kf, vf, dof, stf_ref, dqf, dkf, dvf, ds_f), (qb, kb, vb, dob, stb_ref, dqb, dkb, dvb, ds_b))
        weights = [_position_weights(d) for d in range(2)]
        group = RET_INTERLEAVE if nh % RET_INTERLEAVE == 0 else 1
        blocks = [[(h, d) for h in range(h0, h0 + group) for d in range(2)] for h0 in range(0, nh, group)]
        for blk_list in blocks:
            first = []
            for h, d in blk_list:
                q_r, k_r, v_r, do_r, st_ref, _, _, _, ds = dirs[d]
                sl = slice(h * HEAD, (h + 1) * HEAD)
                r = d * nh + h
                q, k = q_r[:, sl], k_r[:, sl]
                qb16, kb16, vb16, dob16 = _bf(q), _bf(k), v_r[:, sl], _bf(do_r[:, sl])
                state, dstate = st_ref[h], ds[h]
                dsb16 = _bf(dstate)
                a = _nt(qb16, kb16) * dm_t[r]
                both = _nt(dob16, jnp.concatenate([vb16, _bf(state)], axis=0))
                dk_i = _nt(vb16, dsb16) * kd_t[r]
                first.append((q, k, qb16, kb16, dob16, state, dstate, dsb16, a, both, dk_i))
            for idx, (h, d) in enumerate(blk_list):
                _, _, _, _, _, dq_ref, dk_ref, dv_ref, ds = dirs[d]
                sl = slice(h * HEAD, (h + 1) * HEAD)
                r = d * nh + h
                q, k, qb16, kb16, dob16, state, dstate, dsb16, a, both, dk_i = first[idx]
                sd, qw, kw = weights[d]
                da = both[:, 0:c]
                dp = _bf(da * dm_t[r])
                dq_i = both[:, c:2 * c] * qd_t[r]
                dq_ref[:, sl] = _bf(_nn(dp, kb16) + dq_i)
                dk_ref[:, sl] = _bf(_tn(dp, qb16) + dk_i)
                dv_ref[:, sl] = _bf(_nn(jnp.concatenate([_bf(a.T), _bf(k * kd_t[r])], axis=1),
                                        jnp.concatenate([dob16, dsb16], axis=0)))
                gcr = gc_t[r, 0:1, :]
                ds[h] = gcr * dstate + _tn(_bf(q * qd_t[r]), dob16)
                lg_acc[r] += (da * a * jnp.maximum(sd, 0.0) + (dq_i * q) * qw + (dk_i * k) * kw
                              + (float(c) * gcr) * (state * dstate))

        @pl.when(step == ns - 1)
        def _():
            for d in range(2):
                for h in range(nh):
                    r = d * nh + h
                    lg = -jnp.exp(jnp.zeros((1, HEAD), F32) + dec_ref[d, h])
                    dlg_ref[r:r + 1, :] = (jnp.zeros((1, HEAD), F32) + jnp.sum(lg_acc[r])) * lg

    dshape = jax.ShapeDtypeStruct((t, dc), BF16)
    return pl.pallas_call(
        body, name=name,
        out_shape=(dshape, dshape, dshape, dshape, dshape, dshape, jax.ShapeDtypeStruct((2 * nh, HEAD), F32)),
        grid=(ns,),
        in_specs=[blk(fmap(0)), blk(fmap(0)), blk(fmap(6)), blk(fmap(0)),
                  blk(bmap(0)), blk(bmap(0)), blk(bmap(6)), blk(bmap(0)), stf_spec, stb_spec, SMEM_SPEC],
        out_specs=(blk(fmap(0)), blk(fmap(0)), blk(fmap(0)), blk(bmap(0)), blk(bmap(0)), blk(bmap(0)),
                   pl.BlockSpec((2 * nh, HEAD), lambda s: (0, 0))),
        scratch_shapes=[pltpu.VMEM((nh, HEAD, HEAD), F32), pltpu.VMEM((nh, HEAD, HEAD), F32)]
        + _decay_table_scratch(nh) + [pltpu.VMEM((2 * nh, CHUNK, HEAD), F32)],
        compiler_params=_params("arbitrary"),
    )(qr, kr, u, do, qr, kr, u, do, st_f, st_b, dec)


def _mod_matvec(c16, w_mod, b_mod):
    nl, d, nm = w_mod.shape
    tn = _pick(nm, 512, 128)

    def body(c_ref, w_ref, b_ref, sc_ref, o_ref):
        sc = _silu(c_ref[...])
        sc_ref[...] = sc
        o_ref[...] = _nn(_bf(sc), _bf(w_ref[...])) + b_ref[...]

    return pl.pallas_call(
        body, name="mod_matvec",
        out_shape=(jax.ShapeDtypeStruct((16, d), F32), jax.ShapeDtypeStruct((nl, 16, nm), F32)),
        grid=(nl, nm // tn),
        in_specs=[pl.BlockSpec((16, d), lambda l, j: (0, 0)), pl.BlockSpec((None, d, tn), lambda l, j: (l, 0, j)),
                  pl.BlockSpec((None, 1, tn), lambda l, j: (l, 0, j))],
        out_specs=(pl.BlockSpec((16, d), lambda l, j: (0, 0)), pl.BlockSpec((None, 16, tn), lambda l, j: (l, 0, j))),
        compiler_params=_params("arbitrary", "arbitrary"),
    )(c16, w_mod, b_mod)


def _wmod_grad(sc16, dm, w_mod):
    nl, d, nm = w_mod.shape
    tn = _pick(nm, 512, 128)

    def body(sc_ref, dm_ref, w_ref, gw_ref, pc_ref):
        @pl.when(jnp.logical_and(pl.program_id(0) == 0, pl.program_id(1) == 0))
        def _():
            pc_ref[...] = jnp.zeros_like(pc_ref)

        dmb = _bf(dm_ref[...])
        gw_ref[...] = _tn(_bf(sc_ref[...]), dmb)
        pc_ref[...] += _nt(dmb, _bf(w_ref[...]))

    return pl.pallas_call(
        body, name="wmod_grad",
        out_shape=(jax.ShapeDtypeStruct((nl, d, nm), F32), jax.ShapeDtypeStruct((16, d), F32)),
        grid=(nl, nm // tn),
        in_specs=[pl.BlockSpec((16, d), lambda l, j: (0, 0)), pl.BlockSpec((None, 16, tn), lambda l, j: (l, 0, j)),
                  pl.BlockSpec((None, d, tn), lambda l, j: (l, 0, j))],
        out_specs=(pl.BlockSpec((None, d, tn), lambda l, j: (l, 0, j)), pl.BlockSpec((16, d), lambda l, j: (0, 0))),
        compiler_params=_params("arbitrary", "arbitrary"),
    )(sc16, dm, w_mod)


def _rowsum(a, name):
    r, n = a.shape
    tn = _pick(n, 4096, 128)

    def body(a_ref, o_ref):
        acc = a_ref[0:1, :]
        for i in range(1, r):
            acc = acc + a_ref[i:i + 1, :]
        o_ref[...] = jnp.zeros((8, tn), F32) + acc

    return pl.pallas_call(
        body, name=name, out_shape=jax.ShapeDtypeStruct((8, n), F32), grid=(n // tn,),
        in_specs=[pl.BlockSpec((r, tn), lambda j: (0, j))], out_specs=pl.BlockSpec((8, tn), lambda j: (0, j)),
        compiler_params=_params("parallel"),
    )(a)


def _cctx_grad(parts, c_ctx8):
    _, _, d = parts.shape

    def body(p_ref, c_ref, o_ref):
        acc = ((p_ref[0] + p_ref[1]) + p_ref[2]) + p_ref[3]
        o_ref[...] = acc * _dsilu(c_ref[...])

    return pl.pallas_call(
        body, name="cctx_grad", out_shape=jax.ShapeDtypeStruct((8, d), F32),
        in_specs=[VMEM_SPEC, VMEM_SPEC], out_specs=VMEM_SPEC,
    )(parts, c_ctx8)


def _adamw(g, w, m, v, name):
    r, n = w.shape
    tr = _pick(r, max(8, min(256, (512 * 1024) // n // 8 * 8)), 8)
    bc1 = 1.0 - ADAM_B1 ** ADAM_STEP
    bc2 = 1.0 - ADAM_B2 ** ADAM_STEP

    def body(g_ref, w_ref, m_ref, v_ref, d_ref, mo_ref, vo_ref):
        gg = g_ref[...]
        mn = ADAM_B1 * m_ref[...] + (1.0 - ADAM_B1) * gg
        vn = ADAM_B2 * v_ref[...] + (1.0 - ADAM_B2) * (gg * gg)
        mo_ref[...] = mn
        vo_ref[...] = vn
        d_ref[...] = -ADAM_LR * ((mn / bc1) / (jnp.sqrt(vn / bc2) + ADAM_EPS) + ADAM_WD * w_ref[...])

    spec = pl.BlockSpec((tr, n), lambda i: (i, 0))
    shp = jax.ShapeDtypeStruct((r, n), F32)
    return pl.pallas_call(
        body, name=name, out_shape=(shp, shp, shp), grid=(r // tr,),
        in_specs=[spec, spec, spec, spec], out_specs=(spec, spec, spec),
        compiler_params=_params("parallel"),
    )(g, w, m, v)


def _adamw_layer(g, w, m, v, layer, prev, name):
    r, n = g.shape
    tr = _pick(r, max(8, min(256, (512 * 1024) // n // 8 * 8)), 8)
    nb = r // tr
    bc1 = 1.0 - ADAM_B1 ** ADAM_STEP
    bc2 = 1.0 - ADAM_B2 ** ADAM_STEP
    n_prev = 0 if prev is None else 4

    def body(g_ref, w_ref, m_ref, v_ref, *rest):
        go_ref, d_ref, mo_ref, vo_ref = rest[n_prev:]
        gg = g_ref[...]
        mn = ADAM_B1 * m_ref[...] + (1.0 - ADAM_B1) * gg
        vn = ADAM_B2 * v_ref[...] + (1.0 - ADAM_B2) * (gg * gg)
        go_ref[...] = gg
        mo_ref[...] = mn
        vo_ref[...] = vn
        d_ref[...] = -ADAM_LR * ((mn / bc1) / (jnp.sqrt(vn / bc2) + ADAM_EPS) + ADAM_WD * w_ref[...])

    spec = pl.BlockSpec((tr, n), lambda i: (layer * nb + i, 0))
    shp = jax.ShapeDtypeStruct(w.shape, F32)
    return pl.pallas_call(
        body, name=name, out_shape=(shp, shp, shp, shp), grid=(nb,),
        in_specs=[pl.BlockSpec((tr, n), lambda i: (i, 0)), spec, spec, spec] + [ANY] * n_prev,
        out_specs=(spec, spec, spec, spec),
        input_output_aliases={4 + k: k for k in range(n_prev)},
        compiler_params=_params("parallel"),
    )(g, w, m, v, *(prev or ()))


def _rope_tables(seq, n_ctx):
    f = HEAD // 4
    inv = ROPE_BASE ** (-jnp.arange(f, dtype=F32) / f)
    n_rows = seq // GRID_W
    ang_r = jnp.arange(n_rows, dtype=F32)[:, None] * inv[None, :]
    ang_c = jnp.arange(GRID_W, dtype=F32)[:, None] * inv[None, :]
    by_row = lambda a: jnp.repeat(a, GRID_W, axis=0)
    by_col = lambda a: jnp.tile(a, (n_rows, 1))
    cos_r, sin_r, cos_c, sin_c = by_row(jnp.cos(ang_r)), by_row(jnp.sin(ang_r)), by_col(jnp.cos(ang_c)), by_col(jnp.sin(ang_c))
    cos = jnp.concatenate([cos_r, cos_r, cos_c, cos_c], axis=-1)
    sin = jnp.concatenate([-sin_r, sin_r, -sin_c, sin_c], axis=-1)
    cos = jnp.concatenate([jnp.ones((n_ctx, HEAD), F32), cos], axis=0)
    sin = jnp.concatenate([jnp.zeros((n_ctx, HEAD), F32), sin], axis=0)
    return cos, sin


def _pad_rows(a, rows):
    return jnp.concatenate([a, jnp.zeros((rows - a.shape[0],) + a.shape[1:], a.dtype)], axis=0)


def _pad_flat(parts, mult):
    flat = jnp.concatenate([p.reshape(-1) for p in parts])
    pad = (-flat.shape[0]) % mult
    return jnp.concatenate([flat, jnp.zeros((pad,), flat.dtype)])


def kernel(x, c, ctx, c_ctx, norm_w, w_mod, b_mod, w_in, conv_w, conv_norm_w, ret_norm_w, ret_decay_f, ret_decay_b, w_out, final_norm_w, loss_target, m_c_ctx, m_norm_w, m_w_mod, m_b_mod, m_w_in, m_conv_w, m_conv_norm_w, m_ret_norm_w, m_ret_decay_f, m_ret_decay_b, m_w_out, m_final_norm_w, v_c_ctx, v_norm_w, v_w_mod, v_b_mod, v_w_in, v_conv_w, v_conv_norm_w, v_ret_norm_w, v_ret_decay_f, v_ret_decay_b, v_w_out, v_final_norm_w):
    xi, yi, ci = _mesh_pos()
    b_idx = 4 * xi + 2 * yi + ci
    s_idx = 2 * xi + yi
    pos = jnp.stack([ci, s_idx]).astype(jnp.int32)

    x2, ctx2, tgt = x[0], ctx[0], loss_target[0]
    seq, d = x2.shape
    n_ctx = ctx2.shape[0]
    t = seq + n_ctx
    dc = d // 2
    nh = dc // HEAD
    nl, _, nm = w_mod.shape
    n4 = w_in.shape[2]
    r4 = w_out.shape[1]
    cw4 = conv_w.shape[2]
    nct = n_ctx // ROW_TILE
    ncc = n_ctx // CHUNK
    assert n_ctx % ROW_TILE == 0 and seq % ROW_TILE == 0 and dc % HEAD == 0 and nl * 3 <= 8

    win_b, wout_b = w_in.astype(BF16), w_out.astype(BF16)
    wg_in, wg_out = [None] * nl, [None] * nl
    wg_in[0], = _run_comm(_gather_comm(win_b[0], True), "gather_w_in_0")

    small = jnp.concatenate([jnp.broadcast_to(c, (8, d)), _pad_rows(conv_w.reshape(nl * 3, cw4), 8)], axis=1)
    small_g = _allgather_small(small, "gather_c_convw").reshape(N_DEV, 8, d + cw4)
    c_all = small_g[:, 0, :d]
    cw_full = small_g[0::2, :, d:].transpose(1, 0, 2).reshape(8, 4 * cw4)
    cw_l = [_pad_rows(cw_full[l * 3:(l + 1) * 3], 8) for l in range(nl)]

    c16 = _pad_rows(jnp.concatenate([c_all, c_ctx[None]], axis=0), 16)
    b_shard = lax.dynamic_slice(b_mod, (0, s_idx * nm), (nl, nm)).reshape(nl, 1, nm)
    sc16, modp = _mod_matvec(c16, w_mod, b_shard)
    modg = _allgather_small(modp.reshape(nl * 16, nm), "gather_mod").reshape(N_DEV, nl, 16, nm)
    mod_full = modg[0::2].transpose(1, 2, 0, 3).reshape(nl, 16, 4 * nm)
    mod_me = lax.dynamic_index_in_dim(mod_full, b_idx, axis=1, keepdims=False)
    mod_cx = mod_full[:, 8]
    zrow = jnp.zeros((d,), F32)
    rows = []
    for l in range(nl):
        gate_c = mod_cx[l, 2 * d:] if l < nl - 1 else zrow
        rows.append(jnp.stack([mod_me[l, :d], mod_me[l, d:2 * d], mod_me[l, 2 * d:],
                               mod_cx[l, :d], mod_cx[l, d:2 * d], gate_c, zrow, zrow]))

    cos, sin = _rope_tables(seq, n_ctx)
    xs = jnp.concatenate([ctx2, x2], axis=0)
    saved = []
    for l in range(nl):
        dec = jnp.stack([ret_decay_f[l], ret_decay_b[l]])
        hx, hx_t = _modulate(xs, norm_w[l][None], rows[l], nct, f"modulate_{l}")
        comms = [_gather_comm(wout_b[l], False)]
        if l + 1 < nl:
            comms += [_gather_comm(win_b[l + 1], True)]
        u, wg_out[l], *nxt = _matmul(hx, wg_in[l], "nn", f"in_proj_{l}", comms=comms, out_dtype=BF16)
        if nxt:
            wg_in[l + 1] = nxt[0]
        o_f, o_b, st_f, st_b, qr, kr = _ret_fwd(u, cos, sin, dec, ncc, f"ret_fwd_{l}")
        y, y_t, cv = _mix_fwd(u, o_f, o_b, cw_l[l], conv_norm_w[l][None], ret_norm_w[l][None], nct, f"mix_fwd_{l}")
        xs_new, proj = _out_proj(y, wg_out[l], xs, rows[l], n_ctx, f"out_proj_{l}")
        saved.append((xs, hx_t, u, o_f, o_b, st_f, st_b, qr, kr, y_t, cv, proj, dec))
        xs = xs_new

    g, facc, gp, gacc = _final_loss(xs, tgt, final_norm_w[None], (saved[nl - 1][11], rows[nl - 1]), nct, "final_loss")
    full = [None] * nl
    join = []
    mine, other = ci.reshape(1).astype(jnp.int32), (1 - ci).reshape(1).astype(jnp.int32)
    dmx, dmc, gnw, gcnw, grnw, gcw, gdf, gdb = ([None] * nl for _ in range(8))
    for l in reversed(range(nl)):
        xs_l, hx_t, u, o_f, o_b, st_f, st_b, qr, kr, y_t, cv, proj, dec = saved[l]
        gout = _matmul(y_t, gp, "nn", f"out_proj_dw_{l}", tm_t=1024, tn_t=1024, tk_t=2816, comms=join)
        if join:
            gout, *full[l + 1] = gout
        dy = _matmul(gp, wg_out[l], "nt", f"out_proj_dx_{l}")
        dab, daz, drz, dcv, do, macc = _mix_bwd(dy, u, cv, o_f, o_b, conv_norm_w[l][None], ret_norm_w[l][None],
                                                f"mix_bwd_{l}")
        dqf, dkf, dvf, dqb, dkb, dvb, dlg = _ret_bwd(u, qr, kr, do, st_f, st_b, dec, ncc, f"ret_bwd_{l}")
        du, cacc = _assemble_du(dcv, u, cw_l[l], dab, daz, drz, dqf, dqb, dkf, dkb, dvf, dvb, cos, sin, nct,
                                f"assemble_du_{l}")
        gout_v = gout.reshape(4, 2, r4 // 2, d)
        gin_other = _matmul(hx_t, du, "nn", f"in_proj_dw_other_{l}", tm_t=1024, tn_t=1024, tk_t=2816, row_half=other)
        gin_mine, land_in, land_out = _matmul(hx_t, du, "nn", f"in_proj_dw_mine_{l}", tm_t=1024, tn_t=1024,
                                              tk_t=2816, row_half=mine, comms=[_swap_comm(gin_other, gout_v)])
        p_in = _sum_half_in(gin_mine, land_in, f"grad_sum_half_in_{l}")
        p_out = _sum_half_out(pos, gout_v, land_out, f"grad_sum_half_out_{l}")
        dhx, l3_in, l3_out = _matmul(du, wg_in[l], "nt", f"in_proj_dx_{l}", comms=[_exchange_comm(p_in, p_out, n4)],
                                     out_dtype=BF16)
        join = [_join_comm(_sum_quarters_in(pos, p_in, l3_in, n4, f"grad_sum_quarters_in_{l}"),
                           _sum_quarters_out(pos, p_out, l3_out, f"grad_sum_quarters_out_{l}"))]
        dmx_gate, dmc_gate = gacc[2], gacc[5]
        below = (saved[l - 1][11], rows[l - 1]) if l > 0 else None
        g, nacc, *nxt = _modulate_bwd(dhx, xs_l, g, norm_w[l][None], rows[l], nct, f"modulate_bwd_{l}", below)
        if nxt:
            gp, gacc = nxt
        dmx[l] = jnp.concatenate([nacc[0], nacc[1], dmx_gate])
        dmc[l] = jnp.concatenate([nacc[3], nacc[4], dmc_gate])
        gnw[l], gcnw[l], grnw[l], gcw[l] = nacc[6], macc[0], macc[1], cacc[0:3]
        gdf[l], gdb[l] = dlg[0:nh, 0], dlg[nh:2 * nh, 0]
    grad_x = g[None]
    full[0] = _run_comm(join[0], "grad_join_0")

    res_in, res_out = None, None
    for l in reversed(range(nl)):
        res_in = _adamw_layer(full[l][0].reshape(d, n4), w_in.reshape(nl * d, n4), m_w_in.reshape(nl * d, n4),
                              v_w_in.reshape(nl * d, n4), l, res_in, f"adamw_w_in_{l}")
        res_out = _adamw_layer(full[l][1].reshape(r4, d), w_out.reshape(nl * r4, d), m_w_out.reshape(nl * r4, d),
                               v_w_out.reshape(nl * r4, d), l, res_out, f"adamw_w_out_{l}")
    g_w_in, d_w_in, nm_w_in, nv_w_in = res_in
    g_w_out, d_w_out, nm_w_out, nv_w_out = res_out

    seg = [jnp.stack(dmx), jnp.stack(dmc), jnp.stack(gnw), jnp.stack(gcnw), jnp.stack(grnw), facc[0],
           jnp.stack(gcw), jnp.stack(gdf), jnp.stack(gdb), facc[1, 0:1]]
    sizes = [int(s.size) for s in seg]
    offs = [sum(sizes[:i]) for i in range(len(sizes))]
    packed = _pad_flat(seg, 8 * 128)
    pw = packed.shape[0] // 8
    allv = _allgather_small(packed.reshape(8, pw), "gather_small_grads").reshape(N_DEV, 8 * pw)
    n_mod = nl * 3 * d
    dmx_all, dmc_all = allv[:, offs[0]:offs[0] + n_mod], allv[:, offs[1]:offs[1] + n_mod]
    rest_all = allv[:, offs[2]:]
    zeros_mod = jnp.zeros_like(dmc_all)
    stack = jnp.concatenate([
        jnp.concatenate([dmx_all, dmc_all], axis=0),
        jnp.concatenate([dmc_all, zeros_mod], axis=0),
        jnp.concatenate([rest_all, jnp.zeros_like(rest_all)], axis=0)], axis=1)
    sums = _rowsum(stack, "sum_small_grads")[0]
    g_b_mod = sums[:n_mod].reshape(nl, 3 * d)
    dmc_tot = sums[n_mod:2 * n_mod].reshape(nl, 3 * d)
    rest = sums[2 * n_mod:]

    def take(i):
        o = offs[i] - offs[2]
        return rest[o:o + sizes[i]]

    g_norm_w = take(2).reshape(nl, d)
    g_conv_norm_w = take(3).reshape(nl, dc)
    g_ret_norm_w = take(4).reshape(nl, dc)
    g_final_norm_w = take(5)
    g_conv_w = lax.dynamic_slice(take(6).reshape(nl, 3, dc), (0, 0, s_idx * cw4), (nl, 3, cw4))
    g_decay_f = take(7).reshape(nl, nh)
    g_decay_b = take(8).reshape(nl, nh)
    loss = take(9)[0]

    dmx_mine = lax.dynamic_slice(dmx_all.reshape(N_DEV, nl, 3 * d), (0, 0, s_idx * nm), (N_DEV, nl, nm))
    dmc_mine = lax.dynamic_slice(dmc_tot, (0, s_idx * nm), (nl, nm))
    dm = jnp.concatenate([dmx_mine.transpose(1, 0, 2), dmc_mine[:, None, :], jnp.zeros((nl, 7, nm), F32)], axis=1)
    g_w_mod, pc = _wmod_grad(sc16, dm, w_mod)
    pc_g = _allgather_small(pc[8:16], "gather_cctx_partials").reshape(N_DEV, 8, d)
    g_c_ctx = _cctx_grad(pc_g[0::2], jnp.broadcast_to(c_ctx[None], (8, d)))[0]
    d_w_mod, nm_w_mod, nv_w_mod = _adamw(g_w_mod.reshape(nl * d, nm), w_mod.reshape(nl * d, nm),
                                         m_w_mod.reshape(nl * d, nm), v_w_mod.reshape(nl * d, nm), "adamw_w_mod")

    small_g = [g_c_ctx, g_norm_w, g_b_mod, g_conv_w, g_conv_norm_w, g_ret_norm_w, g_decay_f, g_decay_b, g_final_norm_w]
    small_w = [c_ctx, norm_w, b_mod, conv_w, conv_norm_w, ret_norm_w, ret_decay_f, ret_decay_b, final_norm_w]
    small_m = [m_c_ctx, m_norm_w, m_b_mod, m_conv_w, m_conv_norm_w, m_ret_norm_w, m_ret_decay_f, m_ret_decay_b,
               m_final_norm_w]
    small_v = [v_c_ctx, v_norm_w, v_b_mod, v_conv_w, v_conv_norm_w, v_ret_norm_w, v_ret_decay_f, v_ret_decay_b,
               v_final_norm_w]
    pk = lambda parts: _pad_flat(parts, 8 * 128).reshape(-1, 128)
    sd, sm, sv = _adamw(pk(small_g), pk(small_w), pk(small_m), pk(small_v), "adamw_small")

    def unpack(flat2d):
        flat = flat2d.reshape(-1)
        out, o = [], 0
        for wgt in small_w:
            out.append(flat[o:o + wgt.size].reshape(wgt.shape))
            o += wgt.size
        return out

    sd, sm, sv = unpack(sd), unpack(sm), unpack(sv)

    def order(small, mod, w_in_, w_out_):
        return [small[0], small[1], mod, small[2], w_in_, small[3], small[4], small[5], small[6], small[7], w_out_,
                small[8]]

    grads = order(small_g, g_w_mod, g_w_in.reshape(nl, d, n4), g_w_out.reshape(nl, r4, d))
    deltas = order(sd, d_w_mod.reshape(nl, d, nm), d_w_in.reshape(nl, d, n4), d_w_out.reshape(nl, r4, d))
    new_m = order(sm, nm_w_mod.reshape(nl, d, nm), nm_w_in.reshape(nl, d, n4), nm_w_out.reshape(nl, r4, d))
    new_v = order(sv, nv_w_mod.reshape(nl, d, nm), nv_w_in.reshape(nl, d, n4), nv_w_out.reshape(nl, r4, d))
    return (loss, grad_x, *grads, *deltas, *new_m, *new_v)
```

```python
import functools

import jax
import jax.numpy as jnp
from jax import lax
from jax.experimental import pallas as pl
from jax.experimental.pallas import tpu as pltpu

F32 = jnp.float32
BF16 = jnp.bfloat16
EPS = 1e-6
CHUNK = 128
HEAD = 128
GRID_W = 64
ROPE_BASE = 10000.0
ROW_TILE = 256
RET_INTERLEAVE = 4
VMEM_LIMIT_BYTES = 56 * 1024 * 1024
N_DEV = 8
MESH = pl.DeviceIdType.MESH

ADAM_LR = 0.001
ADAM_B1 = 0.9
ADAM_B2 = 0.999
ADAM_EPS = 1e-08
ADAM_WD = 0.01
ADAM_STEP = 10

ANY = pl.BlockSpec(memory_space=pl.ANY)
VMEM_SPEC = pl.BlockSpec(memory_space=pltpu.VMEM)
SMEM_SPEC = pl.BlockSpec(memory_space=pltpu.SMEM)


def _params(*sem):
    return pltpu.CompilerParams(dimension_semantics=sem, vmem_limit_bytes=VMEM_LIMIT_BYTES)


def _pick(n, target, mult):
    best = None
    for d in range(mult, min(n, target) + 1, mult):
        if n % d == 0:
            best = d
    assert best is not None, (n, target, mult)
    return best


def _silu(z):
    return z * jax.nn.sigmoid(z)


def _dsilu(z):
    s = jax.nn.sigmoid(z)
    return s * (1.0 + z * (1.0 - s))


def _nn(a, b):
    return lax.dot_general(a, b, (((1,), (0,)), ((), ())), preferred_element_type=F32)


def _nt(a, b):
    return lax.dot_general(a, b, (((1,), (1,)), ((), ())), preferred_element_type=F32)


def _tn(a, b):
    return lax.dot_general(a, b, (((0,), (0,)), ((), ())), preferred_element_type=F32)


def _bf(a):
    return a.astype(BF16)


def _mesh_pos():
    return lax.axis_index("x"), lax.axis_index("y"), lax.axis_index("c")


def _allgather_small(blk, name):
    m_per, n = blk.shape

    def body(x_ref, out_ref, send_sems, recv_sems, local_sem):
        x, y, c = _mesh_pos()
        me, sibling = (x, y, c), (x, y, 1 - c)
        chips = [(1 - x, y), (x, 1 - y), (1 - x, 1 - y)]

        def rows(px, py, pc):
            return out_ref.at[pl.ds((4 * px + 2 * py + pc) * m_per, m_per), :]

        def copy(k, block, to, src=None):
            return pltpu.make_async_remote_copy(
                src_ref=rows(*block) if src is None else src, dst_ref=rows(*block),
                send_sem=send_sems.at[k], recv_sem=recv_sems.at[k], device_id=to, device_id_type=MESH)

        mine = pltpu.make_async_copy(x_ref, rows(*me), local_sem)
        mine.start()
        first = [copy(0, me, sibling, src=x_ref)]
        first += [copy(1 + j, me, (*chip, c), src=x_ref) for j, chip in enumerate(chips)]
        for cp in first:
            cp.start()
        passed = [copy(4 + j, (*chip, c), sibling) for j, chip in enumerate(chips)]
        for j, chip in enumerate(chips):
            copy(1 + j, (*chip, c), me).wait_recv()
            passed[j].start()
        copy(0, sibling, me).wait_recv()
        for j, chip in enumerate(chips):
            copy(4 + j, (*chip, 1 - c), me).wait_recv()
        for cp in first + passed:
            cp.wait_send()
        mine.wait()

    return pl.pallas_call(
        body, name=name,
        out_shape=jax.ShapeDtypeStruct((N_DEV * m_per, n), blk.dtype),
        in_specs=[VMEM_SPEC], out_specs=VMEM_SPEC,
        scratch_shapes=[pltpu.SemaphoreType.DMA((7,)), pltpu.SemaphoreType.DMA((7,)), pltpu.SemaphoreType.DMA],
    )(blk)


class _Comm:
    def __init__(self, ins, outs, n_remote, n_local, start, finish, aliases=None):
        self.ins, self.outs, self.n_remote, self.n_local = list(ins), list(outs), n_remote, n_local
        self.start, self.finish, self.aliases = start, finish, dict(aliases or {})

    def scratch(self):
        return [pltpu.SemaphoreType.DMA((self.n_remote,)), pltpu.SemaphoreType.DMA((self.n_remote,)),
                pltpu.SemaphoreType.DMA((max(self.n_local, 1),))]


def _run_comm(comm, name):
    n_in, n_out = len(comm.ins), len(comm.outs)

    def body(*refs):
        args = (refs[:n_in], refs[n_in:n_in + n_out]) + tuple(refs[n_in + n_out:])
        comm.start(*args)
        comm.finish(*args)

    return pl.pallas_call(
        body, name=name, out_shape=tuple(comm.outs), in_specs=[ANY] * n_in, out_specs=tuple([ANY] * n_out),
        scratch_shapes=comm.scratch(), input_output_aliases=comm.aliases,
    )(*comm.ins)


def _remote(src, dst, send, recv, k, to):
    return pltpu.make_async_remote_copy(src_ref=src, dst_ref=dst, send_sem=send.at[k], recv_sem=recv.at[k],
                                        device_id=to, device_id_type=MESH)


def _gather_comm(shard, by_columns):
    rows, cols = shard.shape
    hr = rows // 2

    def build(ins, outs, send, recv, loc):
        x, y, c = _mesh_pos()
        sib = (x, y, 1 - c)
        peers = [(1 - x, y), (x, 1 - y), (1 - x, 1 - y)]
        s_me = 2 * x + y

        def slot(s, h):
            if by_columns:
                return outs[0].at[pl.ds(pl.multiple_of(h * hr, 16), hr), pl.ds(pl.multiple_of(s * cols, 128), cols)]
            return outs[0].at[pl.ds(pl.multiple_of(s * rows + h * hr, 16), hr), :]

        def half(h):
            return ins[0].at[pl.ds(pl.multiple_of(h * hr, 16), hr), :]

        def local():
            if by_columns:
                mine = outs[0].at[:, pl.ds(pl.multiple_of(s_me * cols, 128), cols)]
            else:
                mine = outs[0].at[pl.ds(pl.multiple_of(s_me * rows, 16), rows), :]
            return [pltpu.make_async_copy(ins[0], mine, loc.at[0])]

        def per_copy(make):
            return [make(j, 2 * px + py, (px, py, c)) for j, (px, py) in enumerate(peers)]

        ici_send = lambda: per_copy(lambda j, s_p, to: _remote(half(c), slot(s_me, c), send, recv, j, to))
        ici_recv = lambda: per_copy(lambda j, s_p, to: _remote(half(c), slot(s_p, c), send, recv, j, to))
        d2d_send = lambda: per_copy(lambda j, s_p, to: _remote(slot(s_p, c), slot(s_p, c), send, recv, 3 + j, sib))
        d2d_recv = lambda: per_copy(lambda j, s_p, to: _remote(slot(s_p, c), slot(s_p, 1 - c), send, recv, 3 + j, sib))
        return local, ici_send, ici_recv, d2d_send, d2d_recv

    def start(*refs):
        local, ici_send, _, _, _ = build(*refs)
        for cp in local() + ici_send():
            cp.start()

    def finish(*refs):
        local, ici_send, ici_recv, d2d_send, d2d_recv = build(*refs)
        forwards = d2d_send()
        for arrived, forward in zip(ici_recv(), forwards):
            arrived.wait_recv()
            forward.start()
        for cp in d2d_recv():
            cp.wait_recv()
        for cp in ici_send() + forwards:
            cp.wait_send()
        for cp in local():
            cp.wait()

    full = (rows, 4 * cols) if by_columns else (4 * rows, cols)
    return _Comm([shard], [jax.ShapeDtypeStruct(full, BF16)], 6, 1, start, finish)


def _swap_comm(gin_other, gout_v):
    dh, w = gin_other.shape
    ns, _, rh, d = gout_v.shape

    def build(ins, outs, send, recv, loc):
        x, y, c = _mesh_pos()
        sib = (x, y, 1 - c)
        return [_remote(ins[0], outs[0], send, recv, 0, sib),
                _remote(ins[1].at[:, 1 - c], outs[1], send, recv, 1, sib)]

    def start(*refs):
        for cp in build(*refs):
            cp.start()

    def finish(*refs):
        for cp in build(*refs):
            cp.wait()

    outs = [jax.ShapeDtypeStruct((dh, w), F32), jax.ShapeDtypeStruct((ns, rh, d), F32)]
    return _Comm([gin_other, gout_v], outs, 2, 0, start, finish)


def _exchange_comm(p_in, p_out, n4):
    dh, _ = p_in.shape
    _, rh, d = p_out.shape

    def build(ins, outs, send, recv, loc):
        x, y, c = _mesh_pos()
        cps = []
        for j, (px, py) in enumerate([(1 - x, y), (x, 1 - y), (1 - x, 1 - y)]):
            s_p = 2 * px + py
            cps.append(_remote(ins[0].at[:, pl.ds(pl.multiple_of(s_p * n4, 128), n4)], outs[0].at[j], send, recv,
                               2 * j, (px, py, c)))
            cps.append(_remote(ins[1].at[s_p], outs[1].at[j], send, recv, 2 * j + 1, (px, py, c)))
        return cps

    def start(*refs):
        for cp in build(*refs):
            cp.start()

    def finish(*refs):
        cps = build(*refs)
        for cp in cps:
            cp.wait_recv()
        for cp in cps:
            cp.wait_send()

    outs = [jax.ShapeDtypeStruct((3, dh, n4), p_in.dtype), jax.ShapeDtypeStruct((3, rh, d), p_out.dtype)]
    return _Comm([p_in, p_out], outs, 6, 0, start, finish)


def _join_comm(q_in, q_out):
    def build(ins, outs, send, recv, loc):
        x, y, c = _mesh_pos()
        sib = (x, y, 1 - c)
        sends = lambda: [_remote(outs[a].at[c], outs[a].at[c], send, recv, a, sib) for a in range(2)]
        recvs = lambda: [_remote(outs[a].at[c], outs[a].at[1 - c], send, recv, a, sib) for a in range(2)]
        return sends, recvs

    def start(*refs):
        for cp in build(*refs)[0]():
            cp.start()

    def finish(*refs):
        sends, recvs = build(*refs)
        for cp in recvs():
            cp.wait_recv()
        for cp in sends():
            cp.wait_send()

    outs = [jax.ShapeDtypeStruct(q_in.shape, F32), jax.ShapeDtypeStruct(q_out.shape, F32)]
    return _Comm([q_in, q_out], outs, 2, 0, start, finish, aliases={0: 0, 1: 1})


def _sum_half_in(mine, land, name):
    dh, w = mine.shape
    tr, tc = _pick(dh, 256, 16), _pick(w, 2048, 128)

    def body(a_ref, b_ref, o_ref):
        o_ref[...] = (a_ref[...] + b_ref[...]).astype(BF16)

    spec = pl.BlockSpec((tr, tc), lambda i, j: (i, j))
    return pl.pallas_call(
        body, name=name, out_shape=jax.ShapeDtypeStruct((dh, w), BF16), grid=(dh // tr, w // tc),
        in_specs=[spec, spec], out_specs=spec, compiler_params=_params("parallel", "parallel"),
    )(mine, land)


def _sum_half_out(pos, gout_v, land, name):
    ns, _, rh, d = gout_v.shape

    def body(pos_ref, a_ref, b_ref, o_ref):
        o_ref[...] = (a_ref[...] + b_ref[...]).astype(BF16)

    return pl.pallas_call(
        body, name=name,
        out_shape=jax.ShapeDtypeStruct((ns, rh, d), BF16),
        grid_spec=pltpu.PrefetchScalarGridSpec(
            num_scalar_prefetch=1, grid=(ns,),
            in_specs=[pl.BlockSpec((None, None, rh, d), lambda s, p: (s, p[0], 0, 0)),
                      pl.BlockSpec((None, rh, d), lambda s, p: (s, 0, 0))],
            out_specs=pl.BlockSpec((None, rh, d), lambda s, p: (s, 0, 0))),
        compiler_params=_params("parallel"),
    )(pos, gout_v, land)


def _sum_quarters_in(pos, p_in, land3, n4, name):
    dh, _ = p_in.shape
    tr = _pick(dh, 128, 16)

    def body(pos_ref, a_ref, b_ref, o_ref):
        f = lambda v: v.astype(F32)
        o_ref[...] = ((f(a_ref[...]) + f(b_ref[0])) + f(b_ref[1])) + f(b_ref[2])

    return pl.pallas_call(
        body, name=name,
        out_shape=jax.ShapeDtypeStruct((2, dh, n4), F32),
        grid_spec=pltpu.PrefetchScalarGridSpec(
            num_scalar_prefetch=1, grid=(dh // tr,),
            in_specs=[pl.BlockSpec((tr, n4), lambda i, p: (i, p[1])),
                      pl.BlockSpec((3, tr, n4), lambda i, p: (0, i, 0))],
            out_specs=pl.BlockSpec((None, tr, n4), lambda i, p: (p[0], i, 0))),
        compiler_params=_params("parallel"),
    )(pos, p_in, land3)


def _sum_quarters_out(pos, p_out, land3, name):
    _, rh, d = p_out.shape

    def body(pos_ref, a_ref, b_ref, o_ref):
        f = lambda v: v.astype(F32)
        o_ref[...] = ((f(a_ref[...]) + f(b_ref[0])) + f(b_ref[1])) + f(b_ref[2])

    return pl.pallas_call(
        body, name=name,
        out_shape=jax.ShapeDtypeStruct((2, rh, d), F32),
        grid_spec=pltpu.PrefetchScalarGridSpec(
            num_scalar_prefetch=1, grid=(1,),
            in_specs=[pl.BlockSpec((None, rh, d), lambda i, p: (p[1], 0, 0)),
                      pl.BlockSpec((3, rh, d), lambda i, p: (0, 0, 0))],
            out_specs=pl.BlockSpec((None, rh, d), lambda i, p: (p[0], 0, 0))),
        compiler_params=_params("arbitrary"),
    )(pos, p_out, land3)


def _matmul(a, b, kind, name, tm_t=1056, tn_t=1024, tk_t=2048, comms=(), row_half=None, out_dtype=F32):
    if kind == "nn":
        (m, k), n = a.shape, b.shape[1]
    elif kind == "nt":
        (m, k), n = a.shape, b.shape[0]
    else:
        (k, m), n = a.shape, b.shape[1]
    if row_half is None:
        row_half = jnp.zeros((1,), jnp.int32)
    else:
        assert kind == "nn"
        m = m // 2
    tm = _pick(m, tm_t, 128 if kind == "tn" else 16)
    tn = _pick(n, tn_t, 128)
    tk = _pick(k, tk_t, 16 if kind == "tn" else 128)
    nk = k // tk
    nbm = m // tm
    grid = (nbm, n // tn, nk)
    dot = {"nn": _nn, "nt": _nt, "tn": _tn}[kind]
    a_spec = {"nn": pl.BlockSpec((tm, tk), lambda i, j, q, p: (i + p[0] * nbm, q)),
              "nt": pl.BlockSpec((tm, tk), lambda i, j, q, p: (i, q)),
              "tn": pl.BlockSpec((tk, tm), lambda i, j, q, p: (q, i))}[kind]
    b_spec = {"nn": pl.BlockSpec((tk, tn), lambda i, j, q, p: (q, j)),
              "nt": pl.BlockSpec((tn, tk), lambda i, j, q, p: (j, q)),
              "tn": pl.BlockSpec((tk, tn), lambda i, j, q, p: (q, j))}[kind]
    comm_ins = [x for cm in comms for x in cm.ins]
    comm_outs = [x for cm in comms for x in cm.outs]
    aliases, scratch = {}, []
    i_off, o_off = 3, 1
    for cm in comms:
        for ci, co in cm.aliases.items():
            aliases[i_off + ci] = o_off + co
        i_off, o_off = i_off + len(cm.ins), o_off + len(cm.outs)
        scratch += cm.scratch()

    def body(half_ref, a_ref, b_ref, *rest):
        cin = rest[:len(comm_ins)]
        o_ref = rest[len(comm_ins)]
        cout = rest[len(comm_ins) + 1:len(comm_ins) + 1 + len(comm_outs)]
        acc_ref = rest[len(comm_ins) + 1 + len(comm_outs)]
        sems = rest[len(comm_ins) + 2 + len(comm_outs):]
        ids = [pl.program_id(ax) for ax in range(3)]
        first = functools.reduce(jnp.logical_and, [ids[ax] == 0 for ax in range(3)])
        last = functools.reduce(jnp.logical_and, [ids[ax] == grid[ax] - 1 for ax in range(3)])

        def comm_args(idx):
            i0 = sum(len(cm.ins) for cm in comms[:idx])
            o0 = sum(len(cm.outs) for cm in comms[:idx])
            cm = comms[idx]
            return (cin[i0:i0 + len(cm.ins)], cout[o0:o0 + len(cm.outs)]) + tuple(sems[3 * idx:3 * idx + 3])

        if comms:
            @pl.when(first)
            def _():
                for idx, cm in enumerate(comms):
                    cm.start(*comm_args(idx))

        q = ids[2]
        p = dot(a_ref[...], b_ref[...])
        if nk == 1:
            o_ref[...] = p.astype(out_dtype)
        else:
            @pl.when(q == 0)
            def _():
                acc_ref[...] = p

            @pl.when(q > 0)
            def _():
                acc_ref[...] += p

            @pl.when(q == nk - 1)
            def _():
                o_ref[...] = acc_ref[...].astype(out_dtype)

        if comms:
            @pl.when(last)
            def _():
                for idx, cm in enumerate(comms):
                    cm.finish(*comm_args(idx))

    sem = ("arbitrary",) * 3 if comms else ("parallel", "parallel", "arbitrary")
    res = pl.pallas_call(
        body, name=name,
        out_shape=(jax.ShapeDtypeStruct((m, n), out_dtype), *comm_outs),
        grid_spec=pltpu.PrefetchScalarGridSpec(
            num_scalar_prefetch=1, grid=grid,
            in_specs=[a_spec, b_spec] + [ANY] * len(comm_ins),
            out_specs=(pl.BlockSpec((tm, tn), lambda i, j, q, p: (i, j)), *([ANY] * len(comm_outs))),
            scratch_shapes=[pltpu.VMEM((tm, tn) if nk > 1 else (8, 128), F32)] + scratch),
        input_output_aliases=aliases,
        compiler_params=_params(*sem),
    )(row_half, a, b, *comm_ins)
    return res if comms else res[0]


def _in_proj_gathered(hx, w_shard, order, comm, name, tm_t=1056, tn_t=1024):
    t, d = hx.shape
    n4 = w_shard.shape[1]
    hr = d // 2
    tm, tn = _pick(t, tm_t, 16), _pick(n4, tn_t, 128)
    ncol, nrow = n4 // tn, t // tm
    ntile = 4 * ncol
    n_ci, n_co = len(comm.ins), len(comm.outs)

    def body(order_ref, a_ref, ws_ref, *rest):
        cin, (o_ref, wg_ref), cout = rest[:n_ci], rest[n_ci:n_ci + 2], rest[n_ci + 2:n_ci + 2 + n_co]
        bbuf, fsem, send, recv, loc = rest[n_ci + 2 + n_co:n_ci + 7 + n_co]
        csems = rest[n_ci + 7 + n_co:]
        k, j, i = pl.program_id(0), pl.program_id(1), pl.program_id(2)
        n = k * ncol + j
        x, y, c = _mesh_pos()
        sib = (x, y, 1 - c)
        peers = [(1 - x, y), (x, 1 - y), (1 - x, 1 - y)]
        s_me = 2 * x + y

        def slot(s, h):
            return wg_ref.at[pl.ds(pl.multiple_of(h * hr, 16), hr), pl.ds(pl.multiple_of(s * n4, 128), n4)]

        def my_half():
            return ws_ref.at[pl.ds(pl.multiple_of(c * hr, 16), hr), :]

        def local():
            return pltpu.make_async_copy(ws_ref, wg_ref.at[:, pl.ds(pl.multiple_of(s_me * n4, 128), n4)], loc.at[0])

        def ici_send(p):
            return _remote(my_half(), slot(s_me, c), send, recv, p, (*peers[p], c))

        def ici_recv(p):
            return _remote(my_half(), slot(2 * peers[p][0] + peers[p][1], c), send, recv, p, (*peers[p], c))

        def d2d_send(p):
            s_p = 2 * peers[p][0] + peers[p][1]
            return _remote(slot(s_p, c), slot(s_p, c), send, recv, 3 + p, sib)

        def d2d_recv(p):
            s_p = 2 * peers[p][0] + peers[p][1]
            return _remote(slot(s_p, c), slot(s_p, 1 - c), send, recv, 3 + p, sib)

        def fetch(m, own):
            col = (m % ncol) * tn
            if own:
                src = ws_ref.at[:, pl.ds(pl.multiple_of(col, 128), tn)]
            else:
                src = wg_ref.at[:, pl.ds(pl.multiple_of(order_ref[m // ncol] * n4 + col, 128), tn)]
            return pltpu.make_async_copy(src, bbuf.at[m % 2], fsem.at[m % 2])

        comm_args = (cin, cout) + tuple(csems)

        @pl.when(jnp.logical_and(i == 0, n == 0))
        def _():
            local().start()
            ici_send(0).start()
            ici_send(1).start()
            fetch(n, True).start()

        @pl.when(i == 0)
        def _():
            fetch(n, True).wait()

        @pl.when(i == max(nrow - 2, 0))
        def _():
            for kk in range(1, 4):
                @pl.when(n == kk * ncol - 1)
                def _():
                    ici_recv(kk - 1).wait_recv()
                    d2d_send(kk - 1).start()
                    d2d_recv(kk - 1).wait_recv()
                    if kk == 1:
                        ici_send(0).wait_send()
                        ici_send(1).wait_send()
                        ici_send(2).start()
                    if kk == 3:
                        comm.start(*comm_args)

            @pl.when(n + 1 < ncol)
            def _():
                fetch(n + 1, True).start()

            @pl.when(jnp.logical_and(n + 1 >= ncol, n + 1 < ntile))
            def _():
                fetch(n + 1, False).start()

        o_ref[...] = _nn(a_ref[...], bbuf[n % 2]).astype(BF16)

        @pl.when(jnp.logical_and(n == ntile - 1, i == nrow - 1))
        def _():
            ici_send(2).wait_send()
            for p in range(3):
                d2d_send(p).wait_send()
            local().wait()
            comm.finish(*comm_args)

    res = pl.pallas_call(
        body, name=name,
        out_shape=(jax.ShapeDtypeStruct((t, 4 * n4), BF16), jax.ShapeDtypeStruct((d, 4 * n4), BF16), *comm.outs),
        grid_spec=pltpu.PrefetchScalarGridSpec(
            num_scalar_prefetch=1, grid=(4, ncol, nrow),
            in_specs=[pl.BlockSpec((tm, d), lambda k, j, i, p: (i, 0)), ANY] + [ANY] * n_ci,
            out_specs=(pl.BlockSpec((tm, tn), lambda k, j, i, p: (i, p[k] * ncol + j)), ANY, *([ANY] * n_co)),
            scratch_shapes=[pltpu.VMEM((2, d, tn), BF16), pltpu.SemaphoreType.DMA((2,)),
                            pltpu.SemaphoreType.DMA((6,)), pltpu.SemaphoreType.DMA((6,)),
                            pltpu.SemaphoreType.DMA((1,))] + comm.scratch()),
        compiler_params=_params("arbitrary", "arbitrary", "arbitrary"),
    )(order, hx, w_shard, *comm.ins)
    return res


def _out_proj(y, w_out, xs, rows, n_ctx, name):
    t, k = y.shape
    d = w_out.shape[1]
    tm, tn = _pick(t, 1056, 16), _pick(d, 1024, 128)

    def body(y_ref, w_ref, x_ref, r_ref, xo_ref, p_ref):
        i = pl.program_id(0)
        p = _nn(y_ref[...], w_ref[...])
        ridx = i * tm + lax.broadcasted_iota(jnp.int32, (tm, 1), 0)
        gate = jnp.where(ridx < n_ctx, r_ref[5:6, :], r_ref[2:3, :])
        p_ref[...] = p.astype(BF16)
        xo_ref[...] = x_ref[...] + gate * p

    return pl.pallas_call(
        body, name=name,
        out_shape=(jax.ShapeDtypeStruct((t, d), F32), jax.ShapeDtypeStruct((t, d), BF16)),
        grid=(t // tm, d // tn),
        in_specs=[pl.BlockSpec((tm, k), lambda i, j: (i, 0)), pl.BlockSpec((k, tn), lambda i, j: (0, j)),
                  pl.BlockSpec((tm, tn), lambda i, j: (i, j)), pl.BlockSpec((8, tn), lambda i, j: (0, j))],
        out_specs=(pl.BlockSpec((tm, tn), lambda i, j: (i, j)), pl.BlockSpec((tm, tn), lambda i, j: (i, j))),
        compiler_params=_params("parallel", "parallel"),
    )(y, w_out, xs, rows)


def _modulate(xs, nw, rows, nct, name):
    t, d = xs.shape
    tr = ROW_TILE

    def body(x_ref, nw_ref, r_ref, o_ref, ot_ref):
        is_ctx = pl.program_id(0) < nct
        x = x_ref[...]
        xn = (x * lax.rsqrt(jnp.mean(x * x, axis=-1, keepdims=True) + EPS)) * nw_ref[...]
        shift = jnp.where(is_ctx, r_ref[3:4, :], r_ref[0:1, :])
        scale = jnp.where(is_ctx, r_ref[4:5, :], r_ref[1:2, :])
        hx = xn * (1.0 + scale) + shift
        o_ref[...] = hx.astype(BF16)
        ot_ref[...] = hx.T.astype(BF16)

    return pl.pallas_call(
        body, name=name,
        out_shape=(jax.ShapeDtypeStruct((t, d), BF16), jax.ShapeDtypeStruct((d, t), BF16)), grid=(t // tr,),
        in_specs=[pl.BlockSpec((tr, d), lambda i: (i, 0)), pl.BlockSpec((1, d), lambda i: (0, 0)),
                  pl.BlockSpec((8, d), lambda i: (0, 0))],
        out_specs=(pl.BlockSpec((tr, d), lambda i: (i, 0)), pl.BlockSpec((d, tr), lambda i: (0, i))),
        compiler_params=_params("parallel"),
    )(xs, nw, rows)


def _gate_step(g, is_ctx, below, outs):
    p_ref, r_ref = below
    gp_ref, acc_ref = outs
    gate = jnp.where(is_ctx, r_ref[5:6, :], r_ref[2:3, :])
    gp_ref[...] = (g * gate).astype(BF16)
    s = jnp.sum(g * p_ref[...].astype(F32), axis=0, keepdims=True)
    zero = jnp.zeros_like(s)
    acc_ref[2:3, :] += jnp.where(is_ctx, zero, s)
    acc_ref[5:6, :] += jnp.where(is_ctx, s, zero)


def _modulate_bwd(dhx, xs, g, nw, rows, nct, name, below=None):
    t, d = xs.shape
    tr = ROW_TILE
    latent_only = below is None
    t_out = t - nct * tr if latent_only else t
    out_row = (lambda i: (jnp.maximum(i - nct, 0), 0)) if latent_only else (lambda i: (i, 0))
    n_below = 0 if latent_only else 2

    def body(dh_ref, x_ref, g_ref, nw_ref, r_ref, *rest):
        go_ref, acc_ref = rest[n_below:n_below + 2]
        i = pl.program_id(0)
        is_ctx = i < nct

        @pl.when(i == 0)
        def _():
            acc_ref[...] = jnp.zeros_like(acc_ref)
            if n_below:
                rest[n_below + 3][...] = jnp.zeros_like(rest[n_below + 3])

        x, dh = x_ref[...], dh_ref[...].astype(F32)
        r = lax.rsqrt(jnp.mean(x * x, axis=-1, keepdims=True) + EPS)
        xh = x * r
        xn = xh * nw_ref[...]
        scale = jnp.where(is_ctx, r_ref[4:5, :], r_ref[1:2, :])
        d_shift = jnp.sum(dh, axis=0, keepdims=True)
        d_scale = jnp.sum(dh * xn, axis=0, keepdims=True)
        dxn = dh * (1.0 + scale)
        acc_ref[6:7, :] += jnp.sum(dxn * xh, axis=0, keepdims=True)
        dxh = dxn * nw_ref[...]
        dx = r * (dxh - xh * jnp.mean(dxh * xh, axis=-1, keepdims=True))
        g_in = g_ref[...] + dx
        go_ref[...] = g_in
        zero = jnp.zeros_like(d_shift)
        acc_ref[0:1, :] += jnp.where(is_ctx, zero, d_shift)
        acc_ref[1:2, :] += jnp.where(is_ctx, zero, d_scale)
        acc_ref[3:4, :] += jnp.where(is_ctx, d_shift, zero)
        acc_ref[4:5, :] += jnp.where(is_ctx, d_scale, zero)
        if n_below:
            _gate_step(g_in, is_ctx, rest[:2], rest[n_below + 2:])

    row = pl.BlockSpec((tr, d), lambda i: (i, 0))
    vecs = pl.BlockSpec((8, d), lambda i: (0, 0))
    return pl.pallas_call(
        body, name=name,
        out_shape=(jax.ShapeDtypeStruct((t_out, d), F32), jax.ShapeDtypeStruct((8, d), F32))
        + ((jax.ShapeDtypeStruct((t, d), BF16), jax.ShapeDtypeStruct((8, d), F32)) if n_below else ()),
        grid=(t // tr,),
        in_specs=[row, row, row, pl.BlockSpec((1, d), lambda i: (0, 0)), vecs] + ([row, vecs] if n_below else []),
        out_specs=(pl.BlockSpec((tr, d), out_row), vecs) + ((row, vecs) if n_below else ()),
        compiler_params=_params("arbitrary"),
    )(dhx, xs, g, nw, rows, *(below or ()))


def _final_loss(xs, tgt, fw, below, nct, name):
    t, d = xs.shape
    tr = ROW_TILE

    def body(x_ref, t_ref, w_ref, p_ref, r_ref, g_ref, acc_ref, gp_ref, gacc_ref):
        i = pl.program_id(0)
        is_ctx = i < nct

        @pl.when(i == 0)
        def _():
            acc_ref[...] = jnp.zeros_like(acc_ref)
            gacc_ref[...] = jnp.zeros_like(gacc_ref)

        @pl.when(is_ctx)
        def _():
            g_ref[...] = jnp.zeros_like(g_ref)
            gp_ref[...] = jnp.zeros_like(gp_ref)

        @pl.when(jnp.logical_not(is_ctx))
        def _():
            x = x_ref[...]
            r = lax.rsqrt(jnp.mean(x * x, axis=-1, keepdims=True) + EPS)
            xh = x * r
            e = xh * w_ref[...] - t_ref[...]
            acc_ref[1:2, :] += jnp.zeros((1, d), F32) + jnp.sum(e * e) * (0.5 / d)
            dy = e * (1.0 / d)
            acc_ref[0:1, :] += jnp.sum(dy * xh, axis=0, keepdims=True)
            dxh = dy * w_ref[...]
            g = r * (dxh - xh * jnp.mean(dxh * xh, axis=-1, keepdims=True))
            g_ref[...] = g
            _gate_step(g, is_ctx, (p_ref, r_ref), (gp_ref, gacc_ref))

    row = pl.BlockSpec((tr, d), lambda i: (i, 0))
    vecs = pl.BlockSpec((8, d), lambda i: (0, 0))
    return pl.pallas_call(
        body, name=name,
        out_shape=(jax.ShapeDtypeStruct((t, d), F32), jax.ShapeDtypeStruct((8, d), F32),
                   jax.ShapeDtypeStruct((t, d), BF16), jax.ShapeDtypeStruct((8, d), F32)), grid=(t // tr,),
        in_specs=[row, pl.BlockSpec((tr, d), lambda i: (jnp.maximum(i - nct, 0), 0)),
                  pl.BlockSpec((1, d), lambda i: (0, 0)), row, vecs],
        out_specs=(row, vecs, row, vecs),
        compiler_params=_params("arbitrary"),
    )(xs, tgt, fw, *below)


def _seq_edges(i, nct, nt):
    has_prev = jnp.logical_and(i != 0, i != nct)
    has_next = jnp.logical_and(i != nct - 1, i != nt - 1)
    return has_prev, has_next


def _shift_rows(a, before, after):
    tr = a.shape[0]
    ridx = lax.broadcasted_iota(jnp.int32, (tr, 1), 0)
    down = jnp.where(ridx == 0, before, pltpu.roll(a, 1, 0))
    up = jnp.where(ridx == tr - 1, after, pltpu.roll(a, tr - 1, 0))
    return down, up


def _halo_specs(tr, dc, col, t, hr):
    per = tr // hr
    prev = pl.BlockSpec((hr, dc), lambda i: (jnp.maximum(i * per - 1, 0), col))
    nxt = pl.BlockSpec((hr, dc), lambda i: (jnp.minimum((i + 1) * per, t // hr - 1), col))
    return prev, nxt


def _mix_fwd(u, o_f, o_b, cw, cnw, rnw, nct, name):
    t, w = u.shape
    dc = w // 8
    nh = dc // HEAD
    tr = ROW_TILE
    nt = t // tr
    col = lambda cidx: pl.BlockSpec((tr, dc), lambda i: (i, cidx))
    hr = 16
    acp, acn = _halo_specs(tr, dc, 2, t, hr)
    ahp, ahn = _halo_specs(tr, dc, 0, t, hr)
    f32 = lambda v: v.astype(F32)

    def body(ah, ab, ac, az, rz, acp_r, acn_r, ahp_r, ahn_r, of_r, ob_r, cw_r, cnw_r, rnw_r, y_ref, yt_ref, cv_ref):
        i = pl.program_id(0)
        has_prev, has_next = _seq_edges(i, nct, nt)
        ch = f32(ac[...]) * f32(ah[...])
        before = jnp.where(has_prev, f32(acp_r[hr - 1:hr, :]) * f32(ahp_r[hr - 1:hr, :]), 0.0)
        after = jnp.where(has_next, f32(acn_r[0:1, :]) * f32(ahn_r[0:1, :]), 0.0)
        down, up = _shift_rows(ch, before, after)
        cv = down * cw_r[0:1, :] + ch * cw_r[1:2, :] + up * cw_r[2:3, :]
        cv_ref[...] = cv
        y1 = f32(ab[...]) * cv
        yn = (y1 * lax.rsqrt(jnp.mean(y1 * y1, axis=-1, keepdims=True) + EPS)) * cnw_r[...]
        yc = _silu(f32(az[...])) * yn
        y_ref[:, 0:dc] = yc.astype(BF16)
        yt_ref[0:dc, :] = yc.T.astype(BF16)
        for h in range(nh):
            sl = slice(h * HEAD, (h + 1) * HEAD)
            o = of_r[:, sl] + ob_r[:, sl]
            dev = o - jnp.mean(o, axis=-1, keepdims=True)
            on = dev * lax.rsqrt(jnp.mean(dev * dev, axis=-1, keepdims=True) + EPS)
            yr = _silu(f32(rz[:, sl])) * (on * rnw_r[:, sl])
            y_ref[:, dc + h * HEAD:dc + (h + 1) * HEAD] = yr.astype(BF16)
            yt_ref[dc + h * HEAD:dc + (h + 1) * HEAD, :] = yr.T.astype(BF16)

    vec = pl.BlockSpec((1, dc), lambda i: (0, 0))
    return pl.pallas_call(
        body, name=name,
        out_shape=(jax.ShapeDtypeStruct((t, 2 * dc), BF16), jax.ShapeDtypeStruct((2 * dc, t), BF16),
                   jax.ShapeDtypeStruct((t, dc), F32)), grid=(nt,),
        in_specs=[col(0), col(1), col(2), col(3), col(7), acp, acn, ahp, ahn,
                  pl.BlockSpec((tr, dc), lambda i: (i, 0)), pl.BlockSpec((tr, dc), lambda i: (i, 0)),
                  pl.BlockSpec((8, dc), lambda i: (0, 0)), vec, vec],
        out_specs=(pl.BlockSpec((tr, 2 * dc), lambda i: (i, 0)), pl.BlockSpec((2 * dc, tr), lambda i: (0, i)),
                   pl.BlockSpec((tr, dc), lambda i: (i, 0))),
        compiler_params=_params("parallel"),
    )(u, u, u, u, u, u, u, u, u, o_f, o_b, cw, cnw, rnw)


def _mix_bwd(dy, u, cv, o_f, o_b, cnw, rnw, name):
    t, w = u.shape
    dc = w // 8
    nh = dc // HEAD
    tr = ROW_TILE
    col = lambda cidx: pl.BlockSpec((tr, dc), lambda i: (i, cidx))

    def body(dyc, dyr, ab, az, rz, cv_r, of_r, ob_r, cnw_r, rnw_r, dab, daz, drz, dcv, do, acc_ref):
        @pl.when(pl.program_id(0) == 0)
        def _():
            acc_ref[...] = jnp.zeros_like(acc_ref)

        z, b, c_out = az[...].astype(F32), ab[...].astype(F32), cv_r[...]
        y1 = b * c_out
        r = lax.rsqrt(jnp.mean(y1 * y1, axis=-1, keepdims=True) + EPS)
        yh = y1 * r
        yn = yh * cnw_r[...]
        dyc_v = dyc[...]
        daz[...] = (dyc_v * yn * _dsilu(z)).astype(BF16)
        dyn = dyc_v * _silu(z)
        acc_ref[0:1, :] += jnp.sum(dyn * yh, axis=0, keepdims=True)
        dyh = dyn * cnw_r[...]
        dy1 = r * (dyh - yh * jnp.mean(dyh * yh, axis=-1, keepdims=True))
        dab[...] = (dy1 * c_out).astype(BF16)
        dcv[...] = dy1 * b
        for h in range(nh):
            sl = slice(h * HEAD, (h + 1) * HEAD)
            o = of_r[:, sl] + ob_r[:, sl]
            dev = o - jnp.mean(o, axis=-1, keepdims=True)
            rs = lax.rsqrt(jnp.mean(dev * dev, axis=-1, keepdims=True) + EPS)
            on = dev * rs
            zz, dyr_v, gw = rz[:, sl].astype(F32), dyr[:, sl], rnw_r[:, sl]
            drz[:, sl] = (dyr_v * (on * gw) * _dsilu(zz)).astype(BF16)
            dong = dyr_v * _silu(zz)
            acc_ref[1:2, sl] += jnp.sum(dong * on, axis=0, keepdims=True)
            don = dong * gw
            do[:, sl] = rs * (don - jnp.mean(don, axis=-1, keepdims=True)
                              - on * jnp.mean(don * on, axis=-1, keepdims=True))

    vec = pl.BlockSpec((1, dc), lambda i: (0, 0))
    row = pl.BlockSpec((tr, dc), lambda i: (i, 0))
    return pl.pallas_call(
        body, name=name,
        out_shape=(jax.ShapeDtypeStruct((t, dc), BF16), jax.ShapeDtypeStruct((t, dc), BF16),
                   jax.ShapeDtypeStruct((t, dc), BF16), jax.ShapeDtypeStruct((t, dc), F32),
                   jax.ShapeDtypeStruct((t, dc), F32), jax.ShapeDtypeStruct((8, dc), F32)),
        grid=(t // tr,),
        in_specs=[col(0), col(1), col(1), col(3), col(7), row, row, row, vec, vec],
        out_specs=(row, row, row, row, row, pl.BlockSpec((8, dc), lambda i: (0, 0))),
        compiler_params=_params("arbitrary"),
    )(dy, dy, u, u, u, cv, o_f, o_b, cnw, rnw)


def _rope_masks():
    lane = lax.broadcasted_iota(jnp.int32, (1, HEAD), 1)
    return (lane % (HEAD // 2)) >= (HEAD // 4)


def _rot(a, upper):
    return jnp.where(upper, pltpu.roll(a, HEAD // 4, 1), pltpu.roll(a, HEAD - HEAD // 4, 1))


def _assemble_du(dcv, u, cw, dab, daz, drz, dq_f, dq_b, dk_f, dk_b, dv_f, dv_b, cos, sin, nct, name):
    t, w = u.shape
    dc = w // 8
    nh = dc // HEAD
    tr = ROW_TILE
    nt = t // tr
    k_scale = HEAD ** -0.5
    col = lambda cidx: pl.BlockSpec((tr, dc), lambda i: (i, cidx))
    row = pl.BlockSpec((tr, dc), lambda i: (i, 0))
    dcp, dcn = _halo_specs(tr, dc, 0, t, 8)
    f32 = lambda v: v.astype(F32)

    def body(dcv_r, dcp_r, dcn_r, ah, ac, cw_r, dab_r, daz_r, drz_r, dqf, dqb, dkf, dkb, dvf, dvb, cos_r, sin_r,
             du, acc_ref):
        i = pl.program_id(0)

        @pl.when(i == 0)
        def _():
            acc_ref[...] = jnp.zeros_like(acc_ref)

        has_prev, has_next = _seq_edges(i, nct, nt)
        d_c = dcv_r[...]
        before = jnp.where(has_prev, dcp_r[7:8, :], 0.0)
        after = jnp.where(has_next, dcn_r[0:1, :], 0.0)
        d_prev, d_next = _shift_rows(d_c, before, after)
        a_h, a_c = f32(ah[...]), f32(ac[...])
        ch = a_c * a_h
        dch = d_next * cw_r[0:1, :] + d_c * cw_r[1:2, :] + d_prev * cw_r[2:3, :]
        acc_ref[0:1, :] += jnp.sum(ch * d_next, axis=0, keepdims=True)
        acc_ref[1:2, :] += jnp.sum(ch * d_c, axis=0, keepdims=True)
        acc_ref[2:3, :] += jnp.sum(ch * d_prev, axis=0, keepdims=True)
        du[:, 0:dc] = (dch * a_c).astype(BF16)
        du[:, dc:2 * dc] = dab_r[...]
        du[:, 2 * dc:3 * dc] = (dch * a_h).astype(BF16)
        du[:, 3 * dc:4 * dc] = daz_r[...]
        du[:, 7 * dc:8 * dc] = drz_r[...]
        du[:, 6 * dc:7 * dc] = (f32(dvf[...]) + f32(dvb[...])).astype(BF16)
        upper = _rope_masks()
        cs, sn = cos_r[...], sin_r[...]
        for h in range(nh):
            sl = slice(h * HEAD, (h + 1) * HEAD)
            dq = f32(dqf[:, sl]) + f32(dqb[:, sl])
            dk = (f32(dkf[:, sl]) + f32(dkb[:, sl])) * k_scale
            du[:, 4 * dc + h * HEAD:4 * dc + (h + 1) * HEAD] = (dq * cs + _rot(dq * sn, upper)).astype(BF16)
            du[:, 5 * dc + h * HEAD:5 * dc + (h + 1) * HEAD] = (dk * cs + _rot(dk * sn, upper)).astype(BF16)

    tab = pl.BlockSpec((tr, HEAD), lambda i: (i, 0))
    return pl.pallas_call(
        body, name=name,
        out_shape=(jax.ShapeDtypeStruct((t, w), BF16), jax.ShapeDtypeStruct((8, dc), F32)), grid=(nt,),
        in_specs=[row, dcp, dcn, col(0), col(2), pl.BlockSpec((8, dc), lambda i: (0, 0)),
                  row, row, row, row, row, row, row, row, row, tab, tab],
        out_specs=(pl.BlockSpec((tr, w), lambda i: (i, 0)), pl.BlockSpec((8, dc), lambda i: (0, 0))),
        compiler_params=_params("arbitrary"),
    )(dcv, dcv, dcv, u, u, cw, dab, daz, drz, dq_f, dq_b, dk_f, dk_b, dv_f, dv_b, cos, sin)


def _bwd_dir_chunk(s, ncc, ns):
    return jnp.where(s < ncc, ncc - 1 - s, ns + ncc - 1 - s)


def _position_weights(d):
    c = CHUNK
    ii = lax.broadcasted_iota(jnp.int32, (c, c), 0)
    jj = lax.broadcasted_iota(jnp.int32, (c, c), 1)
    sd = ((ii - jj) if d == 0 else (jj - ii)).astype(F32)
    pos = lax.broadcasted_iota(jnp.int32, (c, HEAD), 0).astype(F32)
    qw, kw = (pos + 1.0, (c - 1.0) - pos) if d == 0 else (c - pos, pos)
    return sd, qw, kw


def _fill_decay_tables(dec_ref, nh, dm_t, qd_t, kd_t, gc_t):
    for d in range(2):
        sd, qw, kw = _position_weights(d)
        for h in range(nh):
            r = d * nh + h
            lg = -jnp.exp(jnp.zeros((1, HEAD), F32) + dec_ref[d, h])
            dm_t[r] = jnp.where(sd >= 0, jnp.exp(lg * jnp.maximum(sd, 0.0)), 0.0)
            qd_t[r] = jnp.exp(lg * qw)
            kd_t[r] = jnp.exp(lg * kw)
            gc_t[r] = jnp.zeros((8, HEAD), F32) + jnp.exp(lg * float(CHUNK))


def _decay_table_scratch(nh):
    return [pltpu.VMEM((2 * nh, CHUNK, CHUNK), F32), pltpu.VMEM((2 * nh, CHUNK, HEAD), F32),
            pltpu.VMEM((2 * nh, CHUNK, HEAD), F32), pltpu.VMEM((2 * nh, 8, HEAD), F32)]


def _ret_fwd(u, cos, sin, dec, ncc, name):
    t, w = u.shape
    dc = w // 8
    nh = dc // HEAD
    c = CHUNK
    ns = t // c
    k_scale = HEAD ** -0.5
    fmap = lambda g: (lambda s: (s, g))
    bmap = lambda g: (lambda s: (_bwd_dir_chunk(s, ncc, ns), g))
    blk = lambda m: pl.BlockSpec((c, dc), m)
    tab = lambda m: pl.BlockSpec((c, HEAD), m)
    st_f = pl.BlockSpec((nh, None, HEAD, HEAD), lambda s: (0, s, 0, 0))
    st_b = pl.BlockSpec((nh, None, HEAD, HEAD), lambda s: (0, _bwd_dir_chunk(s, ncc, ns), 0, 0))

    def body(qf, kf, vf, qb, kb, vb, cf, sf, cb, sb, dec_ref, of_ref, ob_ref, stf_ref, stb_ref, qr_ref, kr_ref,
             s_f, s_b, dm_t, qd_t, kd_t, gc_t):
        @pl.when(pl.program_id(0) == 0)
        def _():
            s_f[...] = jnp.zeros_like(s_f)
            s_b[...] = jnp.zeros_like(s_b)
            _fill_decay_tables(dec_ref, nh, dm_t, qd_t, kd_t, gc_t)

        upper = _rope_masks()
        dirs = ((qf, kf, vf, cf, sf, of_ref, stf_ref, s_f), (qb, kb, vb, cb, sb, ob_ref, stb_ref, s_b))
        group = RET_INTERLEAVE if nh % RET_INTERLEAVE == 0 else 1
        blocks = [[(h, d) for h in range(h0, h0 + group) for d in range(2)] for h0 in range(0, nh, group)]
        for blk_list in blocks:
            first = []
            for h, d in blk_list:
                q_r, k_r, v_r, cs_r, sn_r, _, st_ref, st = dirs[d]
                sl = slice(h * HEAD, (h + 1) * HEAD)
                r = d * nh + h
                cs, sn = cs_r[...], sn_r[...]
                q = q_r[:, sl].astype(F32)
                q = q * cs + _rot(q, upper) * sn
                k = k_r[:, sl].astype(F32)
                k = (k * cs + _rot(k, upper) * sn) * k_scale
                if d == 0:
                    qr_ref[:, sl] = q
                    kr_ref[:, sl] = k
                vv = v_r[:, sl]
                state = st[h]
                st_ref[h] = state
                p = _nt(_bf(q), _bf(k)) * dm_t[r]
                kv = _tn(_bf(k * kd_t[r]), vv)
                first.append((q, vv, state, p, kv))
            for idx, (h, d) in enumerate(blk_list):
                _, _, _, _, _, o_ref, _, st = dirs[d]
                sl = slice(h * HEAD, (h + 1) * HEAD)
                r = d * nh + h
                q, vv, state, p, kv = first[idx]
                o_ref[:, sl] = _nn(jnp.concatenate([_bf(p), _bf(q * qd_t[r])], axis=1),
                                   jnp.concatenate([vv, _bf(state)], axis=0))
                st[h] = gc_t[r, 0:1, :] * state + kv

    return pl.pallas_call(
        body, name=name,
        out_shape=(jax.ShapeDtypeStruct((t, dc), F32), jax.ShapeDtypeStruct((t, dc), F32),
                   jax.ShapeDtypeStruct((nh, ns, HEAD, HEAD), F32), jax.ShapeDtypeStruct((nh, ns, HEAD, HEAD), F32),
                   jax.ShapeDtypeStruct((t, dc), F32), jax.ShapeDtypeStruct((t, dc), F32)),
        grid=(ns,),
        in_specs=[blk(fmap(4)), blk(fmap(5)), blk(fmap(6)), blk(bmap(4)), blk(bmap(5)), blk(bmap(6)),
                  tab(fmap(0)), tab(fmap(0)), tab(bmap(0)), tab(bmap(0)), SMEM_SPEC],
        out_specs=(blk(fmap(0)), blk(bmap(0)), st_f, st_b, blk(fmap(0)), blk(fmap(0))),
        scratch_shapes=[pltpu.VMEM((nh, HEAD, HEAD), F32), pltpu.VMEM((nh, HEAD, HEAD), F32)]
        + _decay_table_scratch(nh),
        compiler_params=_params("arbitrary"),
    )(u, u, u, u, u, u, cos, sin, cos, sin, dec)


def _ret_bwd(u, qr, kr, do, st_f, st_b, dec, ncc, name):
    t, w = u.shape
    dc = w // 8
    nh = dc // HEAD
    c = CHUNK
    ns = t // c
    fchunk = lambda s: ns - 1 - s
    bchunk = lambda s: _bwd_dir_chunk(ns - 1 - s, ncc, ns)
    fmap = lambda g: (lambda s: (fchunk(s), g))
    bmap = lambda g: (lambda s: (bchunk(s), g))
    blk = lambda m: pl.BlockSpec((c, dc), m)
    stf_spec = pl.BlockSpec((nh, None, HEAD, HEAD), lambda s: (0, fchunk(s), 0, 0))
    stb_spec = pl.BlockSpec((nh, None, HEAD, HEAD), lambda s: (0, bchunk(s), 0, 0))

    def body(qf, kf, vf, dof, qb, kb, vb, dob, stf_ref, stb_ref, dec_ref,
             dqf, dkf, dvf, dqb, dkb, dvb, dlg_ref, ds_f, ds_b, dm_t, qd_t, kd_t, gc_t, lg_acc):
        step = pl.program_id(0)

        @pl.when(step == 0)
        def _():
            ds_f[...] = jnp.zeros_like(ds_f)
            ds_b[...] = jnp.zeros_like(ds_b)
            lg_acc[...] = jnp.zeros_like(lg_acc)
            _fill_decay_tables(dec_ref, nh, dm_t, qd_t, kd_t, gc_t)

        dirs = ((qf, kf, vf, dof, stf_ref, dqf, dkf, dvf, ds_f), (qb, kb, vb, dob, stb_ref, dqb, dkb, dvb, ds_b))
        weights = [_position_weights(d) for d in range(2)]
        group = RET_INTERLEAVE if nh % RET_INTERLEAVE == 0 else 1
        blocks = [[(h, d) for h in range(h0, h0 + group) for d in range(2)] for h0 in range(0, nh, group)]
        for blk_list in blocks:
            first = []
            for h, d in blk_list:
                q_r, k_r, v_r, do_r, st_ref, _, _, _, ds = dirs[d]
                sl = slice(h * HEAD, (h + 1) * HEAD)
                r = d * nh + h
                q, k = q_r[:, sl], k_r[:, sl]
                qb16, kb16, vb16, dob16 = _bf(q), _bf(k), v_r[:, sl], _bf(do_r[:, sl])
                state, dstate = st_ref[h], ds[h]
                dsb16 = _bf(dstate)
                a = _nt(qb16, kb16) * dm_t[r]
                both = _nt(dob16, jnp.concatenate([vb16, _bf(state)], axis=0))
                dk_i = _nt(vb16, dsb16) * kd_t[r]
                first.append((q, k, qb16, kb16, dob16, state, dstate, dsb16, a, both, dk_i))
            for idx, (h, d) in enumerate(blk_list):
                _, _, _, _, _, dq_ref, dk_ref, dv_ref, ds = dirs[d]
                sl = slice(h * HEAD, (h + 1) * HEAD)
                r = d * nh + h
                q, k, qb16, kb16, dob16, state, dstate, dsb16, a, both, dk_i = first[idx]
                sd, qw, kw = weights[d]
                da = both[:, 0:c]
                dp = _bf(da * dm_t[r])
                dq_i = both[:, c:2 * c] * qd_t[r]
                dq_ref[:, sl] = _bf(_nn(dp, kb16) + dq_i)
                dk_ref[:, sl] = _bf(_tn(dp, qb16) + dk_i)
                dv_ref[:, sl] = _bf(_nn(jnp.concatenate([_bf(a.T), _bf(k * kd_t[r])], axis=1),
                                        jnp.concatenate([dob16, dsb16], axis=0)))
                gcr = gc_t[r, 0:1, :]
                ds[h] = gcr * dstate + _tn(_bf(q * qd_t[r]), dob16)
                lg_acc[r] += (da * a * jnp.maximum(sd, 0.0) + (dq_i * q) * qw + (dk_i * k) * kw
                              + (float(c) * gcr) * (state * dstate))

        @pl.when(step == ns - 1)
        def _():
            for d in range(2):
                for h in range(nh):
                    r = d * nh + h
                    lg = -jnp.exp(jnp.zeros((1, HEAD), F32) + dec_ref[d, h])
                    dlg_ref[r:r + 1, :] = (jnp.zeros((1, HEAD), F32) + jnp.sum(lg_acc[r])) * lg

    dshape = jax.ShapeDtypeStruct((t, dc), BF16)
    return pl.pallas_call(
        body, name=name,
        out_shape=(dshape, dshape, dshape, dshape, dshape, dshape, jax.ShapeDtypeStruct((2 * nh, HEAD), F32)),
        grid=(ns,),
        in_specs=[blk(fmap(0)), blk(fmap(0)), blk(fmap(6)), blk(fmap(0)),
                  blk(bmap(0)), blk(bmap(0)), blk(bmap(6)), blk(bmap(0)), stf_spec, stb_spec, SMEM_SPEC],
        out_specs=(blk(fmap(0)), blk(fmap(0)), blk(fmap(0)), blk(bmap(0)), blk(bmap(0)), blk(bmap(0)),
                   pl.BlockSpec((2 * nh, HEAD), lambda s: (0, 0))),
        scratch_shapes=[pltpu.VMEM((nh, HEAD, HEAD), F32), pltpu.VMEM((nh, HEAD, HEAD), F32)]
        + _decay_table_scratch(nh) + [pltpu.VMEM((2 * nh, CHUNK, HEAD), F32)],
        compiler_params=_params("arbitrary"),
    )(qr, kr, u, do, qr, kr, u, do, st_f, st_b, dec)


def _mod_matvec(c16, w_mod, b_mod):
    nl, d, nm = w_mod.shape
    tn = _pick(nm, 512, 128)

    def body(c_ref, w_ref, b_ref, sc_ref, o_ref):
        sc = _silu(c_ref[...])
        sc_ref[...] = sc
        o_ref[...] = _nn(_bf(sc), _bf(w_ref[...])) + b_ref[...]

    return pl.pallas_call(
        body, name="mod_matvec",
        out_shape=(jax.ShapeDtypeStruct((16, d), F32), jax.ShapeDtypeStruct((nl, 16, nm), F32)),
        grid=(nl, nm // tn),
        in_specs=[pl.BlockSpec((16, d), lambda l, j: (0, 0)), pl.BlockSpec((None, d, tn), lambda l, j: (l, 0, j)),
                  pl.BlockSpec((None, 1, tn), lambda l, j: (l, 0, j))],
        out_specs=(pl.BlockSpec((16, d), lambda l, j: (0, 0)), pl.BlockSpec((None, 16, tn), lambda l, j: (l, 0, j))),
        compiler_params=_params("arbitrary", "arbitrary"),
    )(c16, w_mod, b_mod)


def _wmod_grad(sc16, dm, w_mod):
    nl, d, nm = w_mod.shape
    tn = _pick(nm, 512, 128)

    def body(sc_ref, dm_ref, w_ref, gw_ref, pc_ref):
        @pl.when(jnp.logical_and(pl.program_id(0) == 0, pl.program_id(1) == 0))
        def _():
            pc_ref[...] = jnp.zeros_like(pc_ref)

        dmb = _bf(dm_ref[...])
        gw_ref[...] = _tn(_bf(sc_ref[...]), dmb)
        pc_ref[...] += _nt(dmb, _bf(w_ref[...]))

    return pl.pallas_call(
        body, name="wmod_grad",
        out_shape=(jax.ShapeDtypeStruct((nl, d, nm), F32), jax.ShapeDtypeStruct((16, d), F32)),
        grid=(nl, nm // tn),
        in_specs=[pl.BlockSpec((16, d), lambda l, j: (0, 0)), pl.BlockSpec((None, 16, tn), lambda l, j: (l, 0, j)),
                  pl.BlockSpec((None, d, tn), lambda l, j: (l, 0, j))],
        out_specs=(pl.BlockSpec((None, d, tn), lambda l, j: (l, 0, j)), pl.BlockSpec((16, d), lambda l, j: (0, 0))),
        compiler_params=_params("arbitrary", "arbitrary"),
    )(sc16, dm, w_mod)


def _rowsum(a, name):
    r, n = a.shape
    tn = _pick(n, 4096, 128)

    def body(a_ref, o_ref):
        acc = a_ref[0:1, :]
        for i in range(1, r):
            acc = acc + a_ref[i:i + 1, :]
        o_ref[...] = jnp.zeros((8, tn), F32) + acc

    return pl.pallas_call(
        body, name=name, out_shape=jax.ShapeDtypeStruct((8, n), F32), grid=(n // tn,),
        in_specs=[pl.BlockSpec((r, tn), lambda j: (0, j))], out_specs=pl.BlockSpec((8, tn), lambda j: (0, j)),
        compiler_params=_params("parallel"),
    )(a)


def _cctx_grad(parts, c_ctx8):
    _, _, d = parts.shape

    def body(p_ref, c_ref, o_ref):
        acc = ((p_ref[0] + p_ref[1]) + p_ref[2]) + p_ref[3]
        o_ref[...] = acc * _dsilu(c_ref[...])

    return pl.pallas_call(
        body, name="cctx_grad", out_shape=jax.ShapeDtypeStruct((8, d), F32),
        in_specs=[VMEM_SPEC, VMEM_SPEC], out_specs=VMEM_SPEC,
    )(parts, c_ctx8)


def _adamw(g, w, m, v, name):
    r, n = w.shape
    tr = _pick(r, max(8, min(256, (512 * 1024) // n // 8 * 8)), 8)
    bc1 = 1.0 - ADAM_B1 ** ADAM_STEP
    bc2 = 1.0 - ADAM_B2 ** ADAM_STEP

    def body(g_ref, w_ref, m_ref, v_ref, d_ref, mo_ref, vo_ref):
        gg = g_ref[...]
        mn = ADAM_B1 * m_ref[...] + (1.0 - ADAM_B1) * gg
        vn = ADAM_B2 * v_ref[...] + (1.0 - ADAM_B2) * (gg * gg)
        mo_ref[...] = mn
        vo_ref[...] = vn
        d_ref[...] = -ADAM_LR * ((mn / bc1) / (jnp.sqrt(vn / bc2) + ADAM_EPS) + ADAM_WD * w_ref[...])

    spec = pl.BlockSpec((tr, n), lambda i: (i, 0))
    shp = jax.ShapeDtypeStruct((r, n), F32)
    return pl.pallas_call(
        body, name=name, out_shape=(shp, shp, shp), grid=(r // tr,),
        in_specs=[spec, spec, spec, spec], out_specs=(spec, spec, spec),
        compiler_params=_params("parallel"),
    )(g, w, m, v)


def _adamw_layer(g, w, m, v, layer, prev, name):
    r, n = g.shape
    tr = _pick(r, max(8, min(256, (512 * 1024) // n // 8 * 8)), 8)
    nb = r // tr
    bc1 = 1.0 - ADAM_B1 ** ADAM_STEP
    bc2 = 1.0 - ADAM_B2 ** ADAM_STEP
    n_prev = 0 if prev is None else 4

    def body(g_ref, w_ref, m_ref, v_ref, *rest):
        go_ref, d_ref, mo_ref, vo_ref = rest[n_prev:]
        gg = g_ref[...]
        mn = ADAM_B1 * m_ref[...] + (1.0 - ADAM_B1) * gg
        vn = ADAM_B2 * v_ref[...] + (1.0 - ADAM_B2) * (gg * gg)
        go_ref[...] = gg
        mo_ref[...] = mn
        vo_ref[...] = vn
        d_ref[...] = -ADAM_LR * ((mn / bc1) / (jnp.sqrt(vn / bc2) + ADAM_EPS) + ADAM_WD * w_ref[...])

    spec = pl.BlockSpec((tr, n), lambda i: (layer * nb + i, 0))
    shp = jax.ShapeDtypeStruct(w.shape, F32)
    return pl.pallas_call(
        body, name=name, out_shape=(shp, shp, shp, shp), grid=(nb,),
        in_specs=[pl.BlockSpec((tr, n), lambda i: (i, 0)), spec, spec, spec] + [ANY] * n_prev,
        out_specs=(spec, spec, spec, spec),
        input_output_aliases={4 + k: k for k in range(n_prev)},
        compiler_params=_params("parallel"),
    )(g, w, m, v, *(prev or ()))


def _rope_tables(seq, n_ctx):
    f = HEAD // 4
    inv = ROPE_BASE ** (-jnp.arange(f, dtype=F32) / f)
    n_rows = seq // GRID_W
    ang_r = jnp.arange(n_rows, dtype=F32)[:, None] * inv[None, :]
    ang_c = jnp.arange(GRID_W, dtype=F32)[:, None] * inv[None, :]
    by_row = lambda a: jnp.repeat(a, GRID_W, axis=0)
    by_col = lambda a: jnp.tile(a, (n_rows, 1))
    cos_r, sin_r, cos_c, sin_c = by_row(jnp.cos(ang_r)), by_row(jnp.sin(ang_r)), by_col(jnp.cos(ang_c)), by_col(jnp.sin(ang_c))
    cos = jnp.concatenate([cos_r, cos_r, cos_c, cos_c], axis=-1)
    sin = jnp.concatenate([-sin_r, sin_r, -sin_c, sin_c], axis=-1)
    cos = jnp.concatenate([jnp.ones((n_ctx, HEAD), F32), cos], axis=0)
    sin = jnp.concatenate([jnp.zeros((n_ctx, HEAD), F32), sin], axis=0)
    return cos, sin


def _pad_rows(a, rows):
    return jnp.concatenate([a, jnp.zeros((rows - a.shape[0],) + a.shape[1:], a.dtype)], axis=0)


def _pad_flat(parts, mult):
    flat = jnp.concatenate([p.reshape(-1) for p in parts])
    pad = (-flat.shape[0]) % mult
    return jnp.concatenate([flat, jnp.zeros((pad,), flat.dtype)])


def kernel(x, c, ctx, c_ctx, norm_w, w_mod, b_mod, w_in, conv_w, conv_norm_w, ret_norm_w, ret_decay_f, ret_decay_b, w_out, final_norm_w, loss_target, m_c_ctx, m_norm_w, m_w_mod, m_b_mod, m_w_in, m_conv_w, m_conv_norm_w, m_ret_norm_w, m_ret_decay_f, m_ret_decay_b, m_w_out, m_final_norm_w, v_c_ctx, v_norm_w, v_w_mod, v_b_mod, v_w_in, v_conv_w, v_conv_norm_w, v_ret_norm_w, v_ret_decay_f, v_ret_decay_b, v_w_out, v_final_norm_w):
    xi, yi, ci = _mesh_pos()
    b_idx = 4 * xi + 2 * yi + ci
    s_idx = 2 * xi + yi
    pos = jnp.stack([ci, s_idx]).astype(jnp.int32)

    x2, ctx2, tgt = x[0], ctx[0], loss_target[0]
    seq, d = x2.shape
    n_ctx = ctx2.shape[0]
    t = seq + n_ctx
    dc = d // 2
    nh = dc // HEAD
    nl, _, nm = w_mod.shape
    n4 = w_in.shape[2]
    r4 = w_out.shape[1]
    cw4 = conv_w.shape[2]
    nct = n_ctx // ROW_TILE
    ncc = n_ctx // CHUNK
    assert n_ctx % ROW_TILE == 0 and seq % ROW_TILE == 0 and dc % HEAD == 0 and nl * 3 <= 8

    win_b, wout_b = w_in.astype(BF16), w_out.astype(BF16)
    wg_in, wg_out = [None] * nl, [None] * nl
    shard_order = jnp.stack([s_idx, 2 * (1 - xi) + yi, 2 * xi + (1 - yi), 2 * (1 - xi) + (1 - yi)]).astype(jnp.int32)

    small = jnp.concatenate([jnp.broadcast_to(c, (8, d)), _pad_rows(conv_w.reshape(nl * 3, cw4), 8)], axis=1)
    small_g = _allgather_small(small, "gather_c_convw").reshape(N_DEV, 8, d + cw4)
    c_all = small_g[:, 0, :d]
    cw_full = small_g[0::2, :, d:].transpose(1, 0, 2).reshape(8, 4 * cw4)
    cw_l = [_pad_rows(cw_full[l * 3:(l + 1) * 3], 8) for l in range(nl)]

    c16 = _pad_rows(jnp.concatenate([c_all, c_ctx[None]], axis=0), 16)
    b_shard = lax.dynamic_slice(b_mod, (0, s_idx * nm), (nl, nm)).reshape(nl, 1, nm)
    sc16, modp = _mod_matvec(c16, w_mod, b_shard)
    modg = _allgather_small(modp.reshape(nl * 16, nm), "gather_mod").reshape(N_DEV, nl, 16, nm)
    mod_full = modg[0::2].transpose(1, 2, 0, 3).reshape(nl, 16, 4 * nm)
    mod_me = lax.dynamic_index_in_dim(mod_full, b_idx, axis=1, keepdims=False)
    mod_cx = mod_full[:, 8]
    zrow = jnp.zeros((d,), F32)
    rows = []
    for l in range(nl):
        gate_c = mod_cx[l, 2 * d:] if l < nl - 1 else zrow
        rows.append(jnp.stack([mod_me[l, :d], mod_me[l, d:2 * d], mod_me[l, 2 * d:],
                               mod_cx[l, :d], mod_cx[l, d:2 * d], gate_c, zrow, zrow]))

    cos, sin = _rope_tables(seq, n_ctx)
    xs = jnp.concatenate([ctx2, x2], axis=0)
    saved = []
    for l in range(nl):
        dec = jnp.stack([ret_decay_f[l], ret_decay_b[l]])
        hx, hx_t = _modulate(xs, norm_w[l][None], rows[l], nct, f"modulate_{l}")
        u, wg_in[l], wg_out[l] = _in_proj_gathered(hx, win_b[l], shard_order, _gather_comm(wout_b[l], False),
                                                   f"in_proj_{l}")
        o_f, o_b, st_f, st_b, qr, kr = _ret_fwd(u, cos, sin, dec, ncc, f"ret_fwd_{l}")
        y, y_t, cv = _mix_fwd(u, o_f, o_b, cw_l[l], conv_norm_w[l][None], ret_norm_w[l][None], nct, f"mix_fwd_{l}")
        xs_new, proj = _out_proj(y, wg_out[l], xs, rows[l], n_ctx, f"out_proj_{l}")
        saved.append((xs, hx_t, u, o_f, o_b, st_f, st_b, qr, kr, y_t, cv, proj, dec))
        xs = xs_new

    g, facc, gp, gacc = _final_loss(xs, tgt, final_norm_w[None], (saved[nl - 1][11], rows[nl - 1]), nct, "final_loss")
    full = [None] * nl
    join = []
    mine, other = ci.reshape(1).astype(jnp.int32), (1 - ci).reshape(1).astype(jnp.int32)
    dmx, dmc, gnw, gcnw, grnw, gcw, gdf, gdb = ([None] * nl for _ in range(8))
    for l in reversed(range(nl)):
        xs_l, hx_t, u, o_f, o_b, st_f, st_b, qr, kr, y_t, cv, proj, dec = saved[l]
        gout = _matmul(y_t, gp, "nn", f"out_proj_dw_{l}", tm_t=1024, tn_t=1024, tk_t=2816, comms=join)
        if join:
            gout, *full[l + 1] = gout
        dy = _matmul(gp, wg_out[l], "nt", f"out_proj_dx_{l}")
        dab, daz, drz, dcv, do, macc = _mix_bwd(dy, u, cv, o_f, o_b, conv_norm_w[l][None], ret_norm_w[l][None],
                                                f"mix_bwd_{l}")
        dqf, dkf, dvf, dqb, dkb, dvb, dlg = _ret_bwd(u, qr, kr, do, st_f, st_b, dec, ncc, f"ret_bwd_{l}")
        du, cacc = _assemble_du(dcv, u, cw_l[l], dab, daz, drz, dqf, dqb, dkf, dkb, dvf, dvb, cos, sin, nct,
                                f"assemble_du_{l}")
        gout_v = gout.reshape(4, 2, r4 // 2, d)
        gin_other = _matmul(hx_t, du, "nn", f"in_proj_dw_other_{l}", tm_t=1024, tn_t=1024, tk_t=2816, row_half=other)
        gin_mine, land_in, land_out = _matmul(hx_t, du, "nn", f"in_proj_dw_mine_{l}", tm_t=1024, tn_t=1024,
                                              tk_t=2816, row_half=mine, comms=[_swap_comm(gin_other, gout_v)])
        p_in = _sum_half_in(gin_mine, land_in, f"grad_sum_half_in_{l}")
        p_out = _sum_half_out(pos, gout_v, land_out, f"grad_sum_half_out_{l}")
        dhx, l3_in, l3_out = _matmul(du, wg_in[l], "nt", f"in_proj_dx_{l}", comms=[_exchange_comm(p_in, p_out, n4)],
                                     out_dtype=BF16)
        join = [_join_comm(_sum_quarters_in(pos, p_in, l3_in, n4, f"grad_sum_quarters_in_{l}"),
                           _sum_quarters_out(pos, p_out, l3_out, f"grad_sum_quarters_out_{l}"))]
        dmx_gate, dmc_gate = gacc[2], gacc[5]
        below = (saved[l - 1][11], rows[l - 1]) if l > 0 else None
        g, nacc, *nxt = _modulate_bwd(dhx, xs_l, g, norm_w[l][None], rows[l], nct, f"modulate_bwd_{l}", below)
        if nxt:
            gp, gacc = nxt
        dmx[l] = jnp.concatenate([nacc[0], nacc[1], dmx_gate])
        dmc[l] = jnp.concatenate([nacc[3], nacc[4], dmc_gate])
        gnw[l], gcnw[l], grnw[l], gcw[l] = nacc[6], macc[0], macc[1], cacc[0:3]
        gdf[l], gdb[l] = dlg[0:nh, 0], dlg[nh:2 * nh, 0]
    grad_x = g[None]
    full[0] = _run_comm(join[0], "grad_join_0")

    res_in, res_out = None, None
    for l in reversed(range(nl)):
        res_in = _adamw_layer(full[l][0].reshape(d, n4), w_in.reshape(nl * d, n4), m_w_in.reshape(nl * d, n4),
                              v_w_in.reshape(nl * d, n4), l, res_in, f"adamw_w_in_{l}")
        res_out = _adamw_layer(full[l][1].reshape(r4, d), w_out.reshape(nl * r4, d), m_w_out.reshape(nl * r4, d),
                               v_w_out.reshape(nl * r4, d), l, res_out, f"adamw_w_out_{l}")
    g_w_in, d_w_in, nm_w_in, nv_w_in = res_in
    g_w_out, d_w_out, nm_w_out, nv_w_out = res_out

    seg = [jnp.stack(dmx), jnp.stack(dmc), jnp.stack(gnw), jnp.stack(gcnw), jnp.stack(grnw), facc[0],
           jnp.stack(gcw), jnp.stack(gdf), jnp.stack(gdb), facc[1, 0:1]]
    sizes = [int(s.size) for s in seg]
    offs = [sum(sizes[:i]) for i in range(len(sizes))]
    packed = _pad_flat(seg, 8 * 128)
    pw = packed.shape[0] // 8
    allv = _allgather_small(packed.reshape(8, pw), "gather_small_grads").reshape(N_DEV, 8 * pw)
    n_mod = nl * 3 * d
    dmx_all, dmc_all = allv[:, offs[0]:offs[0] + n_mod], allv[:, offs[1]:offs[1] + n_mod]
    rest_all = allv[:, offs[2]:]
    zeros_mod = jnp.zeros_like(dmc_all)
    stack = jnp.concatenate([
        jnp.concatenate([dmx_all, dmc_all], axis=0),
        jnp.concatenate([dmc_all, zeros_mod], axis=0),
        jnp.concatenate([rest_all, jnp.zeros_like(rest_all)], axis=0)], axis=1)
    sums = _rowsum(stack, "sum_small_grads")[0]
    g_b_mod = sums[:n_mod].reshape(nl, 3 * d)
    dmc_tot = sums[n_mod:2 * n_mod].reshape(nl, 3 * d)
    rest = sums[2 * n_mod:]

    def take(i):
        o = offs[i] - offs[2]
        return rest[o:o + sizes[i]]

    g_norm_w = take(2).reshape(nl, d)
    g_conv_norm_w = take(3).reshape(nl, dc)
    g_ret_norm_w = take(4).reshape(nl, dc)
    g_final_norm_w = take(5)
    g_conv_w = lax.dynamic_slice(take(6).reshape(nl, 3, dc), (0, 0, s_idx * cw4), (nl, 3, cw4))
    g_decay_f = take(7).reshape(nl, nh)
    g_decay_b = take(8).reshape(nl, nh)
    loss = take(9)[0]

    dmx_mine = lax.dynamic_slice(dmx_all.reshape(N_DEV, nl, 3 * d), (0, 0, s_idx * nm), (N_DEV, nl, nm))
    dmc_mine = lax.dynamic_slice(dmc_tot, (0, s_idx * nm), (nl, nm))
    dm = jnp.concatenate([dmx_mine.transpose(1, 0, 2), dmc_mine[:, None, :], jnp.zeros((nl, 7, nm), F32)], axis=1)
    g_w_mod, pc = _wmod_grad(sc16, dm, w_mod)
    pc_g = _allgather_small(pc[8:16], "gather_cctx_partials").reshape(N_DEV, 8, d)
    g_c_ctx = _cctx_grad(pc_g[0::2], jnp.broadcast_to(c_ctx[None], (8, d)))[0]
    d_w_mod, nm_w_mod, nv_w_mod = _adamw(g_w_mod.reshape(nl * d, nm), w_mod.reshape(nl * d, nm),
                                         m_w_mod.reshape(nl * d, nm), v_w_mod.reshape(nl * d, nm), "adamw_w_mod")

    small_g = [g_c_ctx, g_norm_w, g_b_mod, g_conv_w, g_conv_norm_w, g_ret_norm_w, g_decay_f, g_decay_b, g_final_norm_w]
    small_w = [c_ctx, norm_w, b_mod, conv_w, conv_norm_w, ret_norm_w, ret_decay_f, ret_decay_b, final_norm_w]
    small_m = [m_c_ctx, m_norm_w, m_b_mod, m_conv_w, m_conv_norm_w, m_ret_norm_w, m_ret_decay_f, m_ret_decay_b,
               m_final_norm_w]
    small_v = [v_c_ctx, v_norm_w, v_b_mod, v_conv_w, v_conv_norm_w, v_ret_norm_w, v_ret_decay_f, v_ret_decay_b,
               v_final_norm_w]
    pk = lambda parts: _pad_flat(parts, 8 * 128).reshape(-1, 128)
    sd, sm, sv = _adamw(pk(small_g), pk(small_w), pk(small_m), pk(small_v), "adamw_small")

    def unpack(flat2d):
        flat = flat2d.reshape(-1)
        out, o = [], 0
        for wgt in small_w:
            out.append(flat[o:o + wgt.size].reshape(wgt.shape))
            o += wgt.size
        return out

    sd, sm, sv = unpack(sd), unpack(sm), unpack(sv)

    def order(small, mod, w_in_, w_out_):
        return [small[0], small[1], mod, small[2], w_in_, small[3], small[4], small[5], small[6], small[7], w_out_,
                small[8]]

    grads = order(small_g, g_w_mod, g_w_in.reshape(nl, d, n4), g_w_out.reshape(nl, r4, d))
    deltas = order(sd, d_w_mod.reshape(nl, d, nm), d_w_in.reshape(nl, d, n4), d_w_out.reshape(nl, r4, d))
    new_m = order(sm, nm_w_mod.reshape(nl, d, nm), nm_w_in.reshape(nl, d, n4), nm_w_out.reshape(nl, r4, d))
    new_v = order(sv, nv_w_mod.reshape(nl, d, nm), nv_w_in.reshape(nl, d, n4), nv_w_out.reshape(nl, r4, d))
    return (loss, grad_x, *grads, *deltas, *new_m, *new_v)
```

```python
import functools

import jax
import jax.numpy as jnp
from jax import lax
from jax.experimental import pallas as pl
from jax.experimental.pallas import tpu as pltpu

F32 = jnp.float32
BF16 = jnp.bfloat16
EPS = 1e-6
CHUNK = 128
HEAD = 128
GRID_W = 64
ROPE_BASE = 10000.0
ROW_TILE = 256
ROW_CHUNK = 16
CHUNK_UNROLL = 4
RET_INTERLEAVE = 4
VMEM_LIMIT_BYTES = 56 * 1024 * 1024
N_DEV = 8
MESH = pl.DeviceIdType.MESH

ADAM_LR = 0.001
ADAM_B1 = 0.9
ADAM_B2 = 0.999
ADAM_EPS = 1e-08
ADAM_WD = 0.01
ADAM_STEP = 10

ANY = pl.BlockSpec(memory_space=pl.ANY)
VMEM_SPEC = pl.BlockSpec(memory_space=pltpu.VMEM)
SMEM_SPEC = pl.BlockSpec(memory_space=pltpu.SMEM)


def _params(*sem):
    return pltpu.CompilerParams(dimension_semantics=sem, vmem_limit_bytes=VMEM_LIMIT_BYTES)


def _pick(n, target, mult):
    best = None
    for d in range(mult, min(n, target) + 1, mult):
        if n % d == 0:
            best = d
    assert best is not None, (n, target, mult)
    return best


def _for_row_chunks(n_rows, fn):
    def step(r, carry):
        for q in range(CHUNK_UNROLL):
            fn(pl.multiple_of((r * CHUNK_UNROLL + q) * ROW_CHUNK, ROW_CHUNK), q)
        return carry

    lax.fori_loop(0, n_rows // (ROW_CHUNK * CHUNK_UNROLL), step, 0)


class _Partials:
    def __init__(self, ref, n_sums):
        self.ref, self.n_sums = ref, n_sums

    def zero(self):
        self.ref[...] = jnp.zeros_like(self.ref)

    def add(self, q, k, value, cols=slice(None)):
        r = (q * self.n_sums + k) * 8
        self.ref[r:r + 8, cols] += value

    def total(self, k):
        rows = [(q * self.n_sums + k) * 8 for q in range(CHUNK_UNROLL)]
        acc = self.ref[rows[0]:rows[0] + 8, :]
        for r in rows[1:]:
            acc = acc + self.ref[r:r + 8, :]
        return jnp.sum(acc, axis=0, keepdims=True)

    @staticmethod
    def scratch(n_sums, n):
        return pltpu.VMEM((CHUNK_UNROLL * n_sums * 8, n), F32)


def _fold8(a):
    out = a[0:8]
    for hh in range(1, ROW_CHUNK // 8):
        out = out + a[8 * hh:8 * hh + 8]
    return out


def _silu(z):
    return z * jax.nn.sigmoid(z)


def _dsilu(z):
    s = jax.nn.sigmoid(z)
    return s * (1.0 + z * (1.0 - s))


def _nn(a, b):
    return lax.dot_general(a, b, (((1,), (0,)), ((), ())), preferred_element_type=F32)


def _nt(a, b):
    return lax.dot_general(a, b, (((1,), (1,)), ((), ())), preferred_element_type=F32)


def _tn(a, b):
    return lax.dot_general(a, b, (((0,), (0,)), ((), ())), preferred_element_type=F32)


def _bf(a):
    return a.astype(BF16)


def _mesh_pos():
    return lax.axis_index("x"), lax.axis_index("y"), lax.axis_index("c")


def _allgather_small(blk, name):
    m_per, n = blk.shape

    def body(x_ref, out_ref, send_sems, recv_sems, local_sem):
        x, y, c = _mesh_pos()
        me, sibling = (x, y, c), (x, y, 1 - c)
        chips = [(1 - x, y), (x, 1 - y), (1 - x, 1 - y)]

        def rows(px, py, pc):
            return out_ref.at[pl.ds((4 * px + 2 * py + pc) * m_per, m_per), :]

        def copy(k, block, to, src=None):
            return pltpu.make_async_remote_copy(
                src_ref=rows(*block) if src is None else src, dst_ref=rows(*block),
                send_sem=send_sems.at[k], recv_sem=recv_sems.at[k], device_id=to, device_id_type=MESH)

        mine = pltpu.make_async_copy(x_ref, rows(*me), local_sem)
        mine.start()
        first = [copy(0, me, sibling, src=x_ref)]
        first += [copy(1 + j, me, (*chip, c), src=x_ref) for j, chip in enumerate(chips)]
        for cp in first:
            cp.start()
        passed = [copy(4 + j, (*chip, c), sibling) for j, chip in enumerate(chips)]
        for j, chip in enumerate(chips):
            copy(1 + j, (*chip, c), me).wait_recv()
            passed[j].start()
        copy(0, sibling, me).wait_recv()
        for j, chip in enumerate(chips):
            copy(4 + j, (*chip, 1 - c), me).wait_recv()
        for cp in first + passed:
            cp.wait_send()
        mine.wait()

    return pl.pallas_call(
        body, name=name,
        out_shape=jax.ShapeDtypeStruct((N_DEV * m_per, n), blk.dtype),
        in_specs=[VMEM_SPEC], out_specs=VMEM_SPEC,
        scratch_shapes=[pltpu.SemaphoreType.DMA((7,)), pltpu.SemaphoreType.DMA((7,)), pltpu.SemaphoreType.DMA],
    )(blk)


class _Comm:
    def __init__(self, ins, outs, n_remote, n_local, start, finish, aliases=None):
        self.ins, self.outs, self.n_remote, self.n_local = list(ins), list(outs), n_remote, n_local
        self.start, self.finish, self.aliases = start, finish, dict(aliases or {})

    def scratch(self):
        return [pltpu.SemaphoreType.DMA((self.n_remote,)), pltpu.SemaphoreType.DMA((self.n_remote,)),
                pltpu.SemaphoreType.DMA((max(self.n_local, 1),))]


def _run_comm(comm, name):
    n_in, n_out = len(comm.ins), len(comm.outs)

    def body(*refs):
        args = (refs[:n_in], refs[n_in:n_in + n_out]) + tuple(refs[n_in + n_out:])
        comm.start(*args)
        comm.finish(*args)

    return pl.pallas_call(
        body, name=name, out_shape=tuple(comm.outs), in_specs=[ANY] * n_in, out_specs=tuple([ANY] * n_out),
        scratch_shapes=comm.scratch(), input_output_aliases=comm.aliases,
    )(*comm.ins)


def _remote(src, dst, send, recv, k, to):
    return pltpu.make_async_remote_copy(src_ref=src, dst_ref=dst, send_sem=send.at[k], recv_sem=recv.at[k],
                                        device_id=to, device_id_type=MESH)


def _gather_comm(shard, by_columns):
    rows, cols = shard.shape
    hr = rows // 2

    def build(ins, outs, send, recv, loc):
        x, y, c = _mesh_pos()
        sib = (x, y, 1 - c)
        peers = [(1 - x, y), (x, 1 - y), (1 - x, 1 - y)]
        s_me = 2 * x + y

        def slot(s, h):
            if by_columns:
                return outs[0].at[pl.ds(pl.multiple_of(h * hr, 16), hr), pl.ds(pl.multiple_of(s * cols, 128), cols)]
            return outs[0].at[pl.ds(pl.multiple_of(s * rows + h * hr, 16), hr), :]

        def half(h):
            return ins[0].at[pl.ds(pl.multiple_of(h * hr, 16), hr), :]

        def local():
            if by_columns:
                mine = outs[0].at[:, pl.ds(pl.multiple_of(s_me * cols, 128), cols)]
            else:
                mine = outs[0].at[pl.ds(pl.multiple_of(s_me * rows, 16), rows), :]
            return [pltpu.make_async_copy(ins[0], mine, loc.at[0])]

        def per_copy(make):
            return [make(j, 2 * px + py, (px, py, c)) for j, (px, py) in enumerate(peers)]

        ici_send = lambda: per_copy(lambda j, s_p, to: _remote(half(c), slot(s_me, c), send, recv, j, to))
        ici_recv = lambda: per_copy(lambda j, s_p, to: _remote(half(c), slot(s_p, c), send, recv, j, to))
        d2d_send = lambda: per_copy(lambda j, s_p, to: _remote(slot(s_p, c), slot(s_p, c), send, recv, 3 + j, sib))
        d2d_recv = lambda: per_copy(lambda j, s_p, to: _remote(slot(s_p, c), slot(s_p, 1 - c), send, recv, 3 + j, sib))
        return local, ici_send, ici_recv, d2d_send, d2d_recv

    def start(*refs):
        local, ici_send, _, _, _ = build(*refs)
        for cp in local() + ici_send():
            cp.start()

    def finish(*refs):
        local, ici_send, ici_recv, d2d_send, d2d_recv = build(*refs)
        forwards = d2d_send()
        for arrived, forward in zip(ici_recv(), forwards):
            arrived.wait_recv()
            forward.start()
        for cp in d2d_recv():
            cp.wait_recv()
        for cp in ici_send() + forwards:
            cp.wait_send()
        for cp in local():
            cp.wait()

    full = (rows, 4 * cols) if by_columns else (4 * rows, cols)
    return _Comm([shard], [jax.ShapeDtypeStruct(full, BF16)], 6, 1, start, finish)


def _swap_comm(gin_other, gout_v):
    dh, w = gin_other.shape
    ns, _, rh, d = gout_v.shape

    def build(ins, outs, send, recv, loc):
        x, y, c = _mesh_pos()
        sib = (x, y, 1 - c)
        return [_remote(ins[0], outs[0], send, recv, 0, sib),
                _remote(ins[1].at[:, 1 - c], outs[1], send, recv, 1, sib)]

    def start(*refs):
        for cp in build(*refs):
            cp.start()

    def finish(*refs):
        for cp in build(*refs):
            cp.wait()

    outs = [jax.ShapeDtypeStruct((dh, w), F32), jax.ShapeDtypeStruct((ns, rh, d), F32)]
    return _Comm([gin_other, gout_v], outs, 2, 0, start, finish)


def _exchange_comm(p_in, p_out, n4):
    dh, _ = p_in.shape
    _, rh, d = p_out.shape

    def build(ins, outs, send, recv, loc):
        x, y, c = _mesh_pos()
        cps = []
        for j, (px, py) in enumerate([(1 - x, y), (x, 1 - y), (1 - x, 1 - y)]):
            s_p = 2 * px + py
            cps.append(_remote(ins[0].at[:, pl.ds(pl.multiple_of(s_p * n4, 128), n4)], outs[0].at[j], send, recv,
                               2 * j, (px, py, c)))
            cps.append(_remote(ins[1].at[s_p], outs[1].at[j], send, recv, 2 * j + 1, (px, py, c)))
        return cps

    def start(*refs):
        for cp in build(*refs):
            cp.start()

    def finish(*refs):
        cps = build(*refs)
        for cp in cps:
            cp.wait_recv()
        for cp in cps:
            cp.wait_send()

    outs = [jax.ShapeDtypeStruct((3, dh, n4), p_in.dtype), jax.ShapeDtypeStruct((3, rh, d), p_out.dtype)]
    return _Comm([p_in, p_out], outs, 6, 0, start, finish)


def _join_comm(q_in, q_out):
    def build(ins, outs, send, recv, loc):
        x, y, c = _mesh_pos()
        sib = (x, y, 1 - c)
        sends = lambda: [_remote(outs[a].at[c], outs[a].at[c], send, recv, a, sib) for a in range(2)]
        recvs = lambda: [_remote(outs[a].at[c], outs[a].at[1 - c], send, recv, a, sib) for a in range(2)]
        return sends, recvs

    def start(*refs):
        for cp in build(*refs)[0]():
            cp.start()

    def finish(*refs):
        sends, recvs = build(*refs)
        for cp in recvs():
            cp.wait_recv()
        for cp in sends():
            cp.wait_send()

    outs = [jax.ShapeDtypeStruct(q_in.shape, F32), jax.ShapeDtypeStruct(q_out.shape, F32)]
    return _Comm([q_in, q_out], outs, 2, 0, start, finish, aliases={0: 0, 1: 1})


def _sum_half_in(mine, land, name):
    dh, w = mine.shape
    tr, tc = _pick(dh, 256, 16), _pick(w, 2048, 128)

    def body(a_ref, b_ref, o_ref):
        o_ref[...] = (a_ref[...] + b_ref[...]).astype(BF16)

    spec = pl.BlockSpec((tr, tc), lambda i, j: (i, j))
    return pl.pallas_call(
        body, name=name, out_shape=jax.ShapeDtypeStruct((dh, w), BF16), grid=(dh // tr, w // tc),
        in_specs=[spec, spec], out_specs=spec, compiler_params=_params("parallel", "parallel"),
    )(mine, land)


def _sum_half_out(pos, gout_v, land, name):
    ns, _, rh, d = gout_v.shape

    def body(pos_ref, a_ref, b_ref, o_ref):
        o_ref[...] = (a_ref[...] + b_ref[...]).astype(BF16)

    return pl.pallas_call(
        body, name=name,
        out_shape=jax.ShapeDtypeStruct((ns, rh, d), BF16),
        grid_spec=pltpu.PrefetchScalarGridSpec(
            num_scalar_prefetch=1, grid=(ns,),
            in_specs=[pl.BlockSpec((None, None, rh, d), lambda s, p: (s, p[0], 0, 0)),
                      pl.BlockSpec((None, rh, d), lambda s, p: (s, 0, 0))],
            out_specs=pl.BlockSpec((None, rh, d), lambda s, p: (s, 0, 0))),
        compiler_params=_params("parallel"),
    )(pos, gout_v, land)


def _sum_quarters_in(pos, p_in, land3, n4, name):
    dh, _ = p_in.shape
    tr = _pick(dh, 128, 16)

    def body(pos_ref, a_ref, b_ref, o_ref):
        f = lambda v: v.astype(F32)
        o_ref[...] = ((f(a_ref[...]) + f(b_ref[0])) + f(b_ref[1])) + f(b_ref[2])

    return pl.pallas_call(
        body, name=name,
        out_shape=jax.ShapeDtypeStruct((2, dh, n4), F32),
        grid_spec=pltpu.PrefetchScalarGridSpec(
            num_scalar_prefetch=1, grid=(dh // tr,),
            in_specs=[pl.BlockSpec((tr, n4), lambda i, p: (i, p[1])),
                      pl.BlockSpec((3, tr, n4), lambda i, p: (0, i, 0))],
            out_specs=pl.BlockSpec((None, tr, n4), lambda i, p: (p[0], i, 0))),
        compiler_params=_params("parallel"),
    )(pos, p_in, land3)


def _sum_quarters_out(pos, p_out, land3, name):
    _, rh, d = p_out.shape

    def body(pos_ref, a_ref, b_ref, o_ref):
        f = lambda v: v.astype(F32)
        o_ref[...] = ((f(a_ref[...]) + f(b_ref[0])) + f(b_ref[1])) + f(b_ref[2])

    return pl.pallas_call(
        body, name=name,
        out_shape=jax.ShapeDtypeStruct((2, rh, d), F32),
        grid_spec=pltpu.PrefetchScalarGridSpec(
            num_scalar_prefetch=1, grid=(1,),
            in_specs=[pl.BlockSpec((None, rh, d), lambda i, p: (p[1], 0, 0)),
                      pl.BlockSpec((3, rh, d), lambda i, p: (0, 0, 0))],
            out_specs=pl.BlockSpec((None, rh, d), lambda i, p: (p[0], 0, 0))),
        compiler_params=_params("arbitrary"),
    )(pos, p_out, land3)


def _matmul(a, b, kind, name, tm_t=1056, tn_t=1024, tk_t=2048, comms=(), row_half=None, out_dtype=F32):
    if kind == "nn":
        (m, k), n = a.shape, b.shape[1]
    elif kind == "nt":
        (m, k), n = a.shape, b.shape[0]
    else:
        (k, m), n = a.shape, b.shape[1]
    if row_half is None:
        row_half = jnp.zeros((1,), jnp.int32)
    else:
        assert kind == "nn"
        m = m // 2
    tm = _pick(m, tm_t, 128 if kind == "tn" else 16)
    tn = _pick(n, tn_t, 128)
    tk = _pick(k, tk_t, 16 if kind == "tn" else 128)
    nk = k // tk
    nbm = m // tm
    grid = (nbm, n // tn, nk)
    dot = {"nn": _nn, "nt": _nt, "tn": _tn}[kind]
    a_spec = {"nn": pl.BlockSpec((tm, tk), lambda i, j, q, p: (i + p[0] * nbm, q)),
              "nt": pl.BlockSpec((tm, tk), lambda i, j, q, p: (i, q)),
              "tn": pl.BlockSpec((tk, tm), lambda i, j, q, p: (q, i))}[kind]
    b_spec = {"nn": pl.BlockSpec((tk, tn), lambda i, j, q, p: (q, j)),
              "nt": pl.BlockSpec((tn, tk), lambda i, j, q, p: (j, q)),
              "tn": pl.BlockSpec((tk, tn), lambda i, j, q, p: (q, j))}[kind]
    comm_ins = [x for cm in comms for x in cm.ins]
    comm_outs = [x for cm in comms for x in cm.outs]
    aliases, scratch = {}, []
    i_off, o_off = 3, 1
    for cm in comms:
        for ci, co in cm.aliases.items():
            aliases[i_off + ci] = o_off + co
        i_off, o_off = i_off + len(cm.ins), o_off + len(cm.outs)
        scratch += cm.scratch()

    def body(half_ref, a_ref, b_ref, *rest):
        cin = rest[:len(comm_ins)]
        o_ref = rest[len(comm_ins)]
        cout = rest[len(comm_ins) + 1:len(comm_ins) + 1 + len(comm_outs)]
        acc_ref = rest[len(comm_ins) + 1 + len(comm_outs)]
        sems = rest[len(comm_ins) + 2 + len(comm_outs):]
        ids = [pl.program_id(ax) for ax in range(3)]
        first = functools.reduce(jnp.logical_and, [ids[ax] == 0 for ax in range(3)])
        last = functools.reduce(jnp.logical_and, [ids[ax] == grid[ax] - 1 for ax in range(3)])

        def comm_args(idx):
            i0 = sum(len(cm.ins) for cm in comms[:idx])
            o0 = sum(len(cm.outs) for cm in comms[:idx])
            cm = comms[idx]
            return (cin[i0:i0 + len(cm.ins)], cout[o0:o0 + len(cm.outs)]) + tuple(sems[3 * idx:3 * idx + 3])

        if comms:
            @pl.when(first)
            def _():
                for idx, cm in enumerate(comms):
                    cm.start(*comm_args(idx))

        q = ids[2]
        p = dot(a_ref[...], b_ref[...])
        if nk == 1:
            o_ref[...] = p.astype(out_dtype)
        else:
            @pl.when(q == 0)
            def _():
                acc_ref[...] = p

            @pl.when(q > 0)
            def _():
                acc_ref[...] += p

            @pl.when(q == nk - 1)
            def _():
                o_ref[...] = acc_ref[...].astype(out_dtype)

        if comms:
            @pl.when(last)
            def _():
                for idx, cm in enumerate(comms):
                    cm.finish(*comm_args(idx))

    sem = ("arbitrary",) * 3 if comms else ("parallel", "parallel", "arbitrary")
    res = pl.pallas_call(
        body, name=name,
        out_shape=(jax.ShapeDtypeStruct((m, n), out_dtype), *comm_outs),
        grid_spec=pltpu.PrefetchScalarGridSpec(
            num_scalar_prefetch=1, grid=grid,
            in_specs=[a_spec, b_spec] + [ANY] * len(comm_ins),
            out_specs=(pl.BlockSpec((tm, tn), lambda i, j, q, p: (i, j)), *([ANY] * len(comm_outs))),
            scratch_shapes=[pltpu.VMEM((tm, tn) if nk > 1 else (8, 128), F32)] + scratch),
        input_output_aliases=aliases,
        compiler_params=_params(*sem),
    )(row_half, a, b, *comm_ins)
    return res if comms else res[0]


def _in_proj_gathered(hx, w_shard, order, comm, name, tm_t=1056, tn_t=1024):
    t, d = hx.shape
    n4 = w_shard.shape[1]
    hr = d // 2
    tm, tn = _pick(t, tm_t, 16), _pick(n4, tn_t, 128)
    ncol, nrow = n4 // tn, t // tm
    ntile = 4 * ncol
    n_ci, n_co = len(comm.ins), len(comm.outs)

    def body(order_ref, a_ref, ws_ref, *rest):
        cin, (o_ref, wg_ref), cout = rest[:n_ci], rest[n_ci:n_ci + 2], rest[n_ci + 2:n_ci + 2 + n_co]
        bbuf, fsem, send, recv, loc = rest[n_ci + 2 + n_co:n_ci + 7 + n_co]
        csems = rest[n_ci + 7 + n_co:]
        k, j, i = pl.program_id(0), pl.program_id(1), pl.program_id(2)
        n = k * ncol + j
        x, y, c = _mesh_pos()
        sib = (x, y, 1 - c)
        peers = [(1 - x, y), (x, 1 - y), (1 - x, 1 - y)]
        s_me = 2 * x + y

        def slot(s, h):
            return wg_ref.at[pl.ds(pl.multiple_of(h * hr, 16), hr), pl.ds(pl.multiple_of(s * n4, 128), n4)]

        def my_half():
            return ws_ref.at[pl.ds(pl.multiple_of(c * hr, 16), hr), :]

        def local():
            return pltpu.make_async_copy(ws_ref, wg_ref.at[:, pl.ds(pl.multiple_of(s_me * n4, 128), n4)], loc.at[0])

        def ici_send(p):
            return _remote(my_half(), slot(s_me, c), send, recv, p, (*peers[p], c))

        def ici_recv(p):
            return _remote(my_half(), slot(2 * peers[p][0] + peers[p][1], c), send, recv, p, (*peers[p], c))

        def d2d_send(p):
            s_p = 2 * peers[p][0] + peers[p][1]
            return _remote(slot(s_p, c), slot(s_p, c), send, recv, 3 + p, sib)

        def d2d_recv(p):
            s_p = 2 * peers[p][0] + peers[p][1]
            return _remote(slot(s_p, c), slot(s_p, 1 - c), send, recv, 3 + p, sib)

        def fetch(m, own):
            col = (m % ncol) * tn
            if own:
                src = ws_ref.at[:, pl.ds(pl.multiple_of(col, 128), tn)]
            else:
                src = wg_ref.at[:, pl.ds(pl.multiple_of(order_ref[m // ncol] * n4 + col, 128), tn)]
            return pltpu.make_async_copy(src, bbuf.at[m % 2], fsem.at[m % 2])

        comm_args = (cin, cout) + tuple(csems)

        @pl.when(jnp.logical_and(i == 0, n == 0))
        def _():
            local().start()
            ici_send(0).start()
            ici_send(1).start()
            fetch(n, True).start()

        @pl.when(i == 0)
        def _():
            fetch(n, True).wait()

        @pl.when(i == max(nrow - 2, 0))
        def _():
            for kk in range(1, 4):
                @pl.when(n == kk * ncol - 1)
                def _():
                    ici_recv(kk - 1).wait_recv()
                    d2d_send(kk - 1).start()
                    d2d_recv(kk - 1).wait_recv()
                    if kk == 1:
                        ici_send(0).wait_send()
                        ici_send(1).wait_send()
                        ici_send(2).start()
                    if kk == 3:
                        comm.start(*comm_args)

            @pl.when(n + 1 < ncol)
            def _():
                fetch(n + 1, True).start()

            @pl.when(jnp.logical_and(n + 1 >= ncol, n + 1 < ntile))
            def _():
                fetch(n + 1, False).start()

        o_ref[...] = _nn(a_ref[...], bbuf[n % 2]).astype(BF16)

        @pl.when(jnp.logical_and(n == ntile - 1, i == nrow - 1))
        def _():
            ici_send(2).wait_send()
            for p in range(3):
                d2d_send(p).wait_send()
            local().wait()
            comm.finish(*comm_args)

    res = pl.pallas_call(
        body, name=name,
        out_shape=(jax.ShapeDtypeStruct((t, 4 * n4), BF16), jax.ShapeDtypeStruct((d, 4 * n4), BF16), *comm.outs),
        grid_spec=pltpu.PrefetchScalarGridSpec(
            num_scalar_prefetch=1, grid=(4, ncol, nrow),
            in_specs=[pl.BlockSpec((tm, d), lambda k, j, i, p: (i, 0)), ANY] + [ANY] * n_ci,
            out_specs=(pl.BlockSpec((tm, tn), lambda k, j, i, p: (i, p[k] * ncol + j)), ANY, *([ANY] * n_co)),
            scratch_shapes=[pltpu.VMEM((2, d, tn), BF16), pltpu.SemaphoreType.DMA((2,)),
                            pltpu.SemaphoreType.DMA((6,)), pltpu.SemaphoreType.DMA((6,)),
                            pltpu.SemaphoreType.DMA((1,))] + comm.scratch()),
        compiler_params=_params("arbitrary", "arbitrary", "arbitrary"),
    )(order, hx, w_shard, *comm.ins)
    return res


def _out_proj(y, w_out, xs, rows, n_ctx, name):
    t, k = y.shape
    d = w_out.shape[1]
    tm, tn = _pick(t, 1056, 16), _pick(d, 1024, 128)

    def body(y_ref, w_ref, x_ref, r_ref, xo_ref, p_ref):
        i = pl.program_id(0)
        p = _nn(y_ref[...], w_ref[...])
        ridx = i * tm + lax.broadcasted_iota(jnp.int32, (tm, 1), 0)
        gate = jnp.where(ridx < n_ctx, r_ref[5:6, :], r_ref[2:3, :])
        p_ref[...] = p.astype(BF16)
        xo_ref[...] = x_ref[...] + gate * p

    return pl.pallas_call(
        body, name=name,
        out_shape=(jax.ShapeDtypeStruct((t, d), F32), jax.ShapeDtypeStruct((t, d), BF16)),
        grid=(t // tm, d // tn),
        in_specs=[pl.BlockSpec((tm, k), lambda i, j: (i, 0)), pl.BlockSpec((k, tn), lambda i, j: (0, j)),
                  pl.BlockSpec((tm, tn), lambda i, j: (i, j)), pl.BlockSpec((8, tn), lambda i, j: (0, j))],
        out_specs=(pl.BlockSpec((tm, tn), lambda i, j: (i, j)), pl.BlockSpec((tm, tn), lambda i, j: (i, j))),
        compiler_params=_params("parallel", "parallel"),
    )(y, w_out, xs, rows)


def _modulate(xs, nw, rows, nct, name):
    t, d = xs.shape
    tr = ROW_TILE

    def body(x_ref, nw_ref, r_ref, o_ref, ot_ref):
        is_ctx = pl.program_id(0) < nct
        x = x_ref[...]
        xn = (x * lax.rsqrt(jnp.mean(x * x, axis=-1, keepdims=True) + EPS)) * nw_ref[...]
        shift = jnp.where(is_ctx, r_ref[3:4, :], r_ref[0:1, :])
        scale = jnp.where(is_ctx, r_ref[4:5, :], r_ref[1:2, :])
        hx = xn * (1.0 + scale) + shift
        o_ref[...] = hx.astype(BF16)
        ot_ref[...] = hx.T.astype(BF16)

    return pl.pallas_call(
        body, name=name,
        out_shape=(jax.ShapeDtypeStruct((t, d), BF16), jax.ShapeDtypeStruct((d, t), BF16)), grid=(t // tr,),
        in_specs=[pl.BlockSpec((tr, d), lambda i: (i, 0)), pl.BlockSpec((1, d), lambda i: (0, 0)),
                  pl.BlockSpec((8, d), lambda i: (0, 0))],
        out_specs=(pl.BlockSpec((tr, d), lambda i: (i, 0)), pl.BlockSpec((d, tr), lambda i: (0, i))),
        compiler_params=_params("parallel"),
    )(xs, nw, rows)


def _modulate_bwd(dhx, xs, g, nw, rows, nct, name, below=None):
    t, d = xs.shape
    tr = ROW_TILE
    latent_only = below is None
    t_out = t - nct * tr if latent_only else t
    out_row = (lambda i: (jnp.maximum(i - nct, 0), 0)) if latent_only else (lambda i: (i, 0))
    n_below = 0 if latent_only else 2

    def body(dh_ref, x_ref, g_ref, nw_ref, r_ref, *rest):
        go_ref, acc_ref = rest[n_below:n_below + 2]
        part = _Partials(rest[-1], 4)
        i = pl.program_id(0)
        is_ctx = i < nct

        @pl.when(i == 0)
        def _():
            acc_ref[...] = jnp.zeros_like(acc_ref)
            if n_below:
                rest[n_below + 3][...] = jnp.zeros_like(rest[n_below + 3])

        part.zero()

        def chunk(r0, q):
            dh16 = dh_ref[pl.ds(r0, ROW_CHUNK), :].astype(F32)
            one_scale = 1.0 + jnp.where(is_ctx, r_ref[4:5, :], r_ref[1:2, :])
            nwv = nw_ref[...]
            if n_below:
                p16 = rest[0][pl.ds(r0, ROW_CHUNK), :].astype(F32)
                gate = jnp.where(is_ctx, rest[1][5:6, :], rest[1][2:3, :])
            gps = []
            for hh in range(ROW_CHUNK // 8):
                rows8 = pl.ds(r0 + 8 * hh, 8)
                x, dh = x_ref[rows8, :], dh16[8 * hh:8 * hh + 8]
                r = lax.rsqrt(jnp.mean(x * x, axis=-1, keepdims=True) + EPS)
                xh = x * r
                part.add(q, 0, dh)
                part.add(q, 1, dh * (xh * nwv))
                dxn = dh * one_scale
                part.add(q, 2, dxn * xh)
                dxh = dxn * nwv
                g_in = g_ref[rows8, :] + r * (dxh - xh * jnp.mean(dxh * xh, axis=-1, keepdims=True))
                go_ref[rows8, :] = g_in
                if n_below:
                    gps.append(g_in * gate)
                    part.add(q, 3, g_in * p16[8 * hh:8 * hh + 8])
            if n_below:
                rest[n_below + 2][pl.ds(r0, ROW_CHUNK), :] = jnp.concatenate(gps, axis=0).astype(BF16)

        _for_row_chunks(tr, chunk)
        total = part.total
        d_shift, d_scale = total(0), total(1)
        acc_ref[6:7, :] += total(2)
        zero = jnp.zeros_like(d_shift)
        acc_ref[0:1, :] += jnp.where(is_ctx, zero, d_shift)
        acc_ref[1:2, :] += jnp.where(is_ctx, zero, d_scale)
        acc_ref[3:4, :] += jnp.where(is_ctx, d_shift, zero)
        acc_ref[4:5, :] += jnp.where(is_ctx, d_scale, zero)
        if n_below:
            gacc_ref, s = rest[n_below + 3], total(3)
            gacc_ref[2:3, :] += jnp.where(is_ctx, zero, s)
            gacc_ref[5:6, :] += jnp.where(is_ctx, s, zero)

    row = pl.BlockSpec((tr, d), lambda i: (i, 0))
    vecs = pl.BlockSpec((8, d), lambda i: (0, 0))
    return pl.pallas_call(
        body, name=name,
        out_shape=(jax.ShapeDtypeStruct((t_out, d), F32), jax.ShapeDtypeStruct((8, d), F32))
        + ((jax.ShapeDtypeStruct((t, d), BF16), jax.ShapeDtypeStruct((8, d), F32)) if n_below else ()),
        grid=(t // tr,),
        in_specs=[row, row, row, pl.BlockSpec((1, d), lambda i: (0, 0)), vecs] + ([row, vecs] if n_below else []),
        out_specs=(pl.BlockSpec((tr, d), out_row), vecs) + ((row, vecs) if n_below else ()),
        scratch_shapes=[_Partials.scratch(4, d)],
        compiler_params=_params("arbitrary"),
    )(dhx, xs, g, nw, rows, *(below or ()))


def _final_loss(xs, tgt, fw, below, nct, name):
    t, d = xs.shape
    tr = ROW_TILE

    def body(x_ref, t_ref, w_ref, p_ref, r_ref, g_ref, acc_ref, gp_ref, gacc_ref, part_ref):
        part = _Partials(part_ref, 3)
        i = pl.program_id(0)
        is_ctx = i < nct

        @pl.when(i == 0)
        def _():
            acc_ref[...] = jnp.zeros_like(acc_ref)
            gacc_ref[...] = jnp.zeros_like(gacc_ref)

        @pl.when(is_ctx)
        def _():
            g_ref[...] = jnp.zeros_like(g_ref)
            gp_ref[...] = jnp.zeros_like(gp_ref)

        @pl.when(jnp.logical_not(is_ctx))
        def _():
            part.zero()

            def chunk(r0, q):
                p16 = p_ref[pl.ds(r0, ROW_CHUNK), :].astype(F32)
                wv, gate = w_ref[...], r_ref[2:3, :]
                gps = []
                for hh in range(ROW_CHUNK // 8):
                    rows8 = pl.ds(r0 + 8 * hh, 8)
                    x = x_ref[rows8, :]
                    r = lax.rsqrt(jnp.mean(x * x, axis=-1, keepdims=True) + EPS)
                    xh = x * r
                    e = xh * wv - t_ref[rows8, :]
                    part.add(q, 1, e * e)
                    dy = e * (1.0 / d)
                    part.add(q, 0, dy * xh)
                    dxh = dy * wv
                    g = r * (dxh - xh * jnp.mean(dxh * xh, axis=-1, keepdims=True))
                    g_ref[rows8, :] = g
                    gps.append(g * gate)
                    part.add(q, 2, g * p16[8 * hh:8 * hh + 8])
                gp_ref[pl.ds(r0, ROW_CHUNK), :] = jnp.concatenate(gps, axis=0).astype(BF16)

            _for_row_chunks(tr, chunk)
            acc_ref[0:1, :] += part.total(0)
            acc_ref[1:2, :] += jnp.zeros((1, d), F32) + jnp.sum(part.total(1)) * (0.5 / d)
            gacc_ref[2:3, :] += part.total(2)

    row = pl.BlockSpec((tr, d), lambda i: (i, 0))
    vecs = pl.BlockSpec((8, d), lambda i: (0, 0))
    return pl.pallas_call(
        body, name=name,
        out_shape=(jax.ShapeDtypeStruct((t, d), F32), jax.ShapeDtypeStruct((8, d), F32),
                   jax.ShapeDtypeStruct((t, d), BF16), jax.ShapeDtypeStruct((8, d), F32)), grid=(t // tr,),
        in_specs=[row, pl.BlockSpec((tr, d), lambda i: (jnp.maximum(i - nct, 0), 0)),
                  pl.BlockSpec((1, d), lambda i: (0, 0)), row, vecs],
        out_specs=(row, vecs, row, vecs),
        scratch_shapes=[_Partials.scratch(3, d)],
        compiler_params=_params("arbitrary"),
    )(xs, tgt, fw, *below)


def _seq_edges(i, nct, nt):
    has_prev = jnp.logical_and(i != 0, i != nct)
    has_next = jnp.logical_and(i != nct - 1, i != nt - 1)
    return has_prev, has_next


def _shift_rows(a, before, after):
    tr = a.shape[0]
    ridx = lax.broadcasted_iota(jnp.int32, (tr, 1), 0)
    down = jnp.where(ridx == 0, before, pltpu.roll(a, 1, 0))
    up = jnp.where(ridx == tr - 1, after, pltpu.roll(a, tr - 1, 0))
    return down, up


def _halo_specs(tr, dc, col, t, hr):
    per = tr // hr
    prev = pl.BlockSpec((hr, dc), lambda i: (jnp.maximum(i * per - 1, 0), col))
    nxt = pl.BlockSpec((hr, dc), lambda i: (jnp.minimum((i + 1) * per, t // hr - 1), col))
    return prev, nxt


def _mix_fwd(u, o_f, o_b, cw, cnw, rnw, nct, name):
    t, w = u.shape
    dc = w // 8
    nh = dc // HEAD
    tr = ROW_TILE
    nt = t // tr
    col = lambda cidx: pl.BlockSpec((tr, dc), lambda i: (i, cidx))
    hr = 16
    acp, acn = _halo_specs(tr, dc, 2, t, hr)
    ahp, ahn = _halo_specs(tr, dc, 0, t, hr)
    f32 = lambda v: v.astype(F32)

    def body(ah, ab, ac, az, rz, acp_r, acn_r, ahp_r, ahn_r, of_r, ob_r, cw_r, cnw_r, rnw_r, y_ref, yt_ref, cv_ref):
        i = pl.program_id(0)
        has_prev, has_next = _seq_edges(i, nct, nt)
        ch = f32(ac[...]) * f32(ah[...])
        before = jnp.where(has_prev, f32(acp_r[hr - 1:hr, :]) * f32(ahp_r[hr - 1:hr, :]), 0.0)
        after = jnp.where(has_next, f32(acn_r[0:1, :]) * f32(ahn_r[0:1, :]), 0.0)
        down, up = _shift_rows(ch, before, after)
        cv = down * cw_r[0:1, :] + ch * cw_r[1:2, :] + up * cw_r[2:3, :]
        cv_ref[...] = cv
        y1 = f32(ab[...]) * cv
        yn = (y1 * lax.rsqrt(jnp.mean(y1 * y1, axis=-1, keepdims=True) + EPS)) * cnw_r[...]
        yc = _silu(f32(az[...])) * yn
        y_ref[:, 0:dc] = yc.astype(BF16)
        yt_ref[0:dc, :] = yc.T.astype(BF16)
        for h in range(nh):
            sl = slice(h * HEAD, (h + 1) * HEAD)
            o = of_r[:, sl] + ob_r[:, sl]
            dev = o - jnp.mean(o, axis=-1, keepdims=True)
            on = dev * lax.rsqrt(jnp.mean(dev * dev, axis=-1, keepdims=True) + EPS)
            yr = _silu(f32(rz[:, sl])) * (on * rnw_r[:, sl])
            y_ref[:, dc + h * HEAD:dc + (h + 1) * HEAD] = yr.astype(BF16)
            yt_ref[dc + h * HEAD:dc + (h + 1) * HEAD, :] = yr.T.astype(BF16)

    vec = pl.BlockSpec((1, dc), lambda i: (0, 0))
    return pl.pallas_call(
        body, name=name,
        out_shape=(jax.ShapeDtypeStruct((t, 2 * dc), BF16), jax.ShapeDtypeStruct((2 * dc, t), BF16),
                   jax.ShapeDtypeStruct((t, dc), F32)), grid=(nt,),
        in_specs=[col(0), col(1), col(2), col(3), col(7), acp, acn, ahp, ahn,
                  pl.BlockSpec((tr, dc), lambda i: (i, 0)), pl.BlockSpec((tr, dc), lambda i: (i, 0)),
                  pl.BlockSpec((8, dc), lambda i: (0, 0)), vec, vec],
        out_specs=(pl.BlockSpec((tr, 2 * dc), lambda i: (i, 0)), pl.BlockSpec((2 * dc, tr), lambda i: (0, i)),
                   pl.BlockSpec((tr, dc), lambda i: (i, 0))),
        compiler_params=_params("parallel"),
    )(u, u, u, u, u, u, u, u, u, o_f, o_b, cw, cnw, rnw)


def _mix_bwd(dy, u, cv, o_f, o_b, cnw, rnw, name):
    t, w = u.shape
    dc = w // 8
    nh = dc // HEAD
    tr = ROW_TILE
    col = lambda cidx: pl.BlockSpec((tr, dc), lambda i: (i, cidx))

    def body(dyc, dyr, ab, az, rz, cv_r, of_r, ob_r, cnw_r, rnw_r, dab, daz, drz, dcv, do, acc_ref, part_ref):
        part = _Partials(part_ref, 1)

        @pl.when(pl.program_id(0) == 0)
        def _():
            acc_ref[...] = jnp.zeros_like(acc_ref)

        part.zero()

        def chunk(r0, q):
            rows = pl.ds(r0, ROW_CHUNK)
            z, b, c_out = az[rows, :].astype(F32), ab[rows, :].astype(F32), cv_r[rows, :]
            nwv = cnw_r[...]
            y1 = b * c_out
            r = lax.rsqrt(jnp.mean(y1 * y1, axis=-1, keepdims=True) + EPS)
            yh = y1 * r
            dyc_v = dyc[rows, :]
            sg = jax.nn.sigmoid(z)
            daz[rows, :] = (dyc_v * (yh * nwv) * (sg * (1.0 + z * (1.0 - sg)))).astype(BF16)
            dyn = dyc_v * (z * sg)
            part.add(q, 0, _fold8(dyn * yh))
            dyh = dyn * nwv
            dy1 = r * (dyh - yh * jnp.mean(dyh * yh, axis=-1, keepdims=True))
            dab[rows, :] = (dy1 * c_out).astype(BF16)
            dcv[rows, :] = dy1 * b

        _for_row_chunks(tr, chunk)
        acc_ref[0:1, :] += part.total(0)
        for h in range(nh):
            sl = slice(h * HEAD, (h + 1) * HEAD)
            o = of_r[:, sl] + ob_r[:, sl]
            dev = o - jnp.mean(o, axis=-1, keepdims=True)
            rs = lax.rsqrt(jnp.mean(dev * dev, axis=-1, keepdims=True) + EPS)
            on = dev * rs
            zz, dyr_v, gw = rz[:, sl].astype(F32), dyr[:, sl], rnw_r[:, sl]
            sg = jax.nn.sigmoid(zz)
            drz[:, sl] = (dyr_v * (on * gw) * (sg * (1.0 + zz * (1.0 - sg)))).astype(BF16)
            dong = dyr_v * (zz * sg)
            acc_ref[1:2, sl] += jnp.sum(dong * on, axis=0, keepdims=True)
            don = dong * gw
            do[:, sl] = rs * (don - jnp.mean(don, axis=-1, keepdims=True)
                              - on * jnp.mean(don * on, axis=-1, keepdims=True))

    vec = pl.BlockSpec((1, dc), lambda i: (0, 0))
    row = pl.BlockSpec((tr, dc), lambda i: (i, 0))
    return pl.pallas_call(
        body, name=name,
        out_shape=(jax.ShapeDtypeStruct((t, dc), BF16), jax.ShapeDtypeStruct((t, dc), BF16),
                   jax.ShapeDtypeStruct((t, dc), BF16), jax.ShapeDtypeStruct((t, dc), F32),
                   jax.ShapeDtypeStruct((t, dc), F32), jax.ShapeDtypeStruct((8, dc), F32)),
        grid=(t // tr,),
        in_specs=[col(0), col(1), col(1), col(3), col(7), row, row, row, vec, vec],
        out_specs=(row, row, row, row, row, pl.BlockSpec((8, dc), lambda i: (0, 0))),
        scratch_shapes=[_Partials.scratch(1, dc)],
        compiler_params=_params("arbitrary"),
    )(dy, dy, u, u, u, cv, o_f, o_b, cnw, rnw)


def _rope_masks():
    lane = lax.broadcasted_iota(jnp.int32, (1, HEAD), 1)
    return (lane % (HEAD // 2)) >= (HEAD // 4)


def _rot(a, upper):
    return jnp.where(upper, pltpu.roll(a, HEAD // 4, 1), pltpu.roll(a, HEAD - HEAD // 4, 1))


def _assemble_du(dcv, u, cw, dab, daz, drz, dq_f, dq_b, dk_f, dk_b, dv_f, dv_b, cos, sin, nct, name):
    t, w = u.shape
    dc = w // 8
    nh = dc // HEAD
    tr = ROW_TILE
    nt = t // tr
    k_scale = HEAD ** -0.5
    col = lambda cidx: pl.BlockSpec((tr, dc), lambda i: (i, cidx))
    row = pl.BlockSpec((tr, dc), lambda i: (i, 0))
    dcp, dcn = _halo_specs(tr, dc, 0, t, 8)
    f32 = lambda v: v.astype(F32)

    def body(dcv_r, dcp_r, dcn_r, ah, ac, cw_r, dab_r, daz_r, drz_r, dqf, dqb, dkf, dkb, dvf, dvb, cos_r, sin_r,
             du, acc_ref, pad, part_ref):
        part = _Partials(part_ref, 3)
        i = pl.program_id(0)

        @pl.when(i == 0)
        def _():
            acc_ref[...] = jnp.zeros_like(acc_ref)

        has_prev, has_next = _seq_edges(i, nct, nt)
        pad[0:8, :] = jnp.where(has_prev, dcp_r[...], 0.0)
        pad[8:8 + tr, :] = dcv_r[...]
        pad[8 + tr:16 + tr, :] = jnp.where(has_next, dcn_r[...], 0.0)
        part.zero()
        upper = _rope_masks()

        def chunk(r0, q):
            rows = pl.ds(r0, ROW_CHUNK)
            d_c = pad[pl.ds(pl.multiple_of(r0 + 8, 8), ROW_CHUNK), :]
            d_prev, d_next = _shift_rows(d_c, pad[pl.ds(r0, 8), :][7:8],
                                         pad[pl.ds(pl.multiple_of(r0 + 8 + ROW_CHUNK, 8), 8), :][0:1])
            a_h, a_c = f32(ah[rows, :]), f32(ac[rows, :])
            ch = a_c * a_h
            dch = d_next * cw_r[0:1, :] + d_c * cw_r[1:2, :] + d_prev * cw_r[2:3, :]
            part.add(q, 0, _fold8(ch * d_next))
            part.add(q, 1, _fold8(ch * d_c))
            part.add(q, 2, _fold8(ch * d_prev))
            du[rows, 0:dc] = (dch * a_c).astype(BF16)
            du[rows, dc:2 * dc] = dab_r[rows, :]
            du[rows, 2 * dc:3 * dc] = (dch * a_h).astype(BF16)
            du[rows, 3 * dc:4 * dc] = daz_r[rows, :]
            du[rows, 7 * dc:8 * dc] = drz_r[rows, :]
            du[rows, 6 * dc:7 * dc] = (f32(dvf[rows, :]) + f32(dvb[rows, :])).astype(BF16)
            cs, sn = cos_r[rows, :], sin_r[rows, :]
            for h in range(nh):
                sl = slice(h * HEAD, (h + 1) * HEAD)
                dq = f32(dqf[rows, sl]) + f32(dqb[rows, sl])
                dk = (f32(dkf[rows, sl]) + f32(dkb[rows, sl])) * k_scale
                du[rows, 4 * dc + h * HEAD:4 * dc + (h + 1) * HEAD] = (dq * cs + _rot(dq * sn, upper)).astype(BF16)
                du[rows, 5 * dc + h * HEAD:5 * dc + (h + 1) * HEAD] = (dk * cs + _rot(dk * sn, upper)).astype(BF16)

        _for_row_chunks(tr, chunk)
        for tap in range(3):
            acc_ref[tap:tap + 1, :] += part.total(tap)

    tab = pl.BlockSpec((tr, HEAD), lambda i: (i, 0))
    return pl.pallas_call(
        body, name=name,
        out_shape=(jax.ShapeDtypeStruct((t, w), BF16), jax.ShapeDtypeStruct((8, dc), F32)), grid=(nt,),
        in_specs=[row, dcp, dcn, col(0), col(2), pl.BlockSpec((8, dc), lambda i: (0, 0)),
                  row, row, row, row, row, row, row, row, row, tab, tab],
        out_specs=(pl.BlockSpec((tr, w), lambda i: (i, 0)), pl.BlockSpec((8, dc), lambda i: (0, 0))),
        scratch_shapes=[pltpu.VMEM((tr + 16, dc), F32), _Partials.scratch(3, dc)],
        compiler_params=_params("arbitrary"),
    )(dcv, dcv, dcv, u, u, cw, dab, daz, drz, dq_f, dq_b, dk_f, dk_b, dv_f, dv_b, cos, sin)


def _bwd_dir_chunk(s, ncc, ns):
    return jnp.where(s < ncc, ncc - 1 - s, ns + ncc - 1 - s)


def _position_weights(d):
    c = CHUNK
    ii = lax.broadcasted_iota(jnp.int32, (c, c), 0)
    jj = lax.broadcasted_iota(jnp.int32, (c, c), 1)
    sd = ((ii - jj) if d == 0 else (jj - ii)).astype(F32)
    pos = lax.broadcasted_iota(jnp.int32, (c, HEAD), 0).astype(F32)
    qw, kw = (pos + 1.0, (c - 1.0) - pos) if d == 0 else (c - pos, pos)
    return sd, qw, kw


def _fill_decay_tables(dec_ref, nh, dm_t, qd_t, kd_t, gc_t):
    for d in range(2):
        sd, qw, kw = _position_weights(d)
        for h in range(nh):
            r = d * nh + h
            lg = -jnp.exp(jnp.zeros((1, HEAD), F32) + dec_ref[d, h])
            dm_t[r] = jnp.where(sd >= 0, jnp.exp(lg * jnp.maximum(sd, 0.0)), 0.0)
            qd_t[r] = jnp.exp(lg * qw)
            kd_t[r] = jnp.exp(lg * kw)
            gc_t[r] = jnp.zeros((8, HEAD), F32) + jnp.exp(lg * float(CHUNK))


def _decay_table_scratch(nh):
    return [pltpu.VMEM((2 * nh, CHUNK, CHUNK), F32), pltpu.VMEM((2 * nh, CHUNK, HEAD), F32),
            pltpu.VMEM((2 * nh, CHUNK, HEAD), F32), pltpu.VMEM((2 * nh, 8, HEAD), F32)]


def _ret_fwd(u, cos, sin, dec, ncc, name):
    t, w = u.shape
    dc = w // 8
    nh = dc // HEAD
    c = CHUNK
    ns = t // c
    k_scale = HEAD ** -0.5
    fmap = lambda g: (lambda s: (s, g))
    bmap = lambda g: (lambda s: (_bwd_dir_chunk(s, ncc, ns), g))
    blk = lambda m: pl.BlockSpec((c, dc), m)
    tab = lambda m: pl.BlockSpec((c, HEAD), m)
    st_f = pl.BlockSpec((nh, None, HEAD, HEAD), lambda s: (0, s, 0, 0))
    st_b = pl.BlockSpec((nh, None, HEAD, HEAD), lambda s: (0, _bwd_dir_chunk(s, ncc, ns), 0, 0))

    def body(qf, kf, vf, qb, kb, vb, cf, sf, cb, sb, dec_ref, of_ref, ob_ref, stf_ref, stb_ref, qr_ref, kr_ref,
             s_f, s_b, dm_t, qd_t, kd_t, gc_t):
        @pl.when(pl.program_id(0) == 0)
        def _():
            s_f[...] = jnp.zeros_like(s_f)
            s_b[...] = jnp.zeros_like(s_b)
            _fill_decay_tables(dec_ref, nh, dm_t, qd_t, kd_t, gc_t)

        upper = _rope_masks()
        dirs = ((qf, kf, vf, cf, sf, of_ref, stf_ref, s_f), (qb, kb, vb, cb, sb, ob_ref, stb_ref, s_b))
        group = RET_INTERLEAVE if nh % RET_INTERLEAVE == 0 else 1
        blocks = [[(h, d) for h in range(h0, h0 + group) for d in range(2)] for h0 in range(0, nh, group)]
        for blk_list in blocks:
            first = []
            for h, d in blk_list:
                q_r, k_r, v_r, cs_r, sn_r, _, st_ref, st = dirs[d]
                sl = slice(h * HEAD, (h + 1) * HEAD)
                r = d * nh + h
                cs, sn = cs_r[...], sn_r[...]
                q = q_r[:, sl].astype(F32)
                q = q * cs + _rot(q, upper) * sn
                k = k_r[:, sl].astype(F32)
                k = (k * cs + _rot(k, upper) * sn) * k_scale
                if d == 0:
                    qr_ref[:, sl] = q
                    kr_ref[:, sl] = k
                vv = v_r[:, sl]
                state = st[h]
                st_ref[h] = state
                p = _nt(_bf(q), _bf(k)) * dm_t[r]
                kv = _tn(_bf(k * kd_t[r]), vv)
                first.append((q, vv, state, p, kv))
            for idx, (h, d) in enumerate(blk_list):
                _, _, _, _, _, o_ref, _, st = dirs[d]
                sl = slice(h * HEAD, (h + 1) * HEAD)
                r = d * nh + h
                q, vv, state, p, kv = first[idx]
                o_ref[:, sl] = _nn(jnp.concatenate([_bf(p), _bf(q * qd_t[r])], axis=1),
                                   jnp.concatenate([vv, _bf(state)], axis=0))
                st[h] = gc_t[r, 0:1, :] * state + kv

    return pl.pallas_call(
        body, name=name,
        out_shape=(jax.ShapeDtypeStruct((t, dc), F32), jax.ShapeDtypeStruct((t, dc), F32),
                   jax.ShapeDtypeStruct((nh, ns, HEAD, HEAD), F32), jax.ShapeDtypeStruct((nh, ns, HEAD, HEAD), F32),
                   jax.ShapeDtypeStruct((t, dc), F32), jax.ShapeDtypeStruct((t, dc), F32)),
        grid=(ns,),
        in_specs=[blk(fmap(4)), blk(fmap(5)), blk(fmap(6)), blk(bmap(4)), blk(bmap(5)), blk(bmap(6)),
                  tab(fmap(0)), tab(fmap(0)), tab(bmap(0)), tab(bmap(0)), SMEM_SPEC],
        out_specs=(blk(fmap(0)), blk(bmap(0)), st_f, st_b, blk(fmap(0)), blk(fmap(0))),
        scratch_shapes=[pltpu.VMEM((nh, HEAD, HEAD), F32), pltpu.VMEM((nh, HEAD, HEAD), F32)]
        + _decay_table_scratch(nh),
        compiler_params=_params("arbitrary"),
    )(u, u, u, u, u, u, cos, sin, cos, sin, dec)


def _ret_bwd(u, qr, kr, do, st_f, st_b, dec, ncc, name):
    t, w = u.shape
    dc = w // 8
    nh = dc // HEAD
    c = CHUNK
    ns = t // c
    fchunk = lambda s: ns - 1 - s
    bchunk = lambda s: _bwd_dir_chunk(ns - 1 - s, ncc, ns)
    fmap = lambda g: (lambda s: (fchunk(s), g))
    bmap = lambda g: (lambda s: (bchunk(s), g))
    blk = lambda m: pl.BlockSpec((c, dc), m)
    stf_spec = pl.BlockSpec((nh, None, HEAD, HEAD), lambda s: (0, fchunk(s), 0, 0))
    stb_spec = pl.BlockSpec((nh, None, HEAD, HEAD), lambda s: (0, bchunk(s), 0, 0))

    def body(qf, kf, vf, dof, qb, kb, vb, dob, stf_ref, stb_ref, dec_ref,
             dqf, dkf, dvf, dqb, dkb, dvb, dlg_ref, ds_f, ds_b, dm_t, qd_t, kd_t, gc_t, lg_acc):
        step = pl.program_id(0)

        @pl.when(step == 0)
        def _():
            ds_f[...] = jnp.zeros_like(ds_f)
            ds_b[...] = jnp.zeros_like(ds_b)
            lg_acc[...] = jnp.zeros_like(lg_acc)
            _fill_decay_tables(dec_ref, nh, dm_t, qd_t, kd_t, gc_t)

        dirs = ((qf, kf, vf, dof, stf_ref, dqf, dkf, dvf, ds_f), (qb, kb, vb, dob, stb_ref, dqb, dkb, dvb, ds_b))
        weights = [_position_weights(d) for d in range(2)]
        group = RET_INTERLEAVE if nh % RET_INTERLEAVE == 0 else 1
        blocks = [[(h, d) for h in range(h0, h0 + group) for d in range(2)] for h0 in range(0, nh, group)]
        for blk_list in blocks:
            first = []
            for h, d in blk_list:
                q_r, k_r, v_r, do_r, st_ref, _, _, _, ds = dirs[d]
                sl = slice(h * HEAD, (h + 1) * HEAD)
                r = d * nh + h
                q, k = q_r[:, sl], k_r[:, sl]
                qb16, kb16, vb16, dob16 = _bf(q), _bf(k), v_r[:, sl], _bf(do_r[:, sl])
                state, dstate = st_ref[h], ds[h]
                dsb16 = _bf(dstate)
                a = _nt(qb16, kb16) * dm_t[r]
                both = _nt(dob16, jnp.concatenate([vb16, _bf(state)], axis=0))
                dk_i = _nt(vb16, dsb16) * kd_t[r]
                first.append((q, k, qb16, kb16, dob16, state, dstate, dsb16, a, both, dk_i))
            for idx, (h, d) in enumerate(blk_list):
                _, _, _, _, _, dq_ref, dk_ref, dv_ref, ds = dirs[d]
                sl = slice(h * HEAD, (h + 1) * HEAD)
                r = d * nh + h
                q, k, qb16, kb16, dob16, state, dstate, dsb16, a, both, dk_i = first[idx]
                sd, qw, kw = weights[d]
                da = both[:, 0:c]
                dp = _bf(da * dm_t[r])
                dq_i = both[:, c:2 * c] * qd_t[r]
                dq_ref[:, sl] = _bf(_nn(dp, kb16) + dq_i)
                dk_ref[:, sl] = _bf(_tn(dp, qb16) + dk_i)
                dv_ref[:, sl] = _bf(_nn(jnp.concatenate([_bf(a.T), _bf(k * kd_t[r])], axis=1),
                                        jnp.concatenate([dob16, dsb16], axis=0)))
                gcr = gc_t[r, 0:1, :]
                ds[h] = gcr * dstate + _tn(_bf(q * qd_t[r]), dob16)
                lg_acc[r] += (da * a * jnp.maximum(sd, 0.0) + (dq_i * q) * qw + (dk_i * k) * kw
                              + (float(c) * gcr) * (state * dstate))

        @pl.when(step == ns - 1)
        def _():
            for d in range(2):
                for h in range(nh):
                    r = d * nh + h
                    lg = -jnp.exp(jnp.zeros((1, HEAD), F32) + dec_ref[d, h])
                    dlg_ref[r:r + 1, :] = (jnp.zeros((1, HEAD), F32) + jnp.sum(lg_acc[r])) * lg

    dshape = jax.ShapeDtypeStruct((t, dc), BF16)
    return pl.pallas_call(
        body, name=name,
        out_shape=(dshape, dshape, dshape, dshape, dshape, dshape, jax.ShapeDtypeStruct((2 * nh, HEAD), F32)),
        grid=(ns,),
        in_specs=[blk(fmap(0)), blk(fmap(0)), blk(fmap(6)), blk(fmap(0)),
                  blk(bmap(0)), blk(bmap(0)), blk(bmap(6)), blk(bmap(0)), stf_spec, stb_spec, SMEM_SPEC],
        out_specs=(blk(fmap(0)), blk(fmap(0)), blk(fmap(0)), blk(bmap(0)), blk(bmap(0)), blk(bmap(0)),
                   pl.BlockSpec((2 * nh, HEAD), lambda s: (0, 0))),
        scratch_shapes=[pltpu.VMEM((nh, HEAD, HEAD), F32), pltpu.VMEM((nh, HEAD, HEAD), F32)]
        + _decay_table_scratch(nh) + [pltpu.VMEM((2 * nh, CHUNK, HEAD), F32)],
        compiler_params=_params("arbitrary"),
    )(qr, kr, u, do, qr, kr, u, do, st_f, st_b, dec)


def _mod_matvec(c16, w_mod, b_mod):
    nl, d, nm = w_mod.shape
    tn = _pick(nm, 512, 128)

    def body(c_ref, w_ref, b_ref, sc_ref, o_ref):
        sc = _silu(c_ref[...])
        sc_ref[...] = sc
        o_ref[...] = _nn(_bf(sc), _bf(w_ref[...])) + b_ref[...]

    return pl.pallas_call(
        body, name="mod_matvec",
        out_shape=(jax.ShapeDtypeStruct((16, d), F32), jax.ShapeDtypeStruct((nl, 16, nm), F32)),
        grid=(nl, nm // tn),
        in_specs=[pl.BlockSpec((16, d), lambda l, j: (0, 0)), pl.BlockSpec((None, d, tn), lambda l, j: (l, 0, j)),
                  pl.BlockSpec((None, 1, tn), lambda l, j: (l, 0, j))],
        out_specs=(pl.BlockSpec((16, d), lambda l, j: (0, 0)), pl.BlockSpec((None, 16, tn), lambda l, j: (l, 0, j))),
        compiler_params=_params("arbitrary", "arbitrary"),
    )(c16, w_mod, b_mod)


def _wmod_grad(sc16, dm, w_mod):
    nl, d, nm = w_mod.shape
    tn = _pick(nm, 512, 128)

    def body(sc_ref, dm_ref, w_ref, gw_ref, pc_ref):
        @pl.when(jnp.logical_and(pl.program_id(0) == 0, pl.program_id(1) == 0))
        def _():
            pc_ref[...] = jnp.zeros_like(pc_ref)

        dmb = _bf(dm_ref[...])
        gw_ref[...] = _tn(_bf(sc_ref[...]), dmb)
        pc_ref[...] += _nt(dmb, _bf(w_ref[...]))

    return pl.pallas_call(
        body, name="wmod_grad",
        out_shape=(jax.ShapeDtypeStruct((nl, d, nm), F32), jax.ShapeDtypeStruct((16, d), F32)),
        grid=(nl, nm // tn),
        in_specs=[pl.BlockSpec((16, d), lambda l, j: (0, 0)), pl.BlockSpec((None, 16, tn), lambda l, j: (l, 0, j)),
                  pl.BlockSpec((None, d, tn), lambda l, j: (l, 0, j))],
        out_specs=(pl.BlockSpec((None, d, tn), lambda l, j: (l, 0, j)), pl.BlockSpec((16, d), lambda l, j: (0, 0))),
        compiler_params=_params("arbitrary", "arbitrary"),
    )(sc16, dm, w_mod)


def _rowsum(a, name):
    r, n = a.shape
    tn = _pick(n, 4096, 128)

    def body(a_ref, o_ref):
        acc = a_ref[0:1, :]
        for i in range(1, r):
            acc = acc + a_ref[i:i + 1, :]
        o_ref[...] = jnp.zeros((8, tn), F32) + acc

    return pl.pallas_call(
        body, name=name, out_shape=jax.ShapeDtypeStruct((8, n), F32), grid=(n // tn,),
        in_specs=[pl.BlockSpec((r, tn), lambda j: (0, j))], out_specs=pl.BlockSpec((8, tn), lambda j: (0, j)),
        compiler_params=_params("parallel"),
    )(a)


def _cctx_grad(parts, c_ctx8):
    _, _, d = parts.shape

    def body(p_ref, c_ref, o_ref):
        acc = ((p_ref[0] + p_ref[1]) + p_ref[2]) + p_ref[3]
        o_ref[...] = acc * _dsilu(c_ref[...])

    return pl.pallas_call(
        body, name="cctx_grad", out_shape=jax.ShapeDtypeStruct((8, d), F32),
        in_specs=[VMEM_SPEC, VMEM_SPEC], out_specs=VMEM_SPEC,
    )(parts, c_ctx8)


def _adamw(g, w, m, v, name):
    r, n = w.shape
    tr = _pick(r, max(8, min(256, (512 * 1024) // n // 8 * 8)), 8)
    bc1 = 1.0 - ADAM_B1 ** ADAM_STEP
    bc2 = 1.0 - ADAM_B2 ** ADAM_STEP

    def body(g_ref, w_ref, m_ref, v_ref, d_ref, mo_ref, vo_ref):
        gg = g_ref[...]
        mn = ADAM_B1 * m_ref[...] + (1.0 - ADAM_B1) * gg
        vn = ADAM_B2 * v_ref[...] + (1.0 - ADAM_B2) * (gg * gg)
        mo_ref[...] = mn
        vo_ref[...] = vn
        d_ref[...] = -ADAM_LR * ((mn / bc1) / (jnp.sqrt(vn / bc2) + ADAM_EPS) + ADAM_WD * w_ref[...])

    spec = pl.BlockSpec((tr, n), lambda i: (i, 0))
    shp = jax.ShapeDtypeStruct((r, n), F32)
    return pl.pallas_call(
        body, name=name, out_shape=(shp, shp, shp), grid=(r // tr,),
        in_specs=[spec, spec, spec, spec], out_specs=(spec, spec, spec),
        compiler_params=_params("parallel"),
    )(g, w, m, v)


def _adamw_layer(g, w, m, v, layer, prev, name):
    r, n = g.shape
    tr = _pick(r, max(8, min(256, (512 * 1024) // n // 8 * 8)), 8)
    nb = r // tr
    bc1 = 1.0 - ADAM_B1 ** ADAM_STEP
    bc2 = 1.0 - ADAM_B2 ** ADAM_STEP
    n_prev = 0 if prev is None else 4

    def body(g_ref, w_ref, m_ref, v_ref, *rest):
        go_ref, d_ref, mo_ref, vo_ref = rest[n_prev:]
        gg = g_ref[...]
        mn = ADAM_B1 * m_ref[...] + (1.0 - ADAM_B1) * gg
        vn = ADAM_B2 * v_ref[...] + (1.0 - ADAM_B2) * (gg * gg)
        go_ref[...] = gg
        mo_ref[...] = mn
        vo_ref[...] = vn
        d_ref[...] = -ADAM_LR * ((mn / bc1) / (jnp.sqrt(vn / bc2) + ADAM_EPS) + ADAM_WD * w_ref[...])

    spec = pl.BlockSpec((tr, n), lambda i: (layer * nb + i, 0))
    shp = jax.ShapeDtypeStruct(w.shape, F32)
    return pl.pallas_call(
        body, name=name, out_shape=(shp, shp, shp, shp), grid=(nb,),
        in_specs=[pl.BlockSpec((tr, n), lambda i: (i, 0)), spec, spec, spec] + [ANY] * n_prev,
        out_specs=(spec, spec, spec, spec),
        input_output_aliases={4 + k: k for k in range(n_prev)},
        compiler_params=_params("parallel"),
    )(g, w, m, v, *(prev or ()))


def _rope_tables(seq, n_ctx):
    f = HEAD // 4
    inv = ROPE_BASE ** (-jnp.arange(f, dtype=F32) / f)
    n_rows = seq // GRID_W
    ang_r = jnp.arange(n_rows, dtype=F32)[:, None] * inv[None, :]
    ang_c = jnp.arange(GRID_W, dtype=F32)[:, None] * inv[None, :]
    by_row = lambda a: jnp.repeat(a, GRID_W, axis=0)
    by_col = lambda a: jnp.tile(a, (n_rows, 1))
    cos_r, sin_r, cos_c, sin_c = by_row(jnp.cos(ang_r)), by_row(jnp.sin(ang_r)), by_col(jnp.cos(ang_c)), by_col(jnp.sin(ang_c))
    cos = jnp.concatenate([cos_r, cos_r, cos_c, cos_c], axis=-1)
    sin = jnp.concatenate([-sin_r, sin_r, -sin_c, sin_c], axis=-1)
    cos = jnp.concatenate([jnp.ones((n_ctx, HEAD), F32), cos], axis=0)
    sin = jnp.concatenate([jnp.zeros((n_ctx, HEAD), F32), sin], axis=0)
    return cos, sin


def _pad_rows(a, rows):
    return jnp.concatenate([a, jnp.zeros((rows - a.shape[0],) + a.shape[1:], a.dtype)], axis=0)


def _pad_flat(parts, mult):
    flat = jnp.concatenate([p.reshape(-1) for p in parts])
    pad = (-flat.shape[0]) % mult
    return jnp.concatenate([flat, jnp.zeros((pad,), flat.dtype)])


def kernel(x, c, ctx, c_ctx, norm_w, w_mod, b_mod, w_in, conv_w, conv_norm_w, ret_norm_w, ret_decay_f, ret_decay_b, w_out, final_norm_w, loss_target, m_c_ctx, m_norm_w, m_w_mod, m_b_mod, m_w_in, m_conv_w, m_conv_norm_w, m_ret_norm_w, m_ret_decay_f, m_ret_decay_b, m_w_out, m_final_norm_w, v_c_ctx, v_norm_w, v_w_mod, v_b_mod, v_w_in, v_conv_w, v_conv_norm_w, v_ret_norm_w, v_ret_decay_f, v_ret_decay_b, v_w_out, v_final_norm_w):
    xi, yi, ci = _mesh_pos()
    b_idx = 4 * xi + 2 * yi + ci
    s_idx = 2 * xi + yi
    pos = jnp.stack([ci, s_idx]).astype(jnp.int32)

    x2, ctx2, tgt = x[0], ctx[0], loss_target[0]
    seq, d = x2.shape
    n_ctx = ctx2.shape[0]
    t = seq + n_ctx
    dc = d // 2
    nh = dc // HEAD
    nl, _, nm = w_mod.shape
    n4 = w_in.shape[2]
    r4 = w_out.shape[1]
    cw4 = conv_w.shape[2]
    nct = n_ctx // ROW_TILE
    ncc = n_ctx // CHUNK
    assert n_ctx % ROW_TILE == 0 and seq % ROW_TILE == 0 and dc % HEAD == 0 and nl * 3 <= 8

    win_b, wout_b = w_in.astype(BF16), w_out.astype(BF16)
    wg_in, wg_out = [None] * nl, [None] * nl
    shard_order = jnp.stack([s_idx, 2 * (1 - xi) + yi, 2 * xi + (1 - yi), 2 * (1 - xi) + (1 - yi)]).astype(jnp.int32)

    small = jnp.concatenate([jnp.broadcast_to(c, (8, d)), _pad_rows(conv_w.reshape(nl * 3, cw4), 8)], axis=1)
    small_g = _allgather_small(small, "gather_c_convw").reshape(N_DEV, 8, d + cw4)
    c_all = small_g[:, 0, :d]
    cw_full = small_g[0::2, :, d:].transpose(1, 0, 2).reshape(8, 4 * cw4)
    cw_l = [_pad_rows(cw_full[l * 3:(l + 1) * 3], 8) for l in range(nl)]

    c16 = _pad_rows(jnp.concatenate([c_all, c_ctx[None]], axis=0), 16)
    b_shard = lax.dynamic_slice(b_mod, (0, s_idx * nm), (nl, nm)).reshape(nl, 1, nm)
    sc16, modp = _mod_matvec(c16, w_mod, b_shard)
    modg = _allgather_small(modp.reshape(nl * 16, nm), "gather_mod").reshape(N_DEV, nl, 16, nm)
    mod_full = modg[0::2].transpose(1, 2, 0, 3).reshape(nl, 16, 4 * nm)
    mod_me = lax.dynamic_index_in_dim(mod_full, b_idx, axis=1, keepdims=False)
    mod_cx = mod_full[:, 8]
    zrow = jnp.zeros((d,), F32)
    rows = []
    for l in range(nl):
        gate_c = mod_cx[l, 2 * d:] if l < nl - 1 else zrow
        rows.append(jnp.stack([mod_me[l, :d], mod_me[l, d:2 * d], mod_me[l, 2 * d:],
                               mod_cx[l, :d], mod_cx[l, d:2 * d], gate_c, zrow, zrow]))

    cos, sin = _rope_tables(seq, n_ctx)
    xs = jnp.concatenate([ctx2, x2], axis=0)
    saved = []
    for l in range(nl):
        dec = jnp.stack([ret_decay_f[l], ret_decay_b[l]])
        hx, hx_t = _modulate(xs, norm_w[l][None], rows[l], nct, f"modulate_{l}")
        u, wg_in[l], wg_out[l] = _in_proj_gathered(hx, win_b[l], shard_order, _gather_comm(wout_b[l], False),
                                                   f"in_proj_{l}")
        o_f, o_b, st_f, st_b, qr, kr = _ret_fwd(u, cos, sin, dec, ncc, f"ret_fwd_{l}")
        y, y_t, cv = _mix_fwd(u, o_f, o_b, cw_l[l], conv_norm_w[l][None], ret_norm_w[l][None], nct, f"mix_fwd_{l}")
        xs_new, proj = _out_proj(y, wg_out[l], xs, rows[l], n_ctx, f"out_proj_{l}")
        saved.append((xs, hx_t, u, o_f, o_b, st_f, st_b, qr, kr, y_t, cv, proj, dec))
        xs = xs_new

    g, facc, gp, gacc = _final_loss(xs, tgt, final_norm_w[None], (saved[nl - 1][11], rows[nl - 1]), nct, "final_loss")
    full = [None] * nl
    join = []
    mine, other = ci.reshape(1).astype(jnp.int32), (1 - ci).reshape(1).astype(jnp.int32)
    dmx, dmc, gnw, gcnw, grnw, gcw, gdf, gdb = ([None] * nl for _ in range(8))
    for l in reversed(range(nl)):
        xs_l, hx_t, u, o_f, o_b, st_f, st_b, qr, kr, y_t, cv, proj, dec = saved[l]
        gout = _matmul(y_t, gp, "nn", f"out_proj_dw_{l}", tm_t=1024, tn_t=1024, tk_t=2816, comms=join)
        if join:
            gout, *full[l + 1] = gout
        dy = _matmul(gp, wg_out[l], "nt", f"out_proj_dx_{l}")
        dab, daz, drz, dcv, do, macc = _mix_bwd(dy, u, cv, o_f, o_b, conv_norm_w[l][None], ret_norm_w[l][None],
                                                f"mix_bwd_{l}")
        dqf, dkf, dvf, dqb, dkb, dvb, dlg = _ret_bwd(u, qr, kr, do, st_f, st_b, dec, ncc, f"ret_bwd_{l}")
        du, cacc = _assemble_du(dcv, u, cw_l[l], dab, daz, drz, dqf, dqb, dkf, dkb, dvf, dvb, cos, sin, nct,
                                f"assemble_du_{l}")
        gout_v = gout.reshape(4, 2, r4 // 2, d)
        gin_other = _matmul(hx_t, du, "nn", f"in_proj_dw_other_{l}", tm_t=1024, tn_t=1024, tk_t=2816, row_half=other)
        gin_mine, land_in, land_out = _matmul(hx_t, du, "nn", f"in_proj_dw_mine_{l}", tm_t=1024, tn_t=1024,
                                              tk_t=2816, row_half=mine, comms=[_swap_comm(gin_other, gout_v)])
        p_in = _sum_half_in(gin_mine, land_in, f"grad_sum_half_in_{l}")
        p_out = _sum_half_out(pos, gout_v, land_out, f"grad_sum_half_out_{l}")
        dhx, l3_in, l3_out = _matmul(du, wg_in[l], "nt", f"in_proj_dx_{l}", comms=[_exchange_comm(p_in, p_out, n4)],
                                     out_dtype=BF16)
        join = [_join_comm(_sum_quarters_in(pos, p_in, l3_in, n4, f"grad_sum_quarters_in_{l}"),
                           _sum_quarters_out(pos, p_out, l3_out, f"grad_sum_quarters_out_{l}"))]
        dmx_gate, dmc_gate = gacc[2], gacc[5]
        below = (saved[l - 1][11], rows[l - 1]) if l > 0 else None
        g, nacc, *nxt = _modulate_bwd(dhx, xs_l, g, norm_w[l][None], rows[l], nct, f"modulate_bwd_{l}", below)
        if nxt:
            gp, gacc = nxt
        dmx[l] = jnp.concatenate([nacc[0], nacc[1], dmx_gate])
        dmc[l] = jnp.concatenate([nacc[3], nacc[4], dmc_gate])
        gnw[l], gcnw[l], grnw[l], gcw[l] = nacc[6], macc[0], macc[1], cacc[0:3]
        gdf[l], gdb[l] = dlg[0:nh, 0], dlg[nh:2 * nh, 0]
    grad_x = g[None]
    full[0] = _run_comm(join[0], "grad_join_0")

    res_in, res_out = None, None
    for l in reversed(range(nl)):
        res_in = _adamw_layer(full[l][0].reshape(d, n4), w_in.reshape(nl * d, n4), m_w_in.reshape(nl * d, n4),
                              v_w_in.reshape(nl * d, n4), l, res_in, f"adamw_w_in_{l}")
        res_out = _adamw_layer(full[l][1].reshape(r4, d), w_out.reshape(nl * r4, d), m_w_out.reshape(nl * r4, d),
                               v_w_out.reshape(nl * r4, d), l, res_out, f"adamw_w_out_{l}")
    g_w_in, d_w_in, nm_w_in, nv_w_in = res_in
    g_w_out, d_w_out, nm_w_out, nv_w_out = res_out

    seg = [jnp.stack(dmx), jnp.stack(dmc), jnp.stack(gnw), jnp.stack(gcnw), jnp.stack(grnw), facc[0],
           jnp.stack(gcw), jnp.stack(gdf), jnp.stack(gdb), facc[1, 0:1]]
    sizes = [int(s.size) for s in seg]
    offs = [sum(sizes[:i]) for i in range(len(sizes))]
    packed = _pad_flat(seg, 8 * 128)
    pw = packed.shape[0] // 8
    allv = _allgather_small(packed.reshape(8, pw), "gather_small_grads").reshape(N_DEV, 8 * pw)
    n_mod = nl * 3 * d
    dmx_all, dmc_all = allv[:, offs[0]:offs[0] + n_mod], allv[:, offs[1]:offs[1] + n_mod]
    rest_all = allv[:, offs[2]:]
    zeros_mod = jnp.zeros_like(dmc_all)
    stack = jnp.concatenate([
        jnp.concatenate([dmx_all, dmc_all], axis=0),
        jnp.concatenate([dmc_all, zeros_mod], axis=0),
        jnp.concatenate([rest_all, jnp.zeros_like(rest_all)], axis=0)], axis=1)
    sums = _rowsum(stack, "sum_small_grads")[0]
    g_b_mod = sums[:n_mod].reshape(nl, 3 * d)
    dmc_tot = sums[n_mod:2 * n_mod].reshape(nl, 3 * d)
    rest = sums[2 * n_mod:]

    def take(i):
        o = offs[i] - offs[2]
        return rest[o:o + sizes[i]]

    g_norm_w = take(2).reshape(nl, d)
    g_conv_norm_w = take(3).reshape(nl, dc)
    g_ret_norm_w = take(4).reshape(nl, dc)
    g_final_norm_w = take(5)
    g_conv_w = lax.dynamic_slice(take(6).reshape(nl, 3, dc), (0, 0, s_idx * cw4), (nl, 3, cw4))
    g_decay_f = take(7).reshape(nl, nh)
    g_decay_b = take(8).reshape(nl, nh)
    loss = take(9)[0]

    dmx_mine = lax.dynamic_slice(dmx_all.reshape(N_DEV, nl, 3 * d), (0, 0, s_idx * nm), (N_DEV, nl, nm))
    dmc_mine = lax.dynamic_slice(dmc_tot, (0, s_idx * nm), (nl, nm))
    dm = jnp.concatenate([dmx_mine.transpose(1, 0, 2), dmc_mine[:, None, :], jnp.zeros((nl, 7, nm), F32)], axis=1)
    g_w_mod, pc = _wmod_grad(sc16, dm, w_mod)
    pc_g = _allgather_small(pc[8:16], "gather_cctx_partials").reshape(N_DEV, 8, d)
    g_c_ctx = _cctx_grad(pc_g[0::2], jnp.broadcast_to(c_ctx[None], (8, d)))[0]
    d_w_mod, nm_w_mod, nv_w_mod = _adamw(g_w_mod.reshape(nl * d, nm), w_mod.reshape(nl * d, nm),
                                         m_w_mod.reshape(nl * d, nm), v_w_mod.reshape(nl * d, nm), "adamw_w_mod")

    small_g = [g_c_ctx, g_norm_w, g_b_mod, g_conv_w, g_conv_norm_w, g_ret_norm_w, g_decay_f, g_decay_b, g_final_norm_w]
    small_w = [c_ctx, norm_w, b_mod, conv_w, conv_norm_w, ret_norm_w, ret_decay_f, ret_decay_b, final_norm_w]
    small_m = [m_c_ctx, m_norm_w, m_b_mod, m_conv_w, m_conv_norm_w, m_ret_norm_w, m_ret_decay_f, m_ret_decay_b,
               m_final_norm_w]
    small_v = [v_c_ctx, v_norm_w, v_b_mod, v_conv_w, v_conv_norm_w, v_ret_norm_w, v_ret_decay_f, v_ret_decay_b,
               v_final_norm_w]
    pk = lambda parts: _pad_flat(parts, 8 * 128).reshape(-1, 128)
    sd, sm, sv = _adamw(pk(small_g), pk(small_w), pk(small_m), pk(small_v), "adamw_small")

    def unpack(flat2d):
        flat = flat2d.reshape(-1)
        out, o = [], 0
        for wgt in small_w:
            out.append(flat[o:o + wgt.size].reshape(wgt.shape))
            o += wgt.size
        return out

    sd, sm, sv = unpack(sd), unpack(sm), unpack(sv)

    def order(small, mod, w_in_, w_out_):
        return [small[0], small[1], mod, small[2], w_in_, small[3], small[4], small[5], small[6], small[7], w_out_,
                small[8]]

    grads = order(small_g, g_w_mod, g_w_in.reshape(nl, d, n4), g_w_out.reshape(nl, r4, d))
    deltas = order(sd, d_w_mod.reshape(nl, d, nm), d_w_in.reshape(nl, d, n4), d_w_out.reshape(nl, r4, d))
    new_m = order(sm, nm_w_mod.reshape(nl, d, nm), nm_w_in.reshape(nl, d, n4), nm_w_out.reshape(nl, r4, d))
    new_v = order(sv, nv_w_mod.reshape(nl, d, nm), nv_w_in.reshape(nl, d, n4), nv_w_out.reshape(nl, r4, d))
    return (loss, grad_x, *grads, *deltas, *new_m, *new_v)
```

```python
import functools

import jax
import jax.numpy as jnp
from jax import lax
from jax.experimental import pallas as pl
from jax.experimental.pallas import tpu as pltpu

F32 = jnp.float32
BF16 = jnp.bfloat16
EPS = 1e-6
CHUNK = 128
HEAD = 128
GRID_W = 64
ROPE_BASE = 10000.0
ROW_TILE = 256
ROW_CHUNK = 16
CHUNK_UNROLL = 4
RET_INTERLEAVE = 4
VMEM_LIMIT_BYTES = 56 * 1024 * 1024
N_DEV = 8
MESH = pl.DeviceIdType.MESH

ADAM_LR = 0.001
ADAM_B1 = 0.9
ADAM_B2 = 0.999
ADAM_EPS = 1e-08
ADAM_WD = 0.01
ADAM_STEP = 10

ANY = pl.BlockSpec(memory_space=pl.ANY)
VMEM_SPEC = pl.BlockSpec(memory_space=pltpu.VMEM)
SMEM_SPEC = pl.BlockSpec(memory_space=pltpu.SMEM)


def _params(*sem):
    return pltpu.CompilerParams(dimension_semantics=sem, vmem_limit_bytes=VMEM_LIMIT_BYTES)


def _pick(n, target, mult):
    best = None
    for d in range(mult, min(n, target) + 1, mult):
        if n % d == 0:
            best = d
    assert best is not None, (n, target, mult)
    return best


def _for_row_chunks(n_rows, fn):
    def step(r, carry):
        for q in range(CHUNK_UNROLL):
            fn(pl.multiple_of((r * CHUNK_UNROLL + q) * ROW_CHUNK, ROW_CHUNK), q)
        return carry

    lax.fori_loop(0, n_rows // (ROW_CHUNK * CHUNK_UNROLL), step, 0)


class _Partials:
    def __init__(self, ref, n_sums):
        self.ref, self.n_sums = ref, n_sums

    def zero(self):
        self.ref[...] = jnp.zeros_like(self.ref)

    def add(self, q, k, value, cols=slice(None)):
        r = (q * self.n_sums + k) * 8
        self.ref[r:r + 8, cols] += value

    def total(self, k):
        rows = [(q * self.n_sums + k) * 8 for q in range(CHUNK_UNROLL)]
        acc = self.ref[rows[0]:rows[0] + 8, :]
        for r in rows[1:]:
            acc = acc + self.ref[r:r + 8, :]
        return jnp.sum(acc, axis=0, keepdims=True)

    @staticmethod
    def scratch(n_sums, n):
        return pltpu.VMEM((CHUNK_UNROLL * n_sums * 8, n), F32)


def _fold8(a):
    out = a[0:8]
    for hh in range(1, ROW_CHUNK // 8):
        out = out + a[8 * hh:8 * hh + 8]
    return out


def _silu(z):
    return z * jax.nn.sigmoid(z)


def _dsilu(z):
    s = jax.nn.sigmoid(z)
    return s * (1.0 + z * (1.0 - s))


def _nn(a, b):
    return lax.dot_general(a, b, (((1,), (0,)), ((), ())), preferred_element_type=F32)


def _nt(a, b):
    return lax.dot_general(a, b, (((1,), (1,)), ((), ())), preferred_element_type=F32)


def _tn(a, b):
    return lax.dot_general(a, b, (((0,), (0,)), ((), ())), preferred_element_type=F32)


def _bf(a):
    return a.astype(BF16)


def _mesh_pos():
    return lax.axis_index("x"), lax.axis_index("y"), lax.axis_index("c")


def _allgather_small(blk, name):
    m_per, n = blk.shape

    def body(x_ref, out_ref, send_sems, recv_sems, local_sem):
        x, y, c = _mesh_pos()
        me, sibling = (x, y, c), (x, y, 1 - c)
        chips = [(1 - x, y), (x, 1 - y), (1 - x, 1 - y)]

        def rows(px, py, pc):
            return out_ref.at[pl.ds((4 * px + 2 * py + pc) * m_per, m_per), :]

        def copy(k, block, to, src=None):
            return pltpu.make_async_remote_copy(
                src_ref=rows(*block) if src is None else src, dst_ref=rows(*block),
                send_sem=send_sems.at[k], recv_sem=recv_sems.at[k], device_id=to, device_id_type=MESH)

        mine = pltpu.make_async_copy(x_ref, rows(*me), local_sem)
        mine.start()
        first = [copy(0, me, sibling, src=x_ref)]
        first += [copy(1 + j, me, (*chip, c), src=x_ref) for j, chip in enumerate(chips)]
        for cp in first:
            cp.start()
        passed = [copy(4 + j, (*chip, c), sibling) for j, chip in enumerate(chips)]
        for j, chip in enumerate(chips):
            copy(1 + j, (*chip, c), me).wait_recv()
            passed[j].start()
        copy(0, sibling, me).wait_recv()
        for j, chip in enumerate(chips):
            copy(4 + j, (*chip, 1 - c), me).wait_recv()
        for cp in first + passed:
            cp.wait_send()
        mine.wait()

    return pl.pallas_call(
        body, name=name,
        out_shape=jax.ShapeDtypeStruct((N_DEV * m_per, n), blk.dtype),
        in_specs=[VMEM_SPEC], out_specs=VMEM_SPEC,
        scratch_shapes=[pltpu.SemaphoreType.DMA((7,)), pltpu.SemaphoreType.DMA((7,)), pltpu.SemaphoreType.DMA],
    )(blk)


class _Comm:
    def __init__(self, ins, outs, n_remote, n_local, start, finish, aliases=None):
        self.ins, self.outs, self.n_remote, self.n_local = list(ins), list(outs), n_remote, n_local
        self.start, self.finish, self.aliases = start, finish, dict(aliases or {})

    def scratch(self):
        return [pltpu.SemaphoreType.DMA((self.n_remote,)), pltpu.SemaphoreType.DMA((self.n_remote,)),
                pltpu.SemaphoreType.DMA((max(self.n_local, 1),))]


def _run_comm(comm, name):
    n_in, n_out = len(comm.ins), len(comm.outs)

    def body(*refs):
        args = (refs[:n_in], refs[n_in:n_in + n_out]) + tuple(refs[n_in + n_out:])
        comm.start(*args)
        comm.finish(*args)

    return pl.pallas_call(
        body, name=name, out_shape=tuple(comm.outs), in_specs=[ANY] * n_in, out_specs=tuple([ANY] * n_out),
        scratch_shapes=comm.scratch(), input_output_aliases=comm.aliases,
    )(*comm.ins)


def _remote(src, dst, send, recv, k, to):
    return pltpu.make_async_remote_copy(src_ref=src, dst_ref=dst, send_sem=send.at[k], recv_sem=recv.at[k],
                                        device_id=to, device_id_type=MESH)


def _gather_comm(shard, by_columns):
    rows, cols = shard.shape
    hr = rows // 2

    def build(ins, outs, send, recv, loc):
        x, y, c = _mesh_pos()
        sib = (x, y, 1 - c)
        peers = [(1 - x, y), (x, 1 - y), (1 - x, 1 - y)]
        s_me = 2 * x + y

        def slot(s, h):
            if by_columns:
                return outs[0].at[pl.ds(pl.multiple_of(h * hr, 16), hr), pl.ds(pl.multiple_of(s * cols, 128), cols)]
            return outs[0].at[pl.ds(pl.multiple_of(s * rows + h * hr, 16), hr), :]

        def half(h):
            return ins[0].at[pl.ds(pl.multiple_of(h * hr, 16), hr), :]

        def local():
            if by_columns:
                mine = outs[0].at[:, pl.ds(pl.multiple_of(s_me * cols, 128), cols)]
            else:
                mine = outs[0].at[pl.ds(pl.multiple_of(s_me * rows, 16), rows), :]
            return [pltpu.make_async_copy(ins[0], mine, loc.at[0])]

        def per_copy(make):
            return [make(j, 2 * px + py, (px, py, c)) for j, (px, py) in enumerate(peers)]

        ici_send = lambda: per_copy(lambda j, s_p, to: _remote(half(c), slot(s_me, c), send, recv, j, to))
        ici_recv = lambda: per_copy(lambda j, s_p, to: _remote(half(c), slot(s_p, c), send, recv, j, to))
        d2d_send = lambda: per_copy(lambda j, s_p, to: _remote(slot(s_p, c), slot(s_p, c), send, recv, 3 + j, sib))
        d2d_recv = lambda: per_copy(lambda j, s_p, to: _remote(slot(s_p, c), slot(s_p, 1 - c), send, recv, 3 + j, sib))
        return local, ici_send, ici_recv, d2d_send, d2d_recv

    def start(*refs):
        local, ici_send, _, _, _ = build(*refs)
        for cp in local() + ici_send():
            cp.start()

    def finish(*refs):
        local, ici_send, ici_recv, d2d_send, d2d_recv = build(*refs)
        forwards = d2d_send()
        for arrived, forward in zip(ici_recv(), forwards):
            arrived.wait_recv()
            forward.start()
        for cp in d2d_recv():
            cp.wait_recv()
        for cp in ici_send() + forwards:
            cp.wait_send()
        for cp in local():
            cp.wait()

    full = (rows, 4 * cols) if by_columns else (4 * rows, cols)
    return _Comm([shard], [jax.ShapeDtypeStruct(full, BF16)], 6, 1, start, finish)


def _swap_comm(gin_other, gout_v):
    dh, w = gin_other.shape
    ns, _, rh, d = gout_v.shape

    def build(ins, outs, send, recv, loc):
        x, y, c = _mesh_pos()
        sib = (x, y, 1 - c)
        return [_remote(ins[0], outs[0], send, recv, 0, sib),
                _remote(ins[1].at[:, 1 - c], outs[1], send, recv, 1, sib)]

    def start(*refs):
        for cp in build(*refs):
            cp.start()

    def finish(*refs):
        for cp in build(*refs):
            cp.wait()

    outs = [jax.ShapeDtypeStruct((dh, w), F32), jax.ShapeDtypeStruct((ns, rh, d), F32)]
    return _Comm([gin_other, gout_v], outs, 2, 0, start, finish)


def _exchange_comm(p_in, p_out, n4):
    dh, _ = p_in.shape
    _, rh, d = p_out.shape

    def build(ins, outs, send, recv, loc):
        x, y, c = _mesh_pos()
        cps = []
        for j, (px, py) in enumerate([(1 - x, y), (x, 1 - y), (1 - x, 1 - y)]):
            s_p = 2 * px + py
            cps.append(_remote(ins[0].at[:, pl.ds(pl.multiple_of(s_p * n4, 128), n4)], outs[0].at[j], send, recv,
                               2 * j, (px, py, c)))
            cps.append(_remote(ins[1].at[s_p], outs[1].at[j], send, recv, 2 * j + 1, (px, py, c)))
        return cps

    def start(*refs):
        for cp in build(*refs):
            cp.start()

    def finish(*refs):
        cps = build(*refs)
        for cp in cps:
            cp.wait_recv()
        for cp in cps:
            cp.wait_send()

    outs = [jax.ShapeDtypeStruct((3, dh, n4), p_in.dtype), jax.ShapeDtypeStruct((3, rh, d), p_out.dtype)]
    return _Comm([p_in, p_out], outs, 6, 0, start, finish)


def _join_comm(q_in, q_out):
    def build(ins, outs, send, recv, loc):
        x, y, c = _mesh_pos()
        sib = (x, y, 1 - c)
        sends = lambda: [_remote(outs[a].at[c], outs[a].at[c], send, recv, a, sib) for a in range(2)]
        recvs = lambda: [_remote(outs[a].at[c], outs[a].at[1 - c], send, recv, a, sib) for a in range(2)]
        return sends, recvs

    def start(*refs):
        for cp in build(*refs)[0]():
            cp.start()

    def finish(*refs):
        sends, recvs = build(*refs)
        for cp in recvs():
            cp.wait_recv()
        for cp in sends():
            cp.wait_send()

    outs = [jax.ShapeDtypeStruct(q_in.shape, F32), jax.ShapeDtypeStruct(q_out.shape, F32)]
    return _Comm([q_in, q_out], outs, 2, 0, start, finish, aliases={0: 0, 1: 1})


def _sum_half_in(mine, land, name):
    dh, w = mine.shape
    tr, tc = _pick(dh, 256, 16), _pick(w, 2048, 128)

    def body(a_ref, b_ref, o_ref):
        o_ref[...] = (a_ref[...] + b_ref[...]).astype(BF16)

    spec = pl.BlockSpec((tr, tc), lambda i, j: (i, j))
    return pl.pallas_call(
        body, name=name, out_shape=jax.ShapeDtypeStruct((dh, w), BF16), grid=(dh // tr, w // tc),
        in_specs=[spec, spec], out_specs=spec, compiler_params=_params("parallel", "parallel"),
    )(mine, land)


def _sum_half_out(pos, gout_v, land, name):
    ns, _, rh, d = gout_v.shape

    def body(pos_ref, a_ref, b_ref, o_ref):
        o_ref[...] = (a_ref[...] + b_ref[...]).astype(BF16)

    return pl.pallas_call(
        body, name=name,
        out_shape=jax.ShapeDtypeStruct((ns, rh, d), BF16),
        grid_spec=pltpu.PrefetchScalarGridSpec(
            num_scalar_prefetch=1, grid=(ns,),
            in_specs=[pl.BlockSpec((None, None, rh, d), lambda s, p: (s, p[0], 0, 0)),
                      pl.BlockSpec((None, rh, d), lambda s, p: (s, 0, 0))],
            out_specs=pl.BlockSpec((None, rh, d), lambda s, p: (s, 0, 0))),
        compiler_params=_params("parallel"),
    )(pos, gout_v, land)


def _sum_quarters_in(pos, p_in, land3, n4, name):
    dh, _ = p_in.shape
    tr = _pick(dh, 128, 16)

    def body(pos_ref, a_ref, b_ref, o_ref):
        f = lambda v: v.astype(F32)
        o_ref[...] = ((f(a_ref[...]) + f(b_ref[0])) + f(b_ref[1])) + f(b_ref[2])

    return pl.pallas_call(
        body, name=name,
        out_shape=jax.ShapeDtypeStruct((2, dh, n4), F32),
        grid_spec=pltpu.PrefetchScalarGridSpec(
            num_scalar_prefetch=1, grid=(dh // tr,),
            in_specs=[pl.BlockSpec((tr, n4), lambda i, p: (i, p[1])),
                      pl.BlockSpec((3, tr, n4), lambda i, p: (0, i, 0))],
            out_specs=pl.BlockSpec((None, tr, n4), lambda i, p: (p[0], i, 0))),
        compiler_params=_params("parallel"),
    )(pos, p_in, land3)


def _sum_quarters_out(pos, p_out, land3, name):
    _, rh, d = p_out.shape

    def body(pos_ref, a_ref, b_ref, o_ref):
        f = lambda v: v.astype(F32)
        o_ref[...] = ((f(a_ref[...]) + f(b_ref[0])) + f(b_ref[1])) + f(b_ref[2])

    return pl.pallas_call(
        body, name=name,
        out_shape=jax.ShapeDtypeStruct((2, rh, d), F32),
        grid_spec=pltpu.PrefetchScalarGridSpec(
            num_scalar_prefetch=1, grid=(1,),
            in_specs=[pl.BlockSpec((None, rh, d), lambda i, p: (p[1], 0, 0)),
                      pl.BlockSpec((3, rh, d), lambda i, p: (0, 0, 0))],
            out_specs=pl.BlockSpec((None, rh, d), lambda i, p: (p[0], 0, 0))),
        compiler_params=_params("arbitrary"),
    )(pos, p_out, land3)


def _matmul(a, b, kind, name, tm_t=1056, tn_t=1024, tk_t=2048, comms=(), row_half=None, out_dtype=F32):
    if kind == "nn":
        (m, k), n = a.shape, b.shape[1]
    elif kind == "nt":
        (m, k), n = a.shape, b.shape[0]
    else:
        (k, m), n = a.shape, b.shape[1]
    if row_half is None:
        row_half = jnp.zeros((1,), jnp.int32)
    else:
        assert kind == "nn"
        m = m // 2
    tm = _pick(m, tm_t, 128 if kind == "tn" else 16)
    tn = _pick(n, tn_t, 128)
    tk = _pick(k, tk_t, 16 if kind == "tn" else 128)
    nk = k // tk
    nbm = m // tm
    grid = (nbm, n // tn, nk)
    dot = {"nn": _nn, "nt": _nt, "tn": _tn}[kind]
    a_spec = {"nn": pl.BlockSpec((tm, tk), lambda i, j, q, p: (i + p[0] * nbm, q)),
              "nt": pl.BlockSpec((tm, tk), lambda i, j, q, p: (i, q)),
              "tn": pl.BlockSpec((tk, tm), lambda i, j, q, p: (q, i))}[kind]
    b_spec = {"nn": pl.BlockSpec((tk, tn), lambda i, j, q, p: (q, j)),
              "nt": pl.BlockSpec((tn, tk), lambda i, j, q, p: (j, q)),
              "tn": pl.BlockSpec((tk, tn), lambda i, j, q, p: (q, j))}[kind]
    comm_ins = [x for cm in comms for x in cm.ins]
    comm_outs = [x for cm in comms for x in cm.outs]
    aliases, scratch = {}, []
    i_off, o_off = 3, 1
    for cm in comms:
        for ci, co in cm.aliases.items():
            aliases[i_off + ci] = o_off + co
        i_off, o_off = i_off + len(cm.ins), o_off + len(cm.outs)
        scratch += cm.scratch()

    def body(half_ref, a_ref, b_ref, *rest):
        cin = rest[:len(comm_ins)]
        o_ref = rest[len(comm_ins)]
        cout = rest[len(comm_ins) + 1:len(comm_ins) + 1 + len(comm_outs)]
        acc_ref = rest[len(comm_ins) + 1 + len(comm_outs)]
        sems = rest[len(comm_ins) + 2 + len(comm_outs):]
        ids = [pl.program_id(ax) for ax in range(3)]
        first = functools.reduce(jnp.logical_and, [ids[ax] == 0 for ax in range(3)])
        last = functools.reduce(jnp.logical_and, [ids[ax] == grid[ax] - 1 for ax in range(3)])

        def comm_args(idx):
            i0 = sum(len(cm.ins) for cm in comms[:idx])
            o0 = sum(len(cm.outs) for cm in comms[:idx])
            cm = comms[idx]
            return (cin[i0:i0 + len(cm.ins)], cout[o0:o0 + len(cm.outs)]) + tuple(sems[3 * idx:3 * idx + 3])

        if comms:
            @pl.when(first)
            def _():
                for idx, cm in enumerate(comms):
                    cm.start(*comm_args(idx))

        q = ids[2]
        p = dot(a_ref[...], b_ref[...])
        if nk == 1:
            o_ref[...] = p.astype(out_dtype)
        else:
            @pl.when(q == 0)
            def _():
                acc_ref[...] = p

            @pl.when(q > 0)
            def _():
                acc_ref[...] += p

            @pl.when(q == nk - 1)
            def _():
                o_ref[...] = acc_ref[...].astype(out_dtype)

        if comms:
            @pl.when(last)
            def _():
                for idx, cm in enumerate(comms):
                    cm.finish(*comm_args(idx))

    sem = ("arbitrary",) * 3 if comms else ("parallel", "parallel", "arbitrary")
    res = pl.pallas_call(
        body, name=name,
        out_shape=(jax.ShapeDtypeStruct((m, n), out_dtype), *comm_outs),
        grid_spec=pltpu.PrefetchScalarGridSpec(
            num_scalar_prefetch=1, grid=grid,
            in_specs=[a_spec, b_spec] + [ANY] * len(comm_ins),
            out_specs=(pl.BlockSpec((tm, tn), lambda i, j, q, p: (i, j)), *([ANY] * len(comm_outs))),
            scratch_shapes=[pltpu.VMEM((tm, tn) if nk > 1 else (8, 128), F32)] + scratch),
        input_output_aliases=aliases,
        compiler_params=_params(*sem),
    )(row_half, a, b, *comm_ins)
    return res if comms else res[0]


def _in_proj_gathered(hx, w_shard, order, comm, name, tm_t=1056, tn_t=1024):
    t, d = hx.shape
    n4 = w_shard.shape[1]
    hr = d // 2
    tm, tn = _pick(t, tm_t, 16), _pick(n4, tn_t, 128)
    ncol, nrow = n4 // tn, t // tm
    ntile = 4 * ncol
    n_ci, n_co = len(comm.ins), len(comm.outs)

    def body(order_ref, a_ref, ws_ref, *rest):
        cin, (o_ref, wg_ref), cout = rest[:n_ci], rest[n_ci:n_ci + 2], rest[n_ci + 2:n_ci + 2 + n_co]
        bbuf, fsem, send, recv, loc = rest[n_ci + 2 + n_co:n_ci + 7 + n_co]
        csems = rest[n_ci + 7 + n_co:]
        k, j, i = pl.program_id(0), pl.program_id(1), pl.program_id(2)
        n = k * ncol + j
        x, y, c = _mesh_pos()
        sib = (x, y, 1 - c)
        peers = [(1 - x, y), (x, 1 - y), (1 - x, 1 - y)]
        s_me = 2 * x + y

        def slot(s, h, jj):
            return wg_ref.at[pl.ds(pl.multiple_of(h * hr, 16), hr), pl.ds(pl.multiple_of(s * n4 + jj * tn, 128), tn)]

        def my_piece(jj):
            return ws_ref.at[pl.ds(pl.multiple_of(c * hr, 16), hr), pl.ds(jj * tn, tn)]

        def local():
            return pltpu.make_async_copy(ws_ref, wg_ref.at[:, pl.ds(pl.multiple_of(s_me * n4, 128), n4)], loc.at[0])

        def ici_send(p, jj):
            return _remote(my_piece(jj), slot(s_me, c, jj), send, recv, p * ncol + jj, (*peers[p], c))

        def ici_recv(p, jj):
            s_p = 2 * peers[p][0] + peers[p][1]
            return _remote(my_piece(jj), slot(s_p, c, jj), send, recv, p * ncol + jj, (*peers[p], c))

        def d2d_send(p, jj):
            s_p = 2 * peers[p][0] + peers[p][1]
            return _remote(slot(s_p, c, jj), slot(s_p, c, jj), send, recv, (3 + p) * ncol + jj, sib)

        def d2d_recv(p, jj):
            s_p = 2 * peers[p][0] + peers[p][1]
            return _remote(slot(s_p, c, jj), slot(s_p, 1 - c, jj), send, recv, (3 + p) * ncol + jj, sib)

        def fetch(m, own):
            col = (m % ncol) * tn
            if own:
                src = ws_ref.at[:, pl.ds(pl.multiple_of(col, 128), tn)]
            else:
                src = wg_ref.at[:, pl.ds(pl.multiple_of(order_ref[m // ncol] * n4 + col, 128), tn)]
            return pltpu.make_async_copy(src, bbuf.at[m % 2], fsem.at[m % 2])

        comm_args = (cin, cout) + tuple(csems)

        @pl.when(jnp.logical_and(i == 0, n == 0))
        def _():
            local().start()
            for p, jj in [(p, jj) for ps in ((0, 1), (2,)) for jj in range(ncol) for p in ps]:
                ici_send(p, jj).start()
            fetch(n, True).start()

        @pl.when(i == 0)
        def _():
            fetch(n, True).wait()

        @pl.when(i == max(nrow - 2, 0))
        def _():
            for m in range(ncol, ntile):
                @pl.when(n == m - 1)
                def _():
                    p, jj = m // ncol - 1, m % ncol
                    ici_recv(p, jj).wait_recv()
                    d2d_send(p, jj).start()
                    d2d_recv(p, jj).wait_recv()
                    if m == ntile - ncol:
                        comm.start(*comm_args)

            @pl.when(n + 1 < ncol)
            def _():
                fetch(n + 1, True).start()

            @pl.when(jnp.logical_and(n + 1 >= ncol, n + 1 < ntile))
            def _():
                fetch(n + 1, False).start()

        o_ref[...] = _nn(a_ref[...], bbuf[n % 2]).astype(BF16)

        @pl.when(jnp.logical_and(n == ntile - 1, i == nrow - 1))
        def _():
            for p in range(3):
                for jj in range(ncol):
                    ici_send(p, jj).wait_send()
                    d2d_send(p, jj).wait_send()
            local().wait()
            comm.finish(*comm_args)

    res = pl.pallas_call(
        body, name=name,
        out_shape=(jax.ShapeDtypeStruct((t, 4 * n4), BF16), jax.ShapeDtypeStruct((d, 4 * n4), BF16), *comm.outs),
        grid_spec=pltpu.PrefetchScalarGridSpec(
            num_scalar_prefetch=1, grid=(4, ncol, nrow),
            in_specs=[pl.BlockSpec((tm, d), lambda k, j, i, p: (i, 0)), ANY] + [ANY] * n_ci,
            out_specs=(pl.BlockSpec((tm, tn), lambda k, j, i, p: (i, p[k] * ncol + j)), ANY, *([ANY] * n_co)),
            scratch_shapes=[pltpu.VMEM((2, d, tn), BF16), pltpu.SemaphoreType.DMA((2,)),
                            pltpu.SemaphoreType.DMA((6 * ncol,)), pltpu.SemaphoreType.DMA((6 * ncol,)),
                            pltpu.SemaphoreType.DMA((1,))] + comm.scratch()),
        compiler_params=_params("arbitrary", "arbitrary", "arbitrary"),
    )(order, hx, w_shard, *comm.ins)
    return res


def _out_proj(y, w_out, xs, rows, n_ctx, name):
    t, k = y.shape
    d = w_out.shape[1]
    tm, tn = _pick(t, 1056, 16), _pick(d, 1024, 128)

    n_sub = 2 if tm % 32 == 0 else 1
    ts = tm // n_sub

    def body(y_ref, w_ref, x_ref, r_ref, xo_ref, p_ref):
        i = pl.program_id(0)
        for sub in range(n_sub):
            rows = slice(sub * ts, (sub + 1) * ts)
            p = _nn(y_ref[rows, :], w_ref[...])
            ridx = i * tm + sub * ts + lax.broadcasted_iota(jnp.int32, (ts, 1), 0)
            gate = jnp.where(ridx < n_ctx, r_ref[5:6, :], r_ref[2:3, :])
            p_ref[rows, :] = p.astype(BF16)
            xo_ref[rows, :] = x_ref[rows, :] + gate * p

    return pl.pallas_call(
        body, name=name,
        out_shape=(jax.ShapeDtypeStruct((t, d), F32), jax.ShapeDtypeStruct((t, d), BF16)),
        grid=(t // tm, d // tn),
        in_specs=[pl.BlockSpec((tm, k), lambda i, j: (i, 0)), pl.BlockSpec((k, tn), lambda i, j: (0, j)),
                  pl.BlockSpec((tm, tn), lambda i, j: (i, j)), pl.BlockSpec((8, tn), lambda i, j: (0, j))],
        out_specs=(pl.BlockSpec((tm, tn), lambda i, j: (i, j)), pl.BlockSpec((tm, tn), lambda i, j: (i, j))),
        compiler_params=_params("parallel", "parallel"),
    )(y, w_out, xs, rows)


def _modulate(xs, nw, rows, nct, name):
    t, d = xs.shape
    tr = ROW_TILE

    def body(x_ref, nw_ref, r_ref, o_ref, ot_ref):
        is_ctx = pl.program_id(0) < nct
        x = x_ref[...]
        xn = (x * lax.rsqrt(jnp.mean(x * x, axis=-1, keepdims=True) + EPS)) * nw_ref[...]
        shift = jnp.where(is_ctx, r_ref[3:4, :], r_ref[0:1, :])
        scale = jnp.where(is_ctx, r_ref[4:5, :], r_ref[1:2, :])
        hx = xn * (1.0 + scale) + shift
        o_ref[...] = hx.astype(BF16)
        ot_ref[...] = hx.T.astype(BF16)

    return pl.pallas_call(
        body, name=name,
        out_shape=(jax.ShapeDtypeStruct((t, d), BF16), jax.ShapeDtypeStruct((d, t), BF16)), grid=(t // tr,),
        in_specs=[pl.BlockSpec((tr, d), lambda i: (i, 0)), pl.BlockSpec((1, d), lambda i: (0, 0)),
                  pl.BlockSpec((8, d), lambda i: (0, 0))],
        out_specs=(pl.BlockSpec((tr, d), lambda i: (i, 0)), pl.BlockSpec((d, tr), lambda i: (0, i))),
        compiler_params=_params("parallel"),
    )(xs, nw, rows)


def _modulate_bwd(dhx, xs, g, nw, rows, nct, name, below=None):
    t, d = xs.shape
    tr = ROW_TILE
    latent_only = below is None
    t_out = t - nct * tr if latent_only else t
    out_row = (lambda i: (jnp.maximum(i - nct, 0), 0)) if latent_only else (lambda i: (i, 0))
    n_below = 0 if latent_only else 2

    def body(dh_ref, x_ref, g_ref, nw_ref, r_ref, *rest):
        go_ref, acc_ref = rest[n_below:n_below + 2]
        part = _Partials(rest[-1], 4)
        i = pl.program_id(0)
        is_ctx = i < nct

        @pl.when(i == 0)
        def _():
            acc_ref[...] = jnp.zeros_like(acc_ref)
            if n_below:
                rest[n_below + 3][...] = jnp.zeros_like(rest[n_below + 3])

        part.zero()

        def chunk(r0, q):
            dh16 = dh_ref[pl.ds(r0, ROW_CHUNK), :].astype(F32)
            one_scale = 1.0 + jnp.where(is_ctx, r_ref[4:5, :], r_ref[1:2, :])
            nwv = nw_ref[...]
            if n_below:
                p16 = rest[0][pl.ds(r0, ROW_CHUNK), :].astype(F32)
                gate = jnp.where(is_ctx, rest[1][5:6, :], rest[1][2:3, :])
            gps = []
            for hh in range(ROW_CHUNK // 8):
                rows8 = pl.ds(r0 + 8 * hh, 8)
                x, dh = x_ref[rows8, :], dh16[8 * hh:8 * hh + 8]
                r = lax.rsqrt(jnp.mean(x * x, axis=-1, keepdims=True) + EPS)
                xh = x * r
                part.add(q, 0, dh)
                part.add(q, 1, dh * (xh * nwv))
                dxn = dh * one_scale
                part.add(q, 2, dxn * xh)
                dxh = dxn * nwv
                g_in = g_ref[rows8, :] + r * (dxh - xh * jnp.mean(dxh * xh, axis=-1, keepdims=True))
                go_ref[rows8, :] = g_in
                if n_below:
                    gps.append(g_in * gate)
                    part.add(q, 3, g_in * p16[8 * hh:8 * hh + 8])
            if n_below:
                rest[n_below + 2][pl.ds(r0, ROW_CHUNK), :] = jnp.concatenate(gps, axis=0).astype(BF16)

        _for_row_chunks(tr, chunk)
        total = part.total
        d_shift, d_scale = total(0), total(1)
        acc_ref[6:7, :] += total(2)
        zero = jnp.zeros_like(d_shift)
        acc_ref[0:1, :] += jnp.where(is_ctx, zero, d_shift)
        acc_ref[1:2, :] += jnp.where(is_ctx, zero, d_scale)
        acc_ref[3:4, :] += jnp.where(is_ctx, d_shift, zero)
        acc_ref[4:5, :] += jnp.where(is_ctx, d_scale, zero)
        if n_below:
            gacc_ref, s = rest[n_below + 3], total(3)
            gacc_ref[2:3, :] += jnp.where(is_ctx, zero, s)
            gacc_ref[5:6, :] += jnp.where(is_ctx, s, zero)

    row = pl.BlockSpec((tr, d), lambda i: (i, 0))
    vecs = pl.BlockSpec((8, d), lambda i: (0, 0))
    return pl.pallas_call(
        body, name=name,
        out_shape=(jax.ShapeDtypeStruct((t_out, d), F32), jax.ShapeDtypeStruct((8, d), F32))
        + ((jax.ShapeDtypeStruct((t, d), BF16), jax.ShapeDtypeStruct((8, d), F32)) if n_below else ()),
        grid=(t // tr,),
        in_specs=[row, row, row, pl.BlockSpec((1, d), lambda i: (0, 0)), vecs] + ([row, vecs] if n_below else []),
        out_specs=(pl.BlockSpec((tr, d), out_row), vecs) + ((row, vecs) if n_below else ()),
        scratch_shapes=[_Partials.scratch(4, d)],
        compiler_params=_params("arbitrary"),
    )(dhx, xs, g, nw, rows, *(below or ()))


def _final_loss(xs, tgt, fw, below, nct, name):
    t, d = xs.shape
    tr = ROW_TILE

    def body(x_ref, t_ref, w_ref, p_ref, r_ref, g_ref, acc_ref, gp_ref, gacc_ref, part_ref):
        part = _Partials(part_ref, 3)
        i = pl.program_id(0)
        is_ctx = i < nct

        @pl.when(i == 0)
        def _():
            acc_ref[...] = jnp.zeros_like(acc_ref)
            gacc_ref[...] = jnp.zeros_like(gacc_ref)

        @pl.when(is_ctx)
        def _():
            g_ref[...] = jnp.zeros_like(g_ref)
            gp_ref[...] = jnp.zeros_like(gp_ref)

        @pl.when(jnp.logical_not(is_ctx))
        def _():
            part.zero()

            def chunk(r0, q):
                p16 = p_ref[pl.ds(r0, ROW_CHUNK), :].astype(F32)
                wv, gate = w_ref[...], r_ref[2:3, :]
                gps = []
                for hh in range(ROW_CHUNK // 8):
                    rows8 = pl.ds(r0 + 8 * hh, 8)
                    x = x_ref[rows8, :]
                    r = lax.rsqrt(jnp.mean(x * x, axis=-1, keepdims=True) + EPS)
                    xh = x * r
                    e = xh * wv - t_ref[rows8, :]
                    part.add(q, 1, e * e)
                    dy = e * (1.0 / d)
                    part.add(q, 0, dy * xh)
                    dxh = dy * wv
                    g = r * (dxh - xh * jnp.mean(dxh * xh, axis=-1, keepdims=True))
                    g_ref[rows8, :] = g
                    gps.append(g * gate)
                    part.add(q, 2, g * p16[8 * hh:8 * hh + 8])
                gp_ref[pl.ds(r0, ROW_CHUNK), :] = jnp.concatenate(gps, axis=0).astype(BF16)

            _for_row_chunks(tr, chunk)
            acc_ref[0:1, :] += part.total(0)
            acc_ref[1:2, :] += jnp.zeros((1, d), F32) + jnp.sum(part.total(1)) * (0.5 / d)
            gacc_ref[2:3, :] += part.total(2)

    row = pl.BlockSpec((tr, d), lambda i: (i, 0))
    vecs = pl.BlockSpec((8, d), lambda i: (0, 0))
    return pl.pallas_call(
        body, name=name,
        out_shape=(jax.ShapeDtypeStruct((t, d), F32), jax.ShapeDtypeStruct((8, d), F32),
                   jax.ShapeDtypeStruct((t, d), BF16), jax.ShapeDtypeStruct((8, d), F32)), grid=(t // tr,),
        in_specs=[row, pl.BlockSpec((tr, d), lambda i: (jnp.maximum(i - nct, 0), 0)),
                  pl.BlockSpec((1, d), lambda i: (0, 0)), row, vecs],
        out_specs=(row, vecs, row, vecs),
        scratch_shapes=[_Partials.scratch(3, d)],
        compiler_params=_params("arbitrary"),
    )(xs, tgt, fw, *below)


def _seq_edges(i, nct, nt):
    has_prev = jnp.logical_and(i != 0, i != nct)
    has_next = jnp.logical_and(i != nct - 1, i != nt - 1)
    return has_prev, has_next


def _shift_rows(a, before, after):
    tr = a.shape[0]
    ridx = lax.broadcasted_iota(jnp.int32, (tr, 1), 0)
    down = jnp.where(ridx == 0, before, pltpu.roll(a, 1, 0))
    up = jnp.where(ridx == tr - 1, after, pltpu.roll(a, tr - 1, 0))
    return down, up


def _halo_specs(tr, dc, col, t, hr):
    per = tr // hr
    prev = pl.BlockSpec((hr, dc), lambda i: (jnp.maximum(i * per - 1, 0), col))
    nxt = pl.BlockSpec((hr, dc), lambda i: (jnp.minimum((i + 1) * per, t // hr - 1), col))
    return prev, nxt


def _mix_fwd(u, o_f, o_b, cw, cnw, rnw, nct, name):
    t, w = u.shape
    dc = w // 8
    nh = dc // HEAD
    tr = ROW_TILE
    nt = t // tr
    col = lambda cidx: pl.BlockSpec((tr, dc), lambda i: (i, cidx))
    hr = 16
    acp, acn = _halo_specs(tr, dc, 2, t, hr)
    ahp, ahn = _halo_specs(tr, dc, 0, t, hr)
    f32 = lambda v: v.astype(F32)

    def body(ah, ab, ac, az, rz, acp_r, acn_r, ahp_r, ahn_r, of_r, ob_r, cw_r, cnw_r, rnw_r, y_ref, yt_ref, cv_ref):
        i = pl.program_id(0)
        has_prev, has_next = _seq_edges(i, nct, nt)
        ch = f32(ac[...]) * f32(ah[...])
        before = jnp.where(has_prev, f32(acp_r[hr - 1:hr, :]) * f32(ahp_r[hr - 1:hr, :]), 0.0)
        after = jnp.where(has_next, f32(acn_r[0:1, :]) * f32(ahn_r[0:1, :]), 0.0)
        down, up = _shift_rows(ch, before, after)
        cv = down * cw_r[0:1, :] + ch * cw_r[1:2, :] + up * cw_r[2:3, :]
        cv_ref[...] = cv
        y1 = f32(ab[...]) * cv
        yn = (y1 * lax.rsqrt(jnp.mean(y1 * y1, axis=-1, keepdims=True) + EPS)) * cnw_r[...]
        yc = _silu(f32(az[...])) * yn
        y_ref[:, 0:dc] = yc.astype(BF16)
        yt_ref[0:dc, :] = yc.T.astype(BF16)
        for h in range(nh):
            sl = slice(h * HEAD, (h + 1) * HEAD)
            o = of_r[:, sl] + ob_r[:, sl]
            dev = o - jnp.mean(o, axis=-1, keepdims=True)
            on = dev * lax.rsqrt(jnp.mean(dev * dev, axis=-1, keepdims=True) + EPS)
            yr = _silu(f32(rz[:, sl])) * (on * rnw_r[:, sl])
            y_ref[:, dc + h * HEAD:dc + (h + 1) * HEAD] = yr.astype(BF16)
            yt_ref[dc + h * HEAD:dc + (h + 1) * HEAD, :] = yr.T.astype(BF16)

    vec = pl.BlockSpec((1, dc), lambda i: (0, 0))
    return pl.pallas_call(
        body, name=name,
        out_shape=(jax.ShapeDtypeStruct((t, 2 * dc), BF16), jax.ShapeDtypeStruct((2 * dc, t), BF16),
                   jax.ShapeDtypeStruct((t, dc), F32)), grid=(nt,),
        in_specs=[col(0), col(1), col(2), col(3), col(7), acp, acn, ahp, ahn,
                  pl.BlockSpec((tr, dc), lambda i: (i, 0)), pl.BlockSpec((tr, dc), lambda i: (i, 0)),
                  pl.BlockSpec((8, dc), lambda i: (0, 0)), vec, vec],
        out_specs=(pl.BlockSpec((tr, 2 * dc), lambda i: (i, 0)), pl.BlockSpec((2 * dc, tr), lambda i: (0, i)),
                   pl.BlockSpec((tr, dc), lambda i: (i, 0))),
        compiler_params=_params("parallel"),
    )(u, u, u, u, u, u, u, u, u, o_f, o_b, cw, cnw, rnw)


def _mix_bwd(dy, u, cv, o_f, o_b, cnw, rnw, name):
    t, w = u.shape
    dc = w // 8
    nh = dc // HEAD
    tr = ROW_TILE
    col = lambda cidx: pl.BlockSpec((tr, dc), lambda i: (i, cidx))

    def body(dyc, dyr, ab, az, rz, cv_r, of_r, ob_r, cnw_r, rnw_r, dab, daz, drz, dcv, do, acc_ref, part_ref):
        part = _Partials(part_ref, 1)

        @pl.when(pl.program_id(0) == 0)
        def _():
            acc_ref[...] = jnp.zeros_like(acc_ref)

        part.zero()

        def chunk(r0, q):
            rows = pl.ds(r0, ROW_CHUNK)
            z, b, c_out = az[rows, :].astype(F32), ab[rows, :].astype(F32), cv_r[rows, :]
            nwv = cnw_r[...]
            y1 = b * c_out
            r = lax.rsqrt(jnp.mean(y1 * y1, axis=-1, keepdims=True) + EPS)
            yh = y1 * r
            dyc_v = dyc[rows, :]
            sg = jax.nn.sigmoid(z)
            daz[rows, :] = (dyc_v * (yh * nwv) * (sg * (1.0 + z * (1.0 - sg)))).astype(BF16)
            dyn = dyc_v * (z * sg)
            part.add(q, 0, _fold8(dyn * yh))
            dyh = dyn * nwv
            dy1 = r * (dyh - yh * jnp.mean(dyh * yh, axis=-1, keepdims=True))
            dab[rows, :] = (dy1 * c_out).astype(BF16)
            dcv[rows, :] = dy1 * b

        _for_row_chunks(tr, chunk)
        acc_ref[0:1, :] += part.total(0)
        for h in range(nh):
            sl = slice(h * HEAD, (h + 1) * HEAD)
            o = of_r[:, sl] + ob_r[:, sl]
            dev = o - jnp.mean(o, axis=-1, keepdims=True)
            rs = lax.rsqrt(jnp.mean(dev * dev, axis=-1, keepdims=True) + EPS)
            on = dev * rs
            zz, dyr_v, gw = rz[:, sl].astype(F32), dyr[:, sl], rnw_r[:, sl]
            sg = jax.nn.sigmoid(zz)
            drz[:, sl] = (dyr_v * (on * gw) * (sg * (1.0 + zz * (1.0 - sg)))).astype(BF16)
            dong = dyr_v * (zz * sg)
            acc_ref[1:2, sl] += jnp.sum(dong * on, axis=0, keepdims=True)
            don = dong * gw
            do[:, sl] = rs * (don - jnp.mean(don, axis=-1, keepdims=True)
                              - on * jnp.mean(don * on, axis=-1, keepdims=True))

    vec = pl.BlockSpec((1, dc), lambda i: (0, 0))
    row = pl.BlockSpec((tr, dc), lambda i: (i, 0))
    return pl.pallas_call(
        body, name=name,
        out_shape=(jax.ShapeDtypeStruct((t, dc), BF16), jax.ShapeDtypeStruct((t, dc), BF16),
                   jax.ShapeDtypeStruct((t, dc), BF16), jax.ShapeDtypeStruct((t, dc), F32),
                   jax.ShapeDtypeStruct((t, dc), F32), jax.ShapeDtypeStruct((8, dc), F32)),
        grid=(t // tr,),
        in_specs=[col(0), col(1), col(1), col(3), col(7), row, row, row, vec, vec],
        out_specs=(row, row, row, row, row, pl.BlockSpec((8, dc), lambda i: (0, 0))),
        scratch_shapes=[_Partials.scratch(1, dc)],
        compiler_params=_params("arbitrary"),
    )(dy, dy, u, u, u, cv, o_f, o_b, cnw, rnw)


def _rope_masks():
    lane = lax.broadcasted_iota(jnp.int32, (1, HEAD), 1)
    return (lane % (HEAD // 2)) >= (HEAD // 4)


def _rot(a, upper):
    return jnp.where(upper, pltpu.roll(a, HEAD // 4, 1), pltpu.roll(a, HEAD - HEAD // 4, 1))


def _assemble_du(dcv, u, cw, dab, daz, drz, dq_f, dq_b, dk_f, dk_b, dv_f, dv_b, cos, sin, nct, name):
    t, w = u.shape
    dc = w // 8
    nh = dc // HEAD
    tr = ROW_TILE
    nt = t // tr
    k_scale = HEAD ** -0.5
    col = lambda cidx: pl.BlockSpec((tr, dc), lambda i: (i, cidx))
    row = pl.BlockSpec((tr, dc), lambda i: (i, 0))
    dcp, dcn = _halo_specs(tr, dc, 0, t, 8)
    f32 = lambda v: v.astype(F32)

    def body(dcv_r, dcp_r, dcn_r, ah, ac, cw_r, dab_r, daz_r, drz_r, dqf, dqb, dkf, dkb, dvf, dvb, cos_r, sin_r,
             du, acc_ref, pad, part_ref):
        part = _Partials(part_ref, 3)
        i = pl.program_id(0)

        @pl.when(i == 0)
        def _():
            acc_ref[...] = jnp.zeros_like(acc_ref)

        has_prev, has_next = _seq_edges(i, nct, nt)
        pad[0:8, :] = jnp.where(has_prev, dcp_r[...], 0.0)
        pad[8:8 + tr, :] = dcv_r[...]
        pad[8 + tr:16 + tr, :] = jnp.where(has_next, dcn_r[...], 0.0)
        part.zero()
        upper = _rope_masks()

        def chunk(r0, q):
            rows = pl.ds(r0, ROW_CHUNK)
            d_c = pad[pl.ds(pl.multiple_of(r0 + 8, 8), ROW_CHUNK), :]
            d_prev, d_next = _shift_rows(d_c, pad[pl.ds(r0, 8), :][7:8],
                                         pad[pl.ds(pl.multiple_of(r0 + 8 + ROW_CHUNK, 8), 8), :][0:1])
            a_h, a_c = f32(ah[rows, :]), f32(ac[rows, :])
            ch = a_c * a_h
            dch = d_next * cw_r[0:1, :] + d_c * cw_r[1:2, :] + d_prev * cw_r[2:3, :]
            part.add(q, 0, _fold8(ch * d_next))
            part.add(q, 1, _fold8(ch * d_c))
            part.add(q, 2, _fold8(ch * d_prev))
            du[rows, 0:dc] = (dch * a_c).astype(BF16)
            du[rows, dc:2 * dc] = dab_r[rows, :]
            du[rows, 2 * dc:3 * dc] = (dch * a_h).astype(BF16)
            du[rows, 3 * dc:4 * dc] = daz_r[rows, :]
            du[rows, 7 * dc:8 * dc] = drz_r[rows, :]
            du[rows, 6 * dc:7 * dc] = (f32(dvf[rows, :]) + f32(dvb[rows, :])).astype(BF16)
            cs, sn = cos_r[rows, :], sin_r[rows, :]
            for h in range(nh):
                sl = slice(h * HEAD, (h + 1) * HEAD)
                dq = f32(dqf[rows, sl]) + f32(dqb[rows, sl])
                dk = (f32(dkf[rows, sl]) + f32(dkb[rows, sl])) * k_scale
                du[rows, 4 * dc + h * HEAD:4 * dc + (h + 1) * HEAD] = (dq * cs + _rot(dq * sn, upper)).astype(BF16)
                du[rows, 5 * dc + h * HEAD:5 * dc + (h + 1) * HEAD] = (dk * cs + _rot(dk * sn, upper)).astype(BF16)

        _for_row_chunks(tr, chunk)
        for tap in range(3):
            acc_ref[tap:tap + 1, :] += part.total(tap)

    tab = pl.BlockSpec((tr, HEAD), lambda i: (i, 0))
    return pl.pallas_call(
        body, name=name,
        out_shape=(jax.ShapeDtypeStruct((t, w), BF16), jax.ShapeDtypeStruct((8, dc), F32)), grid=(nt,),
        in_specs=[row, dcp, dcn, col(0), col(2), pl.BlockSpec((8, dc), lambda i: (0, 0)),
                  row, row, row, row, row, row, row, row, row, tab, tab],
        out_specs=(pl.BlockSpec((tr, w), lambda i: (i, 0)), pl.BlockSpec((8, dc), lambda i: (0, 0))),
        scratch_shapes=[pltpu.VMEM((tr + 16, dc), F32), _Partials.scratch(3, dc)],
        compiler_params=_params("arbitrary"),
    )(dcv, dcv, dcv, u, u, cw, dab, daz, drz, dq_f, dq_b, dk_f, dk_b, dv_f, dv_b, cos, sin)


def _bwd_dir_chunk(s, ncc, ns):
    return jnp.where(s < ncc, ncc - 1 - s, ns + ncc - 1 - s)


def _position_weights(d):
    c = CHUNK
    ii = lax.broadcasted_iota(jnp.int32, (c, c), 0)
    jj = lax.broadcasted_iota(jnp.int32, (c, c), 1)
    sd = ((ii - jj) if d == 0 else (jj - ii)).astype(F32)
    pos = lax.broadcasted_iota(jnp.int32, (c, HEAD), 0).astype(F32)
    qw, kw = (pos + 1.0, (c - 1.0) - pos) if d == 0 else (c - pos, pos)
    return sd, qw, kw


def _fill_decay_tables(dec_ref, nh, dm_t, qd_t, kd_t, gc_t):
    for d in range(2):
        sd, qw, kw = _position_weights(d)
        for h in range(nh):
            r = d * nh + h
            lg = -jnp.exp(jnp.zeros((1, HEAD), F32) + dec_ref[d, h])
            dm_t[r] = jnp.where(sd >= 0, jnp.exp(lg * jnp.maximum(sd, 0.0)), 0.0)
            qd_t[r] = jnp.exp(lg * qw)
            kd_t[r] = jnp.exp(lg * kw)
            gc_t[r] = jnp.zeros((8, HEAD), F32) + jnp.exp(lg * float(CHUNK))


def _decay_table_scratch(nh):
    return [pltpu.VMEM((2 * nh, CHUNK, CHUNK), F32), pltpu.VMEM((2 * nh, CHUNK, HEAD), F32),
            pltpu.VMEM((2 * nh, CHUNK, HEAD), F32), pltpu.VMEM((2 * nh, 8, HEAD), F32)]


def _ret_fwd(u, cos, sin, dec, ncc, name):
    t, w = u.shape
    dc = w // 8
    nh = dc // HEAD
    c = CHUNK
    ns = t // c
    k_scale = HEAD ** -0.5
    fmap = lambda g: (lambda s: (s, g))
    bmap = lambda g: (lambda s: (_bwd_dir_chunk(s, ncc, ns), g))
    blk = lambda m: pl.BlockSpec((c, dc), m)
    tab = lambda m: pl.BlockSpec((c, HEAD), m)
    st_f = pl.BlockSpec((nh, None, HEAD, HEAD), lambda s: (0, s, 0, 0))
    st_b = pl.BlockSpec((nh, None, HEAD, HEAD), lambda s: (0, _bwd_dir_chunk(s, ncc, ns), 0, 0))

    def body(qf, kf, vf, qb, kb, vb, cf, sf, cb, sb, dec_ref, of_ref, ob_ref, stf_ref, stb_ref, qr_ref, kr_ref,
             s_f, s_b, dm_t, qd_t, kd_t, gc_t):
        @pl.when(pl.program_id(0) == 0)
        def _():
            s_f[...] = jnp.zeros_like(s_f)
            s_b[...] = jnp.zeros_like(s_b)
            _fill_decay_tables(dec_ref, nh, dm_t, qd_t, kd_t, gc_t)

        upper = _rope_masks()
        dirs = ((qf, kf, vf, cf, sf, of_ref, stf_ref, s_f), (qb, kb, vb, cb, sb, ob_ref, stb_ref, s_b))
        group = RET_INTERLEAVE if nh % RET_INTERLEAVE == 0 else 1
        blocks = [[(h, d) for h in range(h0, h0 + group) for d in range(2)] for h0 in range(0, nh, group)]
        for blk_list in blocks:
            first = []
            for h, d in blk_list:
                q_r, k_r, v_r, cs_r, sn_r, _, st_ref, st = dirs[d]
                sl = slice(h * HEAD, (h + 1) * HEAD)
                r = d * nh + h
                cs, sn = cs_r[...], sn_r[...]
                q = q_r[:, sl].astype(F32)
                q = q * cs + _rot(q, upper) * sn
                k = k_r[:, sl].astype(F32)
                k = (k * cs + _rot(k, upper) * sn) * k_scale
                if d == 0:
                    qr_ref[:, sl] = q
                    kr_ref[:, sl] = k
                vv = v_r[:, sl]
                state = st[h]
                st_ref[h] = state
                p = _nt(_bf(q), _bf(k)) * dm_t[r]
                kv = _tn(_bf(k * kd_t[r]), vv)
                first.append((q, vv, state, p, kv))
            for idx, (h, d) in enumerate(blk_list):
                _, _, _, _, _, o_ref, _, st = dirs[d]
                sl = slice(h * HEAD, (h + 1) * HEAD)
                r = d * nh + h
                q, vv, state, p, kv = first[idx]
                o_ref[:, sl] = _nn(jnp.concatenate([_bf(p), _bf(q * qd_t[r])], axis=1),
                                   jnp.concatenate([vv, _bf(state)], axis=0))
                st[h] = gc_t[r, 0:1, :] * state + kv

    return pl.pallas_call(
        body, name=name,
        out_shape=(jax.ShapeDtypeStruct((t, dc), F32), jax.ShapeDtypeStruct((t, dc), F32),
                   jax.ShapeDtypeStruct((nh, ns, HEAD, HEAD), F32), jax.ShapeDtypeStruct((nh, ns, HEAD, HEAD), F32),
                   jax.ShapeDtypeStruct((t, dc), F32), jax.ShapeDtypeStruct((t, dc), F32)),
        grid=(ns,),
        in_specs=[blk(fmap(4)), blk(fmap(5)), blk(fmap(6)), blk(bmap(4)), blk(bmap(5)), blk(bmap(6)),
                  tab(fmap(0)), tab(fmap(0)), tab(bmap(0)), tab(bmap(0)), SMEM_SPEC],
        out_specs=(blk(fmap(0)), blk(bmap(0)), st_f, st_b, blk(fmap(0)), blk(fmap(0))),
        scratch_shapes=[pltpu.VMEM((nh, HEAD, HEAD), F32), pltpu.VMEM((nh, HEAD, HEAD), F32)]
        + _decay_table_scratch(nh),
        compiler_params=_params("arbitrary"),
    )(u, u, u, u, u, u, cos, sin, cos, sin, dec)


def _ret_bwd(u, qr, kr, do, st_f, st_b, dec, ncc, name):
    t, w = u.shape
    dc = w // 8
    nh = dc // HEAD
    c = CHUNK
    ns = t // c
    fchunk = lambda s: ns - 1 - s
    bchunk = lambda s: _bwd_dir_chunk(ns - 1 - s, ncc, ns)
    fmap = lambda g: (lambda s: (fchunk(s), g))
    bmap = lambda g: (lambda s: (bchunk(s), g))
    blk = lambda m: pl.BlockSpec((c, dc), m)
    stf_spec = pl.BlockSpec((nh, None, HEAD, HEAD), lambda s: (0, fchunk(s), 0, 0))
    stb_spec = pl.BlockSpec((nh, None, HEAD, HEAD), lambda s: (0, bchunk(s), 0, 0))

    def body(qf, kf, vf, dof, qb, kb, vb, dob, stf_ref, stb_ref, dec_ref,
             dqf, dkf, dvf, dqb, dkb, dvb, dlg_ref, ds_f, ds_b, dm_t, qd_t, kd_t, gc_t, lg_acc):
        step = pl.program_id(0)

        @pl.when(step == 0)
        def _():
            ds_f[...] = jnp.zeros_like(ds_f)
            ds_b[...] = jnp.zeros_like(ds_b)
            lg_acc[...] = jnp.zeros_like(lg_acc)
            _fill_decay_tables(dec_ref, nh, dm_t, qd_t, kd_t, gc_t)

        dirs = ((qf, kf, vf, dof, stf_ref, dqf, dkf, dvf, ds_f), (qb, kb, vb, dob, stb_ref, dqb, dkb, dvb, ds_b))
        weights = [_position_weights(d) for d in range(2)]
        group = RET_INTERLEAVE if nh % RET_INTERLEAVE == 0 else 1
        blocks = [[(h, d) for h in range(h0, h0 + group) for d in range(2)] for h0 in range(0, nh, group)]
        for blk_list in blocks:
            first = []
            for h, d in blk_list:
                q_r, k_r, v_r, do_r, st_ref, _, _, _, ds = dirs[d]
                sl = slice(h * HEAD, (h + 1) * HEAD)
                r = d * nh + h
                q, k = q_r[:, sl], k_r[:, sl]
                qb16, kb16, vb16, dob16 = _bf(q), _bf(k), v_r[:, sl], _bf(do_r[:, sl])
                state, dstate = st_ref[h], ds[h]
                dsb16 = _bf(dstate)
                a = _nt(qb16, kb16) * dm_t[r]
                both = _nt(dob16, jnp.concatenate([vb16, _bf(state)], axis=0))
                dk_i = _nt(vb16, dsb16) * kd_t[r]
                first.append((q, k, qb16, kb16, dob16, state, dstate, dsb16, a, both, dk_i))
            for idx, (h, d) in enumerate(blk_list):
                _, _, _, _, _, dq_ref, dk_ref, dv_ref, ds = dirs[d]
                sl = slice(h * HEAD, (h + 1) * HEAD)
                r = d * nh + h
                q, k, qb16, kb16, dob16, state, dstate, dsb16, a, both, dk_i = first[idx]
                sd, qw, kw = weights[d]
                da = both[:, 0:c]
                dp = _bf(da * dm_t[r])
                dq_i = both[:, c:2 * c] * qd_t[r]
                dq_ref[:, sl] = _bf(_nn(dp, kb16) + dq_i)
                dk_ref[:, sl] = _bf(_tn(dp, qb16) + dk_i)
                dv_ref[:, sl] = _bf(_nn(jnp.concatenate([_bf(a.T), _bf(k * kd_t[r])], axis=1),
                                        jnp.concatenate([dob16, dsb16], axis=0)))
                gcr = gc_t[r, 0:1, :]
                ds[h] = gcr * dstate + _tn(_bf(q * qd_t[r]), dob16)
                lg_acc[r] += (da * a * jnp.maximum(sd, 0.0) + (dq_i * q) * qw + (dk_i * k) * kw
                              + (float(c) * gcr) * (state * dstate))

        @pl.when(step == ns - 1)
        def _():
            for d in range(2):
                for h in range(nh):
                    r = d * nh + h
                    lg = -jnp.exp(jnp.zeros((1, HEAD), F32) + dec_ref[d, h])
                    dlg_ref[r:r + 1, :] = (jnp.zeros((1, HEAD), F32) + jnp.sum(lg_acc[r])) * lg

    dshape = jax.ShapeDtypeStruct((t, dc), BF16)
    return pl.pallas_call(
        body, name=name,
        out_shape=(dshape, dshape, dshape, dshape, dshape, dshape, jax.ShapeDtypeStruct((2 * nh, HEAD), F32)),
        grid=(ns,),
        in_specs=[blk(fmap(0)), blk(fmap(0)), blk(fmap(6)), blk(fmap(0)),
                  blk(bmap(0)), blk(bmap(0)), blk(bmap(6)), blk(bmap(0)), stf_spec, stb_spec, SMEM_SPEC],
        out_specs=(blk(fmap(0)), blk(fmap(0)), blk(fmap(0)), blk(bmap(0)), blk(bmap(0)), blk(bmap(0)),
                   pl.BlockSpec((2 * nh, HEAD), lambda s: (0, 0))),
        scratch_shapes=[pltpu.VMEM((nh, HEAD, HEAD), F32), pltpu.VMEM((nh, HEAD, HEAD), F32)]
        + _decay_table_scratch(nh) + [pltpu.VMEM((2 * nh, CHUNK, HEAD), F32)],
        compiler_params=_params("arbitrary"),
    )(qr, kr, u, do, qr, kr, u, do, st_f, st_b, dec)


def _mod_matvec(c16, w_mod, b_mod):
    nl, d, nm = w_mod.shape
    tn = _pick(nm, 512, 128)

    def body(c_ref, w_ref, b_ref, sc_ref, o_ref):
        sc = _silu(c_ref[...])
        sc_ref[...] = sc
        o_ref[...] = _nn(_bf(sc), _bf(w_ref[...])) + b_ref[...]

    return pl.pallas_call(
        body, name="mod_matvec",
        out_shape=(jax.ShapeDtypeStruct((16, d), F32), jax.ShapeDtypeStruct((nl, 16, nm), F32)),
        grid=(nl, nm // tn),
        in_specs=[pl.BlockSpec((16, d), lambda l, j: (0, 0)), pl.BlockSpec((None, d, tn), lambda l, j: (l, 0, j)),
                  pl.BlockSpec((None, 1, tn), lambda l, j: (l, 0, j))],
        out_specs=(pl.BlockSpec((16, d), lambda l, j: (0, 0)), pl.BlockSpec((None, 16, tn), lambda l, j: (l, 0, j))),
        compiler_params=_params("arbitrary", "arbitrary"),
    )(c16, w_mod, b_mod)


def _wmod_grad(sc16, dm, w_mod):
    nl, d, nm = w_mod.shape
    tn = _pick(nm, 512, 128)

    def body(sc_ref, dm_ref, w_ref, gw_ref, pc_ref):
        @pl.when(jnp.logical_and(pl.program_id(0) == 0, pl.program_id(1) == 0))
        def _():
            pc_ref[...] = jnp.zeros_like(pc_ref)

        dmb = _bf(dm_ref[...])
        gw_ref[...] = _tn(_bf(sc_ref[...]), dmb)
        pc_ref[...] += _nt(dmb, _bf(w_ref[...]))

    return pl.pallas_call(
        body, name="wmod_grad",
        out_shape=(jax.ShapeDtypeStruct((nl, d, nm), F32), jax.ShapeDtypeStruct((16, d), F32)),
        grid=(nl, nm // tn),
        in_specs=[pl.BlockSpec((16, d), lambda l, j: (0, 0)), pl.BlockSpec((None, 16, tn), lambda l, j: (l, 0, j)),
                  pl.BlockSpec((None, d, tn), lambda l, j: (l, 0, j))],
        out_specs=(pl.BlockSpec((None, d, tn), lambda l, j: (l, 0, j)), pl.BlockSpec((16, d), lambda l, j: (0, 0))),
        compiler_params=_params("arbitrary", "arbitrary"),
    )(sc16, dm, w_mod)


def _rowsum(a, name):
    r, n = a.shape
    tn = _pick(n, 4096, 128)

    def body(a_ref, o_ref):
        acc = a_ref[0:1, :]
        for i in range(1, r):
            acc = acc + a_ref[i:i + 1, :]
        o_ref[...] = jnp.zeros((8, tn), F32) + acc

    return pl.pallas_call(
        body, name=name, out_shape=jax.ShapeDtypeStruct((8, n), F32), grid=(n // tn,),
        in_specs=[pl.BlockSpec((r, tn), lambda j: (0, j))], out_specs=pl.BlockSpec((8, tn), lambda j: (0, j)),
        compiler_params=_params("parallel"),
    )(a)


def _cctx_grad(parts, c_ctx8):
    _, _, d = parts.shape

    def body(p_ref, c_ref, o_ref):
        acc = ((p_ref[0] + p_ref[1]) + p_ref[2]) + p_ref[3]
        o_ref[...] = acc * _dsilu(c_ref[...])

    return pl.pallas_call(
        body, name="cctx_grad", out_shape=jax.ShapeDtypeStruct((8, d), F32),
        in_specs=[VMEM_SPEC, VMEM_SPEC], out_specs=VMEM_SPEC,
    )(parts, c_ctx8)


def _adamw(g, w, m, v, name):
    r, n = w.shape
    tr = _pick(r, max(8, min(256, (512 * 1024) // n // 8 * 8)), 8)
    bc1 = 1.0 - ADAM_B1 ** ADAM_STEP
    bc2 = 1.0 - ADAM_B2 ** ADAM_STEP

    def body(g_ref, w_ref, m_ref, v_ref, d_ref, mo_ref, vo_ref):
        gg = g_ref[...]
        mn = ADAM_B1 * m_ref[...] + (1.0 - ADAM_B1) * gg
        vn = ADAM_B2 * v_ref[...] + (1.0 - ADAM_B2) * (gg * gg)
        mo_ref[...] = mn
        vo_ref[...] = vn
        d_ref[...] = -ADAM_LR * ((mn / bc1) / (jnp.sqrt(vn / bc2) + ADAM_EPS) + ADAM_WD * w_ref[...])

    spec = pl.BlockSpec((tr, n), lambda i: (i, 0))
    shp = jax.ShapeDtypeStruct((r, n), F32)
    return pl.pallas_call(
        body, name=name, out_shape=(shp, shp, shp), grid=(r // tr,),
        in_specs=[spec, spec, spec, spec], out_specs=(spec, spec, spec),
        compiler_params=_params("parallel"),
    )(g, w, m, v)


def _adamw_layer(g, w, m, v, layer, prev, name):
    r, n = g.shape
    tr = _pick(r, max(8, min(256, (512 * 1024) // n // 8 * 8)), 8)
    nb = r // tr
    bc1 = 1.0 - ADAM_B1 ** ADAM_STEP
    bc2 = 1.0 - ADAM_B2 ** ADAM_STEP
    n_prev = 0 if prev is None else 4

    def body(g_ref, w_ref, m_ref, v_ref, *rest):
        go_ref, d_ref, mo_ref, vo_ref = rest[n_prev:]
        gg = g_ref[...]
        mn = ADAM_B1 * m_ref[...] + (1.0 - ADAM_B1) * gg
        vn = ADAM_B2 * v_ref[...] + (1.0 - ADAM_B2) * (gg * gg)
        go_ref[...] = gg
        mo_ref[...] = mn
        vo_ref[...] = vn
        d_ref[...] = -ADAM_LR * ((mn / bc1) / (jnp.sqrt(vn / bc2) + ADAM_EPS) + ADAM_WD * w_ref[...])

    spec = pl.BlockSpec((tr, n), lambda i: (layer * nb + i, 0))
    shp = jax.ShapeDtypeStruct(w.shape, F32)
    return pl.pallas_call(
        body, name=name, out_shape=(shp, shp, shp, shp), grid=(nb,),
        in_specs=[pl.BlockSpec((tr, n), lambda i: (i, 0)), spec, spec, spec] + [ANY] * n_prev,
        out_specs=(spec, spec, spec, spec),
        input_output_aliases={4 + k: k for k in range(n_prev)},
        compiler_params=_params("parallel"),
    )(g, w, m, v, *(prev or ()))


def _rope_tables(seq, n_ctx):
    f = HEAD // 4
    inv = ROPE_BASE ** (-jnp.arange(f, dtype=F32) / f)
    n_rows = seq // GRID_W
    ang_r = jnp.arange(n_rows, dtype=F32)[:, None] * inv[None, :]
    ang_c = jnp.arange(GRID_W, dtype=F32)[:, None] * inv[None, :]
    by_row = lambda a: jnp.repeat(a, GRID_W, axis=0)
    by_col = lambda a: jnp.tile(a, (n_rows, 1))
    cos_r, sin_r, cos_c, sin_c = by_row(jnp.cos(ang_r)), by_row(jnp.sin(ang_r)), by_col(jnp.cos(ang_c)), by_col(jnp.sin(ang_c))
    cos = jnp.concatenate([cos_r, cos_r, cos_c, cos_c], axis=-1)
    sin = jnp.concatenate([-sin_r, sin_r, -sin_c, sin_c], axis=-1)
    cos = jnp.concatenate([jnp.ones((n_ctx, HEAD), F32), cos], axis=0)
    sin = jnp.concatenate([jnp.zeros((n_ctx, HEAD), F32), sin], axis=0)
    return cos, sin


def _pad_rows(a, rows):
    return jnp.concatenate([a, jnp.zeros((rows - a.shape[0],) + a.shape[1:], a.dtype)], axis=0)


def _pad_flat(parts, mult):
    flat = jnp.concatenate([p.reshape(-1) for p in parts])
    pad = (-flat.shape[0]) % mult
    return jnp.concatenate([flat, jnp.zeros((pad,), flat.dtype)])


def kernel(x, c, ctx, c_ctx, norm_w, w_mod, b_mod, w_in, conv_w, conv_norm_w, ret_norm_w, ret_decay_f, ret_decay_b, w_out, final_norm_w, loss_target, m_c_ctx, m_norm_w, m_w_mod, m_b_mod, m_w_in, m_conv_w, m_conv_norm_w, m_ret_norm_w, m_ret_decay_f, m_ret_decay_b, m_w_out, m_final_norm_w, v_c_ctx, v_norm_w, v_w_mod, v_b_mod, v_w_in, v_conv_w, v_conv_norm_w, v_ret_norm_w, v_ret_decay_f, v_ret_decay_b, v_w_out, v_final_norm_w):
    xi, yi, ci = _mesh_pos()
    b_idx = 4 * xi + 2 * yi + ci
    s_idx = 2 * xi + yi
    pos = jnp.stack([ci, s_idx]).astype(jnp.int32)

    x2, ctx2, tgt = x[0], ctx[0], loss_target[0]
    seq, d = x2.shape
    n_ctx = ctx2.shape[0]
    t = seq + n_ctx
    dc = d // 2
    nh = dc // HEAD
    nl, _, nm = w_mod.shape
    n4 = w_in.shape[2]
    r4 = w_out.shape[1]
    cw4 = conv_w.shape[2]
    nct = n_ctx // ROW_TILE
    ncc = n_ctx // CHUNK
    assert n_ctx % ROW_TILE == 0 and seq % ROW_TILE == 0 and dc % HEAD == 0 and nl * 3 <= 8

    win_b, wout_b = w_in.astype(BF16), w_out.astype(BF16)
    wg_in, wg_out = [None] * nl, [None] * nl
    shard_order = jnp.stack([s_idx, 2 * (1 - xi) + yi, 2 * xi + (1 - yi), 2 * (1 - xi) + (1 - yi)]).astype(jnp.int32)

    small = jnp.concatenate([jnp.broadcast_to(c, (8, d)), _pad_rows(conv_w.reshape(nl * 3, cw4), 8)], axis=1)
    small_g = _allgather_small(small, "gather_c_convw").reshape(N_DEV, 8, d + cw4)
    c_all = small_g[:, 0, :d]
    cw_full = small_g[0::2, :, d:].transpose(1, 0, 2).reshape(8, 4 * cw4)
    cw_l = [_pad_rows(cw_full[l * 3:(l + 1) * 3], 8) for l in range(nl)]

    c16 = _pad_rows(jnp.concatenate([c_all, c_ctx[None]], axis=0), 16)
    b_shard = lax.dynamic_slice(b_mod, (0, s_idx * nm), (nl, nm)).reshape(nl, 1, nm)
    sc16, modp = _mod_matvec(c16, w_mod, b_shard)
    modg = _allgather_small(modp.reshape(nl * 16, nm), "gather_mod").reshape(N_DEV, nl, 16, nm)
    mod_full = modg[0::2].transpose(1, 2, 0, 3).reshape(nl, 16, 4 * nm)
    mod_me = lax.dynamic_index_in_dim(mod_full, b_idx, axis=1, keepdims=False)
    mod_cx = mod_full[:, 8]
    zrow = jnp.zeros((d,), F32)
    rows = []
    for l in range(nl):
        gate_c = mod_cx[l, 2 * d:] if l < nl - 1 else zrow
        rows.append(jnp.stack([mod_me[l, :d], mod_me[l, d:2 * d], mod_me[l, 2 * d:],
                               mod_cx[l, :d], mod_cx[l, d:2 * d], gate_c, zrow, zrow]))

    cos, sin = _rope_tables(seq, n_ctx)
    xs = jnp.concatenate([ctx2, x2], axis=0)
    saved = []
    for l in range(nl):
        dec = jnp.stack([ret_decay_f[l], ret_decay_b[l]])
        hx, hx_t = _modulate(xs, norm_w[l][None], rows[l], nct, f"modulate_{l}")
        u, wg_in[l], wg_out[l] = _in_proj_gathered(hx, win_b[l], shard_order, _gather_comm(wout_b[l], False),
                                                   f"in_proj_{l}")
        o_f, o_b, st_f, st_b, qr, kr = _ret_fwd(u, cos, sin, dec, ncc, f"ret_fwd_{l}")
        y, y_t, cv = _mix_fwd(u, o_f, o_b, cw_l[l], conv_norm_w[l][None], ret_norm_w[l][None], nct, f"mix_fwd_{l}")
        xs_new, proj = _out_proj(y, wg_out[l], xs, rows[l], n_ctx, f"out_proj_{l}")
        saved.append((xs, hx_t, u, o_f, o_b, st_f, st_b, qr, kr, y_t, cv, proj, dec))
        xs = xs_new

    g, facc, gp, gacc = _final_loss(xs, tgt, final_norm_w[None], (saved[nl - 1][11], rows[nl - 1]), nct, "final_loss")
    full = [None] * nl
    join = []
    mine, other = ci.reshape(1).astype(jnp.int32), (1 - ci).reshape(1).astype(jnp.int32)
    dmx, dmc, gnw, gcnw, grnw, gcw, gdf, gdb = ([None] * nl for _ in range(8))
    for l in reversed(range(nl)):
        xs_l, hx_t, u, o_f, o_b, st_f, st_b, qr, kr, y_t, cv, proj, dec = saved[l]
        gout = _matmul(y_t, gp, "nn", f"out_proj_dw_{l}", tm_t=1024, tn_t=1024, tk_t=2816, comms=join)
        if join:
            gout, *full[l + 1] = gout
        dy = _matmul(gp, wg_out[l], "nt", f"out_proj_dx_{l}", tn_t=2048)
        dab, daz, drz, dcv, do, macc = _mix_bwd(dy, u, cv, o_f, o_b, conv_norm_w[l][None], ret_norm_w[l][None],
                                                f"mix_bwd_{l}")
        dqf, dkf, dvf, dqb, dkb, dvb, dlg = _ret_bwd(u, qr, kr, do, st_f, st_b, dec, ncc, f"ret_bwd_{l}")
        du, cacc = _assemble_du(dcv, u, cw_l[l], dab, daz, drz, dqf, dqb, dkf, dkb, dvf, dvb, cos, sin, nct,
                                f"assemble_du_{l}")
        gout_v = gout.reshape(4, 2, r4 // 2, d)
        gin_other = _matmul(hx_t, du, "nn", f"in_proj_dw_other_{l}", tm_t=1024, tn_t=1024, tk_t=2816, row_half=other)
        gin_mine, land_in, land_out = _matmul(hx_t, du, "nn", f"in_proj_dw_mine_{l}", tm_t=1024, tn_t=1024,
                                              tk_t=2816, row_half=mine, comms=[_swap_comm(gin_other, gout_v)])
        p_in = _sum_half_in(gin_mine, land_in, f"grad_sum_half_in_{l}")
        p_out = _sum_half_out(pos, gout_v, land_out, f"grad_sum_half_out_{l}")
        dhx, l3_in, l3_out = _matmul(du, wg_in[l], "nt", f"in_proj_dx_{l}", tn_t=2048,
                                     comms=[_exchange_comm(p_in, p_out, n4)],
                                     out_dtype=BF16)
        join = [_join_comm(_sum_quarters_in(pos, p_in, l3_in, n4, f"grad_sum_quarters_in_{l}"),
                           _sum_quarters_out(pos, p_out, l3_out, f"grad_sum_quarters_out_{l}"))]
        dmx_gate, dmc_gate = gacc[2], gacc[5]
        below = (saved[l - 1][11], rows[l - 1]) if l > 0 else None
        g, nacc, *nxt = _modulate_bwd(dhx, xs_l, g, norm_w[l][None], rows[l], nct, f"modulate_bwd_{l}", below)
        if nxt:
            gp, gacc = nxt
        dmx[l] = jnp.concatenate([nacc[0], nacc[1], dmx_gate])
        dmc[l] = jnp.concatenate([nacc[3], nacc[4], dmc_gate])
        gnw[l], gcnw[l], grnw[l], gcw[l] = nacc[6], macc[0], macc[1], cacc[0:3]
        gdf[l], gdb[l] = dlg[0:nh, 0], dlg[nh:2 * nh, 0]
    grad_x = g[None]
    full[0] = _run_comm(join[0], "grad_join_0")

    res_in, res_out = None, None
    for l in reversed(range(nl)):
        res_in = _adamw_layer(full[l][0].reshape(d, n4), w_in.reshape(nl * d, n4), m_w_in.reshape(nl * d, n4),
                              v_w_in.reshape(nl * d, n4), l, res_in, f"adamw_w_in_{l}")
        res_out = _adamw_layer(full[l][1].reshape(r4, d), w_out.reshape(nl * r4, d), m_w_out.reshape(nl * r4, d),
                               v_w_out.reshape(nl * r4, d), l, res_out, f"adamw_w_out_{l}")
    g_w_in, d_w_in, nm_w_in, nv_w_in = res_in
    g_w_out, d_w_out, nm_w_out, nv_w_out = res_out

    seg = [jnp.stack(dmx), jnp.stack(dmc), jnp.stack(gnw), jnp.stack(gcnw), jnp.stack(grnw), facc[0],
           jnp.stack(gcw), jnp.stack(gdf), jnp.stack(gdb), facc[1, 0:1]]
    sizes = [int(s.size) for s in seg]
    offs = [sum(sizes[:i]) for i in range(len(sizes))]
    packed = _pad_flat(seg, 8 * 128)
    pw = packed.shape[0] // 8
    allv = _allgather_small(packed.reshape(8, pw), "gather_small_grads").reshape(N_DEV, 8 * pw)
    n_mod = nl * 3 * d
    dmx_all, dmc_all = allv[:, offs[0]:offs[0] + n_mod], allv[:, offs[1]:offs[1] + n_mod]
    rest_all = allv[:, offs[2]:]
    zeros_mod = jnp.zeros_like(dmc_all)
    stack = jnp.concatenate([
        jnp.concatenate([dmx_all, dmc_all], axis=0),
        jnp.concatenate([dmc_all, zeros_mod], axis=0),
        jnp.concatenate([rest_all, jnp.zeros_like(rest_all)], axis=0)], axis=1)
    sums = _rowsum(stack, "sum_small_grads")[0]
    g_b_mod = sums[:n_mod].reshape(nl, 3 * d)
    dmc_tot = sums[n_mod:2 * n_mod].reshape(nl, 3 * d)
    rest = sums[2 * n_mod:]

    def take(i):
        o = offs[i] - offs[2]
        return rest[o:o + sizes[i]]

    g_norm_w = take(2).reshape(nl, d)
    g_conv_norm_w = take(3).reshape(nl, dc)
    g_ret_norm_w = take(4).reshape(nl, dc)
    g_final_norm_w = take(5)
    g_conv_w = lax.dynamic_slice(take(6).reshape(nl, 3, dc), (0, 0, s_idx * cw4), (nl, 3, cw4))
    g_decay_f = take(7).reshape(nl, nh)
    g_decay_b = take(8).reshape(nl, nh)
    loss = take(9)[0]

    dmx_mine = lax.dynamic_slice(dmx_all.reshape(N_DEV, nl, 3 * d), (0, 0, s_idx * nm), (N_DEV, nl, nm))
    dmc_mine = lax.dynamic_slice(dmc_tot, (0, s_idx * nm), (nl, nm))
    dm = jnp.concatenate([dmx_mine.transpose(1, 0, 2), dmc_mine[:, None, :], jnp.zeros((nl, 7, nm), F32)], axis=1)
    g_w_mod, pc = _wmod_grad(sc16, dm, w_mod)
    pc_g = _allgather_small(pc[8:16], "gather_cctx_partials").reshape(N_DEV, 8, d)
    g_c_ctx = _cctx_grad(pc_g[0::2], jnp.broadcast_to(c_ctx[None], (8, d)))[0]
    d_w_mod, nm_w_mod, nv_w_mod = _adamw(g_w_mod.reshape(nl * d, nm), w_mod.reshape(nl * d, nm),
                                         m_w_mod.reshape(nl * d, nm), v_w_mod.reshape(nl * d, nm), "adamw_w_mod")

    small_g = [g_c_ctx, g_norm_w, g_b_mod, g_conv_w, g_conv_norm_w, g_ret_norm_w, g_decay_f, g_decay_b, g_final_norm_w]
    small_w = [c_ctx, norm_w, b_mod, conv_w, conv_norm_w, ret_norm_w, ret_decay_f, ret_decay_b, final_norm_w]
    small_m = [m_c_ctx, m_norm_w, m_b_mod, m_conv_w, m_conv_norm_w, m_ret_norm_w, m_ret_decay_f, m_ret_decay_b,
               m_final_norm_w]
    small_v = [v_c_ctx, v_norm_w, v_b_mod, v_conv_w, v_conv_norm_w, v_ret_norm_w, v_ret_decay_f, v_ret_decay_b,
               v_final_norm_w]
    pk = lambda parts: _pad_flat(parts, 8 * 128).reshape(-1, 128)
    sd, sm, sv = _adamw(pk(small_g), pk(small_w), pk(small_m), pk(small_v), "adamw_small")

    def unpack(flat2d):
        flat = flat2d.reshape(-1)
        out, o = [], 0
        for wgt in small_w:
            out.append(flat[o:o + wgt.size].reshape(wgt.shape))
            o += wgt.size
        return out

    sd, sm, sv = unpack(sd), unpack(sm), unpack(sv)

    def order(small, mod, w_in_, w_out_):
        return [small[0], small[1], mod, small[2], w_in_, small[3], small[4], small[5], small[6], small[7], w_out_,
                small[8]]

    grads = order(small_g, g_w_mod, g_w_in.reshape(nl, d, n4), g_w_out.reshape(nl, r4, d))
    deltas = order(sd, d_w_mod.reshape(nl, d, nm), d_w_in.reshape(nl, d, n4), d_w_out.reshape(nl, r4, d))
    new_m = order(sm, nm_w_mod.reshape(nl, d, nm), nm_w_in.reshape(nl, d, n4), nm_w_out.reshape(nl, r4, d))
    new_v = order(sv, nv_w_mod.reshape(nl, d, nm), nv_w_in.reshape(nl, d, n4), nv_w_out.reshape(nl, r4, d))
    return (loss, grad_x, *grads, *deltas, *new_m, *new_v)
```

```python
import functools

import jax
import jax.numpy as jnp
from jax import lax
from jax.experimental import pallas as pl
from jax.experimental.pallas import tpu as pltpu

F32 = jnp.float32
BF16 = jnp.bfloat16
EPS = 1e-6
CHUNK = 128
HEAD = 128
GRID_W = 64
ROPE_BASE = 10000.0
ROW_TILE = 256
ROW_CHUNK = 16
CHUNK_UNROLL = 4
RET_INTERLEAVE = 4
VMEM_LIMIT_BYTES = 56 * 1024 * 1024
N_DEV = 8
MESH = pl.DeviceIdType.MESH

ADAM_LR = 0.001
ADAM_B1 = 0.9
ADAM_B2 = 0.999
ADAM_EPS = 1e-08
ADAM_WD = 0.01
ADAM_STEP = 10

ANY = pl.BlockSpec(memory_space=pl.ANY)
VMEM_SPEC = pl.BlockSpec(memory_space=pltpu.VMEM)
SMEM_SPEC = pl.BlockSpec(memory_space=pltpu.SMEM)


def _params(*sem):
    return pltpu.CompilerParams(dimension_semantics=sem, vmem_limit_bytes=VMEM_LIMIT_BYTES)


def _pick(n, target, mult):
    best = None
    for d in range(mult, min(n, target) + 1, mult):
        if n % d == 0:
            best = d
    assert best is not None, (n, target, mult)
    return best


def _for_row_chunks(n_rows, fn):
    def step(r, carry):
        for q in range(CHUNK_UNROLL):
            fn(pl.multiple_of((r * CHUNK_UNROLL + q) * ROW_CHUNK, ROW_CHUNK), q)
        return carry

    lax.fori_loop(0, n_rows // (ROW_CHUNK * CHUNK_UNROLL), step, 0)


class _Partials:
    def __init__(self, ref, n_sums):
        self.ref, self.n_sums = ref, n_sums

    def zero(self):
        self.ref[...] = jnp.zeros_like(self.ref)

    def add(self, q, k, value, cols=slice(None)):
        r = (q * self.n_sums + k) * 8
        self.ref[r:r + 8, cols] += value

    def total(self, k):
        rows = [(q * self.n_sums + k) * 8 for q in range(CHUNK_UNROLL)]
        acc = self.ref[rows[0]:rows[0] + 8, :]
        for r in rows[1:]:
            acc = acc + self.ref[r:r + 8, :]
        return jnp.sum(acc, axis=0, keepdims=True)

    @staticmethod
    def scratch(n_sums, n):
        return pltpu.VMEM((CHUNK_UNROLL * n_sums * 8, n), F32)


def _fold8(a):
    out = a[0:8]
    for hh in range(1, ROW_CHUNK // 8):
        out = out + a[8 * hh:8 * hh + 8]
    return out


def _silu(z):
    return z * jax.nn.sigmoid(z)


def _dsilu(z):
    s = jax.nn.sigmoid(z)
    return s * (1.0 + z * (1.0 - s))


def _nn(a, b):
    return lax.dot_general(a, b, (((1,), (0,)), ((), ())), preferred_element_type=F32)


def _nt(a, b):
    return lax.dot_general(a, b, (((1,), (1,)), ((), ())), preferred_element_type=F32)


def _tn(a, b):
    return lax.dot_general(a, b, (((0,), (0,)), ((), ())), preferred_element_type=F32)


def _bf(a):
    return a.astype(BF16)


def _mesh_pos():
    return lax.axis_index("x"), lax.axis_index("y"), lax.axis_index("c")


def _allgather_small(blk, name):
    m_per, n = blk.shape

    def body(x_ref, out_ref, send_sems, recv_sems, local_sem):
        x, y, c = _mesh_pos()
        me, sibling = (x, y, c), (x, y, 1 - c)
        chips = [(1 - x, y), (x, 1 - y), (1 - x, 1 - y)]

        def rows(px, py, pc):
            return out_ref.at[pl.ds((4 * px + 2 * py + pc) * m_per, m_per), :]

        def copy(k, block, to, src=None):
            return pltpu.make_async_remote_copy(
                src_ref=rows(*block) if src is None else src, dst_ref=rows(*block),
                send_sem=send_sems.at[k], recv_sem=recv_sems.at[k], device_id=to, device_id_type=MESH)

        mine = pltpu.make_async_copy(x_ref, rows(*me), local_sem)
        mine.start()
        first = [copy(0, me, sibling, src=x_ref)]
        first += [copy(1 + j, me, (*chip, c), src=x_ref) for j, chip in enumerate(chips)]
        for cp in first:
            cp.start()
        passed = [copy(4 + j, (*chip, c), sibling) for j, chip in enumerate(chips)]
        for j, chip in enumerate(chips):
            copy(1 + j, (*chip, c), me).wait_recv()
            passed[j].start()
        copy(0, sibling, me).wait_recv()
        for j, chip in enumerate(chips):
            copy(4 + j, (*chip, 1 - c), me).wait_recv()
        for cp in first + passed:
            cp.wait_send()
        mine.wait()

    return pl.pallas_call(
        body, name=name,
        out_shape=jax.ShapeDtypeStruct((N_DEV * m_per, n), blk.dtype),
        in_specs=[VMEM_SPEC], out_specs=VMEM_SPEC,
        scratch_shapes=[pltpu.SemaphoreType.DMA((7,)), pltpu.SemaphoreType.DMA((7,)), pltpu.SemaphoreType.DMA],
    )(blk)


class _Comm:
    def __init__(self, ins, outs, n_remote, n_local, start, finish, aliases=None):
        self.ins, self.outs, self.n_remote, self.n_local = list(ins), list(outs), n_remote, n_local
        self.start, self.finish, self.aliases = start, finish, dict(aliases or {})

    def scratch(self):
        return [pltpu.SemaphoreType.DMA((self.n_remote,)), pltpu.SemaphoreType.DMA((self.n_remote,)),
                pltpu.SemaphoreType.DMA((max(self.n_local, 1),))]


def _run_comm(comm, name):
    n_in, n_out = len(comm.ins), len(comm.outs)

    def body(*refs):
        args = (refs[:n_in], refs[n_in:n_in + n_out]) + tuple(refs[n_in + n_out:])
        comm.start(*args)
        comm.finish(*args)

    return pl.pallas_call(
        body, name=name, out_shape=tuple(comm.outs), in_specs=[ANY] * n_in, out_specs=tuple([ANY] * n_out),
        scratch_shapes=comm.scratch(), input_output_aliases=comm.aliases,
    )(*comm.ins)


def _remote(src, dst, send, recv, k, to):
    return pltpu.make_async_remote_copy(src_ref=src, dst_ref=dst, send_sem=send.at[k], recv_sem=recv.at[k],
                                        device_id=to, device_id_type=MESH)


def _gather_comm(shard, by_columns):
    rows, cols = shard.shape
    hr = rows // 2

    def build(ins, outs, send, recv, loc):
        x, y, c = _mesh_pos()
        sib = (x, y, 1 - c)
        peers = [(1 - x, y), (x, 1 - y), (1 - x, 1 - y)]
        s_me = 2 * x + y

        def slot(s, h):
            if by_columns:
                return outs[0].at[pl.ds(pl.multiple_of(h * hr, 16), hr), pl.ds(pl.multiple_of(s * cols, 128), cols)]
            return outs[0].at[pl.ds(pl.multiple_of(s * rows + h * hr, 16), hr), :]

        def half(h):
            return ins[0].at[pl.ds(pl.multiple_of(h * hr, 16), hr), :]

        def local():
            if by_columns:
                mine = outs[0].at[:, pl.ds(pl.multiple_of(s_me * cols, 128), cols)]
            else:
                mine = outs[0].at[pl.ds(pl.multiple_of(s_me * rows, 16), rows), :]
            return [pltpu.make_async_copy(ins[0], mine, loc.at[0])]

        def per_copy(make):
            return [make(j, 2 * px + py, (px, py, c)) for j, (px, py) in enumerate(peers)]

        ici_send = lambda: per_copy(lambda j, s_p, to: _remote(half(c), slot(s_me, c), send, recv, j, to))
        ici_recv = lambda: per_copy(lambda j, s_p, to: _remote(half(c), slot(s_p, c), send, recv, j, to))
        d2d_send = lambda: per_copy(lambda j, s_p, to: _remote(slot(s_p, c), slot(s_p, c), send, recv, 3 + j, sib))
        d2d_recv = lambda: per_copy(lambda j, s_p, to: _remote(slot(s_p, c), slot(s_p, 1 - c), send, recv, 3 + j, sib))
        return local, ici_send, ici_recv, d2d_send, d2d_recv

    def start(*refs):
        local, ici_send, _, _, _ = build(*refs)
        for cp in local() + ici_send():
            cp.start()

    def finish(*refs):
        local, ici_send, ici_recv, d2d_send, d2d_recv = build(*refs)
        forwards = d2d_send()
        for arrived, forward in zip(ici_recv(), forwards):
            arrived.wait_recv()
            forward.start()
        for cp in d2d_recv():
            cp.wait_recv()
        for cp in ici_send() + forwards:
            cp.wait_send()
        for cp in local():
            cp.wait()

    full = (rows, 4 * cols) if by_columns else (4 * rows, cols)
    return _Comm([shard], [jax.ShapeDtypeStruct(full, BF16)], 6, 1, start, finish)


def _swap_comm(gin_other, gout_v):
    dh, w = gin_other.shape
    ns, _, rh, d = gout_v.shape

    def build(ins, outs, send, recv, loc):
        x, y, c = _mesh_pos()
        sib = (x, y, 1 - c)
        return [_remote(ins[0], outs[0], send, recv, 0, sib),
                _remote(ins[1].at[:, 1 - c], outs[1], send, recv, 1, sib)]

    def start(*refs):
        for cp in build(*refs):
            cp.start()

    def finish(*refs):
        for cp in build(*refs):
            cp.wait()

    outs = [jax.ShapeDtypeStruct((dh, w), F32), jax.ShapeDtypeStruct((ns, rh, d), F32)]
    return _Comm([gin_other, gout_v], outs, 2, 0, start, finish)


def _exchange_comm(p_in, p_out, n4):
    dh, _ = p_in.shape
    _, rh, d = p_out.shape

    def build(ins, outs, send, recv, loc):
        x, y, c = _mesh_pos()
        cps = []
        for j, (px, py) in enumerate([(1 - x, y), (x, 1 - y), (1 - x, 1 - y)]):
            s_p = 2 * px + py
            cps.append(_remote(ins[0].at[:, pl.ds(pl.multiple_of(s_p * n4, 128), n4)], outs[0].at[j], send, recv,
                               2 * j, (px, py, c)))
            cps.append(_remote(ins[1].at[s_p], outs[1].at[j], send, recv, 2 * j + 1, (px, py, c)))
        return cps

    def start(*refs):
        for cp in build(*refs):
            cp.start()

    def finish(*refs):
        cps = build(*refs)
        for cp in cps:
            cp.wait_recv()
        for cp in cps:
            cp.wait_send()

    outs = [jax.ShapeDtypeStruct((3, dh, n4), p_in.dtype), jax.ShapeDtypeStruct((3, rh, d), p_out.dtype)]
    return _Comm([p_in, p_out], outs, 6, 0, start, finish)


def _join_comm(q_in, q_out):
    def build(ins, outs, send, recv, loc):
        x, y, c = _mesh_pos()
        sib = (x, y, 1 - c)
        sends = lambda: [_remote(outs[a].at[c], outs[a].at[c], send, recv, a, sib) for a in range(2)]
        recvs = lambda: [_remote(outs[a].at[c], outs[a].at[1 - c], send, recv, a, sib) for a in range(2)]
        return sends, recvs

    def start(*refs):
        for cp in build(*refs)[0]():
            cp.start()

    def finish(*refs):
        sends, recvs = build(*refs)
        for cp in recvs():
            cp.wait_recv()
        for cp in sends():
            cp.wait_send()

    outs = [jax.ShapeDtypeStruct(q_in.shape, F32), jax.ShapeDtypeStruct(q_out.shape, F32)]
    return _Comm([q_in, q_out], outs, 2, 0, start, finish, aliases={0: 0, 1: 1})


def _sum_half_in(mine, land, name):
    dh, w = mine.shape
    tr, tc = _pick(dh, 256, 16), _pick(w, 2048, 128)

    def body(a_ref, b_ref, o_ref):
        o_ref[...] = (a_ref[...] + b_ref[...]).astype(BF16)

    spec = pl.BlockSpec((tr, tc), lambda i, j: (i, j))
    return pl.pallas_call(
        body, name=name, out_shape=jax.ShapeDtypeStruct((dh, w), BF16), grid=(dh // tr, w // tc),
        in_specs=[spec, spec], out_specs=spec, compiler_params=_params("parallel", "parallel"),
    )(mine, land)


def _sum_half_out(pos, gout_v, land, name):
    ns, _, rh, d = gout_v.shape

    def body(pos_ref, a_ref, b_ref, o_ref):
        o_ref[...] = (a_ref[...] + b_ref[...]).astype(BF16)

    return pl.pallas_call(
        body, name=name,
        out_shape=jax.ShapeDtypeStruct((ns, rh, d), BF16),
        grid_spec=pltpu.PrefetchScalarGridSpec(
            num_scalar_prefetch=1, grid=(ns,),
            in_specs=[pl.BlockSpec((None, None, rh, d), lambda s, p: (s, p[0], 0, 0)),
                      pl.BlockSpec((None, rh, d), lambda s, p: (s, 0, 0))],
            out_specs=pl.BlockSpec((None, rh, d), lambda s, p: (s, 0, 0))),
        compiler_params=_params("parallel"),
    )(pos, gout_v, land)


def _sum_quarters_in(pos, p_in, land3, n4, name):
    dh, _ = p_in.shape
    tr = _pick(dh, 128, 16)

    def body(pos_ref, a_ref, b_ref, o_ref):
        f = lambda v: v.astype(F32)
        o_ref[...] = ((f(a_ref[...]) + f(b_ref[0])) + f(b_ref[1])) + f(b_ref[2])

    return pl.pallas_call(
        body, name=name,
        out_shape=jax.ShapeDtypeStruct((2, dh, n4), F32),
        grid_spec=pltpu.PrefetchScalarGridSpec(
            num_scalar_prefetch=1, grid=(dh // tr,),
            in_specs=[pl.BlockSpec((tr, n4), lambda i, p: (i, p[1])),
                      pl.BlockSpec((3, tr, n4), lambda i, p: (0, i, 0))],
            out_specs=pl.BlockSpec((None, tr, n4), lambda i, p: (p[0], i, 0))),
        compiler_params=_params("parallel"),
    )(pos, p_in, land3)


def _sum_quarters_out(pos, p_out, land3, name):
    _, rh, d = p_out.shape

    def body(pos_ref, a_ref, b_ref, o_ref):
        f = lambda v: v.astype(F32)
        o_ref[...] = ((f(a_ref[...]) + f(b_ref[0])) + f(b_ref[1])) + f(b_ref[2])

    return pl.pallas_call(
        body, name=name,
        out_shape=jax.ShapeDtypeStruct((2, rh, d), F32),
        grid_spec=pltpu.PrefetchScalarGridSpec(
            num_scalar_prefetch=1, grid=(1,),
            in_specs=[pl.BlockSpec((None, rh, d), lambda i, p: (p[1], 0, 0)),
                      pl.BlockSpec((3, rh, d), lambda i, p: (0, 0, 0))],
            out_specs=pl.BlockSpec((None, rh, d), lambda i, p: (p[0], 0, 0))),
        compiler_params=_params("arbitrary"),
    )(pos, p_out, land3)


def _matmul(a, b, kind, name, tm_t=1056, tn_t=1024, tk_t=2048, comms=(), row_half=None, out_dtype=F32):
    if kind == "nn":
        (m, k), n = a.shape, b.shape[1]
    elif kind == "nt":
        (m, k), n = a.shape, b.shape[0]
    else:
        (k, m), n = a.shape, b.shape[1]
    if row_half is None:
        row_half = jnp.zeros((1,), jnp.int32)
    else:
        assert kind == "nn"
        m = m // 2
    tm = _pick(m, tm_t, 128 if kind == "tn" else 16)
    tn = _pick(n, tn_t, 128)
    tk = _pick(k, tk_t, 16 if kind == "tn" else 128)
    nk = k // tk
    nbm = m // tm
    grid = (nbm, n // tn, nk)
    dot = {"nn": _nn, "nt": _nt, "tn": _tn}[kind]
    a_spec = {"nn": pl.BlockSpec((tm, tk), lambda i, j, q, p: (i + p[0] * nbm, q)),
              "nt": pl.BlockSpec((tm, tk), lambda i, j, q, p: (i, q)),
              "tn": pl.BlockSpec((tk, tm), lambda i, j, q, p: (q, i))}[kind]
    b_spec = {"nn": pl.BlockSpec((tk, tn), lambda i, j, q, p: (q, j)),
              "nt": pl.BlockSpec((tn, tk), lambda i, j, q, p: (j, q)),
              "tn": pl.BlockSpec((tk, tn), lambda i, j, q, p: (q, j))}[kind]
    comm_ins = [x for cm in comms for x in cm.ins]
    comm_outs = [x for cm in comms for x in cm.outs]
    aliases, scratch = {}, []
    i_off, o_off = 3, 1
    for cm in comms:
        for ci, co in cm.aliases.items():
            aliases[i_off + ci] = o_off + co
        i_off, o_off = i_off + len(cm.ins), o_off + len(cm.outs)
        scratch += cm.scratch()

    def body(half_ref, a_ref, b_ref, *rest):
        cin = rest[:len(comm_ins)]
        o_ref = rest[len(comm_ins)]
        cout = rest[len(comm_ins) + 1:len(comm_ins) + 1 + len(comm_outs)]
        acc_ref = rest[len(comm_ins) + 1 + len(comm_outs)]
        sems = rest[len(comm_ins) + 2 + len(comm_outs):]
        ids = [pl.program_id(ax) for ax in range(3)]
        first = functools.reduce(jnp.logical_and, [ids[ax] == 0 for ax in range(3)])
        last = functools.reduce(jnp.logical_and, [ids[ax] == grid[ax] - 1 for ax in range(3)])

        def comm_args(idx):
            i0 = sum(len(cm.ins) for cm in comms[:idx])
            o0 = sum(len(cm.outs) for cm in comms[:idx])
            cm = comms[idx]
            return (cin[i0:i0 + len(cm.ins)], cout[o0:o0 + len(cm.outs)]) + tuple(sems[3 * idx:3 * idx + 3])

        if comms:
            @pl.when(first)
            def _():
                for idx, cm in enumerate(comms):
                    cm.start(*comm_args(idx))

        q = ids[2]
        p = dot(a_ref[...], b_ref[...])
        if nk == 1:
            o_ref[...] = p.astype(out_dtype)
        else:
            @pl.when(q == 0)
            def _():
                acc_ref[...] = p

            @pl.when(q > 0)
            def _():
                acc_ref[...] += p

            @pl.when(q == nk - 1)
            def _():
                o_ref[...] = acc_ref[...].astype(out_dtype)

        if comms:
            @pl.when(last)
            def _():
                for idx, cm in enumerate(comms):
                    cm.finish(*comm_args(idx))

    sem = ("arbitrary",) * 3 if comms else ("parallel", "parallel", "arbitrary")
    res = pl.pallas_call(
        body, name=name,
        out_shape=(jax.ShapeDtypeStruct((m, n), out_dtype), *comm_outs),
        grid_spec=pltpu.PrefetchScalarGridSpec(
            num_scalar_prefetch=1, grid=grid,
            in_specs=[a_spec, b_spec] + [ANY] * len(comm_ins),
            out_specs=(pl.BlockSpec((tm, tn), lambda i, j, q, p: (i, j)), *([ANY] * len(comm_outs))),
            scratch_shapes=[pltpu.VMEM((tm, tn) if nk > 1 else (8, 128), F32)] + scratch),
        input_output_aliases=aliases,
        compiler_params=_params(*sem),
    )(row_half, a, b, *comm_ins)
    return res if comms else res[0]


def _in_proj_gathered(hx, w_shard, order, comm, name, tm_t=1056, tn_t=1024):
    t, d = hx.shape
    n4 = w_shard.shape[1]
    hr = d // 2
    tm, tn = _pick(t, tm_t, 16), _pick(n4, tn_t, 128)
    ncol, nrow = n4 // tn, t // tm
    ntile = 4 * ncol
    n_ci, n_co = len(comm.ins), len(comm.outs)

    def body(order_ref, a_ref, ws_ref, *rest):
        cin, (o_ref, wg_ref), cout = rest[:n_ci], rest[n_ci:n_ci + 2], rest[n_ci + 2:n_ci + 2 + n_co]
        bbuf, fsem, send, recv, loc = rest[n_ci + 2 + n_co:n_ci + 7 + n_co]
        csems = rest[n_ci + 7 + n_co:]
        k, j, i = pl.program_id(0), pl.program_id(1), pl.program_id(2)
        n = k * ncol + j
        x, y, c = _mesh_pos()
        sib = (x, y, 1 - c)
        peers = [(1 - x, y), (x, 1 - y), (1 - x, 1 - y)]
        s_me = 2 * x + y

        def slot(s, h, jj):
            return wg_ref.at[pl.ds(pl.multiple_of(h * hr, 16), hr), pl.ds(pl.multiple_of(s * n4 + jj * tn, 128), tn)]

        def my_piece(jj):
            return ws_ref.at[pl.ds(pl.multiple_of(c * hr, 16), hr), pl.ds(jj * tn, tn)]

        def local():
            return pltpu.make_async_copy(ws_ref, wg_ref.at[:, pl.ds(pl.multiple_of(s_me * n4, 128), n4)], loc.at[0])

        def ici_send(p, jj):
            return _remote(my_piece(jj), slot(s_me, c, jj), send, recv, p * ncol + jj, (*peers[p], c))

        def ici_recv(p, jj):
            s_p = 2 * peers[p][0] + peers[p][1]
            return _remote(my_piece(jj), slot(s_p, c, jj), send, recv, p * ncol + jj, (*peers[p], c))

        def d2d_send(p, jj):
            s_p = 2 * peers[p][0] + peers[p][1]
            return _remote(slot(s_p, c, jj), slot(s_p, c, jj), send, recv, (3 + p) * ncol + jj, sib)

        def d2d_recv(p, jj):
            s_p = 2 * peers[p][0] + peers[p][1]
            return _remote(slot(s_p, c, jj), slot(s_p, 1 - c, jj), send, recv, (3 + p) * ncol + jj, sib)

        def fetch(m, own):
            col = (m % ncol) * tn
            if own:
                src = ws_ref.at[:, pl.ds(pl.multiple_of(col, 128), tn)]
            else:
                src = wg_ref.at[:, pl.ds(pl.multiple_of(order_ref[m // ncol] * n4 + col, 128), tn)]
            return pltpu.make_async_copy(src, bbuf.at[m % 2], fsem.at[m % 2])

        comm_args = (cin, cout) + tuple(csems)

        @pl.when(jnp.logical_and(i == 0, n == 0))
        def _():
            local().start()
            for p, jj in [(p, jj) for ps in ((0, 1), (2,)) for jj in range(ncol) for p in ps]:
                ici_send(p, jj).start()
            fetch(n, True).start()

        @pl.when(i == 0)
        def _():
            fetch(n, True).wait()

        @pl.when(i == max(nrow - 2, 0))
        def _():
            for m in range(ncol, ntile):
                @pl.when(n == m - 1)
                def _():
                    p, jj = m // ncol - 1, m % ncol
                    ici_recv(p, jj).wait_recv()
                    d2d_send(p, jj).start()
                    d2d_recv(p, jj).wait_recv()
                    if m == ntile - ncol:
                        comm.start(*comm_args)

            @pl.when(n + 1 < ncol)
            def _():
                fetch(n + 1, True).start()

            @pl.when(jnp.logical_and(n + 1 >= ncol, n + 1 < ntile))
            def _():
                fetch(n + 1, False).start()

        o_ref[...] = _nn(a_ref[...], bbuf[n % 2]).astype(BF16)

        @pl.when(jnp.logical_and(n == ntile - 1, i == nrow - 1))
        def _():
            for p in range(3):
                for jj in range(ncol):
                    ici_send(p, jj).wait_send()
                    d2d_send(p, jj).wait_send()
            local().wait()
            comm.finish(*comm_args)

    res = pl.pallas_call(
        body, name=name,
        out_shape=(jax.ShapeDtypeStruct((t, 4 * n4), BF16), jax.ShapeDtypeStruct((d, 4 * n4), BF16), *comm.outs),
        grid_spec=pltpu.PrefetchScalarGridSpec(
            num_scalar_prefetch=1, grid=(4, ncol, nrow),
            in_specs=[pl.BlockSpec((tm, d), lambda k, j, i, p: (i, 0)), ANY] + [ANY] * n_ci,
            out_specs=(pl.BlockSpec((tm, tn), lambda k, j, i, p: (i, p[k] * ncol + j)), ANY, *([ANY] * n_co)),
            scratch_shapes=[pltpu.VMEM((2, d, tn), BF16), pltpu.SemaphoreType.DMA((2,)),
                            pltpu.SemaphoreType.DMA((6 * ncol,)), pltpu.SemaphoreType.DMA((6 * ncol,)),
                            pltpu.SemaphoreType.DMA((1,))] + comm.scratch()),
        compiler_params=_params("arbitrary", "arbitrary", "arbitrary"),
    )(order, hx, w_shard, *comm.ins)
    return res


def _out_proj(y, w_out, xs, rows, n_ctx, name):
    t, k = y.shape
    d = w_out.shape[1]
    tm, tn = _pick(t, 1056, 16), _pick(d, 1024, 128)

    n_sub = 2 if tm % 32 == 0 else 1
    ts = tm // n_sub

    def body(y_ref, w_ref, x_ref, r_ref, xo_ref, p_ref):
        i = pl.program_id(0)
        for sub in range(n_sub):
            rows = slice(sub * ts, (sub + 1) * ts)
            p = _nn(y_ref[rows, :], w_ref[...])
            ridx = i * tm + sub * ts + lax.broadcasted_iota(jnp.int32, (ts, 1), 0)
            gate = jnp.where(ridx < n_ctx, r_ref[5:6, :], r_ref[2:3, :])
            p_ref[rows, :] = p.astype(BF16)
            xo_ref[rows, :] = x_ref[rows, :] + gate * p

    return pl.pallas_call(
        body, name=name,
        out_shape=(jax.ShapeDtypeStruct((t, d), F32), jax.ShapeDtypeStruct((t, d), BF16)),
        grid=(t // tm, d // tn),
        in_specs=[pl.BlockSpec((tm, k), lambda i, j: (i, 0)), pl.BlockSpec((k, tn), lambda i, j: (0, j)),
                  pl.BlockSpec((tm, tn), lambda i, j: (i, j)), pl.BlockSpec((8, tn), lambda i, j: (0, j))],
        out_specs=(pl.BlockSpec((tm, tn), lambda i, j: (i, j)), pl.BlockSpec((tm, tn), lambda i, j: (i, j))),
        compiler_params=_params("parallel", "parallel"),
    )(y, w_out, xs, rows)


def _modulate(xs, nw, rows, nct, name):
    t, d = xs.shape
    tr = ROW_TILE

    def body(x_ref, nw_ref, r_ref, o_ref, ot_ref):
        is_ctx = pl.program_id(0) < nct
        x = x_ref[...]
        xn = (x * lax.rsqrt(jnp.mean(x * x, axis=-1, keepdims=True) + EPS)) * nw_ref[...]
        shift = jnp.where(is_ctx, r_ref[3:4, :], r_ref[0:1, :])
        scale = jnp.where(is_ctx, r_ref[4:5, :], r_ref[1:2, :])
        hx = xn * (1.0 + scale) + shift
        o_ref[...] = hx.astype(BF16)
        ot_ref[...] = hx.T.astype(BF16)

    return pl.pallas_call(
        body, name=name,
        out_shape=(jax.ShapeDtypeStruct((t, d), BF16), jax.ShapeDtypeStruct((d, t), BF16)), grid=(t // tr,),
        in_specs=[pl.BlockSpec((tr, d), lambda i: (i, 0)), pl.BlockSpec((1, d), lambda i: (0, 0)),
                  pl.BlockSpec((8, d), lambda i: (0, 0))],
        out_specs=(pl.BlockSpec((tr, d), lambda i: (i, 0)), pl.BlockSpec((d, tr), lambda i: (0, i))),
        compiler_params=_params("parallel"),
    )(xs, nw, rows)


def _modulate_bwd(dhx, xs, g, nw, rows, nct, name, below=None):
    t, d = xs.shape
    tr = ROW_TILE
    latent_only = below is None
    t_out = t - nct * tr if latent_only else t
    out_row = (lambda i: (jnp.maximum(i - nct, 0), 0)) if latent_only else (lambda i: (i, 0))
    n_below = 0 if latent_only else 2

    def body(dh_ref, x_ref, g_ref, nw_ref, r_ref, *rest):
        go_ref, acc_ref = rest[n_below:n_below + 2]
        part = _Partials(rest[-1], 4)
        i = pl.program_id(0)
        is_ctx = i < nct

        @pl.when(i == 0)
        def _():
            acc_ref[...] = jnp.zeros_like(acc_ref)
            if n_below:
                rest[n_below + 3][...] = jnp.zeros_like(rest[n_below + 3])

        part.zero()

        def chunk(r0, q):
            dh16 = dh_ref[pl.ds(r0, ROW_CHUNK), :].astype(F32)
            one_scale = 1.0 + jnp.where(is_ctx, r_ref[4:5, :], r_ref[1:2, :])
            nwv = nw_ref[...]
            if n_below:
                p16 = rest[0][pl.ds(r0, ROW_CHUNK), :].astype(F32)
                gate = jnp.where(is_ctx, rest[1][5:6, :], rest[1][2:3, :])
            gps = []
            for hh in range(ROW_CHUNK // 8):
                rows8 = pl.ds(r0 + 8 * hh, 8)
                x, dh = x_ref[rows8, :], dh16[8 * hh:8 * hh + 8]
                r = lax.rsqrt(jnp.mean(x * x, axis=-1, keepdims=True) + EPS)
                xh = x * r
                part.add(q, 0, dh)
                part.add(q, 1, dh * (xh * nwv))
                dxn = dh * one_scale
                part.add(q, 2, dxn * xh)
                dxh = dxn * nwv
                g_in = g_ref[rows8, :] + r * (dxh - xh * jnp.mean(dxh * xh, axis=-1, keepdims=True))
                go_ref[rows8, :] = g_in
                if n_below:
                    gps.append(g_in * gate)
                    part.add(q, 3, g_in * p16[8 * hh:8 * hh + 8])
            if n_below:
                rest[n_below + 2][pl.ds(r0, ROW_CHUNK), :] = jnp.concatenate(gps, axis=0).astype(BF16)

        _for_row_chunks(tr, chunk)
        total = part.total
        d_shift, d_scale = total(0), total(1)
        acc_ref[6:7, :] += total(2)
        zero = jnp.zeros_like(d_shift)
        acc_ref[0:1, :] += jnp.where(is_ctx, zero, d_shift)
        acc_ref[1:2, :] += jnp.where(is_ctx, zero, d_scale)
        acc_ref[3:4, :] += jnp.where(is_ctx, d_shift, zero)
        acc_ref[4:5, :] += jnp.where(is_ctx, d_scale, zero)
        if n_below:
            gacc_ref, s = rest[n_below + 3], total(3)
            gacc_ref[2:3, :] += jnp.where(is_ctx, zero, s)
            gacc_ref[5:6, :] += jnp.where(is_ctx, s, zero)

    row = pl.BlockSpec((tr, d), lambda i: (i, 0))
    vecs = pl.BlockSpec((8, d), lambda i: (0, 0))
    return pl.pallas_call(
        body, name=name,
        out_shape=(jax.ShapeDtypeStruct((t_out, d), F32), jax.ShapeDtypeStruct((8, d), F32))
        + ((jax.ShapeDtypeStruct((t, d), BF16), jax.ShapeDtypeStruct((8, d), F32)) if n_below else ()),
        grid=(t // tr,),
        in_specs=[row, row, row, pl.BlockSpec((1, d), lambda i: (0, 0)), vecs] + ([row, vecs] if n_below else []),
        out_specs=(pl.BlockSpec((tr, d), out_row), vecs) + ((row, vecs) if n_below else ()),
        scratch_shapes=[_Partials.scratch(4, d)],
        compiler_params=_params("arbitrary"),
    )(dhx, xs, g, nw, rows, *(below or ()))


def _final_loss(xs, tgt, fw, below, nct, name):
    t, d = xs.shape
    tr = ROW_TILE

    def body(x_ref, t_ref, w_ref, p_ref, r_ref, g_ref, acc_ref, gp_ref, gacc_ref, part_ref):
        part = _Partials(part_ref, 3)
        i = pl.program_id(0)
        is_ctx = i < nct

        @pl.when(i == 0)
        def _():
            acc_ref[...] = jnp.zeros_like(acc_ref)
            gacc_ref[...] = jnp.zeros_like(gacc_ref)

        @pl.when(is_ctx)
        def _():
            g_ref[...] = jnp.zeros_like(g_ref)
            gp_ref[...] = jnp.zeros_like(gp_ref)

        @pl.when(jnp.logical_not(is_ctx))
        def _():
            part.zero()

            def chunk(r0, q):
                p16 = p_ref[pl.ds(r0, ROW_CHUNK), :].astype(F32)
                wv, gate = w_ref[...], r_ref[2:3, :]
                gps = []
                for hh in range(ROW_CHUNK // 8):
                    rows8 = pl.ds(r0 + 8 * hh, 8)
                    x = x_ref[rows8, :]
                    r = lax.rsqrt(jnp.mean(x * x, axis=-1, keepdims=True) + EPS)
                    xh = x * r
                    e = xh * wv - t_ref[rows8, :]
                    part.add(q, 1, e * e)
                    dy = e * (1.0 / d)
                    part.add(q, 0, dy * xh)
                    dxh = dy * wv
                    g = r * (dxh - xh * jnp.mean(dxh * xh, axis=-1, keepdims=True))
                    g_ref[rows8, :] = g
                    gps.append(g * gate)
                    part.add(q, 2, g * p16[8 * hh:8 * hh + 8])
                gp_ref[pl.ds(r0, ROW_CHUNK), :] = jnp.concatenate(gps, axis=0).astype(BF16)

            _for_row_chunks(tr, chunk)
            acc_ref[0:1, :] += part.total(0)
            acc_ref[1:2, :] += jnp.zeros((1, d), F32) + jnp.sum(part.total(1)) * (0.5 / d)
            gacc_ref[2:3, :] += part.total(2)

    row = pl.BlockSpec((tr, d), lambda i: (i, 0))
    vecs = pl.BlockSpec((8, d), lambda i: (0, 0))
    return pl.pallas_call(
        body, name=name,
        out_shape=(jax.ShapeDtypeStruct((t, d), F32), jax.ShapeDtypeStruct((8, d), F32),
                   jax.ShapeDtypeStruct((t, d), BF16), jax.ShapeDtypeStruct((8, d), F32)), grid=(t // tr,),
        in_specs=[row, pl.BlockSpec((tr, d), lambda i: (jnp.maximum(i - nct, 0), 0)),
                  pl.BlockSpec((1, d), lambda i: (0, 0)), row, vecs],
        out_specs=(row, vecs, row, vecs),
        scratch_shapes=[_Partials.scratch(3, d)],
        compiler_params=_params("arbitrary"),
    )(xs, tgt, fw, *below)


def _seq_edges(i, nct, nt):
    has_prev = jnp.logical_and(i != 0, i != nct)
    has_next = jnp.logical_and(i != nct - 1, i != nt - 1)
    return has_prev, has_next


def _shift_rows(a, before, after):
    tr = a.shape[0]
    ridx = lax.broadcasted_iota(jnp.int32, (tr, 1), 0)
    down = jnp.where(ridx == 0, before, pltpu.roll(a, 1, 0))
    up = jnp.where(ridx == tr - 1, after, pltpu.roll(a, tr - 1, 0))
    return down, up


def _halo_specs(tr, dc, col, t, hr):
    per = tr // hr
    prev = pl.BlockSpec((hr, dc), lambda i: (jnp.maximum(i * per - 1, 0), col))
    nxt = pl.BlockSpec((hr, dc), lambda i: (jnp.minimum((i + 1) * per, t // hr - 1), col))
    return prev, nxt


def _mix_fwd(u, o_f, o_b, cw, cnw, rnw, nct, name):
    t, w = u.shape
    dc = w // 8
    nh = dc // HEAD
    tr = ROW_TILE
    nt = t // tr
    col = lambda cidx: pl.BlockSpec((tr, dc), lambda i: (i, cidx))
    hr = 16
    acp, acn = _halo_specs(tr, dc, 2, t, hr)
    ahp, ahn = _halo_specs(tr, dc, 0, t, hr)
    f32 = lambda v: v.astype(F32)

    def body(ah, ab, ac, az, rz, acp_r, acn_r, ahp_r, ahn_r, of_r, ob_r, cw_r, cnw_r, rnw_r, y_ref, yt_ref, cv_ref):
        i = pl.program_id(0)
        has_prev, has_next = _seq_edges(i, nct, nt)
        ch = f32(ac[...]) * f32(ah[...])
        before = jnp.where(has_prev, f32(acp_r[hr - 1:hr, :]) * f32(ahp_r[hr - 1:hr, :]), 0.0)
        after = jnp.where(has_next, f32(acn_r[0:1, :]) * f32(ahn_r[0:1, :]), 0.0)
        down, up = _shift_rows(ch, before, after)
        cv = down * cw_r[0:1, :] + ch * cw_r[1:2, :] + up * cw_r[2:3, :]
        cv_ref[...] = cv.astype(BF16)
        y1 = f32(ab[...]) * cv
        yn = (y1 * lax.rsqrt(jnp.mean(y1 * y1, axis=-1, keepdims=True) + EPS)) * cnw_r[...]
        yc = _silu(f32(az[...])) * yn
        y_ref[:, 0:dc] = yc.astype(BF16)
        yt_ref[0:dc, :] = yc.T.astype(BF16)
        for h in range(nh):
            sl = slice(h * HEAD, (h + 1) * HEAD)
            o = f32(of_r[:, sl]) + f32(ob_r[:, sl])
            dev = o - jnp.mean(o, axis=-1, keepdims=True)
            on = dev * lax.rsqrt(jnp.mean(dev * dev, axis=-1, keepdims=True) + EPS)
            yr = _silu(f32(rz[:, sl])) * (on * rnw_r[:, sl])
            y_ref[:, dc + h * HEAD:dc + (h + 1) * HEAD] = yr.astype(BF16)
            yt_ref[dc + h * HEAD:dc + (h + 1) * HEAD, :] = yr.T.astype(BF16)

    vec = pl.BlockSpec((1, dc), lambda i: (0, 0))
    return pl.pallas_call(
        body, name=name,
        out_shape=(jax.ShapeDtypeStruct((t, 2 * dc), BF16), jax.ShapeDtypeStruct((2 * dc, t), BF16),
                   jax.ShapeDtypeStruct((t, dc), BF16)), grid=(nt,),
        in_specs=[col(0), col(1), col(2), col(3), col(7), acp, acn, ahp, ahn,
                  pl.BlockSpec((tr, dc), lambda i: (i, 0)), pl.BlockSpec((tr, dc), lambda i: (i, 0)),
                  pl.BlockSpec((8, dc), lambda i: (0, 0)), vec, vec],
        out_specs=(pl.BlockSpec((tr, 2 * dc), lambda i: (i, 0)), pl.BlockSpec((2 * dc, tr), lambda i: (0, i)),
                   pl.BlockSpec((tr, dc), lambda i: (i, 0))),
        compiler_params=_params("parallel"),
    )(u, u, u, u, u, u, u, u, u, o_f, o_b, cw, cnw, rnw)


def _mix_bwd(dy, u, cv, o_f, o_b, cnw, rnw, name):
    t, w = u.shape
    dc = w // 8
    nh = dc // HEAD
    tr = ROW_TILE
    col = lambda cidx: pl.BlockSpec((tr, dc), lambda i: (i, cidx))

    def body(dyc, dyr, ab, az, rz, cv_r, of_r, ob_r, cnw_r, rnw_r, dab, daz, drz, dcv, do, acc_ref, part_ref):
        part = _Partials(part_ref, 1)

        @pl.when(pl.program_id(0) == 0)
        def _():
            acc_ref[...] = jnp.zeros_like(acc_ref)

        part.zero()

        def chunk(r0, q):
            rows = pl.ds(r0, ROW_CHUNK)
            z, b, c_out = az[rows, :].astype(F32), ab[rows, :].astype(F32), cv_r[rows, :].astype(F32)
            nwv = cnw_r[...]
            y1 = b * c_out
            r = lax.rsqrt(jnp.mean(y1 * y1, axis=-1, keepdims=True) + EPS)
            yh = y1 * r
            dyc_v = dyc[rows, :].astype(F32)
            sg = jax.nn.sigmoid(z)
            daz[rows, :] = (dyc_v * (yh * nwv) * (sg * (1.0 + z * (1.0 - sg)))).astype(BF16)
            dyn = dyc_v * (z * sg)
            part.add(q, 0, _fold8(dyn * yh))
            dyh = dyn * nwv
            dy1 = r * (dyh - yh * jnp.mean(dyh * yh, axis=-1, keepdims=True))
            dab[rows, :] = (dy1 * c_out).astype(BF16)
            dcv[rows, :] = dy1 * b

        _for_row_chunks(tr, chunk)
        acc_ref[0:1, :] += part.total(0)
        for h in range(nh):
            sl = slice(h * HEAD, (h + 1) * HEAD)
            o = of_r[:, sl].astype(F32) + ob_r[:, sl].astype(F32)
            dev = o - jnp.mean(o, axis=-1, keepdims=True)
            rs = lax.rsqrt(jnp.mean(dev * dev, axis=-1, keepdims=True) + EPS)
            on = dev * rs
            zz, dyr_v, gw = rz[:, sl].astype(F32), dyr[:, sl].astype(F32), rnw_r[:, sl]
            sg = jax.nn.sigmoid(zz)
            drz[:, sl] = (dyr_v * (on * gw) * (sg * (1.0 + zz * (1.0 - sg)))).astype(BF16)
            dong = dyr_v * (zz * sg)
            acc_ref[1:2, sl] += jnp.sum(dong * on, axis=0, keepdims=True)
            don = dong * gw
            do[:, sl] = (rs * (don - jnp.mean(don, axis=-1, keepdims=True)
                               - on * jnp.mean(don * on, axis=-1, keepdims=True))).astype(BF16)

    vec = pl.BlockSpec((1, dc), lambda i: (0, 0))
    row = pl.BlockSpec((tr, dc), lambda i: (i, 0))
    return pl.pallas_call(
        body, name=name,
        out_shape=(jax.ShapeDtypeStruct((t, dc), BF16), jax.ShapeDtypeStruct((t, dc), BF16),
                   jax.ShapeDtypeStruct((t, dc), BF16), jax.ShapeDtypeStruct((t, dc), F32),
                   jax.ShapeDtypeStruct((t, dc), BF16), jax.ShapeDtypeStruct((8, dc), F32)),
        grid=(t // tr,),
        in_specs=[col(0), col(1), col(1), col(3), col(7), row, row, row, vec, vec],
        out_specs=(row, row, row, row, row, pl.BlockSpec((8, dc), lambda i: (0, 0))),
        scratch_shapes=[_Partials.scratch(1, dc)],
        compiler_params=_params("arbitrary"),
    )(dy, dy, u, u, u, cv, o_f, o_b, cnw, rnw)


def _rope_masks():
    lane = lax.broadcasted_iota(jnp.int32, (1, HEAD), 1)
    return (lane % (HEAD // 2)) >= (HEAD // 4)


def _rot(a, upper):
    return jnp.where(upper, pltpu.roll(a, HEAD // 4, 1), pltpu.roll(a, HEAD - HEAD // 4, 1))


def _assemble_du(dcv, u, cw, dab, daz, drz, dq_f, dq_b, dk_f, dk_b, dv_f, dv_b, cos, sin, nct, name):
    t, w = u.shape
    dc = w // 8
    nh = dc // HEAD
    tr = ROW_TILE
    nt = t // tr
    k_scale = HEAD ** -0.5
    col = lambda cidx: pl.BlockSpec((tr, dc), lambda i: (i, cidx))
    row = pl.BlockSpec((tr, dc), lambda i: (i, 0))
    dcp, dcn = _halo_specs(tr, dc, 0, t, 8)
    f32 = lambda v: v.astype(F32)

    def body(dcv_r, dcp_r, dcn_r, ah, ac, cw_r, dab_r, daz_r, drz_r, dqf, dqb, dkf, dkb, dvf, dvb, cos_r, sin_r,
             du, acc_ref, pad, part_ref):
        part = _Partials(part_ref, 3)
        i = pl.program_id(0)

        @pl.when(i == 0)
        def _():
            acc_ref[...] = jnp.zeros_like(acc_ref)

        has_prev, has_next = _seq_edges(i, nct, nt)
        pad[0:8, :] = jnp.where(has_prev, dcp_r[...], 0.0)
        pad[8:8 + tr, :] = dcv_r[...]
        pad[8 + tr:16 + tr, :] = jnp.where(has_next, dcn_r[...], 0.0)
        part.zero()
        upper = _rope_masks()

        def chunk(r0, q):
            rows = pl.ds(r0, ROW_CHUNK)
            d_c = pad[pl.ds(pl.multiple_of(r0 + 8, 8), ROW_CHUNK), :]
            d_prev, d_next = _shift_rows(d_c, pad[pl.ds(r0, 8), :][7:8],
                                         pad[pl.ds(pl.multiple_of(r0 + 8 + ROW_CHUNK, 8), 8), :][0:1])
            a_h, a_c = f32(ah[rows, :]), f32(ac[rows, :])
            ch = a_c * a_h
            dch = d_next * cw_r[0:1, :] + d_c * cw_r[1:2, :] + d_prev * cw_r[2:3, :]
            part.add(q, 0, _fold8(ch * d_next))
            part.add(q, 1, _fold8(ch * d_c))
            part.add(q, 2, _fold8(ch * d_prev))
            du[rows, 0:dc] = (dch * a_c).astype(BF16)
            du[rows, dc:2 * dc] = dab_r[rows, :]
            du[rows, 2 * dc:3 * dc] = (dch * a_h).astype(BF16)
            du[rows, 3 * dc:4 * dc] = daz_r[rows, :]
            du[rows, 7 * dc:8 * dc] = drz_r[rows, :]
            du[rows, 6 * dc:7 * dc] = (f32(dvf[rows, :]) + f32(dvb[rows, :])).astype(BF16)
            cs, sn = cos_r[rows, :], sin_r[rows, :]
            for h in range(nh):
                sl = slice(h * HEAD, (h + 1) * HEAD)
                dq = f32(dqf[rows, sl]) + f32(dqb[rows, sl])
                dk = (f32(dkf[rows, sl]) + f32(dkb[rows, sl])) * k_scale
                du[rows, 4 * dc + h * HEAD:4 * dc + (h + 1) * HEAD] = (dq * cs + _rot(dq * sn, upper)).astype(BF16)
                du[rows, 5 * dc + h * HEAD:5 * dc + (h + 1) * HEAD] = (dk * cs + _rot(dk * sn, upper)).astype(BF16)

        _for_row_chunks(tr, chunk)
        for tap in range(3):
            acc_ref[tap:tap + 1, :] += part.total(tap)

    tab = pl.BlockSpec((tr, HEAD), lambda i: (i, 0))
    return pl.pallas_call(
        body, name=name,
        out_shape=(jax.ShapeDtypeStruct((t, w), BF16), jax.ShapeDtypeStruct((8, dc), F32)), grid=(nt,),
        in_specs=[row, dcp, dcn, col(0), col(2), pl.BlockSpec((8, dc), lambda i: (0, 0)),
                  row, row, row, row, row, row, row, row, row, tab, tab],
        out_specs=(pl.BlockSpec((tr, w), lambda i: (i, 0)), pl.BlockSpec((8, dc), lambda i: (0, 0))),
        scratch_shapes=[pltpu.VMEM((tr + 16, dc), F32), _Partials.scratch(3, dc)],
        compiler_params=_params("arbitrary"),
    )(dcv, dcv, dcv, u, u, cw, dab, daz, drz, dq_f, dq_b, dk_f, dk_b, dv_f, dv_b, cos, sin)


def _bwd_dir_chunk(s, ncc, ns):
    return jnp.where(s < ncc, ncc - 1 - s, ns + ncc - 1 - s)


def _position_weights(d):
    c = CHUNK
    ii = lax.broadcasted_iota(jnp.int32, (c, c), 0)
    jj = lax.broadcasted_iota(jnp.int32, (c, c), 1)
    sd = ((ii - jj) if d == 0 else (jj - ii)).astype(F32)
    pos = lax.broadcasted_iota(jnp.int32, (c, HEAD), 0).astype(F32)
    qw, kw = (pos + 1.0, (c - 1.0) - pos) if d == 0 else (c - pos, pos)
    return sd, qw, kw


def _fill_decay_tables(dec_ref, nh, dm_t, qd_t, kd_t, gc_t):
    for d in range(2):
        sd, qw, kw = _position_weights(d)
        for h in range(nh):
            r = d * nh + h
            lg = -jnp.exp(jnp.zeros((1, HEAD), F32) + dec_ref[d, h])
            dm_t[r] = jnp.where(sd >= 0, jnp.exp(lg * jnp.maximum(sd, 0.0)), 0.0)
            qd_t[r] = jnp.exp(lg * qw)
            kd_t[r] = jnp.exp(lg * kw)
            gc_t[r] = jnp.zeros((8, HEAD), F32) + jnp.exp(lg * float(CHUNK))


def _decay_table_scratch(nh):
    return [pltpu.VMEM((2 * nh, CHUNK, CHUNK), F32), pltpu.VMEM((2 * nh, CHUNK, HEAD), F32),
            pltpu.VMEM((2 * nh, CHUNK, HEAD), F32), pltpu.VMEM((2 * nh, 8, HEAD), F32)]


def _ret_fwd(u, cos, sin, dec, ncc, name):
    t, w = u.shape
    dc = w // 8
    nh = dc // HEAD
    c = CHUNK
    ns = t // c
    k_scale = HEAD ** -0.5
    fmap = lambda g: (lambda s: (s, g))
    bmap = lambda g: (lambda s: (_bwd_dir_chunk(s, ncc, ns), g))
    blk = lambda m: pl.BlockSpec((c, dc), m)
    tab = lambda m: pl.BlockSpec((c, HEAD), m)
    st_f = pl.BlockSpec((nh, None, HEAD, HEAD), lambda s: (0, s, 0, 0))
    st_b = pl.BlockSpec((nh, None, HEAD, HEAD), lambda s: (0, _bwd_dir_chunk(s, ncc, ns), 0, 0))

    def body(qf, kf, vf, qb, kb, vb, cf, sf, cb, sb, dec_ref, of_ref, ob_ref, stf_ref, stb_ref, qr_ref, kr_ref,
             s_f, s_b, dm_t, qd_t, kd_t, gc_t):
        @pl.when(pl.program_id(0) == 0)
        def _():
            s_f[...] = jnp.zeros_like(s_f)
            s_b[...] = jnp.zeros_like(s_b)
            _fill_decay_tables(dec_ref, nh, dm_t, qd_t, kd_t, gc_t)

        upper = _rope_masks()
        dirs = ((qf, kf, vf, cf, sf, of_ref, stf_ref, s_f), (qb, kb, vb, cb, sb, ob_ref, stb_ref, s_b))
        group = RET_INTERLEAVE if nh % RET_INTERLEAVE == 0 else 1
        blocks = [[(h, d) for h in range(h0, h0 + group) for d in range(2)] for h0 in range(0, nh, group)]
        for blk_list in blocks:
            first = []
            for h, d in blk_list:
                q_r, k_r, v_r, cs_r, sn_r, _, st_ref, st = dirs[d]
                sl = slice(h * HEAD, (h + 1) * HEAD)
                r = d * nh + h
                cs, sn = cs_r[...], sn_r[...]
                q = q_r[:, sl].astype(F32)
                q = q * cs + _rot(q, upper) * sn
                k = k_r[:, sl].astype(F32)
                k = (k * cs + _rot(k, upper) * sn) * k_scale
                if d == 0:
                    qr_ref[:, sl] = q
                    kr_ref[:, sl] = k
                vv = v_r[:, sl]
                state = st[h]
                st_ref[h] = state
                p = _nt(_bf(q), _bf(k)) * dm_t[r]
                kv = _tn(_bf(k * kd_t[r]), vv)
                first.append((q, vv, state, p, kv))
            for idx, (h, d) in enumerate(blk_list):
                _, _, _, _, _, o_ref, _, st = dirs[d]
                sl = slice(h * HEAD, (h + 1) * HEAD)
                r = d * nh + h
                q, vv, state, p, kv = first[idx]
                o_ref[:, sl] = _bf(_nn(jnp.concatenate([_bf(p), _bf(q * qd_t[r])], axis=1),
                                       jnp.concatenate([vv, _bf(state)], axis=0)))
                st[h] = gc_t[r, 0:1, :] * state + kv

    return pl.pallas_call(
        body, name=name,
        out_shape=(jax.ShapeDtypeStruct((t, dc), BF16), jax.ShapeDtypeStruct((t, dc), BF16),
                   jax.ShapeDtypeStruct((nh, ns, HEAD, HEAD), F32), jax.ShapeDtypeStruct((nh, ns, HEAD, HEAD), F32),
                   jax.ShapeDtypeStruct((t, dc), F32), jax.ShapeDtypeStruct((t, dc), F32)),
        grid=(ns,),
        in_specs=[blk(fmap(4)), blk(fmap(5)), blk(fmap(6)), blk(bmap(4)), blk(bmap(5)), blk(bmap(6)),
                  tab(fmap(0)), tab(fmap(0)), tab(bmap(0)), tab(bmap(0)), SMEM_SPEC],
        out_specs=(blk(fmap(0)), blk(bmap(0)), st_f, st_b, blk(fmap(0)), blk(fmap(0))),
        scratch_shapes=[pltpu.VMEM((nh, HEAD, HEAD), F32), pltpu.VMEM((nh, HEAD, HEAD), F32)]
        + _decay_table_scratch(nh),
        compiler_params=_params("arbitrary"),
    )(u, u, u, u, u, u, cos, sin, cos, sin, dec)


def _ret_bwd(u, qr, kr, do, st_f, st_b, dec, ncc, name):
    t, w = u.shape
    dc = w // 8
    nh = dc // HEAD
    c = CHUNK
    ns = t // c
    fchunk = lambda s: ns - 1 - s
    bchunk = lambda s: _bwd_dir_chunk(ns - 1 - s, ncc, ns)
    fmap = lambda g: (lambda s: (fchunk(s), g))
    bmap = lambda g: (lambda s: (bchunk(s), g))
    blk = lambda m: pl.BlockSpec((c, dc), m)
    stf_spec = pl.BlockSpec((nh, None, HEAD, HEAD), lambda s: (0, fchunk(s), 0, 0))
    stb_spec = pl.BlockSpec((nh, None, HEAD, HEAD), lambda s: (0, bchunk(s), 0, 0))

    def body(qf, kf, vf, dof, qb, kb, vb, dob, stf_ref, stb_ref, dec_ref,
             dqf, dkf, dvf, dqb, dkb, dvb, dlg_ref, ds_f, ds_b, dm_t, qd_t, kd_t, gc_t, lg_acc):
        step = pl.program_id(0)

        @pl.when(step == 0)
        def _():
            ds_f[...] = jnp.zeros_like(ds_f)
            ds_b[...] = jnp.zeros_like(ds_b)
            lg_acc[...] = jnp.zeros_like(lg_acc)
            _fill_decay_tables(dec_ref, nh, dm_t, qd_t, kd_t, gc_t)

        dirs = ((qf, kf, vf, dof, stf_ref, dqf, dkf, dvf, ds_f), (qb, kb, vb, dob, stb_ref, dqb, dkb, dvb, ds_b))
        weights = [_position_weights(d) for d in range(2)]
        group = RET_INTERLEAVE if nh % RET_INTERLEAVE == 0 else 1
        blocks = [[(h, d) for h in range(h0, h0 + group) for d in range(2)] for h0 in range(0, nh, group)]
        for blk_list in blocks:
            first = []
            for h, d in blk_list:
                q_r, k_r, v_r, do_r, st_ref, _, _, _, ds = dirs[d]
                sl = slice(h * HEAD, (h + 1) * HEAD)
                r = d * nh + h
                q, k = q_r[:, sl], k_r[:, sl]
                qb16, kb16, vb16, dob16 = _bf(q), _bf(k), v_r[:, sl], _bf(do_r[:, sl])
                state, dstate = st_ref[h], ds[h]
                dsb16 = _bf(dstate)
                a = _nt(qb16, kb16) * dm_t[r]
                both = _nt(dob16, jnp.concatenate([vb16, _bf(state)], axis=0))
                dk_i = _nt(vb16, dsb16) * kd_t[r]
                first.append((q, k, qb16, kb16, dob16, state, dstate, dsb16, a, both, dk_i))
            for idx, (h, d) in enumerate(blk_list):
                _, _, _, _, _, dq_ref, dk_ref, dv_ref, ds = dirs[d]
                sl = slice(h * HEAD, (h + 1) * HEAD)
                r = d * nh + h
                q, k, qb16, kb16, dob16, state, dstate, dsb16, a, both, dk_i = first[idx]
                sd, qw, kw = weights[d]
                da = both[:, 0:c]
                dp = _bf(da * dm_t[r])
                dq_i = both[:, c:2 * c] * qd_t[r]
                dq_ref[:, sl] = _bf(_nn(dp, kb16) + dq_i)
                dk_ref[:, sl] = _bf(_tn(dp, qb16) + dk_i)
                dv_ref[:, sl] = _bf(_nn(jnp.concatenate([_bf(a.T), _bf(k * kd_t[r])], axis=1),
                                        jnp.concatenate([dob16, dsb16], axis=0)))
                gcr = gc_t[r, 0:1, :]
                ds[h] = gcr * dstate + _tn(_bf(q * qd_t[r]), dob16)
                lg_acc[r] += (da * a * jnp.maximum(sd, 0.0) + (dq_i * q) * qw + (dk_i * k) * kw
                              + (float(c) * gcr) * (state * dstate))

        @pl.when(step == ns - 1)
        def _():
            for d in range(2):
                for h in range(nh):
                    r = d * nh + h
                    lg = -jnp.exp(jnp.zeros((1, HEAD), F32) + dec_ref[d, h])
                    dlg_ref[r:r + 1, :] = (jnp.zeros((1, HEAD), F32) + jnp.sum(lg_acc[r])) * lg

    dshape = jax.ShapeDtypeStruct((t, dc), BF16)
    return pl.pallas_call(
        body, name=name,
        out_shape=(dshape, dshape, dshape, dshape, dshape, dshape, jax.ShapeDtypeStruct((2 * nh, HEAD), F32)),
        grid=(ns,),
        in_specs=[blk(fmap(0)), blk(fmap(0)), blk(fmap(6)), blk(fmap(0)),
                  blk(bmap(0)), blk(bmap(0)), blk(bmap(6)), blk(bmap(0)), stf_spec, stb_spec, SMEM_SPEC],
        out_specs=(blk(fmap(0)), blk(fmap(0)), blk(fmap(0)), blk(bmap(0)), blk(bmap(0)), blk(bmap(0)),
                   pl.BlockSpec((2 * nh, HEAD), lambda s: (0, 0))),
        scratch_shapes=[pltpu.VMEM((nh, HEAD, HEAD), F32), pltpu.VMEM((nh, HEAD, HEAD), F32)]
        + _decay_table_scratch(nh) + [pltpu.VMEM((2 * nh, CHUNK, HEAD), F32)],
        compiler_params=_params("arbitrary"),
    )(qr, kr, u, do, qr, kr, u, do, st_f, st_b, dec)


def _mod_matvec(c16, w_mod, b_mod):
    nl, d, nm = w_mod.shape
    tn = _pick(nm, 512, 128)

    def body(c_ref, w_ref, b_ref, sc_ref, o_ref):
        sc = _silu(c_ref[...])
        sc_ref[...] = sc
        o_ref[...] = _nn(_bf(sc), _bf(w_ref[...])) + b_ref[...]

    return pl.pallas_call(
        body, name="mod_matvec",
        out_shape=(jax.ShapeDtypeStruct((16, d), F32), jax.ShapeDtypeStruct((nl, 16, nm), F32)),
        grid=(nl, nm // tn),
        in_specs=[pl.BlockSpec((16, d), lambda l, j: (0, 0)), pl.BlockSpec((None, d, tn), lambda l, j: (l, 0, j)),
                  pl.BlockSpec((None, 1, tn), lambda l, j: (l, 0, j))],
        out_specs=(pl.BlockSpec((16, d), lambda l, j: (0, 0)), pl.BlockSpec((None, 16, tn), lambda l, j: (l, 0, j))),
        compiler_params=_params("arbitrary", "arbitrary"),
    )(c16, w_mod, b_mod)


def _wmod_grad(sc16, dm, w_mod):
    nl, d, nm = w_mod.shape
    tn = _pick(nm, 512, 128)

    def body(sc_ref, dm_ref, w_ref, gw_ref, pc_ref):
        @pl.when(jnp.logical_and(pl.program_id(0) == 0, pl.program_id(1) == 0))
        def _():
            pc_ref[...] = jnp.zeros_like(pc_ref)

        dmb = _bf(dm_ref[...])
        gw_ref[...] = _tn(_bf(sc_ref[...]), dmb)
        pc_ref[...] += _nt(dmb, _bf(w_ref[...]))

    return pl.pallas_call(
        body, name="wmod_grad",
        out_shape=(jax.ShapeDtypeStruct((nl, d, nm), F32), jax.ShapeDtypeStruct((16, d), F32)),
        grid=(nl, nm // tn),
        in_specs=[pl.BlockSpec((16, d), lambda l, j: (0, 0)), pl.BlockSpec((None, 16, tn), lambda l, j: (l, 0, j)),
                  pl.BlockSpec((None, d, tn), lambda l, j: (l, 0, j))],
        out_specs=(pl.BlockSpec((None, d, tn), lambda l, j: (l, 0, j)), pl.BlockSpec((16, d), lambda l, j: (0, 0))),
        compiler_params=_params("arbitrary", "arbitrary"),
    )(sc16, dm, w_mod)


def _rowsum(a, name):
    r, n = a.shape
    tn = _pick(n, 4096, 128)

    def body(a_ref, o_ref):
        acc = a_ref[0:1, :]
        for i in range(1, r):
            acc = acc + a_ref[i:i + 1, :]
        o_ref[...] = jnp.zeros((8, tn), F32) + acc

    return pl.pallas_call(
        body, name=name, out_shape=jax.ShapeDtypeStruct((8, n), F32), grid=(n // tn,),
        in_specs=[pl.BlockSpec((r, tn), lambda j: (0, j))], out_specs=pl.BlockSpec((8, tn), lambda j: (0, j)),
        compiler_params=_params("parallel"),
    )(a)


def _cctx_grad(parts, c_ctx8):
    _, _, d = parts.shape

    def body(p_ref, c_ref, o_ref):
        acc = ((p_ref[0] + p_ref[1]) + p_ref[2]) + p_ref[3]
        o_ref[...] = acc * _dsilu(c_ref[...])

    return pl.pallas_call(
        body, name="cctx_grad", out_shape=jax.ShapeDtypeStruct((8, d), F32),
        in_specs=[VMEM_SPEC, VMEM_SPEC], out_specs=VMEM_SPEC,
    )(parts, c_ctx8)


def _adamw(g, w, m, v, name):
    r, n = w.shape
    tr = _pick(r, max(8, min(256, (512 * 1024) // n // 8 * 8)), 8)
    bc1 = 1.0 - ADAM_B1 ** ADAM_STEP
    bc2 = 1.0 - ADAM_B2 ** ADAM_STEP

    def body(g_ref, w_ref, m_ref, v_ref, d_ref, mo_ref, vo_ref):
        gg = g_ref[...]
        mn = ADAM_B1 * m_ref[...] + (1.0 - ADAM_B1) * gg
        vn = ADAM_B2 * v_ref[...] + (1.0 - ADAM_B2) * (gg * gg)
        mo_ref[...] = mn
        vo_ref[...] = vn
        d_ref[...] = -ADAM_LR * ((mn / bc1) / (jnp.sqrt(vn / bc2) + ADAM_EPS) + ADAM_WD * w_ref[...])

    spec = pl.BlockSpec((tr, n), lambda i: (i, 0))
    shp = jax.ShapeDtypeStruct((r, n), F32)
    return pl.pallas_call(
        body, name=name, out_shape=(shp, shp, shp), grid=(r // tr,),
        in_specs=[spec, spec, spec, spec], out_specs=(spec, spec, spec),
        compiler_params=_params("parallel"),
    )(g, w, m, v)


def _adamw_layer(g, w, m, v, layer, prev, name):
    r, n = g.shape
    tr = _pick(r, max(8, min(256, (512 * 1024) // n // 8 * 8)), 8)
    nb = r // tr
    bc1 = 1.0 - ADAM_B1 ** ADAM_STEP
    bc2 = 1.0 - ADAM_B2 ** ADAM_STEP
    n_prev = 0 if prev is None else 4

    def body(g_ref, w_ref, m_ref, v_ref, *rest):
        go_ref, d_ref, mo_ref, vo_ref = rest[n_prev:]
        gg = g_ref[...]
        mn = ADAM_B1 * m_ref[...] + (1.0 - ADAM_B1) * gg
        vn = ADAM_B2 * v_ref[...] + (1.0 - ADAM_B2) * (gg * gg)
        go_ref[...] = gg
        mo_ref[...] = mn
        vo_ref[...] = vn
        d_ref[...] = -ADAM_LR * ((mn / bc1) / (jnp.sqrt(vn / bc2) + ADAM_EPS) + ADAM_WD * w_ref[...])

    spec = pl.BlockSpec((tr, n), lambda i: (layer * nb + i, 0))
    shp = jax.ShapeDtypeStruct(w.shape, F32)
    return pl.pallas_call(
        body, name=name, out_shape=(shp, shp, shp, shp), grid=(nb,),
        in_specs=[pl.BlockSpec((tr, n), lambda i: (i, 0)), spec, spec, spec] + [ANY] * n_prev,
        out_specs=(spec, spec, spec, spec),
        input_output_aliases={4 + k: k for k in range(n_prev)},
        compiler_params=_params("parallel"),
    )(g, w, m, v, *(prev or ()))


def _rope_tables(seq, n_ctx):
    f = HEAD // 4
    inv = ROPE_BASE ** (-jnp.arange(f, dtype=F32) / f)
    n_rows = seq // GRID_W
    ang_r = jnp.arange(n_rows, dtype=F32)[:, None] * inv[None, :]
    ang_c = jnp.arange(GRID_W, dtype=F32)[:, None] * inv[None, :]
    by_row = lambda a: jnp.repeat(a, GRID_W, axis=0)
    by_col = lambda a: jnp.tile(a, (n_rows, 1))
    cos_r, sin_r, cos_c, sin_c = by_row(jnp.cos(ang_r)), by_row(jnp.sin(ang_r)), by_col(jnp.cos(ang_c)), by_col(jnp.sin(ang_c))
    cos = jnp.concatenate([cos_r, cos_r, cos_c, cos_c], axis=-1)
    sin = jnp.concatenate([-sin_r, sin_r, -sin_c, sin_c], axis=-1)
    cos = jnp.concatenate([jnp.ones((n_ctx, HEAD), F32), cos], axis=0)
    sin = jnp.concatenate([jnp.zeros((n_ctx, HEAD), F32), sin], axis=0)
    return cos, sin


def _pad_rows(a, rows):
    return jnp.concatenate([a, jnp.zeros((rows - a.shape[0],) + a.shape[1:], a.dtype)], axis=0)


def _pad_flat(parts, mult):
    flat = jnp.concatenate([p.reshape(-1) for p in parts])
    pad = (-flat.shape[0]) % mult
    return jnp.concatenate([flat, jnp.zeros((pad,), flat.dtype)])


def kernel(x, c, ctx, c_ctx, norm_w, w_mod, b_mod, w_in, conv_w, conv_norm_w, ret_norm_w, ret_decay_f, ret_decay_b, w_out, final_norm_w, loss_target, m_c_ctx, m_norm_w, m_w_mod, m_b_mod, m_w_in, m_conv_w, m_conv_norm_w, m_ret_norm_w, m_ret_decay_f, m_ret_decay_b, m_w_out, m_final_norm_w, v_c_ctx, v_norm_w, v_w_mod, v_b_mod, v_w_in, v_conv_w, v_conv_norm_w, v_ret_norm_w, v_ret_decay_f, v_ret_decay_b, v_w_out, v_final_norm_w):
    xi, yi, ci = _mesh_pos()
    b_idx = 4 * xi + 2 * yi + ci
    s_idx = 2 * xi + yi
    pos = jnp.stack([ci, s_idx]).astype(jnp.int32)

    x2, ctx2, tgt = x[0], ctx[0], loss_target[0]
    seq, d = x2.shape
    n_ctx = ctx2.shape[0]
    t = seq + n_ctx
    dc = d // 2
    nh = dc // HEAD
    nl, _, nm = w_mod.shape
    n4 = w_in.shape[2]
    r4 = w_out.shape[1]
    cw4 = conv_w.shape[2]
    nct = n_ctx // ROW_TILE
    ncc = n_ctx // CHUNK
    assert n_ctx % ROW_TILE == 0 and seq % ROW_TILE == 0 and dc % HEAD == 0 and nl * 3 <= 8

    win_b, wout_b = w_in.astype(BF16), w_out.astype(BF16)
    wg_in, wg_out = [None] * nl, [None] * nl
    shard_order = jnp.stack([s_idx, 2 * (1 - xi) + yi, 2 * xi + (1 - yi), 2 * (1 - xi) + (1 - yi)]).astype(jnp.int32)

    small = jnp.concatenate([jnp.broadcast_to(c, (8, d)), _pad_rows(conv_w.reshape(nl * 3, cw4), 8)], axis=1)
    small_g = _allgather_small(small, "gather_c_convw").reshape(N_DEV, 8, d + cw4)
    c_all = small_g[:, 0, :d]
    cw_full = small_g[0::2, :, d:].transpose(1, 0, 2).reshape(8, 4 * cw4)
    cw_l = [_pad_rows(cw_full[l * 3:(l + 1) * 3], 8) for l in range(nl)]

    c16 = _pad_rows(jnp.concatenate([c_all, c_ctx[None]], axis=0), 16)
    b_shard = lax.dynamic_slice(b_mod, (0, s_idx * nm), (nl, nm)).reshape(nl, 1, nm)
    sc16, modp = _mod_matvec(c16, w_mod, b_shard)
    modg = _allgather_small(modp.reshape(nl * 16, nm), "gather_mod").reshape(N_DEV, nl, 16, nm)
    mod_full = modg[0::2].transpose(1, 2, 0, 3).reshape(nl, 16, 4 * nm)
    mod_me = lax.dynamic_index_in_dim(mod_full, b_idx, axis=1, keepdims=False)
    mod_cx = mod_full[:, 8]
    zrow = jnp.zeros((d,), F32)
    rows = []
    for l in range(nl):
        gate_c = mod_cx[l, 2 * d:] if l < nl - 1 else zrow
        rows.append(jnp.stack([mod_me[l, :d], mod_me[l, d:2 * d], mod_me[l, 2 * d:],
                               mod_cx[l, :d], mod_cx[l, d:2 * d], gate_c, zrow, zrow]))

    cos, sin = _rope_tables(seq, n_ctx)
    xs = jnp.concatenate([ctx2, x2], axis=0)
    saved = []
    for l in range(nl):
        dec = jnp.stack([ret_decay_f[l], ret_decay_b[l]])
        hx, hx_t = _modulate(xs, norm_w[l][None], rows[l], nct, f"modulate_{l}")
        u, wg_in[l], wg_out[l] = _in_proj_gathered(hx, win_b[l], shard_order, _gather_comm(wout_b[l], False),
                                                   f"in_proj_{l}")
        o_f, o_b, st_f, st_b, qr, kr = _ret_fwd(u, cos, sin, dec, ncc, f"ret_fwd_{l}")
        y, y_t, cv = _mix_fwd(u, o_f, o_b, cw_l[l], conv_norm_w[l][None], ret_norm_w[l][None], nct, f"mix_fwd_{l}")
        xs_new, proj = _out_proj(y, wg_out[l], xs, rows[l], n_ctx, f"out_proj_{l}")
        saved.append((xs, hx_t, u, o_f, o_b, st_f, st_b, qr, kr, y_t, cv, proj, dec))
        xs = xs_new

    g, facc, gp, gacc = _final_loss(xs, tgt, final_norm_w[None], (saved[nl - 1][11], rows[nl - 1]), nct, "final_loss")
    full = [None] * nl
    join = []
    mine, other = ci.reshape(1).astype(jnp.int32), (1 - ci).reshape(1).astype(jnp.int32)
    dmx, dmc, gnw, gcnw, grnw, gcw, gdf, gdb = ([None] * nl for _ in range(8))
    for l in reversed(range(nl)):
        xs_l, hx_t, u, o_f, o_b, st_f, st_b, qr, kr, y_t, cv, proj, dec = saved[l]
        gout = _matmul(y_t, gp, "nn", f"out_proj_dw_{l}", tm_t=1024, tn_t=1024, tk_t=2816, comms=join)
        if join:
            gout, *full[l + 1] = gout
        dy = _matmul(gp, wg_out[l], "nt", f"out_proj_dx_{l}", tn_t=2048, out_dtype=BF16)
        dab, daz, drz, dcv, do, macc = _mix_bwd(dy, u, cv, o_f, o_b, conv_norm_w[l][None], ret_norm_w[l][None],
                                                f"mix_bwd_{l}")
        dqf, dkf, dvf, dqb, dkb, dvb, dlg = _ret_bwd(u, qr, kr, do, st_f, st_b, dec, ncc, f"ret_bwd_{l}")
        du, cacc = _assemble_du(dcv, u, cw_l[l], dab, daz, drz, dqf, dqb, dkf, dkb, dvf, dvb, cos, sin, nct,
                                f"assemble_du_{l}")
        gout_v = gout.reshape(4, 2, r4 // 2, d)
        gin_other = _matmul(hx_t, du, "nn", f"in_proj_dw_other_{l}", tm_t=1024, tn_t=1024, tk_t=2816, row_half=other)
        gin_mine, land_in, land_out = _matmul(hx_t, du, "nn", f"in_proj_dw_mine_{l}", tm_t=1024, tn_t=1024,
                                              tk_t=2816, row_half=mine, comms=[_swap_comm(gin_other, gout_v)])
        p_in = _sum_half_in(gin_mine, land_in, f"grad_sum_half_in_{l}")
        p_out = _sum_half_out(pos, gout_v, land_out, f"grad_sum_half_out_{l}")
        dhx, l3_in, l3_out = _matmul(du, wg_in[l], "nt", f"in_proj_dx_{l}", tn_t=2048,
                                     comms=[_exchange_comm(p_in, p_out, n4)],
                                     out_dtype=BF16)
        join = [_join_comm(_sum_quarters_in(pos, p_in, l3_in, n4, f"grad_sum_quarters_in_{l}"),
                           _sum_quarters_out(pos, p_out, l3_out, f"grad_sum_quarters_out_{l}"))]
        dmx_gate, dmc_gate = gacc[2], gacc[5]
        below = (saved[l - 1][11], rows[l - 1]) if l > 0 else None
        g, nacc, *nxt = _modulate_bwd(dhx, xs_l, g, norm_w[l][None], rows[l], nct, f"modulate_bwd_{l}", below)
        if nxt:
            gp, gacc = nxt
        dmx[l] = jnp.concatenate([nacc[0], nacc[1], dmx_gate])
        dmc[l] = jnp.concatenate([nacc[3], nacc[4], dmc_gate])
        gnw[l], gcnw[l], grnw[l], gcw[l] = nacc[6], macc[0], macc[1], cacc[0:3]
        gdf[l], gdb[l] = dlg[0:nh, 0], dlg[nh:2 * nh, 0]
    grad_x = g[None]
    full[0] = _run_comm(join[0], "grad_join_0")

    res_in, res_out = None, None
    for l in reversed(range(nl)):
        res_in = _adamw_layer(full[l][0].reshape(d, n4), w_in.reshape(nl * d, n4), m_w_in.reshape(nl * d, n4),
                              v_w_in.reshape(nl * d, n4), l, res_in, f"adamw_w_in_{l}")
        res_out = _adamw_layer(full[l][1].reshape(r4, d), w_out.reshape(nl * r4, d), m_w_out.reshape(nl * r4, d),
                               v_w_out.reshape(nl * r4, d), l, res_out, f"adamw_w_out_{l}")
    g_w_in, d_w_in, nm_w_in, nv_w_in = res_in
    g_w_out, d_w_out, nm_w_out, nv_w_out = res_out

    seg = [jnp.stack(dmx), jnp.stack(dmc), jnp.stack(gnw), jnp.stack(gcnw), jnp.stack(grnw), facc[0],
           jnp.stack(gcw), jnp.stack(gdf), jnp.stack(gdb), facc[1, 0:1]]
    sizes = [int(s.size) for s in seg]
    offs = [sum(sizes[:i]) for i in range(len(sizes))]
    packed = _pad_flat(seg, 8 * 128)
    pw = packed.shape[0] // 8
    allv = _allgather_small(packed.reshape(8, pw), "gather_small_grads").reshape(N_DEV, 8 * pw)
    n_mod = nl * 3 * d
    dmx_all, dmc_all = allv[:, offs[0]:offs[0] + n_mod], allv[:, offs[1]:offs[1] + n_mod]
    rest_all = allv[:, offs[2]:]
    zeros_mod = jnp.zeros_like(dmc_all)
    stack = jnp.concatenate([
        jnp.concatenate([dmx_all, dmc_all], axis=0),
        jnp.concatenate([dmc_all, zeros_mod], axis=0),
        jnp.concatenate([rest_all, jnp.zeros_like(rest_all)], axis=0)], axis=1)
    sums = _rowsum(stack, "sum_small_grads")[0]
    g_b_mod = sums[:n_mod].reshape(nl, 3 * d)
    dmc_tot = sums[n_mod:2 * n_mod].reshape(nl, 3 * d)
    rest = sums[2 * n_mod:]

    def take(i):
        o = offs[i] - offs[2]
        return rest[o:o + sizes[i]]

    g_norm_w = take(2).reshape(nl, d)
    g_conv_norm_w = take(3).reshape(nl, dc)
    g_ret_norm_w = take(4).reshape(nl, dc)
    g_final_norm_w = take(5)
    g_conv_w = lax.dynamic_slice(take(6).reshape(nl, 3, dc), (0, 0, s_idx * cw4), (nl, 3, cw4))
    g_decay_f = take(7).reshape(nl, nh)
    g_decay_b = take(8).reshape(nl, nh)
    loss = take(9)[0]

    dmx_mine = lax.dynamic_slice(dmx_all.reshape(N_DEV, nl, 3 * d), (0, 0, s_idx * nm), (N_DEV, nl, nm))
    dmc_mine = lax.dynamic_slice(dmc_tot, (0, s_idx * nm), (nl, nm))
    dm = jnp.concatenate([dmx_mine.transpose(1, 0, 2), dmc_mine[:, None, :], jnp.zeros((nl, 7, nm), F32)], axis=1)
    g_w_mod, pc = _wmod_grad(sc16, dm, w_mod)
    pc_g = _allgather_small(pc[8:16], "gather_cctx_partials").reshape(N_DEV, 8, d)
    g_c_ctx = _cctx_grad(pc_g[0::2], jnp.broadcast_to(c_ctx[None], (8, d)))[0]
    d_w_mod, nm_w_mod, nv_w_mod = _adamw(g_w_mod.reshape(nl * d, nm), w_mod.reshape(nl * d, nm),
                                         m_w_mod.reshape(nl * d, nm), v_w_mod.reshape(nl * d, nm), "adamw_w_mod")

    small_g = [g_c_ctx, g_norm_w, g_b_mod, g_conv_w, g_conv_norm_w, g_ret_norm_w, g_decay_f, g_decay_b, g_final_norm_w]
    small_w = [c_ctx, norm_w, b_mod, conv_w, conv_norm_w, ret_norm_w, ret_decay_f, ret_decay_b, final_norm_w]
    small_m = [m_c_ctx, m_norm_w, m_b_mod, m_conv_w, m_conv_norm_w, m_ret_norm_w, m_ret_decay_f, m_ret_decay_b,
               m_final_norm_w]
    small_v = [v_c_ctx, v_norm_w, v_b_mod, v_conv_w, v_conv_norm_w, v_ret_norm_w, v_ret_decay_f, v_ret_decay_b,
               v_final_norm_w]
    pk = lambda parts: _pad_flat(parts, 8 * 128).reshape(-1, 128)
    sd, sm, sv = _adamw(pk(small_g), pk(small_w), pk(small_m), pk(small_v), "adamw_small")

    def unpack(flat2d):
        flat = flat2d.reshape(-1)
        out, o = [], 0
        for wgt in small_w:
            out.append(flat[o:o + wgt.size].reshape(wgt.shape))
            o += wgt.size
        return out

    sd, sm, sv = unpack(sd), unpack(sm), unpack(sv)

    def order(small, mod, w_in_, w_out_):
        return [small[0], small[1], mod, small[2], w_in_, small[3], small[4], small[5], small[6], small[7], w_out_,
                small[8]]

    grads = order(small_g, g_w_mod, g_w_in.reshape(nl, d, n4), g_w_out.reshape(nl, r4, d))
    deltas = order(sd, d_w_mod.reshape(nl, d, nm), d_w_in.reshape(nl, d, n4), d_w_out.reshape(nl, r4, d))
    new_m = order(sm, nm_w_mod.reshape(nl, d, nm), nm_w_in.reshape(nl, d, n4), nm_w_out.reshape(nl, r4, d))
    new_v = order(sv, nv_w_mod.reshape(nl, d, nm), nv_w_in.reshape(nl, d, n4), nv_w_out.reshape(nl, r4, d))
    return (loss, grad_x, *grads, *deltas, *new_m, *new_v)
```

```python
import functools

import jax
import jax.numpy as jnp
from jax import lax
from jax.experimental import pallas as pl
from jax.experimental.pallas import tpu as pltpu

F32 = jnp.float32
BF16 = jnp.bfloat16
EPS = 1e-6
CHUNK = 128
HEAD = 128
GRID_W = 64
ROPE_BASE = 10000.0
ROW_TILE = 256
ROW_CHUNK = 16
CHUNK_UNROLL = 4
RET_STEP_CHUNKS = 2
RET_INTERLEAVE = 4
VMEM_LIMIT_BYTES = 56 * 1024 * 1024
N_DEV = 8
MESH = pl.DeviceIdType.MESH

ADAM_LR = 0.001
ADAM_B1 = 0.9
ADAM_B2 = 0.999
ADAM_EPS = 1e-08
ADAM_WD = 0.01
ADAM_STEP = 10

ANY = pl.BlockSpec(memory_space=pl.ANY)
VMEM_SPEC = pl.BlockSpec(memory_space=pltpu.VMEM)
SMEM_SPEC = pl.BlockSpec(memory_space=pltpu.SMEM)


def _params(*sem):
    return pltpu.CompilerParams(dimension_semantics=sem, vmem_limit_bytes=VMEM_LIMIT_BYTES)


def _pick(n, target, mult):
    best = None
    for d in range(mult, min(n, target) + 1, mult):
        if n % d == 0:
            best = d
    assert best is not None, (n, target, mult)
    return best


def _for_row_chunks(n_rows, fn):
    def step(r, carry):
        for q in range(CHUNK_UNROLL):
            fn(pl.multiple_of((r * CHUNK_UNROLL + q) * ROW_CHUNK, ROW_CHUNK), q)
        return carry

    lax.fori_loop(0, n_rows // (ROW_CHUNK * CHUNK_UNROLL), step, 0)


class _Partials:
    def __init__(self, ref, n_sums):
        self.ref, self.n_sums = ref, n_sums

    def zero(self):
        self.ref[...] = jnp.zeros_like(self.ref)

    def add(self, q, k, value, cols=slice(None)):
        r = (q * self.n_sums + k) * 8
        self.ref[r:r + 8, cols] += value

    def total(self, k):
        rows = [(q * self.n_sums + k) * 8 for q in range(CHUNK_UNROLL)]
        acc = self.ref[rows[0]:rows[0] + 8, :]
        for r in rows[1:]:
            acc = acc + self.ref[r:r + 8, :]
        return jnp.sum(acc, axis=0, keepdims=True)

    @staticmethod
    def scratch(n_sums, n):
        return pltpu.VMEM((CHUNK_UNROLL * n_sums * 8, n), F32)


def _fold8(a):
    out = a[0:8]
    for hh in range(1, ROW_CHUNK // 8):
        out = out + a[8 * hh:8 * hh + 8]
    return out


def _silu(z):
    return z * jax.nn.sigmoid(z)


def _dsilu(z):
    s = jax.nn.sigmoid(z)
    return s * (1.0 + z * (1.0 - s))


def _nn(a, b):
    return lax.dot_general(a, b, (((1,), (0,)), ((), ())), preferred_element_type=F32)


def _nt(a, b):
    return lax.dot_general(a, b, (((1,), (1,)), ((), ())), preferred_element_type=F32)


def _tn(a, b):
    return lax.dot_general(a, b, (((0,), (0,)), ((), ())), preferred_element_type=F32)


def _bf(a):
    return a.astype(BF16)


def _mesh_pos():
    return lax.axis_index("x"), lax.axis_index("y"), lax.axis_index("c")


def _allgather_small(blk, name):
    m_per, n = blk.shape

    def body(x_ref, out_ref, send_sems, recv_sems, local_sem):
        x, y, c = _mesh_pos()
        me, sibling = (x, y, c), (x, y, 1 - c)
        chips = [(1 - x, y), (x, 1 - y), (1 - x, 1 - y)]

        def rows(px, py, pc):
            return out_ref.at[pl.ds((4 * px + 2 * py + pc) * m_per, m_per), :]

        def copy(k, block, to, src=None):
            return pltpu.make_async_remote_copy(
                src_ref=rows(*block) if src is None else src, dst_ref=rows(*block),
                send_sem=send_sems.at[k], recv_sem=recv_sems.at[k], device_id=to, device_id_type=MESH)

        mine = pltpu.make_async_copy(x_ref, rows(*me), local_sem)
        mine.start()
        first = [copy(0, me, sibling, src=x_ref)]
        first += [copy(1 + j, me, (*chip, c), src=x_ref) for j, chip in enumerate(chips)]
        for cp in first:
            cp.start()
        passed = [copy(4 + j, (*chip, c), sibling) for j, chip in enumerate(chips)]
        for j, chip in enumerate(chips):
            copy(1 + j, (*chip, c), me).wait_recv()
            passed[j].start()
        copy(0, sibling, me).wait_recv()
        for j, chip in enumerate(chips):
            copy(4 + j, (*chip, 1 - c), me).wait_recv()
        for cp in first + passed:
            cp.wait_send()
        mine.wait()

    return pl.pallas_call(
        body, name=name,
        out_shape=jax.ShapeDtypeStruct((N_DEV * m_per, n), blk.dtype),
        in_specs=[VMEM_SPEC], out_specs=VMEM_SPEC,
        scratch_shapes=[pltpu.SemaphoreType.DMA((7,)), pltpu.SemaphoreType.DMA((7,)), pltpu.SemaphoreType.DMA],
    )(blk)


class _Comm:
    def __init__(self, ins, outs, n_remote, n_local, start, finish, aliases=None):
        self.ins, self.outs, self.n_remote, self.n_local = list(ins), list(outs), n_remote, n_local
        self.start, self.finish, self.aliases = start, finish, dict(aliases or {})

    def scratch(self):
        return [pltpu.SemaphoreType.DMA((self.n_remote,)), pltpu.SemaphoreType.DMA((self.n_remote,)),
                pltpu.SemaphoreType.DMA((max(self.n_local, 1),))]


def _run_comm(comm, name):
    n_in, n_out = len(comm.ins), len(comm.outs)

    def body(*refs):
        args = (refs[:n_in], refs[n_in:n_in + n_out]) + tuple(refs[n_in + n_out:])
        comm.start(*args)
        comm.finish(*args)

    return pl.pallas_call(
        body, name=name, out_shape=tuple(comm.outs), in_specs=[ANY] * n_in, out_specs=tuple([ANY] * n_out),
        scratch_shapes=comm.scratch(), input_output_aliases=comm.aliases,
    )(*comm.ins)


def _remote(src, dst, send, recv, k, to):
    return pltpu.make_async_remote_copy(src_ref=src, dst_ref=dst, send_sem=send.at[k], recv_sem=recv.at[k],
                                        device_id=to, device_id_type=MESH)


def _gather_comm(shard, by_columns):
    rows, cols = shard.shape
    hr = rows // 2

    def build(ins, outs, send, recv, loc):
        x, y, c = _mesh_pos()
        sib = (x, y, 1 - c)
        peers = [(1 - x, y), (x, 1 - y), (1 - x, 1 - y)]
        s_me = 2 * x + y

        def slot(s, h):
            if by_columns:
                return outs[0].at[pl.ds(pl.multiple_of(h * hr, 16), hr), pl.ds(pl.multiple_of(s * cols, 128), cols)]
            return outs[0].at[pl.ds(pl.multiple_of(s * rows + h * hr, 16), hr), :]

        def half(h):
            return ins[0].at[pl.ds(pl.multiple_of(h * hr, 16), hr), :]

        def local():
            if by_columns:
                mine = outs[0].at[:, pl.ds(pl.multiple_of(s_me * cols, 128), cols)]
            else:
                mine = outs[0].at[pl.ds(pl.multiple_of(s_me * rows, 16), rows), :]
            return [pltpu.make_async_copy(ins[0], mine, loc.at[0])]

        def per_copy(make):
            return [make(j, 2 * px + py, (px, py, c)) for j, (px, py) in enumerate(peers)]

        ici_send = lambda: per_copy(lambda j, s_p, to: _remote(half(c), slot(s_me, c), send, recv, j, to))
        ici_recv = lambda: per_copy(lambda j, s_p, to: _remote(half(c), slot(s_p, c), send, recv, j, to))
        d2d_send = lambda: per_copy(lambda j, s_p, to: _remote(slot(s_p, c), slot(s_p, c), send, recv, 3 + j, sib))
        d2d_recv = lambda: per_copy(lambda j, s_p, to: _remote(slot(s_p, c), slot(s_p, 1 - c), send, recv, 3 + j, sib))
        return local, ici_send, ici_recv, d2d_send, d2d_recv

    def start(*refs):
        local, ici_send, _, _, _ = build(*refs)
        for cp in local() + ici_send():
            cp.start()

    def finish(*refs):
        local, ici_send, ici_recv, d2d_send, d2d_recv = build(*refs)
        forwards = d2d_send()
        for arrived, forward in zip(ici_recv(), forwards):
            arrived.wait_recv()
            forward.start()
        for cp in d2d_recv():
            cp.wait_recv()
        for cp in ici_send() + forwards:
            cp.wait_send()
        for cp in local():
            cp.wait()

    full = (rows, 4 * cols) if by_columns else (4 * rows, cols)
    return _Comm([shard], [jax.ShapeDtypeStruct(full, BF16)], 6, 1, start, finish)


def _swap_comm(gin_other, gout_v):
    dh, w = gin_other.shape
    ns, _, rh, d = gout_v.shape

    def build(ins, outs, send, recv, loc):
        x, y, c = _mesh_pos()
        sib = (x, y, 1 - c)
        return [_remote(ins[0], outs[0], send, recv, 0, sib),
                _remote(ins[1].at[:, 1 - c], outs[1], send, recv, 1, sib)]

    def start(*refs):
        for cp in build(*refs):
            cp.start()

    def finish(*refs):
        for cp in build(*refs):
            cp.wait()

    outs = [jax.ShapeDtypeStruct((dh, w), F32), jax.ShapeDtypeStruct((ns, rh, d), F32)]
    return _Comm([gin_other, gout_v], outs, 2, 0, start, finish)


def _exchange_comm(p_in, p_out, n4):
    dh, _ = p_in.shape
    _, rh, d = p_out.shape

    def build(ins, outs, send, recv, loc):
        x, y, c = _mesh_pos()
        cps = []
        for j, (px, py) in enumerate([(1 - x, y), (x, 1 - y), (1 - x, 1 - y)]):
            s_p = 2 * px + py
            cps.append(_remote(ins[0].at[:, pl.ds(pl.multiple_of(s_p * n4, 128), n4)], outs[0].at[j], send, recv,
                               2 * j, (px, py, c)))
            cps.append(_remote(ins[1].at[s_p], outs[1].at[j], send, recv, 2 * j + 1, (px, py, c)))
        return cps

    def start(*refs):
        for cp in build(*refs):
            cp.start()

    def finish(*refs):
        cps = build(*refs)
        for cp in cps:
            cp.wait_recv()
        for cp in cps:
            cp.wait_send()

    outs = [jax.ShapeDtypeStruct((3, dh, n4), p_in.dtype), jax.ShapeDtypeStruct((3, rh, d), p_out.dtype)]
    return _Comm([p_in, p_out], outs, 6, 0, start, finish)


def _join_comm(q_in, q_out):
    def build(ins, outs, send, recv, loc):
        x, y, c = _mesh_pos()
        sib = (x, y, 1 - c)
        sends = lambda: [_remote(outs[a].at[c], outs[a].at[c], send, recv, a, sib) for a in range(2)]
        recvs = lambda: [_remote(outs[a].at[c], outs[a].at[1 - c], send, recv, a, sib) for a in range(2)]
        return sends, recvs

    def start(*refs):
        for cp in build(*refs)[0]():
            cp.start()

    def finish(*refs):
        sends, recvs = build(*refs)
        for cp in recvs():
            cp.wait_recv()
        for cp in sends():
            cp.wait_send()

    outs = [jax.ShapeDtypeStruct(q_in.shape, F32), jax.ShapeDtypeStruct(q_out.shape, F32)]
    return _Comm([q_in, q_out], outs, 2, 0, start, finish, aliases={0: 0, 1: 1})


def _sum_half_in(mine, land, name):
    dh, w = mine.shape
    tr, tc = _pick(dh, 256, 16), _pick(w, 2048, 128)

    def body(a_ref, b_ref, o_ref):
        o_ref[...] = (a_ref[...] + b_ref[...]).astype(BF16)

    spec = pl.BlockSpec((tr, tc), lambda i, j: (i, j))
    return pl.pallas_call(
        body, name=name, out_shape=jax.ShapeDtypeStruct((dh, w), BF16), grid=(dh // tr, w // tc),
        in_specs=[spec, spec], out_specs=spec, compiler_params=_params("parallel", "parallel"),
    )(mine, land)


def _sum_half_out(pos, gout_v, land, name):
    ns, _, rh, d = gout_v.shape

    def body(pos_ref, a_ref, b_ref, o_ref):
        o_ref[...] = (a_ref[...] + b_ref[...]).astype(BF16)

    return pl.pallas_call(
        body, name=name,
        out_shape=jax.ShapeDtypeStruct((ns, rh, d), BF16),
        grid_spec=pltpu.PrefetchScalarGridSpec(
            num_scalar_prefetch=1, grid=(ns,),
            in_specs=[pl.BlockSpec((None, None, rh, d), lambda s, p: (s, p[0], 0, 0)),
                      pl.BlockSpec((None, rh, d), lambda s, p: (s, 0, 0))],
            out_specs=pl.BlockSpec((None, rh, d), lambda s, p: (s, 0, 0))),
        compiler_params=_params("parallel"),
    )(pos, gout_v, land)


def _sum_quarters_in(pos, p_in, land3, n4, name):
    dh, _ = p_in.shape
    tr = _pick(dh, 128, 16)

    def body(pos_ref, a_ref, b_ref, o_ref):
        f = lambda v: v.astype(F32)
        o_ref[...] = ((f(a_ref[...]) + f(b_ref[0])) + f(b_ref[1])) + f(b_ref[2])

    return pl.pallas_call(
        body, name=name,
        out_shape=jax.ShapeDtypeStruct((2, dh, n4), F32),
        grid_spec=pltpu.PrefetchScalarGridSpec(
            num_scalar_prefetch=1, grid=(dh // tr,),
            in_specs=[pl.BlockSpec((tr, n4), lambda i, p: (i, p[1])),
                      pl.BlockSpec((3, tr, n4), lambda i, p: (0, i, 0))],
            out_specs=pl.BlockSpec((None, tr, n4), lambda i, p: (p[0], i, 0))),
        compiler_params=_params("parallel"),
    )(pos, p_in, land3)


def _sum_quarters_out(pos, p_out, land3, name):
    _, rh, d = p_out.shape

    def body(pos_ref, a_ref, b_ref, o_ref):
        f = lambda v: v.astype(F32)
        o_ref[...] = ((f(a_ref[...]) + f(b_ref[0])) + f(b_ref[1])) + f(b_ref[2])

    return pl.pallas_call(
        body, name=name,
        out_shape=jax.ShapeDtypeStruct((2, rh, d), F32),
        grid_spec=pltpu.PrefetchScalarGridSpec(
            num_scalar_prefetch=1, grid=(1,),
            in_specs=[pl.BlockSpec((None, rh, d), lambda i, p: (p[1], 0, 0)),
                      pl.BlockSpec((3, rh, d), lambda i, p: (0, 0, 0))],
            out_specs=pl.BlockSpec((None, rh, d), lambda i, p: (p[0], 0, 0))),
        compiler_params=_params("arbitrary"),
    )(pos, p_out, land3)


def _matmul(a, b, kind, name, tm_t=1056, tn_t=1024, tk_t=2048, comms=(), row_half=None, out_dtype=F32):
    if kind == "nn":
        (m, k), n = a.shape, b.shape[1]
    elif kind == "nt":
        (m, k), n = a.shape, b.shape[0]
    else:
        (k, m), n = a.shape, b.shape[1]
    if row_half is None:
        row_half = jnp.zeros((1,), jnp.int32)
    else:
        assert kind == "nn"
        m = m // 2
    tm = _pick(m, tm_t, 128 if kind == "tn" else 16)
    tn = _pick(n, tn_t, 128)
    tk = _pick(k, tk_t, 16 if kind == "tn" else 128)
    nk = k // tk
    nbm = m // tm
    grid = (nbm, n // tn, nk)
    dot = {"nn": _nn, "nt": _nt, "tn": _tn}[kind]
    a_spec = {"nn": pl.BlockSpec((tm, tk), lambda i, j, q, p: (i + p[0] * nbm, q)),
              "nt": pl.BlockSpec((tm, tk), lambda i, j, q, p: (i, q)),
              "tn": pl.BlockSpec((tk, tm), lambda i, j, q, p: (q, i))}[kind]
    b_spec = {"nn": pl.BlockSpec((tk, tn), lambda i, j, q, p: (q, j)),
              "nt": pl.BlockSpec((tn, tk), lambda i, j, q, p: (j, q)),
              "tn": pl.BlockSpec((tk, tn), lambda i, j, q, p: (q, j))}[kind]
    comm_ins = [x for cm in comms for x in cm.ins]
    comm_outs = [x for cm in comms for x in cm.outs]
    aliases, scratch = {}, []
    i_off, o_off = 3, 1
    for cm in comms:
        for ci, co in cm.aliases.items():
            aliases[i_off + ci] = o_off + co
        i_off, o_off = i_off + len(cm.ins), o_off + len(cm.outs)
        scratch += cm.scratch()

    def body(half_ref, a_ref, b_ref, *rest):
        cin = rest[:len(comm_ins)]
        o_ref = rest[len(comm_ins)]
        cout = rest[len(comm_ins) + 1:len(comm_ins) + 1 + len(comm_outs)]
        acc_ref = rest[len(comm_ins) + 1 + len(comm_outs)]
        sems = rest[len(comm_ins) + 2 + len(comm_outs):]
        ids = [pl.program_id(ax) for ax in range(3)]
        first = functools.reduce(jnp.logical_and, [ids[ax] == 0 for ax in range(3)])
        last = functools.reduce(jnp.logical_and, [ids[ax] == grid[ax] - 1 for ax in range(3)])

        def comm_args(idx):
            i0 = sum(len(cm.ins) for cm in comms[:idx])
            o0 = sum(len(cm.outs) for cm in comms[:idx])
            cm = comms[idx]
            return (cin[i0:i0 + len(cm.ins)], cout[o0:o0 + len(cm.outs)]) + tuple(sems[3 * idx:3 * idx + 3])

        if comms:
            @pl.when(first)
            def _():
                for idx, cm in enumerate(comms):
                    cm.start(*comm_args(idx))

        q = ids[2]
        p = dot(a_ref[...], b_ref[...])
        if nk == 1:
            o_ref[...] = p.astype(out_dtype)
        else:
            @pl.when(q == 0)
            def _():
                acc_ref[...] = p

            @pl.when(q > 0)
            def _():
                acc_ref[...] += p

            @pl.when(q == nk - 1)
            def _():
                o_ref[...] = acc_ref[...].astype(out_dtype)

        if comms:
            @pl.when(last)
            def _():
                for idx, cm in enumerate(comms):
                    cm.finish(*comm_args(idx))

    sem = ("arbitrary",) * 3 if comms else ("parallel", "parallel", "arbitrary")
    res = pl.pallas_call(
        body, name=name,
        out_shape=(jax.ShapeDtypeStruct((m, n), out_dtype), *comm_outs),
        grid_spec=pltpu.PrefetchScalarGridSpec(
            num_scalar_prefetch=1, grid=grid,
            in_specs=[a_spec, b_spec] + [ANY] * len(comm_ins),
            out_specs=(pl.BlockSpec((tm, tn), lambda i, j, q, p: (i, j)), *([ANY] * len(comm_outs))),
            scratch_shapes=[pltpu.VMEM((tm, tn) if nk > 1 else (8, 128), F32)] + scratch),
        input_output_aliases=aliases,
        compiler_params=_params(*sem),
    )(row_half, a, b, *comm_ins)
    return res if comms else res[0]


def _in_proj_gathered(hx, w_shard, order, comm, name, tm_t=1056, tn_t=1024):
    t, d = hx.shape
    n4 = w_shard.shape[1]
    hr = d // 2
    tm, tn = _pick(t, tm_t, 16), _pick(n4, tn_t, 128)
    ncol, nrow = n4 // tn, t // tm
    ntile = 4 * ncol
    n_ci, n_co = len(comm.ins), len(comm.outs)

    def body(order_ref, a_ref, ws_ref, *rest):
        cin, (o_ref, wg_ref), cout = rest[:n_ci], rest[n_ci:n_ci + 2], rest[n_ci + 2:n_ci + 2 + n_co]
        bbuf, fsem, send, recv, loc = rest[n_ci + 2 + n_co:n_ci + 7 + n_co]
        csems = rest[n_ci + 7 + n_co:]
        k, j, i = pl.program_id(0), pl.program_id(1), pl.program_id(2)
        n = k * ncol + j
        x, y, c = _mesh_pos()
        sib = (x, y, 1 - c)
        peers = [(1 - x, y), (x, 1 - y), (1 - x, 1 - y)]
        s_me = 2 * x + y

        def slot(s, h, jj):
            return wg_ref.at[pl.ds(pl.multiple_of(h * hr, 16), hr), pl.ds(pl.multiple_of(s * n4 + jj * tn, 128), tn)]

        def my_piece(jj):
            return ws_ref.at[pl.ds(pl.multiple_of(c * hr, 16), hr), pl.ds(jj * tn, tn)]

        def local():
            return pltpu.make_async_copy(ws_ref, wg_ref.at[:, pl.ds(pl.multiple_of(s_me * n4, 128), n4)], loc.at[0])

        def ici_send(p, jj):
            return _remote(my_piece(jj), slot(s_me, c, jj), send, recv, p * ncol + jj, (*peers[p], c))

        def ici_recv(p, jj):
            s_p = 2 * peers[p][0] + peers[p][1]
            return _remote(my_piece(jj), slot(s_p, c, jj), send, recv, p * ncol + jj, (*peers[p], c))

        def d2d_send(p, jj):
            s_p = 2 * peers[p][0] + peers[p][1]
            return _remote(slot(s_p, c, jj), slot(s_p, c, jj), send, recv, (3 + p) * ncol + jj, sib)

        def d2d_recv(p, jj):
            s_p = 2 * peers[p][0] + peers[p][1]
            return _remote(slot(s_p, c, jj), slot(s_p, 1 - c, jj), send, recv, (3 + p) * ncol + jj, sib)

        def fetch(m, own):
            col = (m % ncol) * tn
            if own:
                src = ws_ref.at[:, pl.ds(pl.multiple_of(col, 128), tn)]
            else:
                src = wg_ref.at[:, pl.ds(pl.multiple_of(order_ref[m // ncol] * n4 + col, 128), tn)]
            return pltpu.make_async_copy(src, bbuf.at[m % 2], fsem.at[m % 2])

        comm_args = (cin, cout) + tuple(csems)

        @pl.when(jnp.logical_and(i == 0, n == 0))
        def _():
            local().start()
            for p, jj in [(p, jj) for ps in ((0, 1), (2,)) for jj in range(ncol) for p in ps]:
                ici_send(p, jj).start()
            fetch(n, True).start()

        @pl.when(i == 0)
        def _():
            fetch(n, True).wait()

        @pl.when(i == max(nrow - 2, 0))
        def _():
            for m in range(ncol, ntile):
                @pl.when(n == m - 1)
                def _():
                    p, jj = m // ncol - 1, m % ncol
                    ici_recv(p, jj).wait_recv()
                    d2d_send(p, jj).start()
                    d2d_recv(p, jj).wait_recv()
                    if m == ntile - ncol:
                        comm.start(*comm_args)

            @pl.when(n + 1 < ncol)
            def _():
                fetch(n + 1, True).start()

            @pl.when(jnp.logical_and(n + 1 >= ncol, n + 1 < ntile))
            def _():
                fetch(n + 1, False).start()

        o_ref[...] = _nn(a_ref[...], bbuf[n % 2]).astype(BF16)

        @pl.when(jnp.logical_and(n == ntile - 1, i == nrow - 1))
        def _():
            for p in range(3):
                for jj in range(ncol):
                    ici_send(p, jj).wait_send()
                    d2d_send(p, jj).wait_send()
            local().wait()
            comm.finish(*comm_args)

    res = pl.pallas_call(
        body, name=name,
        out_shape=(jax.ShapeDtypeStruct((t, 4 * n4), BF16), jax.ShapeDtypeStruct((d, 4 * n4), BF16), *comm.outs),
        grid_spec=pltpu.PrefetchScalarGridSpec(
            num_scalar_prefetch=1, grid=(4, ncol, nrow),
            in_specs=[pl.BlockSpec((tm, d), lambda k, j, i, p: (i, 0)), ANY] + [ANY] * n_ci,
            out_specs=(pl.BlockSpec((tm, tn), lambda k, j, i, p: (i, p[k] * ncol + j)), ANY, *([ANY] * n_co)),
            scratch_shapes=[pltpu.VMEM((2, d, tn), BF16), pltpu.SemaphoreType.DMA((2,)),
                            pltpu.SemaphoreType.DMA((6 * ncol,)), pltpu.SemaphoreType.DMA((6 * ncol,)),
                            pltpu.SemaphoreType.DMA((1,))] + comm.scratch()),
        compiler_params=_params("arbitrary", "arbitrary", "arbitrary"),
    )(order, hx, w_shard, *comm.ins)
    return res


def _out_proj(y, w_out, xs, rows, n_ctx, name):
    t, k = y.shape
    d = w_out.shape[1]
    tm, tn = _pick(t, 1056, 16), _pick(d, 1024, 128)

    n_sub = 2 if tm % 32 == 0 else 1
    ts = tm // n_sub

    def body(y_ref, w_ref, x_ref, r_ref, xo_ref, p_ref):
        i = pl.program_id(0)
        for sub in range(n_sub):
            rows = slice(sub * ts, (sub + 1) * ts)
            p = _nn(y_ref[rows, :], w_ref[...])
            ridx = i * tm + sub * ts + lax.broadcasted_iota(jnp.int32, (ts, 1), 0)
            gate = jnp.where(ridx < n_ctx, r_ref[5:6, :], r_ref[2:3, :])
            p_ref[rows, :] = p.astype(BF16)
            xo_ref[rows, :] = x_ref[rows, :] + gate * p

    return pl.pallas_call(
        body, name=name,
        out_shape=(jax.ShapeDtypeStruct((t, d), F32), jax.ShapeDtypeStruct((t, d), BF16)),
        grid=(t // tm, d // tn),
        in_specs=[pl.BlockSpec((tm, k), lambda i, j: (i, 0)), pl.BlockSpec((k, tn), lambda i, j: (0, j)),
                  pl.BlockSpec((tm, tn), lambda i, j: (i, j)), pl.BlockSpec((8, tn), lambda i, j: (0, j))],
        out_specs=(pl.BlockSpec((tm, tn), lambda i, j: (i, j)), pl.BlockSpec((tm, tn), lambda i, j: (i, j))),
        compiler_params=_params("parallel", "parallel"),
    )(y, w_out, xs, rows)


def _modulate(xs, nw, rows, nct, name):
    t, d = xs.shape
    tr = ROW_TILE

    def body(x_ref, nw_ref, r_ref, o_ref, ot_ref):
        is_ctx = pl.program_id(0) < nct
        x = x_ref[...]
        xn = (x * lax.rsqrt(jnp.mean(x * x, axis=-1, keepdims=True) + EPS)) * nw_ref[...]
        shift = jnp.where(is_ctx, r_ref[3:4, :], r_ref[0:1, :])
        scale = jnp.where(is_ctx, r_ref[4:5, :], r_ref[1:2, :])
        hx = xn * (1.0 + scale) + shift
        o_ref[...] = hx.astype(BF16)
        ot_ref[...] = hx.T.astype(BF16)

    return pl.pallas_call(
        body, name=name,
        out_shape=(jax.ShapeDtypeStruct((t, d), BF16), jax.ShapeDtypeStruct((d, t), BF16)), grid=(t // tr,),
        in_specs=[pl.BlockSpec((tr, d), lambda i: (i, 0)), pl.BlockSpec((1, d), lambda i: (0, 0)),
                  pl.BlockSpec((8, d), lambda i: (0, 0))],
        out_specs=(pl.BlockSpec((tr, d), lambda i: (i, 0)), pl.BlockSpec((d, tr), lambda i: (0, i))),
        compiler_params=_params("parallel"),
    )(xs, nw, rows)


def _modulate_bwd(dhx, xs, g, nw, rows, nct, name, below=None):
    t, d = xs.shape
    tr = ROW_TILE
    latent_only = below is None
    t_out = t - nct * tr if latent_only else t
    out_row = (lambda i: (jnp.maximum(i - nct, 0), 0)) if latent_only else (lambda i: (i, 0))
    n_below = 0 if latent_only else 2

    def body(dh_ref, x_ref, g_ref, nw_ref, r_ref, *rest):
        go_ref, acc_ref = rest[n_below:n_below + 2]
        part = _Partials(rest[-1], 4)
        i = pl.program_id(0)
        is_ctx = i < nct

        @pl.when(i == 0)
        def _():
            acc_ref[...] = jnp.zeros_like(acc_ref)
            if n_below:
                rest[n_below + 3][...] = jnp.zeros_like(rest[n_below + 3])

        part.zero()

        def chunk(r0, q):
            dh16 = dh_ref[pl.ds(r0, ROW_CHUNK), :].astype(F32)
            one_scale = 1.0 + jnp.where(is_ctx, r_ref[4:5, :], r_ref[1:2, :])
            nwv = nw_ref[...]
            if n_below:
                p16 = rest[0][pl.ds(r0, ROW_CHUNK), :].astype(F32)
                gate = jnp.where(is_ctx, rest[1][5:6, :], rest[1][2:3, :])
            gps = []
            for hh in range(ROW_CHUNK // 8):
                rows8 = pl.ds(r0 + 8 * hh, 8)
                x, dh = x_ref[rows8, :], dh16[8 * hh:8 * hh + 8]
                r = lax.rsqrt(jnp.mean(x * x, axis=-1, keepdims=True) + EPS)
                xh = x * r
                part.add(q, 0, dh)
                part.add(q, 1, dh * (xh * nwv))
                dxn = dh * one_scale
                part.add(q, 2, dxn * xh)
                dxh = dxn * nwv
                g_in = g_ref[rows8, :] + r * (dxh - xh * jnp.mean(dxh * xh, axis=-1, keepdims=True))
                go_ref[rows8, :] = g_in
                if n_below:
                    gps.append(g_in * gate)
                    part.add(q, 3, g_in * p16[8 * hh:8 * hh + 8])
            if n_below:
                rest[n_below + 2][pl.ds(r0, ROW_CHUNK), :] = jnp.concatenate(gps, axis=0).astype(BF16)

        _for_row_chunks(tr, chunk)
        total = part.total
        d_shift, d_scale = total(0), total(1)
        acc_ref[6:7, :] += total(2)
        zero = jnp.zeros_like(d_shift)
        acc_ref[0:1, :] += jnp.where(is_ctx, zero, d_shift)
        acc_ref[1:2, :] += jnp.where(is_ctx, zero, d_scale)
        acc_ref[3:4, :] += jnp.where(is_ctx, d_shift, zero)
        acc_ref[4:5, :] += jnp.where(is_ctx, d_scale, zero)
        if n_below:
            gacc_ref, s = rest[n_below + 3], total(3)
            gacc_ref[2:3, :] += jnp.where(is_ctx, zero, s)
            gacc_ref[5:6, :] += jnp.where(is_ctx, s, zero)

    row = pl.BlockSpec((tr, d), lambda i: (i, 0))
    vecs = pl.BlockSpec((8, d), lambda i: (0, 0))
    return pl.pallas_call(
        body, name=name,
        out_shape=(jax.ShapeDtypeStruct((t_out, d), F32), jax.ShapeDtypeStruct((8, d), F32))
        + ((jax.ShapeDtypeStruct((t, d), BF16), jax.ShapeDtypeStruct((8, d), F32)) if n_below else ()),
        grid=(t // tr,),
        in_specs=[row, row, row, pl.BlockSpec((1, d), lambda i: (0, 0)), vecs] + ([row, vecs] if n_below else []),
        out_specs=(pl.BlockSpec((tr, d), out_row), vecs) + ((row, vecs) if n_below else ()),
        scratch_shapes=[_Partials.scratch(4, d)],
        compiler_params=_params("arbitrary"),
    )(dhx, xs, g, nw, rows, *(below or ()))


def _final_loss(xs, tgt, fw, below, nct, name):
    t, d = xs.shape
    tr = ROW_TILE

    def body(x_ref, t_ref, w_ref, p_ref, r_ref, g_ref, acc_ref, gp_ref, gacc_ref, part_ref):
        part = _Partials(part_ref, 3)
        i = pl.program_id(0)
        is_ctx = i < nct

        @pl.when(i == 0)
        def _():
            acc_ref[...] = jnp.zeros_like(acc_ref)
            gacc_ref[...] = jnp.zeros_like(gacc_ref)

        @pl.when(is_ctx)
        def _():
            g_ref[...] = jnp.zeros_like(g_ref)
            gp_ref[...] = jnp.zeros_like(gp_ref)

        @pl.when(jnp.logical_not(is_ctx))
        def _():
            part.zero()

            def chunk(r0, q):
                p16 = p_ref[pl.ds(r0, ROW_CHUNK), :].astype(F32)
                wv, gate = w_ref[...], r_ref[2:3, :]
                gps = []
                for hh in range(ROW_CHUNK // 8):
                    rows8 = pl.ds(r0 + 8 * hh, 8)
                    x = x_ref[rows8, :]
                    r = lax.rsqrt(jnp.mean(x * x, axis=-1, keepdims=True) + EPS)
                    xh = x * r
                    e = xh * wv - t_ref[rows8, :]
                    part.add(q, 1, e * e)
                    dy = e * (1.0 / d)
                    part.add(q, 0, dy * xh)
                    dxh = dy * wv
                    g = r * (dxh - xh * jnp.mean(dxh * xh, axis=-1, keepdims=True))
                    g_ref[rows8, :] = g
                    gps.append(g * gate)
                    part.add(q, 2, g * p16[8 * hh:8 * hh + 8])
                gp_ref[pl.ds(r0, ROW_CHUNK), :] = jnp.concatenate(gps, axis=0).astype(BF16)

            _for_row_chunks(tr, chunk)
            acc_ref[0:1, :] += part.total(0)
            acc_ref[1:2, :] += jnp.zeros((1, d), F32) + jnp.sum(part.total(1)) * (0.5 / d)
            gacc_ref[2:3, :] += part.total(2)

    row = pl.BlockSpec((tr, d), lambda i: (i, 0))
    vecs = pl.BlockSpec((8, d), lambda i: (0, 0))
    return pl.pallas_call(
        body, name=name,
        out_shape=(jax.ShapeDtypeStruct((t, d), F32), jax.ShapeDtypeStruct((8, d), F32),
                   jax.ShapeDtypeStruct((t, d), BF16), jax.ShapeDtypeStruct((8, d), F32)), grid=(t // tr,),
        in_specs=[row, pl.BlockSpec((tr, d), lambda i: (jnp.maximum(i - nct, 0), 0)),
                  pl.BlockSpec((1, d), lambda i: (0, 0)), row, vecs],
        out_specs=(row, vecs, row, vecs),
        scratch_shapes=[_Partials.scratch(3, d)],
        compiler_params=_params("arbitrary"),
    )(xs, tgt, fw, *below)


def _seq_edges(i, nct, nt):
    has_prev = jnp.logical_and(i != 0, i != nct)
    has_next = jnp.logical_and(i != nct - 1, i != nt - 1)
    return has_prev, has_next


def _shift_rows(a, before, after):
    tr = a.shape[0]
    ridx = lax.broadcasted_iota(jnp.int32, (tr, 1), 0)
    down = jnp.where(ridx == 0, before, pltpu.roll(a, 1, 0))
    up = jnp.where(ridx == tr - 1, after, pltpu.roll(a, tr - 1, 0))
    return down, up


def _halo_specs(tr, dc, col, t, hr):
    per = tr // hr
    prev = pl.BlockSpec((hr, dc), lambda i: (jnp.maximum(i * per - 1, 0), col))
    nxt = pl.BlockSpec((hr, dc), lambda i: (jnp.minimum((i + 1) * per, t // hr - 1), col))
    return prev, nxt


def _mix_fwd(u, o_f, o_b, cw, cnw, rnw, nct, name):
    t, w = u.shape
    dc = w // 8
    nh = dc // HEAD
    tr = ROW_TILE
    nt = t // tr
    col = lambda cidx: pl.BlockSpec((tr, dc), lambda i: (i, cidx))
    hr = 16
    acp, acn = _halo_specs(tr, dc, 2, t, hr)
    ahp, ahn = _halo_specs(tr, dc, 0, t, hr)
    f32 = lambda v: v.astype(F32)

    def body(ah, ab, ac, az, rz, acp_r, acn_r, ahp_r, ahn_r, of_r, ob_r, cw_r, cnw_r, rnw_r, y_ref, yt_ref, cv_ref):
        i = pl.program_id(0)
        has_prev, has_next = _seq_edges(i, nct, nt)
        ch = f32(ac[...]) * f32(ah[...])
        before = jnp.where(has_prev, f32(acp_r[hr - 1:hr, :]) * f32(ahp_r[hr - 1:hr, :]), 0.0)
        after = jnp.where(has_next, f32(acn_r[0:1, :]) * f32(ahn_r[0:1, :]), 0.0)
        down, up = _shift_rows(ch, before, after)
        cv = down * cw_r[0:1, :] + ch * cw_r[1:2, :] + up * cw_r[2:3, :]
        cv_ref[...] = cv.astype(BF16)
        y1 = f32(ab[...]) * cv
        yn = (y1 * lax.rsqrt(jnp.mean(y1 * y1, axis=-1, keepdims=True) + EPS)) * cnw_r[...]
        yc = _silu(f32(az[...])) * yn
        y_ref[:, 0:dc] = yc.astype(BF16)
        yt_ref[0:dc, :] = yc.T.astype(BF16)
        for h in range(nh):
            sl = slice(h * HEAD, (h + 1) * HEAD)
            o = f32(of_r[:, sl]) + f32(ob_r[:, sl])
            dev = o - jnp.mean(o, axis=-1, keepdims=True)
            on = dev * lax.rsqrt(jnp.mean(dev * dev, axis=-1, keepdims=True) + EPS)
            yr = _silu(f32(rz[:, sl])) * (on * rnw_r[:, sl])
            y_ref[:, dc + h * HEAD:dc + (h + 1) * HEAD] = yr.astype(BF16)
            yt_ref[dc + h * HEAD:dc + (h + 1) * HEAD, :] = yr.T.astype(BF16)

    vec = pl.BlockSpec((1, dc), lambda i: (0, 0))
    return pl.pallas_call(
        body, name=name,
        out_shape=(jax.ShapeDtypeStruct((t, 2 * dc), BF16), jax.ShapeDtypeStruct((2 * dc, t), BF16),
                   jax.ShapeDtypeStruct((t, dc), BF16)), grid=(nt,),
        in_specs=[col(0), col(1), col(2), col(3), col(7), acp, acn, ahp, ahn,
                  pl.BlockSpec((tr, dc), lambda i: (i, 0)), pl.BlockSpec((tr, dc), lambda i: (i, 0)),
                  pl.BlockSpec((8, dc), lambda i: (0, 0)), vec, vec],
        out_specs=(pl.BlockSpec((tr, 2 * dc), lambda i: (i, 0)), pl.BlockSpec((2 * dc, tr), lambda i: (0, i)),
                   pl.BlockSpec((tr, dc), lambda i: (i, 0))),
        compiler_params=_params("parallel"),
    )(u, u, u, u, u, u, u, u, u, o_f, o_b, cw, cnw, rnw)


def _mix_bwd(dy, u, cv, o_f, o_b, cnw, rnw, name):
    t, w = u.shape
    dc = w // 8
    nh = dc // HEAD
    tr = ROW_TILE
    col = lambda cidx: pl.BlockSpec((tr, dc), lambda i: (i, cidx))

    def body(dyc, dyr, ab, az, rz, cv_r, of_r, ob_r, cnw_r, rnw_r, dab, daz, drz, dcv, do, acc_ref, part_ref):
        part = _Partials(part_ref, 1)

        @pl.when(pl.program_id(0) == 0)
        def _():
            acc_ref[...] = jnp.zeros_like(acc_ref)

        part.zero()

        def chunk(r0, q):
            rows = pl.ds(r0, ROW_CHUNK)
            z, b, c_out = az[rows, :].astype(F32), ab[rows, :].astype(F32), cv_r[rows, :].astype(F32)
            nwv = cnw_r[...]
            y1 = b * c_out
            r = lax.rsqrt(jnp.mean(y1 * y1, axis=-1, keepdims=True) + EPS)
            yh = y1 * r
            dyc_v = dyc[rows, :].astype(F32)
            sg = jax.nn.sigmoid(z)
            daz[rows, :] = (dyc_v * (yh * nwv) * (sg * (1.0 + z * (1.0 - sg)))).astype(BF16)
            dyn = dyc_v * (z * sg)
            part.add(q, 0, _fold8(dyn * yh))
            dyh = dyn * nwv
            dy1 = r * (dyh - yh * jnp.mean(dyh * yh, axis=-1, keepdims=True))
            dab[rows, :] = (dy1 * c_out).astype(BF16)
            dcv[rows, :] = dy1 * b

        _for_row_chunks(tr, chunk)
        acc_ref[0:1, :] += part.total(0)
        for h in range(nh):
            sl = slice(h * HEAD, (h + 1) * HEAD)
            o = of_r[:, sl].astype(F32) + ob_r[:, sl].astype(F32)
            dev = o - jnp.mean(o, axis=-1, keepdims=True)
            rs = lax.rsqrt(jnp.mean(dev * dev, axis=-1, keepdims=True) + EPS)
            on = dev * rs
            zz, dyr_v, gw = rz[:, sl].astype(F32), dyr[:, sl].astype(F32), rnw_r[:, sl]
            sg = jax.nn.sigmoid(zz)
            drz[:, sl] = (dyr_v * (on * gw) * (sg * (1.0 + zz * (1.0 - sg)))).astype(BF16)
            dong = dyr_v * (zz * sg)
            acc_ref[1:2, sl] += jnp.sum(dong * on, axis=0, keepdims=True)
            don = dong * gw
            do[:, sl] = (rs * (don - jnp.mean(don, axis=-1, keepdims=True)
                               - on * jnp.mean(don * on, axis=-1, keepdims=True))).astype(BF16)

    vec = pl.BlockSpec((1, dc), lambda i: (0, 0))
    row = pl.BlockSpec((tr, dc), lambda i: (i, 0))
    return pl.pallas_call(
        body, name=name,
        out_shape=(jax.ShapeDtypeStruct((t, dc), BF16), jax.ShapeDtypeStruct((t, dc), BF16),
                   jax.ShapeDtypeStruct((t, dc), BF16), jax.ShapeDtypeStruct((t, dc), F32),
                   jax.ShapeDtypeStruct((t, dc), BF16), jax.ShapeDtypeStruct((8, dc), F32)),
        grid=(t // tr,),
        in_specs=[col(0), col(1), col(1), col(3), col(7), row, row, row, vec, vec],
        out_specs=(row, row, row, row, row, pl.BlockSpec((8, dc), lambda i: (0, 0))),
        scratch_shapes=[_Partials.scratch(1, dc)],
        compiler_params=_params("arbitrary"),
    )(dy, dy, u, u, u, cv, o_f, o_b, cnw, rnw)


def _rope_masks():
    lane = lax.broadcasted_iota(jnp.int32, (1, HEAD), 1)
    return (lane % (HEAD // 2)) >= (HEAD // 4)


def _rot(a, upper):
    return jnp.where(upper, pltpu.roll(a, HEAD // 4, 1), pltpu.roll(a, HEAD - HEAD // 4, 1))


def _assemble_du(dcv, u, cw, dab, daz, drz, dq_f, dq_b, dk_f, dk_b, dv_f, dv_b, cos, sin, nct, name):
    t, w = u.shape
    dc = w // 8
    nh = dc // HEAD
    tr = ROW_TILE
    nt = t // tr
    k_scale = HEAD ** -0.5
    col = lambda cidx: pl.BlockSpec((tr, dc), lambda i: (i, cidx))
    row = pl.BlockSpec((tr, dc), lambda i: (i, 0))
    dcp, dcn = _halo_specs(tr, dc, 0, t, 8)
    f32 = lambda v: v.astype(F32)

    def body(dcv_r, dcp_r, dcn_r, ah, ac, cw_r, dab_r, daz_r, drz_r, dqf, dqb, dkf, dkb, dvf, dvb, cos_r, sin_r,
             du, acc_ref, pad, part_ref):
        part = _Partials(part_ref, 3)
        i = pl.program_id(0)

        @pl.when(i == 0)
        def _():
            acc_ref[...] = jnp.zeros_like(acc_ref)

        has_prev, has_next = _seq_edges(i, nct, nt)
        pad[0:8, :] = jnp.where(has_prev, dcp_r[...], 0.0)
        pad[8:8 + tr, :] = dcv_r[...]
        pad[8 + tr:16 + tr, :] = jnp.where(has_next, dcn_r[...], 0.0)
        part.zero()
        upper = _rope_masks()

        def chunk(r0, q):
            rows = pl.ds(r0, ROW_CHUNK)
            d_c = pad[pl.ds(pl.multiple_of(r0 + 8, 8), ROW_CHUNK), :]
            d_prev, d_next = _shift_rows(d_c, pad[pl.ds(r0, 8), :][7:8],
                                         pad[pl.ds(pl.multiple_of(r0 + 8 + ROW_CHUNK, 8), 8), :][0:1])
            a_h, a_c = f32(ah[rows, :]), f32(ac[rows, :])
            ch = a_c * a_h
            dch = d_next * cw_r[0:1, :] + d_c * cw_r[1:2, :] + d_prev * cw_r[2:3, :]
            part.add(q, 0, _fold8(ch * d_next))
            part.add(q, 1, _fold8(ch * d_c))
            part.add(q, 2, _fold8(ch * d_prev))
            du[rows, 0:dc] = (dch * a_c).astype(BF16)
            du[rows, dc:2 * dc] = dab_r[rows, :]
            du[rows, 2 * dc:3 * dc] = (dch * a_h).astype(BF16)
            du[rows, 3 * dc:4 * dc] = daz_r[rows, :]
            du[rows, 7 * dc:8 * dc] = drz_r[rows, :]
            du[rows, 6 * dc:7 * dc] = (f32(dvf[rows, :]) + f32(dvb[rows, :])).astype(BF16)
            cs, sn = cos_r[rows, :], sin_r[rows, :]
            for h in range(nh):
                sl = slice(h * HEAD, (h + 1) * HEAD)
                dq = f32(dqf[rows, sl]) + f32(dqb[rows, sl])
                dk = (f32(dkf[rows, sl]) + f32(dkb[rows, sl])) * k_scale
                du[rows, 4 * dc + h * HEAD:4 * dc + (h + 1) * HEAD] = (dq * cs + _rot(dq * sn, upper)).astype(BF16)
                du[rows, 5 * dc + h * HEAD:5 * dc + (h + 1) * HEAD] = (dk * cs + _rot(dk * sn, upper)).astype(BF16)

        _for_row_chunks(tr, chunk)
        for tap in range(3):
            acc_ref[tap:tap + 1, :] += part.total(tap)

    tab = pl.BlockSpec((tr, HEAD), lambda i: (i, 0))
    return pl.pallas_call(
        body, name=name,
        out_shape=(jax.ShapeDtypeStruct((t, w), BF16), jax.ShapeDtypeStruct((8, dc), F32)), grid=(nt,),
        in_specs=[row, dcp, dcn, col(0), col(2), pl.BlockSpec((8, dc), lambda i: (0, 0)),
                  row, row, row, row, row, row, row, row, row, tab, tab],
        out_specs=(pl.BlockSpec((tr, w), lambda i: (i, 0)), pl.BlockSpec((8, dc), lambda i: (0, 0))),
        scratch_shapes=[pltpu.VMEM((tr + 16, dc), F32), _Partials.scratch(3, dc)],
        compiler_params=_params("arbitrary"),
    )(dcv, dcv, dcv, u, u, cw, dab, daz, drz, dq_f, dq_b, dk_f, dk_b, dv_f, dv_b, cos, sin)


def _bwd_dir_chunk(s, ncc, ns):
    return jnp.where(s < ncc, ncc - 1 - s, ns + ncc - 1 - s)


def _position_weights(d):
    c = CHUNK
    ii = lax.broadcasted_iota(jnp.int32, (c, c), 0)
    jj = lax.broadcasted_iota(jnp.int32, (c, c), 1)
    sd = ((ii - jj) if d == 0 else (jj - ii)).astype(F32)
    pos = lax.broadcasted_iota(jnp.int32, (c, HEAD), 0).astype(F32)
    qw, kw = (pos + 1.0, (c - 1.0) - pos) if d == 0 else (c - pos, pos)
    return sd, qw, kw


def _fill_decay_tables(dec_ref, nh, dm_t, qd_t, kd_t, gc_t):
    for d in range(2):
        sd, qw, kw = _position_weights(d)
        for h in range(nh):
            r = d * nh + h
            lg = -jnp.exp(jnp.zeros((1, HEAD), F32) + dec_ref[d, h])
            dm_t[r] = jnp.where(sd >= 0, jnp.exp(lg * jnp.maximum(sd, 0.0)), 0.0)
            qd_t[r] = jnp.exp(lg * qw)
            kd_t[r] = jnp.exp(lg * kw)
            gc_t[r] = jnp.zeros((8, HEAD), F32) + jnp.exp(lg * float(CHUNK))


def _decay_table_scratch(nh):
    return [pltpu.VMEM((2 * nh, CHUNK, CHUNK), F32), pltpu.VMEM((2 * nh, CHUNK, HEAD), F32),
            pltpu.VMEM((2 * nh, CHUNK, HEAD), F32), pltpu.VMEM((2 * nh, 8, HEAD), F32)]


def _ret_fwd(u, cos, sin, dec, ncc, name):
    t, w = u.shape
    dc = w // 8
    nh = dc // HEAD
    c = CHUNK
    ns = t // c
    k_scale = HEAD ** -0.5
    per = RET_STEP_CHUNKS if ncc % RET_STEP_CHUNKS == 0 and ns % RET_STEP_CHUNKS == 0 else 1
    nsb, nbc = ns // per, ncc // per
    fmap = lambda g: (lambda s: (s, g))
    bmap = lambda g: (lambda s: (_bwd_dir_chunk(s, nbc, nsb), g))
    blk = lambda m: pl.BlockSpec((per * c, dc), m)
    tab = lambda m: pl.BlockSpec((per * c, HEAD), m)
    st_f = pl.BlockSpec((nh, per, HEAD, HEAD), lambda s: (0, s, 0, 0))
    st_b = pl.BlockSpec((nh, per, HEAD, HEAD), lambda s: (0, _bwd_dir_chunk(s, nbc, nsb), 0, 0))

    def body(qf, kf, vf, qb, kb, vb, cf, sf, cb, sb, dec_ref, of_ref, ob_ref, stf_ref, stb_ref, qr_ref, kr_ref,
             s_f, s_b, dm_t, qd_t, kd_t, gc_t):
        @pl.when(pl.program_id(0) == 0)
        def _():
            s_f[...] = jnp.zeros_like(s_f)
            s_b[...] = jnp.zeros_like(s_b)
            _fill_decay_tables(dec_ref, nh, dm_t, qd_t, kd_t, gc_t)

        upper = _rope_masks()
        dirs = ((qf, kf, vf, cf, sf, of_ref, stf_ref, s_f), (qb, kb, vb, cb, sb, ob_ref, stb_ref, s_b))
        group = RET_INTERLEAVE if nh % RET_INTERLEAVE == 0 else 1
        blocks = [[(h, d) for h in range(h0, h0 + group) for d in range(2)] for h0 in range(0, nh, group)]
        rows_of = lambda d, sub: (sub if d == 0 else per - 1 - sub)
        for sub, blk_list in [(sub, bl) for sub in range(per) for bl in blocks]:
            first = []
            for h, d in blk_list:
                q_r, k_r, v_r, cs_r, sn_r, _, st_ref, st = dirs[d]
                sl = slice(h * HEAD, (h + 1) * HEAD)
                cc = rows_of(d, sub)
                rs = slice(cc * c, (cc + 1) * c)
                r = d * nh + h
                cs, sn = cs_r[rs, :], sn_r[rs, :]
                q = q_r[rs, sl].astype(F32)
                q = q * cs + _rot(q, upper) * sn
                k = k_r[rs, sl].astype(F32)
                k = (k * cs + _rot(k, upper) * sn) * k_scale
                if d == 0:
                    qr_ref[rs, sl] = q
                    kr_ref[rs, sl] = k
                vv = v_r[rs, sl]
                state = st[h]
                st_ref[h, cc] = state
                p = _nt(_bf(q), _bf(k)) * dm_t[r]
                kv = _tn(_bf(k * kd_t[r]), vv)
                first.append((q, vv, state, p, kv))
            for idx, (h, d) in enumerate(blk_list):
                _, _, _, _, _, o_ref, _, st = dirs[d]
                sl = slice(h * HEAD, (h + 1) * HEAD)
                cc = rows_of(d, sub)
                r = d * nh + h
                q, vv, state, p, kv = first[idx]
                o_ref[cc * c:(cc + 1) * c, sl] = _bf(_nn(jnp.concatenate([_bf(p), _bf(q * qd_t[r])], axis=1),
                                                         jnp.concatenate([vv, _bf(state)], axis=0)))
                st[h] = gc_t[r, 0:1, :] * state + kv

    return pl.pallas_call(
        body, name=name,
        out_shape=(jax.ShapeDtypeStruct((t, dc), BF16), jax.ShapeDtypeStruct((t, dc), BF16),
                   jax.ShapeDtypeStruct((nh, ns, HEAD, HEAD), F32), jax.ShapeDtypeStruct((nh, ns, HEAD, HEAD), F32),
                   jax.ShapeDtypeStruct((t, dc), F32), jax.ShapeDtypeStruct((t, dc), F32)),
        grid=(nsb,),
        in_specs=[blk(fmap(4)), blk(fmap(5)), blk(fmap(6)), blk(bmap(4)), blk(bmap(5)), blk(bmap(6)),
                  tab(fmap(0)), tab(fmap(0)), tab(bmap(0)), tab(bmap(0)), SMEM_SPEC],
        out_specs=(blk(fmap(0)), blk(bmap(0)), st_f, st_b, blk(fmap(0)), blk(fmap(0))),
        scratch_shapes=[pltpu.VMEM((nh, HEAD, HEAD), F32), pltpu.VMEM((nh, HEAD, HEAD), F32)]
        + _decay_table_scratch(nh),
        compiler_params=_params("arbitrary"),
    )(u, u, u, u, u, u, cos, sin, cos, sin, dec)


def _ret_bwd(u, qr, kr, do, st_f, st_b, dec, ncc, name):
    t, w = u.shape
    dc = w // 8
    nh = dc // HEAD
    c = CHUNK
    ns = t // c
    per = RET_STEP_CHUNKS if ncc % RET_STEP_CHUNKS == 0 and ns % RET_STEP_CHUNKS == 0 else 1
    nsb, nbc = ns // per, ncc // per
    fchunk = lambda s: nsb - 1 - s
    bchunk = lambda s: _bwd_dir_chunk(nsb - 1 - s, nbc, nsb)
    fmap = lambda g: (lambda s: (fchunk(s), g))
    bmap = lambda g: (lambda s: (bchunk(s), g))
    blk = lambda m: pl.BlockSpec((per * c, dc), m)
    stf_spec = pl.BlockSpec((nh, per, HEAD, HEAD), lambda s: (0, fchunk(s), 0, 0))
    stb_spec = pl.BlockSpec((nh, per, HEAD, HEAD), lambda s: (0, bchunk(s), 0, 0))

    def body(qf, kf, vf, dof, qb, kb, vb, dob, stf_ref, stb_ref, dec_ref,
             dqf, dkf, dvf, dqb, dkb, dvb, dlg_ref, ds_f, ds_b, dm_t, qd_t, kd_t, gc_t, lg_acc):
        step = pl.program_id(0)

        @pl.when(step == 0)
        def _():
            ds_f[...] = jnp.zeros_like(ds_f)
            ds_b[...] = jnp.zeros_like(ds_b)
            lg_acc[...] = jnp.zeros_like(lg_acc)
            _fill_decay_tables(dec_ref, nh, dm_t, qd_t, kd_t, gc_t)

        dirs = ((qf, kf, vf, dof, stf_ref, dqf, dkf, dvf, ds_f), (qb, kb, vb, dob, stb_ref, dqb, dkb, dvb, ds_b))
        weights = [_position_weights(d) for d in range(2)]
        group = RET_INTERLEAVE if nh % RET_INTERLEAVE == 0 else 1
        blocks = [[(h, d) for h in range(h0, h0 + group) for d in range(2)] for h0 in range(0, nh, group)]
        rows_of = lambda d, sub: (per - 1 - sub if d == 0 else sub)
        for sub, blk_list in [(sub, bl) for sub in range(per) for bl in blocks]:
            first = []
            for h, d in blk_list:
                q_r, k_r, v_r, do_r, st_ref, _, _, _, ds = dirs[d]
                sl = slice(h * HEAD, (h + 1) * HEAD)
                cc = rows_of(d, sub)
                rs = slice(cc * c, (cc + 1) * c)
                r = d * nh + h
                q, k = q_r[rs, sl], k_r[rs, sl]
                qb16, kb16, vb16, dob16 = _bf(q), _bf(k), v_r[rs, sl], _bf(do_r[rs, sl])
                state, dstate = st_ref[h, cc], ds[h]
                dsb16 = _bf(dstate)
                a = _nt(qb16, kb16) * dm_t[r]
                both = _nt(dob16, jnp.concatenate([vb16, _bf(state)], axis=0))
                dk_i = _nt(vb16, dsb16) * kd_t[r]
                first.append((q, k, qb16, kb16, dob16, state, dstate, dsb16, a, both, dk_i))
            for idx, (h, d) in enumerate(blk_list):
                _, _, _, _, _, dq_ref, dk_ref, dv_ref, ds = dirs[d]
                sl = slice(h * HEAD, (h + 1) * HEAD)
                cc = rows_of(d, sub)
                rs = slice(cc * c, (cc + 1) * c)
                r = d * nh + h
                q, k, qb16, kb16, dob16, state, dstate, dsb16, a, both, dk_i = first[idx]
                sd, qw, kw = weights[d]
                da = both[:, 0:c]
                dp = _bf(da * dm_t[r])
                dq_i = both[:, c:2 * c] * qd_t[r]
                dq_ref[rs, sl] = _bf(_nn(dp, kb16) + dq_i)
                dk_ref[rs, sl] = _bf(_tn(dp, qb16) + dk_i)
                dv_ref[rs, sl] = _bf(_nn(jnp.concatenate([_bf(a.T), _bf(k * kd_t[r])], axis=1),
                                         jnp.concatenate([dob16, dsb16], axis=0)))
                gcr = gc_t[r, 0:1, :]
                ds[h] = gcr * dstate + _tn(_bf(q * qd_t[r]), dob16)
                lg_acc[r] += (da * a * jnp.maximum(sd, 0.0) + (dq_i * q) * qw + (dk_i * k) * kw
                              + (float(c) * gcr) * (state * dstate))

        @pl.when(step == nsb - 1)
        def _():
            for d in range(2):
                for h in range(nh):
                    r = d * nh + h
                    lg = -jnp.exp(jnp.zeros((1, HEAD), F32) + dec_ref[d, h])
                    dlg_ref[r:r + 1, :] = (jnp.zeros((1, HEAD), F32) + jnp.sum(lg_acc[r])) * lg

    dshape = jax.ShapeDtypeStruct((t, dc), BF16)
    return pl.pallas_call(
        body, name=name,
        out_shape=(dshape, dshape, dshape, dshape, dshape, dshape, jax.ShapeDtypeStruct((2 * nh, HEAD), F32)),
        grid=(nsb,),
        in_specs=[blk(fmap(0)), blk(fmap(0)), blk(fmap(6)), blk(fmap(0)),
                  blk(bmap(0)), blk(bmap(0)), blk(bmap(6)), blk(bmap(0)), stf_spec, stb_spec, SMEM_SPEC],
        out_specs=(blk(fmap(0)), blk(fmap(0)), blk(fmap(0)), blk(bmap(0)), blk(bmap(0)), blk(bmap(0)),
                   pl.BlockSpec((2 * nh, HEAD), lambda s: (0, 0))),
        scratch_shapes=[pltpu.VMEM((nh, HEAD, HEAD), F32), pltpu.VMEM((nh, HEAD, HEAD), F32)]
        + _decay_table_scratch(nh) + [pltpu.VMEM((2 * nh, CHUNK, HEAD), F32)],
        compiler_params=_params("arbitrary"),
    )(qr, kr, u, do, qr, kr, u, do, st_f, st_b, dec)


def _mod_matvec(c16, w_mod, b_mod):
    nl, d, nm = w_mod.shape
    tn = _pick(nm, 512, 128)

    def body(c_ref, w_ref, b_ref, sc_ref, o_ref):
        sc = _silu(c_ref[...])
        sc_ref[...] = sc
        o_ref[...] = _nn(_bf(sc), _bf(w_ref[...])) + b_ref[...]

    return pl.pallas_call(
        body, name="mod_matvec",
        out_shape=(jax.ShapeDtypeStruct((16, d), F32), jax.ShapeDtypeStruct((nl, 16, nm), F32)),
        grid=(nl, nm // tn),
        in_specs=[pl.BlockSpec((16, d), lambda l, j: (0, 0)), pl.BlockSpec((None, d, tn), lambda l, j: (l, 0, j)),
                  pl.BlockSpec((None, 1, tn), lambda l, j: (l, 0, j))],
        out_specs=(pl.BlockSpec((16, d), lambda l, j: (0, 0)), pl.BlockSpec((None, 16, tn), lambda l, j: (l, 0, j))),
        compiler_params=_params("arbitrary", "arbitrary"),
    )(c16, w_mod, b_mod)


def _wmod_grad(sc16, dm, w_mod):
    nl, d, nm = w_mod.shape
    tn = _pick(nm, 512, 128)

    def body(sc_ref, dm_ref, w_ref, gw_ref, pc_ref):
        @pl.when(jnp.logical_and(pl.program_id(0) == 0, pl.program_id(1) == 0))
        def _():
            pc_ref[...] = jnp.zeros_like(pc_ref)

        dmb = _bf(dm_ref[...])
        gw_ref[...] = _tn(_bf(sc_ref[...]), dmb)
        pc_ref[...] += _nt(dmb, _bf(w_ref[...]))

    return pl.pallas_call(
        body, name="wmod_grad",
        out_shape=(jax.ShapeDtypeStruct((nl, d, nm), F32), jax.ShapeDtypeStruct((16, d), F32)),
        grid=(nl, nm // tn),
        in_specs=[pl.BlockSpec((16, d), lambda l, j: (0, 0)), pl.BlockSpec((None, 16, tn), lambda l, j: (l, 0, j)),
                  pl.BlockSpec((None, d, tn), lambda l, j: (l, 0, j))],
        out_specs=(pl.BlockSpec((None, d, tn), lambda l, j: (l, 0, j)), pl.BlockSpec((16, d), lambda l, j: (0, 0))),
        compiler_params=_params("arbitrary", "arbitrary"),
    )(sc16, dm, w_mod)


def _rowsum(a, name):
    r, n = a.shape
    tn = _pick(n, 4096, 128)

    def body(a_ref, o_ref):
        acc = a_ref[0:1, :]
        for i in range(1, r):
            acc = acc + a_ref[i:i + 1, :]
        o_ref[...] = jnp.zeros((8, tn), F32) + acc

    return pl.pallas_call(
        body, name=name, out_shape=jax.ShapeDtypeStruct((8, n), F32), grid=(n // tn,),
        in_specs=[pl.BlockSpec((r, tn), lambda j: (0, j))], out_specs=pl.BlockSpec((8, tn), lambda j: (0, j)),
        compiler_params=_params("parallel"),
    )(a)


def _cctx_grad(parts, c_ctx8):
    _, _, d = parts.shape

    def body(p_ref, c_ref, o_ref):
        acc = ((p_ref[0] + p_ref[1]) + p_ref[2]) + p_ref[3]
        o_ref[...] = acc * _dsilu(c_ref[...])

    return pl.pallas_call(
        body, name="cctx_grad", out_shape=jax.ShapeDtypeStruct((8, d), F32),
        in_specs=[VMEM_SPEC, VMEM_SPEC], out_specs=VMEM_SPEC,
    )(parts, c_ctx8)


def _adamw(g, w, m, v, name):
    r, n = w.shape
    tr = _pick(r, max(8, min(256, (512 * 1024) // n // 8 * 8)), 8)
    bc1 = 1.0 - ADAM_B1 ** ADAM_STEP
    bc2 = 1.0 - ADAM_B2 ** ADAM_STEP

    def body(g_ref, w_ref, m_ref, v_ref, d_ref, mo_ref, vo_ref):
        gg = g_ref[...]
        mn = ADAM_B1 * m_ref[...] + (1.0 - ADAM_B1) * gg
        vn = ADAM_B2 * v_ref[...] + (1.0 - ADAM_B2) * (gg * gg)
        mo_ref[...] = mn
        vo_ref[...] = vn
        d_ref[...] = -ADAM_LR * ((mn / bc1) / (jnp.sqrt(vn / bc2) + ADAM_EPS) + ADAM_WD * w_ref[...])

    spec = pl.BlockSpec((tr, n), lambda i: (i, 0))
    shp = jax.ShapeDtypeStruct((r, n), F32)
    return pl.pallas_call(
        body, name=name, out_shape=(shp, shp, shp), grid=(r // tr,),
        in_specs=[spec, spec, spec, spec], out_specs=(spec, spec, spec),
        compiler_params=_params("parallel"),
    )(g, w, m, v)


def _adamw_layer(g, w, m, v, layer, prev, name):
    r, n = g.shape
    tr = _pick(r, max(8, min(256, (512 * 1024) // n // 8 * 8)), 8)
    nb = r // tr
    bc1 = 1.0 - ADAM_B1 ** ADAM_STEP
    bc2 = 1.0 - ADAM_B2 ** ADAM_STEP
    n_prev = 0 if prev is None else 4

    def body(g_ref, w_ref, m_ref, v_ref, *rest):
        go_ref, d_ref, mo_ref, vo_ref = rest[n_prev:]
        gg = g_ref[...]
        mn = ADAM_B1 * m_ref[...] + (1.0 - ADAM_B1) * gg
        vn = ADAM_B2 * v_ref[...] + (1.0 - ADAM_B2) * (gg * gg)
        go_ref[...] = gg
        mo_ref[...] = mn
        vo_ref[...] = vn
        d_ref[...] = -ADAM_LR * ((mn / bc1) / (jnp.sqrt(vn / bc2) + ADAM_EPS) + ADAM_WD * w_ref[...])

    spec = pl.BlockSpec((tr, n), lambda i: (layer * nb + i, 0))
    shp = jax.ShapeDtypeStruct(w.shape, F32)
    return pl.pallas_call(
        body, name=name, out_shape=(shp, shp, shp, shp), grid=(nb,),
        in_specs=[pl.BlockSpec((tr, n), lambda i: (i, 0)), spec, spec, spec] + [ANY] * n_prev,
        out_specs=(spec, spec, spec, spec),
        input_output_aliases={4 + k: k for k in range(n_prev)},
        compiler_params=_params("parallel"),
    )(g, w, m, v, *(prev or ()))


def _rope_tables(seq, n_ctx):
    f = HEAD // 4
    inv = ROPE_BASE ** (-jnp.arange(f, dtype=F32) / f)
    n_rows = seq // GRID_W
    ang_r = jnp.arange(n_rows, dtype=F32)[:, None] * inv[None, :]
    ang_c = jnp.arange(GRID_W, dtype=F32)[:, None] * inv[None, :]
    by_row = lambda a: jnp.repeat(a, GRID_W, axis=0)
    by_col = lambda a: jnp.tile(a, (n_rows, 1))
    cos_r, sin_r, cos_c, sin_c = by_row(jnp.cos(ang_r)), by_row(jnp.sin(ang_r)), by_col(jnp.cos(ang_c)), by_col(jnp.sin(ang_c))
    cos = jnp.concatenate([cos_r, cos_r, cos_c, cos_c], axis=-1)
    sin = jnp.concatenate([-sin_r, sin_r, -sin_c, sin_c], axis=-1)
    cos = jnp.concatenate([jnp.ones((n_ctx, HEAD), F32), cos], axis=0)
    sin = jnp.concatenate([jnp.zeros((n_ctx, HEAD), F32), sin], axis=0)
    return cos, sin


def _pad_rows(a, rows):
    return jnp.concatenate([a, jnp.zeros((rows - a.shape[0],) + a.shape[1:], a.dtype)], axis=0)


def _pad_flat(parts, mult):
    flat = jnp.concatenate([p.reshape(-1) for p in parts])
    pad = (-flat.shape[0]) % mult
    return jnp.concatenate([flat, jnp.zeros((pad,), flat.dtype)])


def kernel(x, c, ctx, c_ctx, norm_w, w_mod, b_mod, w_in, conv_w, conv_norm_w, ret_norm_w, ret_decay_f, ret_decay_b, w_out, final_norm_w, loss_target, m_c_ctx, m_norm_w, m_w_mod, m_b_mod, m_w_in, m_conv_w, m_conv_norm_w, m_ret_norm_w, m_ret_decay_f, m_ret_decay_b, m_w_out, m_final_norm_w, v_c_ctx, v_norm_w, v_w_mod, v_b_mod, v_w_in, v_conv_w, v_conv_norm_w, v_ret_norm_w, v_ret_decay_f, v_ret_decay_b, v_w_out, v_final_norm_w):
    xi, yi, ci = _mesh_pos()
    b_idx = 4 * xi + 2 * yi + ci
    s_idx = 2 * xi + yi
    pos = jnp.stack([ci, s_idx]).astype(jnp.int32)

    x2, ctx2, tgt = x[0], ctx[0], loss_target[0]
    seq, d = x2.shape
    n_ctx = ctx2.shape[0]
    t = seq + n_ctx
    dc = d // 2
    nh = dc // HEAD
    nl, _, nm = w_mod.shape
    n4 = w_in.shape[2]
    r4 = w_out.shape[1]
    cw4 = conv_w.shape[2]
    nct = n_ctx // ROW_TILE
    ncc = n_ctx // CHUNK
    assert n_ctx % ROW_TILE == 0 and seq % ROW_TILE == 0 and dc % HEAD == 0 and nl * 3 <= 8

    win_b, wout_b = w_in.astype(BF16), w_out.astype(BF16)
    wg_in, wg_out = [None] * nl, [None] * nl
    shard_order = jnp.stack([s_idx, 2 * (1 - xi) + yi, 2 * xi + (1 - yi), 2 * (1 - xi) + (1 - yi)]).astype(jnp.int32)

    small = jnp.concatenate([jnp.broadcast_to(c, (8, d)), _pad_rows(conv_w.reshape(nl * 3, cw4), 8)], axis=1)
    small_g = _allgather_small(small, "gather_c_convw").reshape(N_DEV, 8, d + cw4)
    c_all = small_g[:, 0, :d]
    cw_full = small_g[0::2, :, d:].transpose(1, 0, 2).reshape(8, 4 * cw4)
    cw_l = [_pad_rows(cw_full[l * 3:(l + 1) * 3], 8) for l in range(nl)]

    c16 = _pad_rows(jnp.concatenate([c_all, c_ctx[None]], axis=0), 16)
    b_shard = lax.dynamic_slice(b_mod, (0, s_idx * nm), (nl, nm)).reshape(nl, 1, nm)
    sc16, modp = _mod_matvec(c16, w_mod, b_shard)
    modg = _allgather_small(modp.reshape(nl * 16, nm), "gather_mod").reshape(N_DEV, nl, 16, nm)
    mod_full = modg[0::2].transpose(1, 2, 0, 3).reshape(nl, 16, 4 * nm)
    mod_me = lax.dynamic_index_in_dim(mod_full, b_idx, axis=1, keepdims=False)
    mod_cx = mod_full[:, 8]
    zrow = jnp.zeros((d,), F32)
    rows = []
    for l in range(nl):
        gate_c = mod_cx[l, 2 * d:] if l < nl - 1 else zrow
        rows.append(jnp.stack([mod_me[l, :d], mod_me[l, d:2 * d], mod_me[l, 2 * d:],
                               mod_cx[l, :d], mod_cx[l, d:2 * d], gate_c, zrow, zrow]))

    cos, sin = _rope_tables(seq, n_ctx)
    xs = jnp.concatenate([ctx2, x2], axis=0)
    saved = []
    for l in range(nl):
        dec = jnp.stack([ret_decay_f[l], ret_decay_b[l]])
        hx, hx_t = _modulate(xs, norm_w[l][None], rows[l], nct, f"modulate_{l}")
        u, wg_in[l], wg_out[l] = _in_proj_gathered(hx, win_b[l], shard_order, _gather_comm(wout_b[l], False),
                                                   f"in_proj_{l}")
        o_f, o_b, st_f, st_b, qr, kr = _ret_fwd(u, cos, sin, dec, ncc, f"ret_fwd_{l}")
        y, y_t, cv = _mix_fwd(u, o_f, o_b, cw_l[l], conv_norm_w[l][None], ret_norm_w[l][None], nct, f"mix_fwd_{l}")
        xs_new, proj = _out_proj(y, wg_out[l], xs, rows[l], n_ctx, f"out_proj_{l}")
        saved.append((xs, hx_t, u, o_f, o_b, st_f, st_b, qr, kr, y_t, cv, proj, dec))
        xs = xs_new

    g, facc, gp, gacc = _final_loss(xs, tgt, final_norm_w[None], (saved[nl - 1][11], rows[nl - 1]), nct, "final_loss")
    full = [None] * nl
    join = []
    mine, other = ci.reshape(1).astype(jnp.int32), (1 - ci).reshape(1).astype(jnp.int32)
    dmx, dmc, gnw, gcnw, grnw, gcw, gdf, gdb = ([None] * nl for _ in range(8))
    for l in reversed(range(nl)):
        xs_l, hx_t, u, o_f, o_b, st_f, st_b, qr, kr, y_t, cv, proj, dec = saved[l]
        gout = _matmul(y_t, gp, "nn", f"out_proj_dw_{l}", tm_t=1024, tn_t=1024, tk_t=2816, comms=join)
        if join:
            gout, *full[l + 1] = gout
        dy = _matmul(gp, wg_out[l], "nt", f"out_proj_dx_{l}", tn_t=2048, out_dtype=BF16)
        dab, daz, drz, dcv, do, macc = _mix_bwd(dy, u, cv, o_f, o_b, conv_norm_w[l][None], ret_norm_w[l][None],
                                                f"mix_bwd_{l}")
        dqf, dkf, dvf, dqb, dkb, dvb, dlg = _ret_bwd(u, qr, kr, do, st_f, st_b, dec, ncc, f"ret_bwd_{l}")
        du, cacc = _assemble_du(dcv, u, cw_l[l], dab, daz, drz, dqf, dqb, dkf, dkb, dvf, dvb, cos, sin, nct,
                                f"assemble_du_{l}")
        gout_v = gout.reshape(4, 2, r4 // 2, d)
        gin_other = _matmul(hx_t, du, "nn", f"in_proj_dw_other_{l}", tm_t=1024, tn_t=1024, tk_t=2816, row_half=other)
        gin_mine, land_in, land_out = _matmul(hx_t, du, "nn", f"in_proj_dw_mine_{l}", tm_t=1024, tn_t=1024,
                                              tk_t=2816, row_half=mine, comms=[_swap_comm(gin_other, gout_v)])
        p_in = _sum_half_in(gin_mine, land_in, f"grad_sum_half_in_{l}")
        p_out = _sum_half_out(pos, gout_v, land_out, f"grad_sum_half_out_{l}")
        dhx, l3_in, l3_out = _matmul(du, wg_in[l], "nt", f"in_proj_dx_{l}", tn_t=2048,
                                     comms=[_exchange_comm(p_in, p_out, n4)],
                                     out_dtype=BF16)
        join = [_join_comm(_sum_quarters_in(pos, p_in, l3_in, n4, f"grad_sum_quarters_in_{l}"),
                           _sum_quarters_out(pos, p_out, l3_out, f"grad_sum_quarters_out_{l}"))]
        dmx_gate, dmc_gate = gacc[2], gacc[5]
        below = (saved[l - 1][11], rows[l - 1]) if l > 0 else None
        g, nacc, *nxt = _modulate_bwd(dhx, xs_l, g, norm_w[l][None], rows[l], nct, f"modulate_bwd_{l}", below)
        if nxt:
            gp, gacc = nxt
        dmx[l] = jnp.concatenate([nacc[0], nacc[1], dmx_gate])
        dmc[l] = jnp.concatenate([nacc[3], nacc[4], dmc_gate])
        gnw[l], gcnw[l], grnw[l], gcw[l] = nacc[6], macc[0], macc[1], cacc[0:3]
        gdf[l], gdb[l] = dlg[0:nh, 0], dlg[nh:2 * nh, 0]
    grad_x = g[None]
    full[0] = _run_comm(join[0], "grad_join_0")

    res_in, res_out = None, None
    for l in reversed(range(nl)):
        res_in = _adamw_layer(full[l][0].reshape(d, n4), w_in.reshape(nl * d, n4), m_w_in.reshape(nl * d, n4),
                              v_w_in.reshape(nl * d, n4), l, res_in, f"adamw_w_in_{l}")
        res_out = _adamw_layer(full[l][1].reshape(r4, d), w_out.reshape(nl * r4, d), m_w_out.reshape(nl * r4, d),
                               v_w_out.reshape(nl * r4, d), l, res_out, f"adamw_w_out_{l}")
    g_w_in, d_w_in, nm_w_in, nv_w_in = res_in
    g_w_out, d_w_out, nm_w_out, nv_w_out = res_out

    seg = [jnp.stack(dmx), jnp.stack(dmc), jnp.stack(gnw), jnp.stack(gcnw), jnp.stack(grnw), facc[0],
           jnp.stack(gcw), jnp.stack(gdf), jnp.stack(gdb), facc[1, 0:1]]
    sizes = [int(s.size) for s in seg]
    offs = [sum(sizes[:i]) for i in range(len(sizes))]
    packed = _pad_flat(seg, 8 * 128)
    pw = packed.shape[0] // 8
    allv = _allgather_small(packed.reshape(8, pw), "gather_small_grads").reshape(N_DEV, 8 * pw)
    n_mod = nl * 3 * d
    dmx_all, dmc_all = allv[:, offs[0]:offs[0] + n_mod], allv[:, offs[1]:offs[1] + n_mod]
    rest_all = allv[:, offs[2]:]
    zeros_mod = jnp.zeros_like(dmc_all)
    stack = jnp.concatenate([
        jnp.concatenate([dmx_all, dmc_all], axis=0),
        jnp.concatenate([dmc_all, zeros_mod], axis=0),
        jnp.concatenate([rest_all, jnp.zeros_like(rest_all)], axis=0)], axis=1)
    sums = _rowsum(stack, "sum_small_grads")[0]
    g_b_mod = sums[:n_mod].reshape(nl, 3 * d)
    dmc_tot = sums[n_mod:2 * n_mod].reshape(nl, 3 * d)
    rest = sums[2 * n_mod:]

    def take(i):
        o = offs[i] - offs[2]
        return rest[o:o + sizes[i]]

    g_norm_w = take(2).reshape(nl, d)
    g_conv_norm_w = take(3).reshape(nl, dc)
    g_ret_norm_w = take(4).reshape(nl, dc)
    g_final_norm_w = take(5)
    g_conv_w = lax.dynamic_slice(take(6).reshape(nl, 3, dc), (0, 0, s_idx * cw4), (nl, 3, cw4))
    g_decay_f = take(7).reshape(nl, nh)
    g_decay_b = take(8).reshape(nl, nh)
    loss = take(9)[0]

    dmx_mine = lax.dynamic_slice(dmx_all.reshape(N_DEV, nl, 3 * d), (0, 0, s_idx * nm), (N_DEV, nl, nm))
    dmc_mine = lax.dynamic_slice(dmc_tot, (0, s_idx * nm), (nl, nm))
    dm = jnp.concatenate([dmx_mine.transpose(1, 0, 2), dmc_mine[:, None, :], jnp.zeros((nl, 7, nm), F32)], axis=1)
    g_w_mod, pc = _wmod_grad(sc16, dm, w_mod)
    pc_g = _allgather_small(pc[8:16], "gather_cctx_partials").reshape(N_DEV, 8, d)
    g_c_ctx = _cctx_grad(pc_g[0::2], jnp.broadcast_to(c_ctx[None], (8, d)))[0]
    d_w_mod, nm_w_mod, nv_w_mod = _adamw(g_w_mod.reshape(nl * d, nm), w_mod.reshape(nl * d, nm),
                                         m_w_mod.reshape(nl * d, nm), v_w_mod.reshape(nl * d, nm), "adamw_w_mod")

    small_g = [g_c_ctx, g_norm_w, g_b_mod, g_conv_w, g_conv_norm_w, g_ret_norm_w, g_decay_f, g_decay_b, g_final_norm_w]
    small_w = [c_ctx, norm_w, b_mod, conv_w, conv_norm_w, ret_norm_w, ret_decay_f, ret_decay_b, final_norm_w]
    small_m = [m_c_ctx, m_norm_w, m_b_mod, m_conv_w, m_conv_norm_w, m_ret_norm_w, m_ret_decay_f, m_ret_decay_b,
               m_final_norm_w]
    small_v = [v_c_ctx, v_norm_w, v_b_mod, v_conv_w, v_conv_norm_w, v_ret_norm_w, v_ret_decay_f, v_ret_decay_b,
               v_final_norm_w]
    pk = lambda parts: _pad_flat(parts, 8 * 128).reshape(-1, 128)
    sd, sm, sv = _adamw(pk(small_g), pk(small_w), pk(small_m), pk(small_v), "adamw_small")

    def unpack(flat2d):
        flat = flat2d.reshape(-1)
        out, o = [], 0
        for wgt in small_w:
            out.append(flat[o:o + wgt.size].reshape(wgt.shape))
            o += wgt.size
        return out

    sd, sm, sv = unpack(sd), unpack(sm), unpack(sv)

    def order(small, mod, w_in_, w_out_):
        return [small[0], small[1], mod, small[2], w_in_, small[3], small[4], small[5], small[6], small[7], w_out_,
                small[8]]

    grads = order(small_g, g_w_mod, g_w_in.reshape(nl, d, n4), g_w_out.reshape(nl, r4, d))
    deltas = order(sd, d_w_mod.reshape(nl, d, nm), d_w_in.reshape(nl, d, n4), d_w_out.reshape(nl, r4, d))
    new_m = order(sm, nm_w_mod.reshape(nl, d, nm), nm_w_in.reshape(nl, d, n4), nm_w_out.reshape(nl, r4, d))
    new_v = order(sv, nv_w_mod.reshape(nl, d, nm), nv_w_in.reshape(nl, d, n4), nv_w_out.reshape(nl, r4, d))
    return (loss, grad_x, *grads, *deltas, *new_m, *new_v)
```
